```python
import math
import jax
import jax.numpy as jnp
from jax import lax
import numpy as np

D_MODEL = 2048
BATCH = 4
SEQ = 2048
DEPTH = 1

D_PLE = 256
MIX_WIDTH = D_MODEL
DIFF_WIDTH = MIX_WIDTH // 2
DIFF_HEAD_DIM = 64
DIFF_V_DIM = 2 * DIFF_HEAD_DIM
DIFF_HEADS = DIFF_WIDTH // DIFF_V_DIM
DIFF_QK_WIDTH = DIFF_HEADS * 2 * DIFF_HEAD_DIM
RWKV_WIDTH = MIX_WIDTH - DIFF_WIDTH
RWKV_HEAD_SIZE = 64
RWKV_HEADS = RWKV_WIDTH // RWKV_HEAD_SIZE
LORA_DECAY = 64
LORA_AAA = 64
LORA_GATE = 160
RWKV_PROJ = 3 * RWKV_WIDTH + LORA_DECAY + LORA_AAA + LORA_GATE
O_RWKV = 2 * DIFF_QK_WIDTH + DIFF_WIDTH
W_IN_COLS = O_RWKV + RWKV_PROJ
REL_BUCKETS = 32
REL_MAX_DIST = 128
Q_BLOCK = 128
N_GROUPS = 4
EXPERTS_PER_GROUP = 8
N_EXPERTS = N_GROUPS * EXPERTS_PER_GROUP
TOP_K_EXPERTS = 2
D_EXPERT = D_MODEL // 4
LN_EPS = 1e-5
RWKV_GN_EPS = 64e-5
NEG_INF = -1e30
DEEPNORM_ALPHA = (2 * DEPTH) ** 0.25
DEEPNORM_BETA = (8 * DEPTH) ** -0.25

kernel_name = 'hybrid_diffattn_rwkv7_hmoe'


def layer_norm(x, g, b):
    xf = x.astype(jnp.float32)
    mu = xf.mean(-1, keepdims=True)
    var = jnp.square(xf - mu).mean(-1, keepdims=True)
    return ((xf - mu) * lax.rsqrt(var + LN_EPS) * g + b).astype(x.dtype)


def rms_norm(x, g):
    xf = x.astype(jnp.float32)
    return xf * lax.rsqrt(jnp.square(xf).mean(-1, keepdims=True) + LN_EPS) * g


def t5_bucket(rel):
    n = jnp.maximum(rel, 0)
    max_exact = REL_BUCKETS // 2
    nf = jnp.maximum(n, 1).astype(jnp.float32)
    large = max_exact + (jnp.log(nf / max_exact) / math.log(REL_MAX_DIST / max_exact)
                         * (REL_BUCKETS - max_exact)).astype(jnp.int32)
    large = jnp.minimum(large, REL_BUCKETS - 1)
    return jnp.where(n < max_exact, n, large)


def diff_attention(q, k, v, rel_bias, lam, subln_g, lambda_init):
    B, S = q.shape[0], q.shape[1]
    nb = S // Q_BLOCK
    q_blocks = q.reshape(B, nb, Q_BLOCK, DIFF_HEADS, 2, DIFF_HEAD_DIM).swapaxes(0, 1)
    k_pos = jnp.arange(S, dtype=jnp.int32)
    vf = v.astype(jnp.float32)
    scale = DIFF_HEAD_DIM ** -0.5

    def one_block(args):
        qb, bi = args
        q_pos = bi * Q_BLOCK + jnp.arange(Q_BLOCK, dtype=jnp.int32)
        rel = q_pos[:, None] - k_pos[None, :]
        bias = jnp.moveaxis(rel_bias[t5_bucket(rel)], -1, 0).astype(jnp.float32)
        logits = jnp.einsum('bqhcd,bkhcd->bchqk', qb, k).astype(jnp.float32) * scale + bias[None, None]
        logits = jnp.where(rel[None, None, None] >= 0, logits, NEG_INF)
        probs = jax.nn.softmax(logits, axis=-1)
        attn = probs[:, 0] - lam * probs[:, 1]
        return jnp.einsum('bhqk,bkhd->bqhd', attn, vf)

    out = lax.map(one_block, (q_blocks, jnp.arange(nb, dtype=jnp.int32)))
    out = out.swapaxes(0, 1).reshape(B, S, DIFF_HEADS, DIFF_V_DIM)
    out = rms_norm(out, subln_g) * (1.0 - lambda_init)
    return out.reshape(B, S, DIFF_WIDTH).astype(q.dtype)


def _wkv7_step(state, inp):
    r, w, k, v, a, b = inp
    sa = jnp.einsum('bhij,bhj->bhi', state, a)
    state = state * w[:, :, None, :] + sa[..., None] * b[:, :, None, :] + v[..., None] * k[:, :, None, :]
    y = jnp.einsum('bhij,bhj->bhi', state, r)
    return state, y


def rwkv7_time_mix(proj, mu, w0, w2, a0, a2, g2, k_k, k_a, r_k, lnx_g, lnx_b):
    B, S = proj.shape[0], proj.shape[1]
    H, N = RWKV_HEADS, RWKV_HEAD_SIZE
    prev = jnp.pad(proj, ((0, 0), (1, 0), (0, 0)))[:, :-1]
    xs = proj + (prev - proj) * mu
    cuts = [RWKV_WIDTH, 2 * RWKV_WIDTH, 3 * RWKV_WIDTH,
            3 * RWKV_WIDTH + LORA_DECAY, 3 * RWKV_WIDTH + LORA_DECAY + LORA_AAA]
    r, k, v, xw, xa, xg = jnp.split(xs, cuts, axis=-1)
    w = -jax.nn.softplus(-(w0 + jnp.tanh(xw) @ w2)) - 0.5
    decay = jnp.exp(-jnp.exp(w.astype(jnp.float32)))
    a = jax.nn.sigmoid(a0 + xa @ a2)
    g = jax.nn.sigmoid(xg) @ g2
    kk = (k * k_k).reshape(B, S, H, N).astype(jnp.float32)
    kk = kk / jnp.maximum(jnp.linalg.norm(kk, axis=-1, keepdims=True), 1e-12)
    k = k * (1.0 + (a - 1.0) * k_a)
    rh = r.reshape(B, S, H, N).astype(jnp.float32)
    kh = k.reshape(B, S, H, N).astype(jnp.float32)
    vh = v.reshape(B, S, H, N).astype(jnp.float32)
    wh = decay.reshape(B, S, H, N)
    ah = a.reshape(B, S, H, N).astype(jnp.float32)
    xs_seq = tuple(jnp.moveaxis(t, 1, 0) for t in (rh, wh, kh, vh, -kk, kk * ah))
    state0 = jnp.zeros((B, H, N, N), jnp.float32)
    _, y = lax.scan(_wkv7_step, state0, xs_seq)
    y = jnp.moveaxis(y, 0, 1)
    y_mu = y.mean(-1, keepdims=True)
    y_var = jnp.square(y - y_mu).mean(-1, keepdims=True)
    y = (y - y_mu) * lax.rsqrt(y_var + RWKV_GN_EPS) * lnx_g.reshape(H, N) + lnx_b.reshape(H, N)
    bonus = jnp.sum(rh * kh * r_k, axis=-1, keepdims=True) * vh
    return ((y + bonus).reshape(B, S, RWKV_WIDTH) * g).astype(proj.dtype)


def hier_moe(h, gw, gb, ew, eb, w_gate, w_up, w_down):
    B, S, D = h.shape
    t = h.reshape(B * S, D)
    gprob = jax.nn.softmax((t @ gw + gb).astype(jnp.float32), axis=-1)
    gtop, gidx = lax.top_k(gprob, 1)
    elog = (t @ ew + eb).astype(jnp.float32).reshape(-1, N_GROUPS, EXPERTS_PER_GROUP)
    elog = jnp.take_along_axis(elog, gidx[:, :, None], axis=1)[:, 0]
    eprob = jax.nn.softmax(elog, axis=-1)
    etop, eidx = lax.top_k(eprob, TOP_K_EXPERTS)
    etop = etop / etop.sum(-1, keepdims=True)
    w_local = jnp.sum(jax.nn.one_hot(eidx, EXPERTS_PER_GROUP, dtype=jnp.float32) * etop[..., None], axis=1) * gtop
    gsel = jax.nn.one_hot(gidx[:, 0], N_GROUPS, dtype=jnp.float32)
    out = jnp.zeros_like(t)
    for gi in range(N_GROUPS):
        wg = (w_local * gsel[:, gi:gi + 1]).astype(t.dtype)
        hid = jax.nn.silu(jnp.einsum('td,edf->tef', t, w_gate[gi])) * jnp.einsum('td,edf->tef', t, w_up[gi])
        out = out + jnp.einsum('tef,efd->td', hid * wg[:, :, None], w_down[gi])
    return out.reshape(B, S, D)


def setup_inputs(seed: int = 0) -> dict:
    key = jax.random.key(seed)
    ks = iter(jax.random.split(key, 64))
    L, D = DEPTH, D_MODEL
    sd = D ** -0.5

    def nrm(shape, s):
        return s * jax.random.normal(next(ks), shape, jnp.float32)

    def unif(shape, lo, hi):
        return jax.random.uniform(next(ks), shape, jnp.float32, minval=lo, maxval=hi)

    x = nrm((BATCH, SEQ, D), 1.0)
    p = nrm((L, BATCH, SEQ, D_PLE), 1.0)
    ln_in_g = 1.0 + nrm((D,), 0.02)
    ln_in_b = nrm((D,), 0.02)
    rel_bias = nrm((REL_BUCKETS, DIFF_HEADS), 0.2)
    w_in = jnp.concatenate([
        nrm((L, D, 2 * DIFF_QK_WIDTH), sd),
        nrm((L, D, DIFF_WIDTH), sd * DEEPNORM_BETA),
        nrm((L, D, 2 * RWKV_WIDTH), sd),
        nrm((L, D, RWKV_WIDTH), sd * DEEPNORM_BETA),
        nrm((L, D, LORA_DECAY + LORA_AAA + LORA_GATE), sd),
    ], axis=-1)
    diff_lam_q1 = nrm((L, DIFF_HEAD_DIM), 0.1)
    diff_lam_k1 = nrm((L, DIFF_HEAD_DIM), 0.1)
    diff_lam_q2 = nrm((L, DIFF_HEAD_DIM), 0.1)
    diff_lam_k2 = nrm((L, DIFF_HEAD_DIM), 0.1)
    diff_subln_g = 1.0 + nrm((L, DIFF_V_DIM), 0.02)
    rwkv_mu = unif((L, RWKV_PROJ), 0.0, 1.0)
    rwkv_w0 = unif((L, RWKV_WIDTH), -6.0, 1.0)
    rwkv_w2 = nrm((L, LORA_DECAY, RWKV_WIDTH), 0.1)
    rwkv_a0 = nrm((L, RWKV_WIDTH), 0.1)
    rwkv_a2 = nrm((L, LORA_AAA, RWKV_WIDTH), 0.5 * LORA_AAA ** -0.5)
    rwkv_g2 = nrm((L, LORA_GATE, RWKV_WIDTH), LORA_GATE ** -0.5)
    rwkv_k_k = 0.85 + nrm((L, RWKV_WIDTH), 0.02)
    rwkv_k_a = 1.0 + nrm((L, RWKV_WIDTH), 0.02)
    rwkv_r_k = nrm((L, RWKV_HEADS, RWKV_HEAD_SIZE), 0.1)
    rwkv_lnx_g = 1.0 + nrm((L, RWKV_WIDTH), 0.02)
    rwkv_lnx_b = nrm((L, RWKV_WIDTH), 0.02)
    w_out = nrm((L, MIX_WIDTH, D), MIX_WIDTH ** -0.5 * DEEPNORM_BETA)
    ln1_g = 1.0 + nrm((L, D), 0.02)
    ln1_b = nrm((L, D), 0.02)
    router_group_w = nrm((L, D, N_GROUPS), sd)
    router_group_b = nrm((L, N_GROUPS), 0.01)
    router_expert_w = nrm((L, D, N_EXPERTS), sd)
    router_expert_b = nrm((L, N_EXPERTS), 0.01)
    moe_w_gate = nrm((L, N_GROUPS, EXPERTS_PER_GROUP, D, D_EXPERT), sd)
    moe_w_up = nrm((L, N_GROUPS, EXPERTS_PER_GROUP, D, D_EXPERT), sd)
    moe_w_down = nrm((L, N_GROUPS, EXPERTS_PER_GROUP, D_EXPERT, D), D_EXPERT ** -0.5 * DEEPNORM_BETA)
    ln2_g = 1.0 + nrm((L, D), 0.02)
    ln2_b = nrm((L, D), 0.02)
    ple_w = nrm((L, D_PLE, D), D_PLE ** -0.5)
    ple_norm_g = 1.0 + nrm((L, D), 0.02)
    ple_gate_w = nrm((L, D, D), sd)
    return {'x': x, 'p': p, 'ln_in_g': ln_in_g, 'ln_in_b': ln_in_b, 'rel_bias': rel_bias,
            'w_in': w_in, 'diff_lam_q1': diff_lam_q1, 'diff_lam_k1': diff_lam_k1,
            'diff_lam_q2': diff_lam_q2, 'diff_lam_k2': diff_lam_k2, 'diff_subln_g': diff_subln_g,
            'rwkv_mu': rwkv_mu, 'rwkv_w0': rwkv_w0, 'rwkv_w2': rwkv_w2, 'rwkv_a0': rwkv_a0,
            'rwkv_a2': rwkv_a2, 'rwkv_g2': rwkv_g2, 'rwkv_k_k': rwkv_k_k, 'rwkv_k_a': rwkv_k_a,
            'rwkv_r_k': rwkv_r_k, 'rwkv_lnx_g': rwkv_lnx_g, 'rwkv_lnx_b': rwkv_lnx_b,
            'w_out': w_out, 'ln1_g': ln1_g, 'ln1_b': ln1_b,
            'router_group_w': router_group_w, 'router_group_b': router_group_b,
            'router_expert_w': router_expert_w, 'router_expert_b': router_expert_b,
            'moe_w_gate': moe_w_gate, 'moe_w_up': moe_w_up, 'moe_w_down': moe_w_down,
            'ln2_g': ln2_g, 'ln2_b': ln2_b, 'ple_w': ple_w, 'ple_norm_g': ple_norm_g,
            'ple_gate_w': ple_gate_w}


def reference(x, p, ln_in_g, ln_in_b, rel_bias, w_in, diff_lam_q1, diff_lam_k1, diff_lam_q2,
              diff_lam_k2, diff_subln_g, rwkv_mu, rwkv_w0, rwkv_w2, rwkv_a0, rwkv_a2, rwkv_g2,
              rwkv_k_k, rwkv_k_a, rwkv_r_k, rwkv_lnx_g, rwkv_lnx_b, w_out, ln1_g, ln1_b,
              router_group_w, router_group_b, router_expert_w, router_expert_b,
              moe_w_gate, moe_w_up, moe_w_down, ln2_g, ln2_b, ple_w, ple_norm_g, ple_gate_w):
    B, S = x.shape[0], x.shape[1]
    h = layer_norm(x, ln_in_g, ln_in_b)
    for i in range(DEPTH):
        proj = h @ w_in[i]
        q = proj[..., :DIFF_QK_WIDTH].reshape(B, S, DIFF_HEADS, 2, DIFF_HEAD_DIM)
        k = proj[..., DIFF_QK_WIDTH:2 * DIFF_QK_WIDTH].reshape(B, S, DIFF_HEADS, 2, DIFF_HEAD_DIM)
        v = proj[..., 2 * DIFF_QK_WIDTH:O_RWKV].reshape(B, S, DIFF_HEADS, DIFF_V_DIM)
        lambda_init = 0.8 - 0.6 * math.exp(-0.3 * i)
        lam = (jnp.exp(jnp.sum(diff_lam_q1[i].astype(jnp.float32) * diff_lam_k1[i]))
               - jnp.exp(jnp.sum(diff_lam_q2[i].astype(jnp.float32) * diff_lam_k2[i]))
               + lambda_init)
        y_diff = diff_attention(q, k, v, rel_bias, lam, diff_subln_g[i], lambda_init)
        y_rwkv = rwkv7_time_mix(proj[..., O_RWKV:], rwkv_mu[i], rwkv_w0[i], rwkv_w2[i], rwkv_a0[i],
                                rwkv_a2[i], rwkv_g2[i], rwkv_k_k[i], rwkv_k_a[i], rwkv_r_k[i],
                                rwkv_lnx_g[i], rwkv_lnx_b[i])
        mix = jnp.concatenate([y_diff, y_rwkv], axis=-1) @ w_out[i]
        h = layer_norm(DEEPNORM_ALPHA * h + mix, ln1_g[i], ln1_b[i])
        ffn = hier_moe(h, router_group_w[i], router_group_b[i], router_expert_w[i], router_expert_b[i],
                       moe_w_gate[i], moe_w_up[i], moe_w_down[i])
        h = layer_norm(DEEPNORM_ALPHA * h + ffn, ln2_g[i], ln2_b[i])
        ple = rms_norm(p[i] @ ple_w[i], ple_norm_g[i]).astype(h.dtype)
        h = h + jax.nn.sigmoid(h @ ple_gate_w[i]) * ple
    return h
```

```python
import functools
import math

import jax
import jax.numpy as jnp
from jax import lax
from jax.experimental import pallas as pl
from jax.experimental.pallas import tpu as pltpu

F32 = jnp.float32
BF16 = jnp.bfloat16

DIFF_HEADS = 8
DIFF_HEAD_DIM = 64
DIFF_V_DIM = 128
DIFF_WIDTH = DIFF_HEADS * DIFF_V_DIM
DIFF_QK_WIDTH = DIFF_HEADS * 2 * DIFF_HEAD_DIM
RWKV_HEADS = 16
RWKV_HEAD_SIZE = 64
RWKV_WIDTH = RWKV_HEADS * RWKV_HEAD_SIZE
LORA_DECAY = 64
LORA_AAA = 64
LORA_GATE = 160
RWKV_PROJ = 3 * RWKV_WIDTH + LORA_DECAY + LORA_AAA + LORA_GATE
O_RWKV = 2 * DIFF_QK_WIDTH + DIFF_WIDTH
REL_BUCKETS = 32
REL_MAX_DIST = 128
N_GROUPS = 4
EXPERTS_PER_GROUP = 8
N_EXPERTS = N_GROUPS * EXPERTS_PER_GROUP
LN_EPS = 1e-5
RWKV_GN_EPS = 64e-5
NEG_INF = -1e30
DEPTH = 1
DEEPNORM_ALPHA = (2 * DEPTH) ** 0.25

LANES = 128
RWKV_PROJ_PAD = 3584
ROUTER_LANES = LANES
VMEM_LIMIT = 48 * 1024 * 1024


def _cparams(sem):
    return pltpu.CompilerParams(dimension_semantics=sem, vmem_limit_bytes=VMEM_LIMIT)


def _bdot(a, b):
    return jnp.dot(a.astype(BF16), b.astype(BF16), preferred_element_type=F32)


def _bdot_nt(a, b):
    return lax.dot_general(a.astype(BF16), b.astype(BF16), (((1,), (1,)), ((), ())),
                           preferred_element_type=F32)


def _bdot_tn(a, b):
    return lax.dot_general(a.astype(BF16), b.astype(BF16), (((0,), (0,)), ((), ())),
                           preferred_element_type=F32)


def _split3(x):
    hi = x.astype(BF16)
    r1 = x - hi.astype(F32)
    mid = r1.astype(BF16)
    lo = (r1 - mid.astype(F32)).astype(BF16)
    return hi, mid, lo


def _dot_x_sel(x, sel):
    hi, mid, lo = _split3(x)
    d = lambda p: jnp.dot(p, sel, preferred_element_type=F32)
    return d(hi) + d(mid) + d(lo)


def _dot_sel_x(sel, x):
    hi, mid, lo = _split3(x)
    d = lambda p: jnp.dot(sel, p, preferred_element_type=F32)
    return d(hi) + d(mid) + d(lo)


def _layer_norm(x, g, b):
    mu = jnp.mean(x, axis=-1, keepdims=True)
    xc = x - mu
    var = jnp.mean(xc * xc, axis=-1, keepdims=True)
    return xc * lax.rsqrt(var + LN_EPS) * g + b


def _sigmoid(z):
    return 1.0 / (1.0 + jnp.exp(-z))


def _ln_mm_kernel(x_ref, g_ref, b_ref, w_ref, o_ref, h_ref, xn_scr):
    @pl.when(pl.program_id(1) == 0)
    def _():
        xn = _layer_norm(x_ref[...], g_ref[...], b_ref[...])
        h_ref[...] = xn
        xn_scr[...] = xn.astype(BF16)

    o_ref[...] = jnp.dot(xn_scr[...], w_ref[...], preferred_element_type=F32).astype(o_ref.dtype)


def ln_matmul(x, g, b, w, out_dtype, tm, tn):
    m, d = x.shape
    n = w.shape[1]
    return pl.pallas_call(
        _ln_mm_kernel,
        grid=(m // tm, n // tn),
        in_specs=[
            pl.BlockSpec((tm, d), lambda i, j: (i, 0)),
            pl.BlockSpec((1, d), lambda i, j: (0, 0)),
            pl.BlockSpec((1, d), lambda i, j: (0, 0)),
            pl.BlockSpec((d, tn), lambda i, j: (0, j)),
        ],
        out_specs=[
            pl.BlockSpec((tm, tn), lambda i, j: (i, j)),
            pl.BlockSpec((tm, d), lambda i, j: (i, 0)),
        ],
        out_shape=[jax.ShapeDtypeStruct((m, n), out_dtype),
                   jax.ShapeDtypeStruct((m, d), F32)],
        scratch_shapes=[pltpu.VMEM((tm, d), BF16)],
        compiler_params=_cparams(("parallel", "arbitrary")),
        name="ln_matmul",
    )(x, g.reshape(1, d), b.reshape(1, d), w)


def _attn_kernel(lam_ref, far_ref, q_ref, k_ref, v_ref, bias_ref, g_ref, o_ref,
                 m_scr, l_scr, acc_scr, *, tq, scale, post_scale):
    h = pl.program_id(1)
    qi = pl.program_id(2)
    lane = lax.broadcasted_iota(jnp.int32, (tq, LANES), 1)
    qs = (q_ref[0].astype(F32) * scale).astype(BF16)
    zero = jnp.zeros_like(qs)
    q_maps = (jnp.where(lane < DIFF_HEAD_DIM, qs, zero), jnp.where(lane >= DIFF_HEAD_DIM, qs, zero))

    m_scr[...] = jnp.full(m_scr.shape, NEG_INF, F32)
    l_scr[...] = jnp.zeros(l_scr.shape, F32)
    acc_scr[...] = jnp.zeros(acc_scr.shape, F32)

    def step(kstart, bias, causal):
        kb = k_ref[0, pl.ds(kstart, tq), :]
        vb = v_ref[0, pl.ds(kstart, tq), :]
        for c in range(2):
            s = lax.dot_general(q_maps[c], kb, (((1,), (1,)), ((), ())),
                                preferred_element_type=F32) + bias
            if causal:
                ri = lax.broadcasted_iota(jnp.int32, (tq, tq), 0)
                ci = lax.broadcasted_iota(jnp.int32, (tq, tq), 1)
                s = jnp.where(ri >= ci, s, NEG_INF)
            m_old = m_scr[c]
            m_new = jnp.maximum(m_old, jnp.max(s, axis=1, keepdims=True))
            p = jnp.exp(s - m_new)
            corr = jnp.exp(m_old - m_new)
            l_scr[c] = corr * l_scr[c] + jnp.sum(p, axis=1, keepdims=True)
            acc_scr[c] = corr * acc_scr[c] + jnp.dot(p.astype(BF16), vb, preferred_element_type=F32)
            m_scr[c] = m_new

    far = far_ref[h]

    def far_body(kb, carry):
        step(pl.multiple_of(kb * tq, tq), far, False)
        return carry

    lax.fori_loop(0, jnp.maximum(qi - 1, 0), far_body, 0)

    @pl.when(qi >= 1)
    def _():
        step(pl.multiple_of((qi - 1) * tq, tq), bias_ref[0, 1], False)

    step(pl.multiple_of(qi * tq, tq), bias_ref[0, 0], True)

    o = acc_scr[0] / l_scr[0] - lam_ref[0] * (acc_scr[1] / l_scr[1])
    ms = jnp.mean(o * o, axis=1, keepdims=True)
    o_ref[0] = (o * lax.rsqrt(ms + LN_EPS) * g_ref[...] * post_scale).astype(o_ref.dtype)


def diff_attention(qkv, bias_tiles, far_bias, lam, subln_g, lambda_init, tq):
    bsz, s, _ = qkv.shape
    kern = functools.partial(_attn_kernel, tq=tq, scale=DIFF_HEAD_DIM ** -0.5,
                             post_scale=1.0 - lambda_init)
    nqk = DIFF_QK_WIDTH // LANES
    return pl.pallas_call(
        kern,
        grid=(bsz, DIFF_HEADS, s // tq),
        in_specs=[
            pl.BlockSpec(memory_space=pltpu.SMEM),
            pl.BlockSpec(memory_space=pltpu.SMEM),
            pl.BlockSpec((1, tq, LANES), lambda b, h, i: (b, i, h)),
            pl.BlockSpec((1, s, LANES), lambda b, h, i: (b, 0, nqk + h)),
            pl.BlockSpec((1, s, LANES), lambda b, h, i: (b, 0, 2 * nqk + h)),
            pl.BlockSpec((1, 2, tq, tq), lambda b, h, i: (h, 0, 0, 0)),
            pl.BlockSpec((1, LANES), lambda b, h, i: (0, 0)),
        ],
        out_specs=pl.BlockSpec((1, tq, LANES), lambda b, h, i: (b, i, h)),
        out_shape=jax.ShapeDtypeStruct((bsz, s, DIFF_WIDTH), BF16),
        scratch_shapes=[pltpu.VMEM((2, tq, 1), F32), pltpu.VMEM((2, tq, 1), F32),
                        pltpu.VMEM((2, tq, LANES), F32)],
        compiler_params=_cparams(("parallel", "parallel", "arbitrary")),
        name="diff_attention",
    )(lam, far_bias, qkv, qkv, qkv, bias_tiles, subln_g.reshape(1, LANES))


def _rwkv_prep_kernel(x_ref, prev_ref, mu_ref, w0_ref, w2_ref, a0_ref, a2_ref, g2_ref,
                      kk_ref, ka_ref, rk_ref, sel_ref, selt_ref,
                      r_out, k_out, v_out, lw_out, kn_out, b_out, g_out, bonus_out, *, tm, seq):
    i = pl.program_id(0)
    x = x_ref[...]
    w = RWKV_WIDTH
    row = lax.broadcasted_iota(jnp.int32, x.shape, 0)
    is_start = (i * tm) % seq == 0
    last_prev = jnp.where(is_start, 0.0, prev_ref[7:8, :])
    prev = jnp.where(row == 0, last_prev, pltpu.roll(x, 1, 0))
    xs = x + (prev - x) * mu_ref[...]

    r = xs[:, 0:w]
    k = xs[:, w:2 * w]
    v = xs[:, 2 * w:3 * w]
    xwa = xs[:, 3 * w:3 * w + 2 * LORA_DECAY]
    xg = xs[:, 3 * w + LANES:3 * w + LANES + 2 * LANES]

    z = w0_ref[...] + _bdot(jnp.tanh(xwa), w2_ref[...])
    softplus_neg_z = jnp.maximum(-z, 0.0) + jnp.log(1.0 + jnp.exp(-jnp.abs(z)))
    lw_out[...] = -jnp.exp(-softplus_neg_z - 0.5)
    a_lr = _sigmoid(a0_ref[...] + _bdot(xwa, a2_ref[...]))
    g_out[...] = _bdot(_sigmoid(xg), g2_ref[...])

    sel = sel_ref[...]
    selt = selt_ref[...]
    kk = k * kk_ref[...]
    ss = _dot_x_sel(kk * kk, sel)
    inv = 1.0 / jnp.maximum(jnp.sqrt(ss), 1e-12)
    kn = kk * _dot_x_sel(inv, selt)
    k2 = k * (1.0 + (a_lr - 1.0) * ka_ref[...])
    bs = _dot_x_sel(r * k2 * rk_ref[...], sel)
    r_out[...] = r
    k_out[...] = k2
    v_out[...] = v
    kn_out[...] = kn
    b_out[...] = kn * a_lr
    bonus_out[...] = _dot_x_sel(bs, selt) * v


def rwkv_prep(pr, mu, w0, w2p, a0, a2p, g2p, k_k, k_a, r_k, sel, selt, seq, tm):
    t, wp = pr.shape
    w = RWKV_WIDTH
    row = lambda a: a.reshape(1, -1)
    full = lambda shp: pl.BlockSpec(shp, lambda i: (0, 0))
    outs = [jax.ShapeDtypeStruct((t, w), F32)] * 8
    kern = functools.partial(_rwkv_prep_kernel, tm=tm, seq=seq)
    return pl.pallas_call(
        kern,
        grid=(t // tm,),
        in_specs=[
            pl.BlockSpec((tm, wp), lambda i: (i, 0)),
            pl.BlockSpec((8, wp), lambda i: (jnp.maximum(i * (tm // 8) - 1, 0), 0)),
            full((1, wp)), full((1, w)), full((LANES, w)), full((1, w)), full((LANES, w)),
            full((2 * LANES, w)), full((1, w)), full((1, w)), full((1, w)),
            full((w, LANES)), full((LANES, w)),
        ],
        out_specs=[pl.BlockSpec((tm, w), lambda i: (i, 0))] * 8,
        out_shape=outs,
        compiler_params=_cparams(("parallel",)),
        name="rwkv_prep",
    )(pr, pr, row(mu), row(w0), w2p, row(a0), a2p, g2p, row(k_k), row(k_a), row(r_k), sel, selt)


def _rwkv_scan_kernel(r_ref, k_ref, v_ref, lw_ref, kn_ref, b_ref, y_ref, s_scr, *, chunk, pairs):
    c2 = 2 * chunk

    @pl.when(pl.program_id(2) == 0)
    def _():
        s_scr[...] = jnp.zeros(s_scr.shape, F32)

    ri = lax.broadcasted_iota(jnp.int32, (chunk, chunk), 0)
    ci = lax.broadcasted_iota(jnp.int32, (chunk, chunk), 1)
    ltri = jnp.where(ri >= ci, 1.0, 0.0).astype(BF16)

    lw = lw_ref[0]
    cl = _dot_sel_x(ltri, lw)
    mid = chunk // 2 - 1
    clm = cl[mid:mid + 1, :]
    cle = cl[chunk - 1:chunk, :]
    w_mid = jnp.exp(cl - clm)
    w_mid_prev = jnp.exp(cl - lw - clm)
    w_mid_inv = jnp.exp(clm - cl)
    w_abs = jnp.exp(cl)
    w_abs_prev = jnp.exp(cl - lw)
    w_end = jnp.exp(cle - cl)
    w_tot = jnp.exp(cle)

    r = r_ref[0]
    k = k_ref[0]
    v = v_ref[0]
    a = -kn_ref[0]
    b = b_ref[0]
    a_mid = a * w_mid_prev
    r_mid = r * w_mid
    b_mid = b * w_mid_inv
    k_mid = k * w_mid_inv
    a_abs = a * w_abs_prev
    r_abs = r * w_abs
    b_end = b * w_end
    k_end = k * w_end

    lane = lax.broadcasted_iota(jnp.int32, (chunk, LANES), 1)
    first = lane < RWKV_HEAD_SIZE

    def stack(x):
        return jnp.concatenate([jnp.where(first, x, 0.0), jnp.where(first, 0.0, x)], axis=0)

    rr = lax.broadcasted_iota(jnp.int32, (c2, c2), 0)
    cc = lax.broadcasted_iota(jnp.int32, (c2, c2), 1)
    same = (rr >= chunk) == (cc >= chunk)
    strict = same & (rr > cc)
    incl = same & (rr >= cc)
    eye = jnp.where(rr == cc, 1.0, 0.0)

    for p in range(pairs):
        sl = slice(p * LANES, (p + 1) * LANES)
        a_m, r_m, b_m, k_m = stack(a_mid[:, sl]), stack(r_mid[:, sl]), stack(b_mid[:, sl]), stack(k_mid[:, sl])
        a_0, r_0 = stack(a_abs[:, sl]), stack(r_abs[:, sl])
        b_e, k_e, v_s = stack(b_end[:, sl]), stack(k_end[:, sl]), stack(v[:, sl])

        l_ab = jnp.where(strict, _bdot_nt(a_m, b_m), 0.0)
        l_ak = jnp.where(strict, _bdot_nt(a_m, k_m), 0.0)
        m_rb = jnp.where(incl, _bdot_nt(r_m, b_m), 0.0)
        m_rk = jnp.where(incl, _bdot_nt(r_m, k_m), 0.0)

        t_inv = eye + l_ab
        lp = l_ab
        for _ in range(max(chunk.bit_length() - 2, 0)):
            lp = _bdot(lp, lp)
            t_inv = t_inv + _bdot(lp, t_inv)

        a_p = _bdot(t_inv, a_0)
        v_p = _bdot(t_inv, _bdot(l_ak, v_s))
        r_p = r_0 + _bdot(m_rb, a_p)
        y_p = _bdot(m_rb, v_p) + _bdot(m_rk, v_s)

        s_old = s_scr[p]
        y_s = _bdot_nt(r_p, s_old) + y_p
        y_ref[0, :, sl] = y_s[:chunk] + y_s[chunk:]
        s_scr[p] = (s_old * w_tot[:, sl] + _bdot(s_old, _bdot_tn(a_p, b_e))
                    + _bdot_tn(v_p, b_e) + _bdot_tn(v_s, k_e))


def rwkv_scan(r, k, v, lw, kn, b, chunk, pairs):
    bsz, s, w = r.shape
    pw = pairs * LANES
    spec = pl.BlockSpec((1, chunk, pw), lambda bi, hi, ci: (bi, ci, hi))
    kern = functools.partial(_rwkv_scan_kernel, chunk=chunk, pairs=pairs)
    return pl.pallas_call(
        kern,
        grid=(bsz, w // pw, s // chunk),
        in_specs=[spec] * 6,
        out_specs=spec,
        out_shape=jax.ShapeDtypeStruct((bsz, s, w), F32),
        scratch_shapes=[pltpu.VMEM((pairs, LANES, LANES), F32)],
        compiler_params=_cparams(("parallel", "parallel", "arbitrary")),
        name="rwkv_scan",
    )(r, k, v, lw, kn, b)


def _rwkv_post_kernel(y_ref, bonus_ref, g_ref, lg_ref, lb_ref, sel_ref, selt_ref, o_ref):
    y = y_ref[...]
    sel = sel_ref[...]
    selt = selt_ref[...]
    inv_n = 1.0 / RWKV_HEAD_SIZE
    mu = _dot_x_sel(_dot_x_sel(y, sel) * inv_n, selt)
    yc = y - mu
    var = _dot_x_sel(_dot_x_sel(yc * yc, sel) * inv_n, selt)
    yn = yc * lax.rsqrt(var + RWKV_GN_EPS) * lg_ref[...] + lb_ref[...]
    o_ref[...] = ((yn + bonus_ref[...]) * g_ref[...]).astype(o_ref.dtype)


def rwkv_post(y, bonus, g, lnx_g, lnx_b, sel, selt, tm):
    t, w = y.shape
    tile = pl.BlockSpec((tm, w), lambda i: (i, 0))
    full = lambda shp: pl.BlockSpec(shp, lambda i: (0, 0))
    return pl.pallas_call(
        _rwkv_post_kernel,
        grid=(t // tm,),
        in_specs=[tile, tile, tile, full((1, w)), full((1, w)), full((w, LANES)), full((LANES, w))],
        out_specs=tile,
        out_shape=jax.ShapeDtypeStruct((t, w), BF16),
        compiler_params=_cparams(("parallel",)),
        name="rwkv_post",
    )(y, bonus, g, lnx_g.reshape(1, w), lnx_b.reshape(1, w), sel, selt)


def _route(logits):
    lane = lax.broadcasted_iota(jnp.int32, logits.shape, 1)
    big = jnp.int32(4 * LANES)
    gmask = lane < N_GROUPS
    gl = jnp.where(gmask, logits, NEG_INF)
    gmax = jnp.max(gl, axis=1, keepdims=True)
    gidx = jnp.min(jnp.where(gl == gmax, lane, big), axis=1, keepdims=True)
    gtop = 1.0 / jnp.sum(jnp.where(gmask, jnp.exp(gl - gmax), 0.0), axis=1, keepdims=True)
    lo = N_GROUPS + EXPERTS_PER_GROUP * gidx
    emask = (lane >= lo) & (lane < lo + EXPERTS_PER_GROUP)
    el = jnp.where(emask, logits, NEG_INF)
    e1 = jnp.max(el, axis=1, keepdims=True)
    i1 = jnp.min(jnp.where(el == e1, lane, big), axis=1, keepdims=True)
    el2 = jnp.where(lane == i1, NEG_INF, el)
    e2 = jnp.max(el2, axis=1, keepdims=True)
    i2 = jnp.min(jnp.where(el2 == e2, lane, big), axis=1, keepdims=True)
    t = jnp.exp(e2 - e1)
    w1 = gtop / (1.0 + t)
    w2 = gtop * t / (1.0 + t)
    return jnp.where(lane == i1, w1, 0.0) + jnp.where(lane == i2, w2, 0.0)


def _outproj_kernel(yd_ref, yr_ref, h_ref, w1_ref, w2_ref, g_ref, b_ref, rw_hi_ref, rw_lo_ref, rb_ref,
                    h1_ref, h1b_ref, wc_ref):
    mix = (jnp.dot(yd_ref[...], w1_ref[...], preferred_element_type=F32)
           + jnp.dot(yr_ref[...], w2_ref[...], preferred_element_type=F32))
    h1 = _layer_norm(DEEPNORM_ALPHA * h_ref[...] + mix, g_ref[...], b_ref[...])
    h1_ref[...] = h1
    h1b_ref[...] = h1.astype(BF16)
    hi = h1.astype(BF16)
    lo = (h1 - hi.astype(F32)).astype(BF16)
    d = lambda a, b: jnp.dot(a, b, preferred_element_type=F32)
    logits = d(hi, rw_hi_ref[...]) + d(lo, rw_hi_ref[...]) + d(hi, rw_lo_ref[...]) + rb_ref[...]
    wc_ref[...] = _route(logits)


def outproj_ln_route(yd, yr, h, w_out_a, w_out_b, g, b, rw_hi, rw_lo, rb, tm):
    t, d = h.shape
    ka = yd.shape[1]
    kb = yr.shape[1]
    full = lambda shp: pl.BlockSpec(shp, lambda i: (0, 0))
    return pl.pallas_call(
        _outproj_kernel,
        grid=(t // tm,),
        in_specs=[
            pl.BlockSpec((tm, ka), lambda i: (i, 0)),
            pl.BlockSpec((tm, kb), lambda i: (i, 0)),
            pl.BlockSpec((tm, d), lambda i: (i, 0)),
            full((ka, d)), full((kb, d)), full((1, d)), full((1, d)),
            full((d, ROUTER_LANES)), full((d, ROUTER_LANES)), full((1, ROUTER_LANES)),
        ],
        out_specs=[
            pl.BlockSpec((tm, d), lambda i: (i, 0)),
            pl.BlockSpec((tm, d), lambda i: (i, 0)),
            pl.BlockSpec((tm, ROUTER_LANES), lambda i: (i, 0)),
        ],
        out_shape=[jax.ShapeDtypeStruct((t, d), F32), jax.ShapeDtypeStruct((t, d), BF16),
                   jax.ShapeDtypeStruct((t, ROUTER_LANES), F32)],
        compiler_params=_cparams(("parallel",)),
        name="outproj_ln_route",
    )(yd, yr, h, w_out_a, w_out_b, g.reshape(1, d), b.reshape(1, d), rw_hi, rw_lo, rb)


def _moe_dense_kernel(x_ref, wc_ref, wg_ref, wu_ref, wd_ref, o_ref):
    e = pl.program_id(1)

    @pl.when(e == 0)
    def _():
        o_ref[...] = jnp.zeros(o_ref.shape, F32)

    x = x_ref[...]
    hg = jnp.dot(x, wg_ref[0], preferred_element_type=F32)
    hu = jnp.dot(x, wu_ref[0], preferred_element_type=F32)
    wc = wc_ref[...]
    lane = lax.broadcasted_iota(jnp.int32, wc.shape, 1)
    wcol = jnp.sum(jnp.where(lane == e + N_GROUPS, wc, 0.0), axis=1, keepdims=True)
    hid = hg * _sigmoid(hg) * hu * wcol
    o_ref[...] += jnp.dot(hid.astype(BF16), wd_ref[0], preferred_element_type=F32)


def moe_dense(x, wc, wg, wu, wd, tm):
    t, d = x.shape
    ne, _, f = wg.shape
    return pl.pallas_call(
        _moe_dense_kernel,
        grid=(t // tm, ne),
        in_specs=[
            pl.BlockSpec((tm, d), lambda i, e: (i, 0)),
            pl.BlockSpec((tm, ROUTER_LANES), lambda i, e: (i, 0)),
            pl.BlockSpec((1, d, f), lambda i, e: (e, 0, 0)),
            pl.BlockSpec((1, d, f), lambda i, e: (e, 0, 0)),
            pl.BlockSpec((1, f, d), lambda i, e: (e, 0, 0)),
        ],
        out_specs=pl.BlockSpec((tm, d), lambda i, e: (i, 0)),
        out_shape=jax.ShapeDtypeStruct((t, d), F32),
        compiler_params=_cparams(("parallel", "arbitrary")),
        name="moe_dense",
    )(x, wc, wg, wu, wd)


def _final_kernel(h1_ref, ffn_ref, p_ref, g_ref, b_ref, pw_ref, pg_ref, gw_ref, o_ref):
    h2 = _layer_norm(DEEPNORM_ALPHA * h1_ref[...] + ffn_ref[...], g_ref[...], b_ref[...])
    pe = jnp.dot(p_ref[...].astype(BF16), pw_ref[...], preferred_element_type=F32)
    ple = pe * lax.rsqrt(jnp.mean(pe * pe, axis=-1, keepdims=True) + LN_EPS) * pg_ref[...]
    gate = _sigmoid(jnp.dot(h2.astype(BF16), gw_ref[...], preferred_element_type=F32))
    o_ref[...] = h2 + gate * ple


def final_block(h1, ffn, p, g, b, ple_w, ple_g, gate_w, tm):
    t, d = h1.shape
    dp = p.shape[1]
    tile = lambda n: pl.BlockSpec((tm, n), lambda i: (i, 0))
    full = lambda shp: pl.BlockSpec(shp, lambda i: (0, 0))
    return pl.pallas_call(
        _final_kernel,
        grid=(t // tm,),
        in_specs=[tile(d), tile(d), tile(dp), full((1, d)), full((1, d)), full((dp, d)), full((1, d)),
                  full((d, d))],
        out_specs=tile(d),
        out_shape=jax.ShapeDtypeStruct((t, d), F32),
        compiler_params=_cparams(("parallel",)),
        name="final_block",
    )(h1, ffn, p, g.reshape(1, d), b.reshape(1, d), ple_w, ple_g.reshape(1, d), gate_w)


def _t5_bucket(rel):
    n = jnp.maximum(rel, 0)
    max_exact = REL_BUCKETS // 2
    nf = jnp.maximum(n, 1).astype(F32)
    large = max_exact + (jnp.log(nf / max_exact) / math.log(REL_MAX_DIST / max_exact)
                         * (REL_BUCKETS - max_exact)).astype(jnp.int32)
    large = jnp.minimum(large, REL_BUCKETS - 1)
    return jnp.where(n < max_exact, n, large)


def _bias_tables(rel_bias, tq):
    rel = jnp.arange(2 * tq, dtype=jnp.int32)
    vals = rel_bias[_t5_bucket(rel)].astype(F32)
    i = jnp.arange(tq, dtype=jnp.int32)[:, None]
    j = jnp.arange(tq, dtype=jnp.int32)[None, :]
    idx = jnp.stack([jnp.maximum(i - j, 0), tq + i - j])
    tiles = jnp.moveaxis(vals[idx], -1, 0)
    far = rel_bias[_t5_bucket(jnp.full((), 2 * tq, jnp.int32))].astype(F32)
    return tiles, far


def _head_selectors():
    lane = jnp.arange(RWKV_WIDTH, dtype=jnp.int32) // RWKV_HEAD_SIZE
    sel = (lane[:, None] == jnp.arange(LANES, dtype=jnp.int32)[None, :]).astype(BF16)
    return sel, sel.T


def kernel(x, p, ln_in_g, ln_in_b, rel_bias, w_in, diff_lam_q1, diff_lam_k1, diff_lam_q2, diff_lam_k2, diff_subln_g, rwkv_mu, rwkv_w0, rwkv_w2, rwkv_a0, rwkv_a2, rwkv_g2, rwkv_k_k, rwkv_k_a, rwkv_r_k, rwkv_lnx_g, rwkv_lnx_b, w_out, ln1_g, ln1_b, router_group_w, router_group_b, router_expert_w, router_expert_b, moe_w_gate, moe_w_up, moe_w_down, ln2_g, ln2_b, ple_w, ple_norm_g, ple_gate_w):
    bsz, seq, d = x.shape
    t = bsz * seq
    i = 0
    lambda_init = 0.8 - 0.6 * math.exp(-0.3 * i)
    tq = 256

    x2 = x.reshape(t, d)
    w_in_b = w_in[i].astype(BF16)
    w_qkv = w_in_b[:, :O_RWKV]
    w_rw = jnp.pad(w_in_b[:, O_RWKV:], ((0, 0), (0, RWKV_PROJ_PAD - RWKV_PROJ)))
    qkv, h = ln_matmul(x2, ln_in_g, ln_in_b, w_qkv, BF16, 512, 512)
    pr, _ = ln_matmul(x2, ln_in_g, ln_in_b, w_rw, F32, 512, 512)

    lam = (jnp.exp(jnp.sum(diff_lam_q1[i].astype(F32) * diff_lam_k1[i]))
           - jnp.exp(jnp.sum(diff_lam_q2[i].astype(F32) * diff_lam_k2[i])) + lambda_init)
    bias_tiles, far_bias = _bias_tables(rel_bias, tq)
    y_diff = diff_attention(qkv.reshape(bsz, seq, O_RWKV), bias_tiles, far_bias, lam.reshape(1),
                            diff_subln_g[i], lambda_init, tq)

    sel, selt = _head_selectors()
    mu_p = jnp.pad(rwkv_mu[i], (0, RWKV_PROJ_PAD - RWKV_PROJ))
    zeros64 = jnp.zeros((LORA_DECAY, RWKV_WIDTH), F32)
    w2p = jnp.concatenate([rwkv_w2[i], zeros64], axis=0).astype(BF16)
    a2p = jnp.concatenate([zeros64, rwkv_a2[i]], axis=0).astype(BF16)
    g2p = jnp.pad(rwkv_g2[i], ((0, 2 * LANES - LORA_GATE), (0, 0))).astype(BF16)
    r, k2, v, lw, kn, bb, g, bonus = rwkv_prep(
        pr, mu_p, rwkv_w0[i], w2p, rwkv_a0[i], a2p, g2p, rwkv_k_k[i], rwkv_k_a[i],
        rwkv_r_k[i].reshape(-1), sel, selt, seq, 256)
    sh = lambda a: a.reshape(bsz, seq, RWKV_WIDTH)
    y_scan = rwkv_scan(sh(r), sh(k2), sh(v), sh(lw), sh(kn), sh(bb), 64, 4)
    y_rwkv = rwkv_post(y_scan.reshape(t, RWKV_WIDTH), bonus, g, rwkv_lnx_g[i], rwkv_lnx_b[i], sel, selt, 512)

    w_out_b = w_out[i].astype(BF16)
    rw = jnp.concatenate([router_group_w[i], router_expert_w[i]], axis=1)
    rw = jnp.pad(rw, ((0, 0), (0, ROUTER_LANES - rw.shape[1])))
    rw_hi = rw.astype(BF16)
    rw_lo = (rw - rw_hi.astype(F32)).astype(BF16)
    rb = jnp.pad(jnp.concatenate([router_group_b[i], router_expert_b[i]]),
                 (0, ROUTER_LANES - N_GROUPS - N_EXPERTS)).reshape(1, ROUTER_LANES)
    h1, h1b, wc = outproj_ln_route(y_diff.reshape(t, DIFF_WIDTH), y_rwkv, h, w_out_b[:DIFF_WIDTH],
                                   w_out_b[DIFF_WIDTH:], ln1_g[i], ln1_b[i], rw_hi, rw_lo, rb, 256)

    f = moe_w_gate.shape[-1]
    wg = moe_w_gate[i].reshape(N_EXPERTS, d, f).astype(BF16)
    wu = moe_w_up[i].reshape(N_EXPERTS, d, f).astype(BF16)
    wd = moe_w_down[i].reshape(N_EXPERTS, f, d).astype(BF16)
    ffn = moe_dense(h1b, wc, wg, wu, wd, 1024)

    out = final_block(h1, ffn, p[i].reshape(t, -1), ln2_g[i], ln2_b[i], ple_w[i].astype(BF16),
                      ple_norm_g[i], ple_gate_w[i].astype(BF16), 256)
    return out.reshape(bsz, seq, d)
```

```python
import functools
import math

import jax
import jax.numpy as jnp
from jax import lax
from jax.experimental import pallas as pl
from jax.experimental.pallas import tpu as pltpu

F32 = jnp.float32
BF16 = jnp.bfloat16

DIFF_HEADS = 8
DIFF_HEAD_DIM = 64
DIFF_V_DIM = 128
DIFF_WIDTH = DIFF_HEADS * DIFF_V_DIM
DIFF_QK_WIDTH = DIFF_HEADS * 2 * DIFF_HEAD_DIM
RWKV_HEADS = 16
RWKV_HEAD_SIZE = 64
RWKV_WIDTH = RWKV_HEADS * RWKV_HEAD_SIZE
LORA_DECAY = 64
LORA_AAA = 64
LORA_GATE = 160
RWKV_PROJ = 3 * RWKV_WIDTH + LORA_DECAY + LORA_AAA + LORA_GATE
O_RWKV = 2 * DIFF_QK_WIDTH + DIFF_WIDTH
REL_BUCKETS = 32
REL_MAX_DIST = 128
N_GROUPS = 4
EXPERTS_PER_GROUP = 8
N_EXPERTS = N_GROUPS * EXPERTS_PER_GROUP
LN_EPS = 1e-5
RWKV_GN_EPS = 64e-5
NEG_INF = -1e30
DEPTH = 1
DEEPNORM_ALPHA = (2 * DEPTH) ** 0.25

LANES = 128
RWKV_PROJ_PAD = 3584
ROUTER_LANES = LANES
VMEM_LIMIT = 48 * 1024 * 1024


def _cparams(sem):
    return pltpu.CompilerParams(dimension_semantics=sem, vmem_limit_bytes=VMEM_LIMIT)


def _bdot(a, b):
    return jnp.dot(a.astype(BF16), b.astype(BF16), preferred_element_type=F32)


def _bdot_nt(a, b):
    return lax.dot_general(a.astype(BF16), b.astype(BF16), (((1,), (1,)), ((), ())),
                           preferred_element_type=F32)


def _bdot_tn(a, b):
    return lax.dot_general(a.astype(BF16), b.astype(BF16), (((0,), (0,)), ((), ())),
                           preferred_element_type=F32)


def _split3(x):
    hi = x.astype(BF16)
    r1 = x - hi.astype(F32)
    mid = r1.astype(BF16)
    lo = (r1 - mid.astype(F32)).astype(BF16)
    return hi, mid, lo


def _dot_x_sel(x, sel):
    hi, mid, lo = _split3(x)
    d = lambda p: jnp.dot(p, sel, preferred_element_type=F32)
    return d(hi) + d(mid) + d(lo)


def _dot_sel_x(sel, x):
    hi, mid, lo = _split3(x)
    d = lambda p: jnp.dot(sel, p, preferred_element_type=F32)
    return d(hi) + d(mid) + d(lo)


def _layer_norm(x, g, b):
    mu = jnp.mean(x, axis=-1, keepdims=True)
    xc = x - mu
    var = jnp.mean(xc * xc, axis=-1, keepdims=True)
    return xc * lax.rsqrt(var + LN_EPS) * g + b


def _sigmoid(z):
    return 1.0 / (1.0 + jnp.exp(-z))


def _ln_mm_kernel(x_ref, g_ref, b_ref, w_ref, o_ref, h_ref, xn_scr):
    @pl.when(pl.program_id(1) == 0)
    def _():
        xn = _layer_norm(x_ref[...], g_ref[...], b_ref[...])
        h_ref[...] = xn
        xn_scr[...] = xn.astype(BF16)

    o_ref[...] = jnp.dot(xn_scr[...], w_ref[...], preferred_element_type=F32).astype(o_ref.dtype)


def ln_matmul(x, g, b, w, out_dtype, tm, tn):
    m, d = x.shape
    n = w.shape[1]
    return pl.pallas_call(
        _ln_mm_kernel,
        grid=(m // tm, n // tn),
        in_specs=[
            pl.BlockSpec((tm, d), lambda i, j: (i, 0)),
            pl.BlockSpec((1, d), lambda i, j: (0, 0)),
            pl.BlockSpec((1, d), lambda i, j: (0, 0)),
            pl.BlockSpec((d, tn), lambda i, j: (0, j)),
        ],
        out_specs=[
            pl.BlockSpec((tm, tn), lambda i, j: (i, j)),
            pl.BlockSpec((tm, d), lambda i, j: (i, 0)),
        ],
        out_shape=[jax.ShapeDtypeStruct((m, n), out_dtype),
                   jax.ShapeDtypeStruct((m, d), F32)],
        scratch_shapes=[pltpu.VMEM((tm, d), BF16)],
        compiler_params=_cparams(("parallel", "arbitrary")),
        name="ln_matmul",
    )(x, g.reshape(1, d), b.reshape(1, d), w)


def _attn_kernel(lam_ref, far_ref, q_ref, k_ref, v_ref, bias_ref, g_ref, o_ref,
                 m_scr, l_scr, acc_scr, *, tq, hps, scale, post_scale):
    hp = pl.program_id(1)
    qi = pl.program_id(2)
    lane = lax.broadcasted_iota(jnp.int32, (tq, LANES), 1)
    q_maps = []
    for hh in range(hps):
        qs = (q_ref[0, :, hh * LANES:(hh + 1) * LANES].astype(F32) * scale).astype(BF16)
        zero = jnp.zeros_like(qs)
        q_maps += [jnp.where(lane < DIFF_HEAD_DIM, qs, zero), jnp.where(lane >= DIFF_HEAD_DIM, qs, zero)]

    m_scr[...] = jnp.full(m_scr.shape, NEG_INF, F32)
    l_scr[...] = jnp.zeros(l_scr.shape, F32)
    acc_scr[...] = jnp.zeros(acc_scr.shape, F32)

    def step(kstart, bias_index, causal):
        for hh in range(hps):
            kb = k_ref[0, pl.ds(kstart, tq), hh * LANES:(hh + 1) * LANES]
            vb = v_ref[0, pl.ds(kstart, tq), hh * LANES:(hh + 1) * LANES]
            for c in range(2 * hh, 2 * hh + 2):
                s = lax.dot_general(q_maps[c], kb, (((1,), (1,)), ((), ())), preferred_element_type=F32)
                m_old = m_scr[c]
                if bias_index is None:
                    far = far_ref[hp * hps + hh]
                    m_new = jnp.maximum(m_old, jnp.max(s, axis=1, keepdims=True) + far)
                    shift = m_new - far
                else:
                    s = s + bias_ref[hh, bias_index]
                    if causal:
                        ri = lax.broadcasted_iota(jnp.int32, (tq, tq), 0)
                        ci = lax.broadcasted_iota(jnp.int32, (tq, tq), 1)
                        s = jnp.where(ri >= ci, s, NEG_INF)
                    m_new = jnp.maximum(m_old, jnp.max(s, axis=1, keepdims=True))
                    shift = m_new
                p = jnp.exp(s - jnp.concatenate([shift] * (tq // LANES), axis=1))
                corr = jnp.exp(m_old - m_new)
                l_scr[c] = corr * l_scr[c] + jnp.sum(p, axis=1, keepdims=True)
                acc_scr[c] = corr * acc_scr[c] + jnp.dot(p.astype(BF16), vb, preferred_element_type=F32)
                m_scr[c] = m_new

    def far_body(kb, carry):
        step(pl.multiple_of(kb * tq, tq), None, False)
        return carry

    lax.fori_loop(0, jnp.maximum(qi - 1, 0), far_body, 0)

    @pl.when(qi >= 1)
    def _():
        step(pl.multiple_of((qi - 1) * tq, tq), 1, False)

    step(pl.multiple_of(qi * tq, tq), 0, True)

    for hh in range(hps):
        c = 2 * hh
        o = acc_scr[c] / l_scr[c] - lam_ref[0] * (acc_scr[c + 1] / l_scr[c + 1])
        ms = jnp.mean(o * o, axis=1, keepdims=True)
        o_ref[0, :, hh * LANES:(hh + 1) * LANES] = (
            o * lax.rsqrt(ms + LN_EPS) * g_ref[...] * post_scale).astype(o_ref.dtype)


def diff_attention(qkv, bias_tiles, far_bias, lam, subln_g, lambda_init, tq, hps):
    bsz, s, _ = qkv.shape
    kern = functools.partial(_attn_kernel, tq=tq, hps=hps, scale=DIFF_HEAD_DIM ** -0.5,
                             post_scale=1.0 - lambda_init)
    hw = hps * LANES
    nqk = DIFF_QK_WIDTH // hw
    return pl.pallas_call(
        kern,
        grid=(bsz, DIFF_HEADS // hps, s // tq),
        in_specs=[
            pl.BlockSpec(memory_space=pltpu.SMEM),
            pl.BlockSpec(memory_space=pltpu.SMEM),
            pl.BlockSpec((1, tq, hw), lambda b, h, i: (b, i, h)),
            pl.BlockSpec((1, s, hw), lambda b, h, i: (b, 0, nqk + h)),
            pl.BlockSpec((1, s, hw), lambda b, h, i: (b, 0, 2 * nqk + h)),
            pl.BlockSpec((hps, 2, tq, tq), lambda b, h, i: (h, 0, 0, 0)),
            pl.BlockSpec((1, LANES), lambda b, h, i: (0, 0)),
        ],
        out_specs=pl.BlockSpec((1, tq, hw), lambda b, h, i: (b, i, h)),
        out_shape=jax.ShapeDtypeStruct((bsz, s, DIFF_WIDTH), BF16),
        scratch_shapes=[pltpu.VMEM((2 * hps, tq, LANES), F32), pltpu.VMEM((2 * hps, tq, LANES), F32),
                        pltpu.VMEM((2 * hps, tq, LANES), F32)],
        compiler_params=_cparams(("parallel", "parallel", "arbitrary")),
        name="diff_attention",
    )(lam, far_bias, qkv, qkv, qkv, bias_tiles, subln_g.reshape(1, LANES))


def _rwkv_prep_kernel(x_ref, prev_ref, mu_ref, w0_ref, w2_ref, a0_ref, a2_ref, g2_ref,
                      kk_ref, ka_ref, rk_ref, sel_ref, selt_ref,
                      r_out, k_out, v_out, lw_out, kn_out, b_out, g_out, bonus_out, *, tm, seq):
    i = pl.program_id(0)
    x = x_ref[...]
    w = RWKV_WIDTH
    row = lax.broadcasted_iota(jnp.int32, x.shape, 0)
    is_start = (i * tm) % seq == 0
    last_prev = jnp.where(is_start, 0.0, prev_ref[7:8, :])
    prev = jnp.where(row == 0, last_prev, pltpu.roll(x, 1, 0))
    xs = x + (prev - x) * mu_ref[...]

    r = xs[:, 0:w]
    k = xs[:, w:2 * w]
    v = xs[:, 2 * w:3 * w]
    xwa = xs[:, 3 * w:3 * w + 2 * LORA_DECAY]
    xg = xs[:, 3 * w + LANES:3 * w + LANES + 2 * LANES]

    z = w0_ref[...] + _bdot(jnp.tanh(xwa), w2_ref[...])
    softplus_neg_z = jnp.maximum(-z, 0.0) + jnp.log(1.0 + jnp.exp(-jnp.abs(z)))
    lw_out[...] = -jnp.exp(-softplus_neg_z - 0.5)
    a_lr = _sigmoid(a0_ref[...] + _bdot(xwa, a2_ref[...]))
    g_out[...] = _bdot(_sigmoid(xg), g2_ref[...])

    sel = sel_ref[...]
    selt = selt_ref[...]
    kk = k * kk_ref[...]
    ss = _dot_x_sel(kk * kk, sel)
    inv = 1.0 / jnp.maximum(jnp.sqrt(ss), 1e-12)
    kn = kk * _dot_x_sel(inv, selt)
    k2 = k * (1.0 + (a_lr - 1.0) * ka_ref[...])
    bs = _dot_x_sel(r * k2 * rk_ref[...], sel)
    r_out[...] = r
    k_out[...] = k2
    v_out[...] = v
    kn_out[...] = kn
    b_out[...] = kn * a_lr
    bonus_out[...] = _dot_x_sel(bs, selt) * v


def rwkv_prep(pr, mu, w0, w2p, a0, a2p, g2p, k_k, k_a, r_k, sel, selt, seq, tm):
    t, wp = pr.shape
    w = RWKV_WIDTH
    row = lambda a: a.reshape(1, -1)
    full = lambda shp: pl.BlockSpec(shp, lambda i: (0, 0))
    outs = [jax.ShapeDtypeStruct((t, w), F32)] * 8
    kern = functools.partial(_rwkv_prep_kernel, tm=tm, seq=seq)
    return pl.pallas_call(
        kern,
        grid=(t // tm,),
        in_specs=[
            pl.BlockSpec((tm, wp), lambda i: (i, 0)),
            pl.BlockSpec((8, wp), lambda i: (jnp.maximum(i * (tm // 8) - 1, 0), 0)),
            full((1, wp)), full((1, w)), full((LANES, w)), full((1, w)), full((LANES, w)),
            full((2 * LANES, w)), full((1, w)), full((1, w)), full((1, w)),
            full((w, LANES)), full((LANES, w)),
        ],
        out_specs=[pl.BlockSpec((tm, w), lambda i: (i, 0))] * 8,
        out_shape=outs,
        compiler_params=_cparams(("parallel",)),
        name="rwkv_prep",
    )(pr, pr, row(mu), row(w0), w2p, row(a0), a2p, g2p, row(k_k), row(k_a), row(r_k), sel, selt)


def _rwkv_scan_kernel(r_ref, k_ref, v_ref, lw_ref, kn_ref, b_ref, y_ref, s_scr, *, chunk, pairs):
    c2 = 2 * chunk

    @pl.when(pl.program_id(2) == 0)
    def _():
        s_scr[...] = jnp.zeros(s_scr.shape, F32)

    ri = lax.broadcasted_iota(jnp.int32, (chunk, chunk), 0)
    ci = lax.broadcasted_iota(jnp.int32, (chunk, chunk), 1)
    ltri = jnp.where(ri >= ci, 1.0, 0.0).astype(BF16)

    lw = lw_ref[0]
    cl = _dot_sel_x(ltri, lw)
    mid = chunk // 2 - 1
    clm = cl[mid:mid + 1, :]
    cle = cl[chunk - 1:chunk, :]
    w_mid = jnp.exp(cl - clm)
    w_mid_prev = jnp.exp(cl - lw - clm)
    w_mid_inv = jnp.exp(clm - cl)
    w_abs = jnp.exp(cl)
    w_abs_prev = jnp.exp(cl - lw)
    w_end = jnp.exp(cle - cl)
    w_tot = jnp.exp(cle)

    r = r_ref[0]
    k = k_ref[0]
    v = v_ref[0]
    a = -kn_ref[0]
    b = b_ref[0]
    a_mid = a * w_mid_prev
    r_mid = r * w_mid
    b_mid = b * w_mid_inv
    k_mid = k * w_mid_inv
    a_abs = a * w_abs_prev
    r_abs = r * w_abs
    b_end = b * w_end
    k_end = k * w_end

    lane = lax.broadcasted_iota(jnp.int32, (chunk, LANES), 1)
    first = lane < RWKV_HEAD_SIZE

    def stack(x):
        return jnp.concatenate([jnp.where(first, x, 0.0), jnp.where(first, 0.0, x)], axis=0)

    rr = lax.broadcasted_iota(jnp.int32, (c2, c2), 0)
    cc = lax.broadcasted_iota(jnp.int32, (c2, c2), 1)
    same = (rr >= chunk) == (cc >= chunk)
    strict = same & (rr > cc)
    incl = same & (rr >= cc)
    eye = jnp.where(rr == cc, 1.0, 0.0)

    for p in range(pairs):
        sl = slice(p * LANES, (p + 1) * LANES)
        a_m, r_m, b_m, k_m = stack(a_mid[:, sl]), stack(r_mid[:, sl]), stack(b_mid[:, sl]), stack(k_mid[:, sl])
        a_0, r_0 = stack(a_abs[:, sl]), stack(r_abs[:, sl])
        b_e, k_e, v_s = stack(b_end[:, sl]), stack(k_end[:, sl]), stack(v[:, sl])

        l_ab = jnp.where(strict, _bdot_nt(a_m, b_m), 0.0)
        l_ak = jnp.where(strict, _bdot_nt(a_m, k_m), 0.0)
        m_rb = jnp.where(incl, _bdot_nt(r_m, b_m), 0.0)
        m_rk = jnp.where(incl, _bdot_nt(r_m, k_m), 0.0)

        t_inv = eye + l_ab
        lp = l_ab
        for _ in range(max(chunk.bit_length() - 2, 0)):
            lp = _bdot(lp, lp)
            t_inv = t_inv + _bdot(lp, t_inv)

        a_p = _bdot(t_inv, a_0)
        v_p = _bdot(t_inv, _bdot(l_ak, v_s))
        r_p = r_0 + _bdot(m_rb, a_p)
        y_p = _bdot(m_rb, v_p) + _bdot(m_rk, v_s)

        s_old = s_scr[p]
        y_s = _bdot_nt(r_p, s_old) + y_p
        y_ref[0, :, sl] = y_s[:chunk] + y_s[chunk:]
        s_scr[p] = (s_old * w_tot[:, sl] + _bdot(s_old, _bdot_tn(a_p, b_e))
                    + _bdot_tn(v_p, b_e) + _bdot_tn(v_s, k_e))


def rwkv_scan(r, k, v, lw, kn, b, chunk, pairs):
    bsz, s, w = r.shape
    pw = pairs * LANES
    spec = pl.BlockSpec((1, chunk, pw), lambda bi, hi, ci: (bi, ci, hi))
    kern = functools.partial(_rwkv_scan_kernel, chunk=chunk, pairs=pairs)
    return pl.pallas_call(
        kern,
        grid=(bsz, w // pw, s // chunk),
        in_specs=[spec] * 6,
        out_specs=spec,
        out_shape=jax.ShapeDtypeStruct((bsz, s, w), F32),
        scratch_shapes=[pltpu.VMEM((pairs, LANES, LANES), F32)],
        compiler_params=_cparams(("parallel", "parallel", "arbitrary")),
        name="rwkv_scan",
    )(r, k, v, lw, kn, b)


def _rwkv_post_kernel(y_ref, bonus_ref, g_ref, lg_ref, lb_ref, sel_ref, selt_ref, o_ref):
    y = y_ref[...]
    sel = sel_ref[...]
    selt = selt_ref[...]
    inv_n = 1.0 / RWKV_HEAD_SIZE
    mu = _dot_x_sel(_dot_x_sel(y, sel) * inv_n, selt)
    yc = y - mu
    var = _dot_x_sel(_dot_x_sel(yc * yc, sel) * inv_n, selt)
    yn = yc * lax.rsqrt(var + RWKV_GN_EPS) * lg_ref[...] + lb_ref[...]
    o_ref[...] = ((yn + bonus_ref[...]) * g_ref[...]).astype(o_ref.dtype)


def rwkv_post(y, bonus, g, lnx_g, lnx_b, sel, selt, tm):
    t, w = y.shape
    tile = pl.BlockSpec((tm, w), lambda i: (i, 0))
    full = lambda shp: pl.BlockSpec(shp, lambda i: (0, 0))
    return pl.pallas_call(
        _rwkv_post_kernel,
        grid=(t // tm,),
        in_specs=[tile, tile, tile, full((1, w)), full((1, w)), full((w, LANES)), full((LANES, w))],
        out_specs=tile,
        out_shape=jax.ShapeDtypeStruct((t, w), BF16),
        compiler_params=_cparams(("parallel",)),
        name="rwkv_post",
    )(y, bonus, g, lnx_g.reshape(1, w), lnx_b.reshape(1, w), sel, selt)


def _route(logits):
    lane = lax.broadcasted_iota(jnp.int32, logits.shape, 1)
    big = jnp.int32(4 * LANES)
    gmask = lane < N_GROUPS
    gl = jnp.where(gmask, logits, NEG_INF)
    gmax = jnp.max(gl, axis=1, keepdims=True)
    gidx = jnp.min(jnp.where(gl == gmax, lane, big), axis=1, keepdims=True)
    gtop = 1.0 / jnp.sum(jnp.where(gmask, jnp.exp(gl - gmax), 0.0), axis=1, keepdims=True)
    lo = N_GROUPS + EXPERTS_PER_GROUP * gidx
    emask = (lane >= lo) & (lane < lo + EXPERTS_PER_GROUP)
    el = jnp.where(emask, logits, NEG_INF)
    e1 = jnp.max(el, axis=1, keepdims=True)
    i1 = jnp.min(jnp.where(el == e1, lane, big), axis=1, keepdims=True)
    el2 = jnp.where(lane == i1, NEG_INF, el)
    e2 = jnp.max(el2, axis=1, keepdims=True)
    i2 = jnp.min(jnp.where(el2 == e2, lane, big), axis=1, keepdims=True)
    t = jnp.exp(e2 - e1)
    w1 = gtop / (1.0 + t)
    w2 = gtop * t / (1.0 + t)
    return jnp.where(lane == i1, w1, 0.0) + jnp.where(lane == i2, w2, 0.0)


def _outproj_kernel(yd_ref, yr_ref, h_ref, w1_ref, w2_ref, g_ref, b_ref, rw_hi_ref, rw_lo_ref, rb_ref,
                    h1_ref, h1b_ref, wc_ref):
    mix = (jnp.dot(yd_ref[...], w1_ref[...], preferred_element_type=F32)
           + jnp.dot(yr_ref[...], w2_ref[...], preferred_element_type=F32))
    h1 = _layer_norm(DEEPNORM_ALPHA * h_ref[...] + mix, g_ref[...], b_ref[...])
    h1_ref[...] = h1
    h1b_ref[...] = h1.astype(BF16)
    hi = h1.astype(BF16)
    lo = (h1 - hi.astype(F32)).astype(BF16)
    d = lambda a, b: jnp.dot(a, b, preferred_element_type=F32)
    logits = d(hi, rw_hi_ref[...]) + d(lo, rw_hi_ref[...]) + d(hi, rw_lo_ref[...]) + rb_ref[...]
    wc_ref[...] = _route(logits)


def outproj_ln_route(yd, yr, h, w_out_a, w_out_b, g, b, rw_hi, rw_lo, rb, tm):
    t, d = h.shape
    ka = yd.shape[1]
    kb = yr.shape[1]
    full = lambda shp: pl.BlockSpec(shp, lambda i: (0, 0))
    return pl.pallas_call(
        _outproj_kernel,
        grid=(t // tm,),
        in_specs=[
            pl.BlockSpec((tm, ka), lambda i: (i, 0)),
            pl.BlockSpec((tm, kb), lambda i: (i, 0)),
            pl.BlockSpec((tm, d), lambda i: (i, 0)),
            full((ka, d)), full((kb, d)), full((1, d)), full((1, d)),
            full((d, ROUTER_LANES)), full((d, ROUTER_LANES)), full((1, ROUTER_LANES)),
        ],
        out_specs=[
            pl.BlockSpec((tm, d), lambda i: (i, 0)),
            pl.BlockSpec((tm, d), lambda i: (i, 0)),
            pl.BlockSpec((tm, ROUTER_LANES), lambda i: (i, 0)),
        ],
        out_shape=[jax.ShapeDtypeStruct((t, d), F32), jax.ShapeDtypeStruct((t, d), BF16),
                   jax.ShapeDtypeStruct((t, ROUTER_LANES), F32)],
        compiler_params=_cparams(("parallel",)),
        name="outproj_ln_route",
    )(yd, yr, h, w_out_a, w_out_b, g.reshape(1, d), b.reshape(1, d), rw_hi, rw_lo, rb)


def _moe_dense_kernel(x_ref, wc_ref, wg_ref, wu_ref, wd_ref, o_ref):
    e = pl.program_id(1)

    @pl.when(e == 0)
    def _():
        o_ref[...] = jnp.zeros(o_ref.shape, F32)

    x = x_ref[...]
    hg = jnp.dot(x, wg_ref[0], preferred_element_type=F32)
    hu = jnp.dot(x, wu_ref[0], preferred_element_type=F32)
    wc = wc_ref[...]
    lane = lax.broadcasted_iota(jnp.int32, wc.shape, 1)
    wcol = jnp.sum(jnp.where(lane == e + N_GROUPS, wc, 0.0), axis=1, keepdims=True)
    hid = hg * _sigmoid(hg) * hu * wcol
    o_ref[...] += jnp.dot(hid.astype(BF16), wd_ref[0], preferred_element_type=F32)


def moe_dense(x, wc, wg, wu, wd, tm):
    t, d = x.shape
    ne, _, f = wg.shape
    return pl.pallas_call(
        _moe_dense_kernel,
        grid=(t // tm, ne),
        in_specs=[
            pl.BlockSpec((tm, d), lambda i, e: (i, 0)),
            pl.BlockSpec((tm, ROUTER_LANES), lambda i, e: (i, 0)),
            pl.BlockSpec((1, d, f), lambda i, e: (e, 0, 0)),
            pl.BlockSpec((1, d, f), lambda i, e: (e, 0, 0)),
            pl.BlockSpec((1, f, d), lambda i, e: (e, 0, 0)),
        ],
        out_specs=pl.BlockSpec((tm, d), lambda i, e: (i, 0)),
        out_shape=jax.ShapeDtypeStruct((t, d), F32),
        compiler_params=_cparams(("parallel", "arbitrary")),
        name="moe_dense",
    )(x, wc, wg, wu, wd)


def _final_kernel(h1_ref, ffn_ref, p_ref, g_ref, b_ref, pw_ref, pg_ref, gw_ref, o_ref):
    h2 = _layer_norm(DEEPNORM_ALPHA * h1_ref[...] + ffn_ref[...], g_ref[...], b_ref[...])
    pe = jnp.dot(p_ref[...].astype(BF16), pw_ref[...], preferred_element_type=F32)
    ple = pe * lax.rsqrt(jnp.mean(pe * pe, axis=-1, keepdims=True) + LN_EPS) * pg_ref[...]
    gate = _sigmoid(jnp.dot(h2.astype(BF16), gw_ref[...], preferred_element_type=F32))
    o_ref[...] = h2 + gate * ple


def final_block(h1, ffn, p, g, b, ple_w, ple_g, gate_w, tm):
    t, d = h1.shape
    dp = p.shape[1]
    tile = lambda n: pl.BlockSpec((tm, n), lambda i: (i, 0))
    full = lambda shp: pl.BlockSpec(shp, lambda i: (0, 0))
    return pl.pallas_call(
        _final_kernel,
        grid=(t // tm,),
        in_specs=[tile(d), tile(d), tile(dp), full((1, d)), full((1, d)), full((dp, d)), full((1, d)),
                  full((d, d))],
        out_specs=tile(d),
        out_shape=jax.ShapeDtypeStruct((t, d), F32),
        compiler_params=_cparams(("parallel",)),
        name="final_block",
    )(h1, ffn, p, g.reshape(1, d), b.reshape(1, d), ple_w, ple_g.reshape(1, d), gate_w)


def _t5_bucket(rel):
    n = jnp.maximum(rel, 0)
    max_exact = REL_BUCKETS // 2
    nf = jnp.maximum(n, 1).astype(F32)
    large = max_exact + (jnp.log(nf / max_exact) / math.log(REL_MAX_DIST / max_exact)
                         * (REL_BUCKETS - max_exact)).astype(jnp.int32)
    large = jnp.minimum(large, REL_BUCKETS - 1)
    return jnp.where(n < max_exact, n, large)


def _bias_tables(rel_bias, tq):
    n = 3 * tq
    rel = 2 * tq - 1 - jnp.arange(n, dtype=jnp.int32)
    g = rel_bias[_t5_bucket(rel)].astype(F32).T
    skew = jnp.tile(g, (1, tq))[:, :tq * (n - 1)].reshape(-1, tq, n - 1)
    tiles = jnp.stack([skew[:, :, 2 * tq - 1:3 * tq - 1], skew[:, :, tq - 1:2 * tq - 1]], axis=1)
    far = rel_bias[_t5_bucket(jnp.full((), 2 * tq, jnp.int32))].astype(F32)
    return tiles, far


def _head_selectors():
    lane = jnp.arange(RWKV_WIDTH, dtype=jnp.int32) // RWKV_HEAD_SIZE
    sel = (lane[:, None] == jnp.arange(LANES, dtype=jnp.int32)[None, :]).astype(BF16)
    return sel, sel.T


def kernel(x, p, ln_in_g, ln_in_b, rel_bias, w_in, diff_lam_q1, diff_lam_k1, diff_lam_q2, diff_lam_k2, diff_subln_g, rwkv_mu, rwkv_w0, rwkv_w2, rwkv_a0, rwkv_a2, rwkv_g2, rwkv_k_k, rwkv_k_a, rwkv_r_k, rwkv_lnx_g, rwkv_lnx_b, w_out, ln1_g, ln1_b, router_group_w, router_group_b, router_expert_w, router_expert_b, moe_w_gate, moe_w_up, moe_w_down, ln2_g, ln2_b, ple_w, ple_norm_g, ple_gate_w):
    bsz, seq, d = x.shape
    t = bsz * seq
    i = 0
    lambda_init = 0.8 - 0.6 * math.exp(-0.3 * i)
    tq = 256

    x2 = x.reshape(t, d)
    w_in_b = w_in[i].astype(BF16)
    w_qkv = w_in_b[:, :O_RWKV]
    w_rw = jnp.pad(w_in_b[:, O_RWKV:], ((0, 0), (0, RWKV_PROJ_PAD - RWKV_PROJ)))
    qkv, h = ln_matmul(x2, ln_in_g, ln_in_b, w_qkv, BF16, 512, 512)
    pr, _ = ln_matmul(x2, ln_in_g, ln_in_b, w_rw, F32, 512, 512)

    lam = (jnp.exp(jnp.sum(diff_lam_q1[i].astype(F32) * diff_lam_k1[i]))
           - jnp.exp(jnp.sum(diff_lam_q2[i].astype(F32) * diff_lam_k2[i])) + lambda_init)
    bias_tiles, far_bias = _bias_tables(rel_bias, tq)
    y_diff = diff_attention(qkv.reshape(bsz, seq, O_RWKV), bias_tiles, far_bias, lam.reshape(1),
                            diff_subln_g[i], lambda_init, tq, 8)

    sel, selt = _head_selectors()
    mu_p = jnp.pad(rwkv_mu[i], (0, RWKV_PROJ_PAD - RWKV_PROJ))
    zeros64 = jnp.zeros((LORA_DECAY, RWKV_WIDTH), F32)
    w2p = jnp.concatenate([rwkv_w2[i], zeros64], axis=0).astype(BF16)
    a2p = jnp.concatenate([zeros64, rwkv_a2[i]], axis=0).astype(BF16)
    g2p = jnp.pad(rwkv_g2[i], ((0, 2 * LANES - LORA_GATE), (0, 0))).astype(BF16)
    r, k2, v, lw, kn, bb, g, bonus = rwkv_prep(
        pr, mu_p, rwkv_w0[i], w2p, rwkv_a0[i], a2p, g2p, rwkv_k_k[i], rwkv_k_a[i],
        rwkv_r_k[i].reshape(-1), sel, selt, seq, 256)
    sh = lambda a: a.reshape(bsz, seq, RWKV_WIDTH)
    y_scan = rwkv_scan(sh(r), sh(k2), sh(v), sh(lw), sh(kn), sh(bb), 64, 4)
    y_rwkv = rwkv_post(y_scan.reshape(t, RWKV_WIDTH), bonus, g, rwkv_lnx_g[i], rwkv_lnx_b[i], sel, selt, 512)

    w_out_b = w_out[i].astype(BF16)
    rw = jnp.concatenate([router_group_w[i], router_expert_w[i]], axis=1)
    rw = jnp.pad(rw, ((0, 0), (0, ROUTER_LANES - rw.shape[1])))
    rw_hi = rw.astype(BF16)
    rw_lo = (rw - rw_hi.astype(F32)).astype(BF16)
    rb = jnp.pad(jnp.concatenate([router_group_b[i], router_expert_b[i]]),
                 (0, ROUTER_LANES - N_GROUPS - N_EXPERTS)).reshape(1, ROUTER_LANES)
    h1, h1b, wc = outproj_ln_route(y_diff.reshape(t, DIFF_WIDTH), y_rwkv, h, w_out_b[:DIFF_WIDTH],
                                   w_out_b[DIFF_WIDTH:], ln1_g[i], ln1_b[i], rw_hi, rw_lo, rb, 256)

    f = moe_w_gate.shape[-1]
    wg = moe_w_gate[i].reshape(N_EXPERTS, d, f).astype(BF16)
    wu = moe_w_up[i].reshape(N_EXPERTS, d, f).astype(BF16)
    wd = moe_w_down[i].reshape(N_EXPERTS, f, d).astype(BF16)
    ffn = moe_dense(h1b, wc, wg, wu, wd, 1024)

    out = final_block(h1, ffn, p[i].reshape(t, -1), ln2_g[i], ln2_b[i], ple_w[i].astype(BF16),
                      ple_norm_g[i], ple_gate_w[i].astype(BF16), 256)
    return out.reshape(bsz, seq, d)
```

```python
import functools
import math

import jax
import jax.numpy as jnp
from jax import lax
from jax.experimental import pallas as pl
from jax.experimental.pallas import tpu as pltpu

F32 = jnp.float32
BF16 = jnp.bfloat16

DIFF_HEADS = 8
DIFF_HEAD_DIM = 64
DIFF_V_DIM = 128
DIFF_WIDTH = DIFF_HEADS * DIFF_V_DIM
DIFF_QK_WIDTH = DIFF_HEADS * 2 * DIFF_HEAD_DIM
RWKV_HEADS = 16
RWKV_HEAD_SIZE = 64
RWKV_WIDTH = RWKV_HEADS * RWKV_HEAD_SIZE
LORA_DECAY = 64
LORA_AAA = 64
LORA_GATE = 160
RWKV_PROJ = 3 * RWKV_WIDTH + LORA_DECAY + LORA_AAA + LORA_GATE
O_RWKV = 2 * DIFF_QK_WIDTH + DIFF_WIDTH
REL_BUCKETS = 32
REL_MAX_DIST = 128
N_GROUPS = 4
EXPERTS_PER_GROUP = 8
N_EXPERTS = N_GROUPS * EXPERTS_PER_GROUP
LN_EPS = 1e-5
RWKV_GN_EPS = 64e-5
NEG_INF = -1e30
DEPTH = 1
DEEPNORM_ALPHA = (2 * DEPTH) ** 0.25

LANES = 128
RWKV_PROJ_PAD = 3584
ROUTER_LANES = LANES
VMEM_LIMIT = 48 * 1024 * 1024
MOE_VMEM_LIMIT = 56 * 1024 * 1024
MOE_ROW_TILE = 256


def _cparams(sem):
    return pltpu.CompilerParams(dimension_semantics=sem, vmem_limit_bytes=VMEM_LIMIT)


def _bdot(a, b):
    return jnp.dot(a.astype(BF16), b.astype(BF16), preferred_element_type=F32)


def _bdot_nt(a, b):
    return lax.dot_general(a.astype(BF16), b.astype(BF16), (((1,), (1,)), ((), ())),
                           preferred_element_type=F32)


def _bdot_tn(a, b):
    return lax.dot_general(a.astype(BF16), b.astype(BF16), (((0,), (0,)), ((), ())),
                           preferred_element_type=F32)


def _split3(x):
    hi = x.astype(BF16)
    r1 = x - hi.astype(F32)
    mid = r1.astype(BF16)
    lo = (r1 - mid.astype(F32)).astype(BF16)
    return hi, mid, lo


def _dot_x_sel(x, sel):
    hi, mid, lo = _split3(x)
    d = lambda p: jnp.dot(p, sel, preferred_element_type=F32)
    return d(hi) + d(mid) + d(lo)


def _dot_sel_x(sel, x):
    hi, mid, lo = _split3(x)
    d = lambda p: jnp.dot(sel, p, preferred_element_type=F32)
    return d(hi) + d(mid) + d(lo)


def _layer_norm(x, g, b):
    mu = jnp.mean(x, axis=-1, keepdims=True)
    xc = x - mu
    var = jnp.mean(xc * xc, axis=-1, keepdims=True)
    return xc * lax.rsqrt(var + LN_EPS) * g + b


def _sigmoid(z):
    return 1.0 / (1.0 + jnp.exp(-z))


def _ln_mm_kernel(x_ref, g_ref, b_ref, w_ref, o_ref, h_ref, xn_scr):
    @pl.when(pl.program_id(1) == 0)
    def _():
        xn = _layer_norm(x_ref[...], g_ref[...], b_ref[...])
        h_ref[...] = xn
        xn_scr[...] = xn.astype(BF16)

    o_ref[...] = jnp.dot(xn_scr[...], w_ref[...], preferred_element_type=F32).astype(o_ref.dtype)


def ln_matmul(x, g, b, w, out_dtype, tm, tn):
    m, d = x.shape
    n = w.shape[1]
    return pl.pallas_call(
        _ln_mm_kernel,
        grid=(m // tm, n // tn),
        in_specs=[
            pl.BlockSpec((tm, d), lambda i, j: (i, 0)),
            pl.BlockSpec((1, d), lambda i, j: (0, 0)),
            pl.BlockSpec((1, d), lambda i, j: (0, 0)),
            pl.BlockSpec((d, tn), lambda i, j: (0, j)),
        ],
        out_specs=[
            pl.BlockSpec((tm, tn), lambda i, j: (i, j)),
            pl.BlockSpec((tm, d), lambda i, j: (i, 0)),
        ],
        out_shape=[jax.ShapeDtypeStruct((m, n), out_dtype),
                   jax.ShapeDtypeStruct((m, d), F32)],
        scratch_shapes=[pltpu.VMEM((tm, d), BF16)],
        compiler_params=_cparams(("parallel", "arbitrary")),
        name="ln_matmul",
    )(x, g.reshape(1, d), b.reshape(1, d), w)


def _attn_kernel(lam_ref, far_ref, q_ref, k_ref, v_ref, bias_ref, g_ref, o_ref,
                 m_scr, l_scr, acc_scr, *, tq, hps, scale, post_scale):
    hp = pl.program_id(1)
    qi = pl.program_id(2)
    lane = lax.broadcasted_iota(jnp.int32, (tq, LANES), 1)
    q_maps = []
    for hh in range(hps):
        qs = (q_ref[0, :, hh * LANES:(hh + 1) * LANES].astype(F32) * scale).astype(BF16)
        zero = jnp.zeros_like(qs)
        q_maps += [jnp.where(lane < DIFF_HEAD_DIM, qs, zero), jnp.where(lane >= DIFF_HEAD_DIM, qs, zero)]

    m_scr[...] = jnp.full(m_scr.shape, NEG_INF, F32)
    l_scr[...] = jnp.zeros(l_scr.shape, F32)
    acc_scr[...] = jnp.zeros(acc_scr.shape, F32)

    def step(kstart, bias_index, causal):
        for hh in range(hps):
            kb = k_ref[0, pl.ds(kstart, tq), hh * LANES:(hh + 1) * LANES]
            vb = v_ref[0, pl.ds(kstart, tq), hh * LANES:(hh + 1) * LANES]
            for c in range(2 * hh, 2 * hh + 2):
                s = lax.dot_general(q_maps[c], kb, (((1,), (1,)), ((), ())), preferred_element_type=F32)
                m_old = m_scr[c]
                if bias_index is None:
                    far = far_ref[hp * hps + hh]
                    m_new = jnp.maximum(m_old, jnp.max(s, axis=1, keepdims=True) + far)
                    shift = m_new - far
                else:
                    s = s + bias_ref[hh, bias_index]
                    if causal:
                        ri = lax.broadcasted_iota(jnp.int32, (tq, tq), 0)
                        ci = lax.broadcasted_iota(jnp.int32, (tq, tq), 1)
                        s = jnp.where(ri >= ci, s, NEG_INF)
                    m_new = jnp.maximum(m_old, jnp.max(s, axis=1, keepdims=True))
                    shift = m_new
                p = jnp.exp(s - jnp.concatenate([shift] * (tq // LANES), axis=1))
                corr = jnp.exp(m_old - m_new)
                l_scr[c] = corr * l_scr[c] + jnp.sum(p, axis=1, keepdims=True)
                acc_scr[c] = corr * acc_scr[c] + jnp.dot(p.astype(BF16), vb, preferred_element_type=F32)
                m_scr[c] = m_new

    def far_body(kb, carry):
        step(pl.multiple_of(kb * tq, tq), None, False)
        return carry

    lax.fori_loop(0, jnp.maximum(qi - 1, 0), far_body, 0)

    @pl.when(qi >= 1)
    def _():
        step(pl.multiple_of((qi - 1) * tq, tq), 1, False)

    step(pl.multiple_of(qi * tq, tq), 0, True)

    for hh in range(hps):
        c = 2 * hh
        o = acc_scr[c] / l_scr[c] - lam_ref[0] * (acc_scr[c + 1] / l_scr[c + 1])
        ms = jnp.mean(o * o, axis=1, keepdims=True)
        o_ref[0, :, hh * LANES:(hh + 1) * LANES] = (
            o * lax.rsqrt(ms + LN_EPS) * g_ref[...] * post_scale).astype(o_ref.dtype)


def diff_attention(qkv, bias_tiles, far_bias, lam, subln_g, lambda_init, tq, hps):
    bsz, s, _ = qkv.shape
    kern = functools.partial(_attn_kernel, tq=tq, hps=hps, scale=DIFF_HEAD_DIM ** -0.5,
                             post_scale=1.0 - lambda_init)
    hw = hps * LANES
    nqk = DIFF_QK_WIDTH // hw
    return pl.pallas_call(
        kern,
        grid=(bsz, DIFF_HEADS // hps, s // tq),
        in_specs=[
            pl.BlockSpec(memory_space=pltpu.SMEM),
            pl.BlockSpec(memory_space=pltpu.SMEM),
            pl.BlockSpec((1, tq, hw), lambda b, h, i: (b, i, h)),
            pl.BlockSpec((1, s, hw), lambda b, h, i: (b, 0, nqk + h)),
            pl.BlockSpec((1, s, hw), lambda b, h, i: (b, 0, 2 * nqk + h)),
            pl.BlockSpec((hps, 2, tq, tq), lambda b, h, i: (h, 0, 0, 0)),
            pl.BlockSpec((1, LANES), lambda b, h, i: (0, 0)),
        ],
        out_specs=pl.BlockSpec((1, tq, hw), lambda b, h, i: (b, i, h)),
        out_shape=jax.ShapeDtypeStruct((bsz, s, DIFF_WIDTH), BF16),
        scratch_shapes=[pltpu.VMEM((2 * hps, tq, LANES), F32), pltpu.VMEM((2 * hps, tq, LANES), F32),
                        pltpu.VMEM((2 * hps, tq, LANES), F32)],
        compiler_params=_cparams(("parallel", "parallel", "arbitrary")),
        name="diff_attention",
    )(lam, far_bias, qkv, qkv, qkv, bias_tiles, subln_g.reshape(1, LANES))


def _rwkv_prep_kernel(x_ref, prev_ref, mu_ref, w0_ref, w2_ref, a0_ref, a2_ref, g2_ref,
                      kk_ref, ka_ref, rk_ref, sel_ref, selt_ref,
                      r_out, k_out, v_out, lw_out, kn_out, b_out, g_out, bonus_out, *, tm, seq):
    i = pl.program_id(0)
    x = x_ref[...]
    w = RWKV_WIDTH
    row = lax.broadcasted_iota(jnp.int32, x.shape, 0)
    is_start = (i * tm) % seq == 0
    last_prev = jnp.where(is_start, 0.0, prev_ref[7:8, :])
    prev = jnp.where(row == 0, last_prev, pltpu.roll(x, 1, 0))
    xs = x + (prev - x) * mu_ref[...]

    r = xs[:, 0:w]
    k = xs[:, w:2 * w]
    v = xs[:, 2 * w:3 * w]
    xwa = xs[:, 3 * w:3 * w + 2 * LORA_DECAY]
    xg = xs[:, 3 * w + LANES:3 * w + LANES + 2 * LANES]

    z = w0_ref[...] + _bdot(jnp.tanh(xwa), w2_ref[...])
    softplus_neg_z = jnp.maximum(-z, 0.0) + jnp.log(1.0 + jnp.exp(-jnp.abs(z)))
    lw_out[...] = -jnp.exp(-softplus_neg_z - 0.5)
    a_lr = _sigmoid(a0_ref[...] + _bdot(xwa, a2_ref[...]))
    g_out[...] = _bdot(_sigmoid(xg), g2_ref[...])

    sel = sel_ref[...]
    selt = selt_ref[...]
    kk = k * kk_ref[...]
    ss = _dot_x_sel(kk * kk, sel)
    inv = 1.0 / jnp.maximum(jnp.sqrt(ss), 1e-12)
    kn = kk * _dot_x_sel(inv, selt)
    k2 = k * (1.0 + (a_lr - 1.0) * ka_ref[...])
    bs = _dot_x_sel(r * k2 * rk_ref[...], sel)
    r_out[...] = r
    k_out[...] = k2
    v_out[...] = v
    kn_out[...] = kn
    b_out[...] = kn * a_lr
    bonus_out[...] = _dot_x_sel(bs, selt) * v


def rwkv_prep(pr, mu, w0, w2p, a0, a2p, g2p, k_k, k_a, r_k, sel, selt, seq, tm):
    t, wp = pr.shape
    w = RWKV_WIDTH
    row = lambda a: a.reshape(1, -1)
    full = lambda shp: pl.BlockSpec(shp, lambda i: (0, 0))
    outs = [jax.ShapeDtypeStruct((t, w), F32)] * 8
    kern = functools.partial(_rwkv_prep_kernel, tm=tm, seq=seq)
    return pl.pallas_call(
        kern,
        grid=(t // tm,),
        in_specs=[
            pl.BlockSpec((tm, wp), lambda i: (i, 0)),
            pl.BlockSpec((8, wp), lambda i: (jnp.maximum(i * (tm // 8) - 1, 0), 0)),
            full((1, wp)), full((1, w)), full((LANES, w)), full((1, w)), full((LANES, w)),
            full((2 * LANES, w)), full((1, w)), full((1, w)), full((1, w)),
            full((w, LANES)), full((LANES, w)),
        ],
        out_specs=[pl.BlockSpec((tm, w), lambda i: (i, 0))] * 8,
        out_shape=outs,
        compiler_params=_cparams(("parallel",)),
        name="rwkv_prep",
    )(pr, pr, row(mu), row(w0), w2p, row(a0), a2p, g2p, row(k_k), row(k_a), row(r_k), sel, selt)


def _rwkv_scan_kernel(r_ref, k_ref, v_ref, lw_ref, kn_ref, b_ref, y_ref, s_scr, *, chunk, pairs):
    c2 = 2 * chunk

    @pl.when(pl.program_id(2) == 0)
    def _():
        s_scr[...] = jnp.zeros(s_scr.shape, F32)

    ri = lax.broadcasted_iota(jnp.int32, (chunk, chunk), 0)
    ci = lax.broadcasted_iota(jnp.int32, (chunk, chunk), 1)
    ltri = jnp.where(ri >= ci, 1.0, 0.0).astype(BF16)

    lw = lw_ref[0]
    cl = _dot_sel_x(ltri, lw)
    mid = chunk // 2 - 1
    clm = cl[mid:mid + 1, :]
    cle = cl[chunk - 1:chunk, :]
    w_mid = jnp.exp(cl - clm)
    w_mid_prev = jnp.exp(cl - lw - clm)
    w_mid_inv = jnp.exp(clm - cl)
    w_abs = jnp.exp(cl)
    w_abs_prev = jnp.exp(cl - lw)
    w_end = jnp.exp(cle - cl)
    w_tot = jnp.exp(cle)

    r = r_ref[0]
    k = k_ref[0]
    v = v_ref[0]
    a = -kn_ref[0]
    b = b_ref[0]
    a_mid = a * w_mid_prev
    r_mid = r * w_mid
    b_mid = b * w_mid_inv
    k_mid = k * w_mid_inv
    a_abs = a * w_abs_prev
    r_abs = r * w_abs
    b_end = b * w_end
    k_end = k * w_end

    lane = lax.broadcasted_iota(jnp.int32, (chunk, LANES), 1)
    first = lane < RWKV_HEAD_SIZE

    def stack(x):
        return jnp.concatenate([jnp.where(first, x, 0.0), jnp.where(first, 0.0, x)], axis=0)

    rr = lax.broadcasted_iota(jnp.int32, (c2, c2), 0)
    cc = lax.broadcasted_iota(jnp.int32, (c2, c2), 1)
    same = (rr >= chunk) == (cc >= chunk)
    strict = same & (rr > cc)
    incl = same & (rr >= cc)
    eye = jnp.where(rr == cc, 1.0, 0.0)

    ps = range(pairs)
    sls = [slice(p * LANES, (p + 1) * LANES) for p in ps]
    bf = lambda x: x.astype(BF16)
    a_m = [bf(stack(a_mid[:, sl])) for sl in sls]
    r_m = [bf(stack(r_mid[:, sl])) for sl in sls]
    b_m = [bf(stack(b_mid[:, sl])) for sl in sls]
    k_m = [bf(stack(k_mid[:, sl])) for sl in sls]
    v_s = [bf(stack(v[:, sl])) for sl in sls]

    l_ab = [jnp.where(strict, _bdot_nt(a_m[p], b_m[p]), 0.0) for p in ps]
    l_ak = [jnp.where(strict, _bdot_nt(a_m[p], k_m[p]), 0.0) for p in ps]
    m_rb = [bf(jnp.where(incl, _bdot_nt(r_m[p], b_m[p]), 0.0)) for p in ps]
    m_rk = [bf(jnp.where(incl, _bdot_nt(r_m[p], k_m[p]), 0.0)) for p in ps]

    t_inv = [eye + l_ab[p] for p in ps]
    lp = l_ab
    for _ in range(max(chunk.bit_length() - 2, 0)):
        lp = [_bdot(lp[p], lp[p]) for p in ps]
        t_inv = [t_inv[p] + _bdot(lp[p], t_inv[p]) for p in ps]
    t_inv = [bf(t) for t in t_inv]

    lv = [_bdot(l_ak[p], v_s[p]) for p in ps]
    a_p = [bf(_bdot(t_inv[p], stack(a_abs[:, sls[p]]))) for p in ps]
    v_p = [bf(_bdot(t_inv[p], lv[p])) for p in ps]
    r_p = [stack(r_abs[:, sls[p]]) + _bdot(m_rb[p], a_p[p]) for p in ps]
    y_p = [_bdot(m_rb[p], v_p[p]) + _bdot(m_rk[p], v_s[p]) for p in ps]

    s_old = [s_scr[p] for p in ps]
    y_s = [_bdot_nt(r_p[p], s_old[p]) + y_p[p] for p in ps]
    for p in ps:
        y_ref[0, :, sls[p]] = y_s[p][:chunk] + y_s[p][chunk:]
    b_e = [bf(stack(b_end[:, sl])) for sl in sls]
    k_e = [bf(stack(k_end[:, sl])) for sl in sls]
    ab = [_bdot_tn(a_p[p], b_e[p]) for p in ps]
    for p in ps:
        s_scr[p] = (s_old[p] * w_tot[:, sls[p]] + _bdot(s_old[p], ab[p])
                    + _bdot_tn(v_p[p], b_e[p]) + _bdot_tn(v_s[p], k_e[p]))


def rwkv_scan(r, k, v, lw, kn, b, chunk, pairs):
    bsz, s, w = r.shape
    pw = pairs * LANES
    spec = pl.BlockSpec((1, chunk, pw), lambda bi, hi, ci: (bi, ci, hi))
    kern = functools.partial(_rwkv_scan_kernel, chunk=chunk, pairs=pairs)
    return pl.pallas_call(
        kern,
        grid=(bsz, w // pw, s // chunk),
        in_specs=[spec] * 6,
        out_specs=spec,
        out_shape=jax.ShapeDtypeStruct((bsz, s, w), F32),
        scratch_shapes=[pltpu.VMEM((pairs, LANES, LANES), F32)],
        compiler_params=_cparams(("parallel", "parallel", "arbitrary")),
        name="rwkv_scan",
    )(r, k, v, lw, kn, b)


def _rwkv_post_kernel(y_ref, bonus_ref, g_ref, lg_ref, lb_ref, sel_ref, selt_ref, o_ref):
    y = y_ref[...]
    sel = sel_ref[...]
    selt = selt_ref[...]
    inv_n = 1.0 / RWKV_HEAD_SIZE
    mu = _dot_x_sel(_dot_x_sel(y, sel) * inv_n, selt)
    yc = y - mu
    var = _dot_x_sel(_dot_x_sel(yc * yc, sel) * inv_n, selt)
    yn = yc * lax.rsqrt(var + RWKV_GN_EPS) * lg_ref[...] + lb_ref[...]
    o_ref[...] = ((yn + bonus_ref[...]) * g_ref[...]).astype(o_ref.dtype)


def rwkv_post(y, bonus, g, lnx_g, lnx_b, sel, selt, tm):
    t, w = y.shape
    tile = pl.BlockSpec((tm, w), lambda i: (i, 0))
    full = lambda shp: pl.BlockSpec(shp, lambda i: (0, 0))
    return pl.pallas_call(
        _rwkv_post_kernel,
        grid=(t // tm,),
        in_specs=[tile, tile, tile, full((1, w)), full((1, w)), full((w, LANES)), full((LANES, w))],
        out_specs=tile,
        out_shape=jax.ShapeDtypeStruct((t, w), BF16),
        compiler_params=_cparams(("parallel",)),
        name="rwkv_post",
    )(y, bonus, g, lnx_g.reshape(1, w), lnx_b.reshape(1, w), sel, selt)


def _route(logits):
    lane = lax.broadcasted_iota(jnp.int32, logits.shape, 1)
    big = jnp.int32(4 * LANES)
    gmask = lane < N_GROUPS
    gl = jnp.where(gmask, logits, NEG_INF)
    gmax = jnp.max(gl, axis=1, keepdims=True)
    gidx = jnp.min(jnp.where(gl == gmax, lane, big), axis=1, keepdims=True)
    gtop = 1.0 / jnp.sum(jnp.where(gmask, jnp.exp(gl - gmax), 0.0), axis=1, keepdims=True)
    lo = N_GROUPS + EXPERTS_PER_GROUP * gidx
    emask = (lane >= lo) & (lane < lo + EXPERTS_PER_GROUP)
    el = jnp.where(emask, logits, NEG_INF)
    e1 = jnp.max(el, axis=1, keepdims=True)
    i1 = jnp.min(jnp.where(el == e1, lane, big), axis=1, keepdims=True)
    el2 = jnp.where(lane == i1, NEG_INF, el)
    e2 = jnp.max(el2, axis=1, keepdims=True)
    i2 = jnp.min(jnp.where(el2 == e2, lane, big), axis=1, keepdims=True)
    t = jnp.exp(e2 - e1)
    w1 = gtop / (1.0 + t)
    w2 = gtop * t / (1.0 + t)
    id1 = (i1 - N_GROUPS).astype(F32)
    id2 = (i2 - N_GROUPS).astype(F32)
    return jnp.where(lane == 0, id1, jnp.where(lane == 1, id2, jnp.where(lane == 2, w1,
                                                                         jnp.where(lane == 3, w2, 0.0))))


def _outproj_kernel(yd_ref, yr_ref, h_ref, w1_ref, w2_ref, g_ref, b_ref, rw_hi_ref, rw_lo_ref, rb_ref,
                    h1_ref, info_ref):
    mix = (jnp.dot(yd_ref[...], w1_ref[...], preferred_element_type=F32)
           + jnp.dot(yr_ref[...], w2_ref[...], preferred_element_type=F32))
    h1 = _layer_norm(DEEPNORM_ALPHA * h_ref[...] + mix, g_ref[...], b_ref[...])
    h1_ref[...] = h1
    hi = h1.astype(BF16)
    lo = (h1 - hi.astype(F32)).astype(BF16)
    d = lambda a, b: jnp.dot(a, b, preferred_element_type=F32)
    logits = d(hi, rw_hi_ref[...]) + d(lo, rw_hi_ref[...]) + d(hi, rw_lo_ref[...]) + rb_ref[...]
    info_ref[...] = _route(logits)


def outproj_ln_route(yd, yr, h, w_out_a, w_out_b, g, b, rw_hi, rw_lo, rb, tm):
    t, d = h.shape
    ka = yd.shape[1]
    kb = yr.shape[1]
    full = lambda shp: pl.BlockSpec(shp, lambda i: (0, 0))
    return pl.pallas_call(
        _outproj_kernel,
        grid=(t // tm,),
        in_specs=[
            pl.BlockSpec((tm, ka), lambda i: (i, 0)),
            pl.BlockSpec((tm, kb), lambda i: (i, 0)),
            pl.BlockSpec((tm, d), lambda i: (i, 0)),
            full((ka, d)), full((kb, d)), full((1, d)), full((1, d)),
            full((d, ROUTER_LANES)), full((d, ROUTER_LANES)), full((1, ROUTER_LANES)),
        ],
        out_specs=[
            pl.BlockSpec((tm, d), lambda i: (i, 0)),
            pl.BlockSpec((tm, ROUTER_LANES), lambda i: (i, 0)),
        ],
        out_shape=[jax.ShapeDtypeStruct((t, d), F32), jax.ShapeDtypeStruct((t, ROUTER_LANES), F32)],
        compiler_params=_cparams(("parallel",)),
        name="outproj_ln_route",
    )(yd, yr, h, w_out_a, w_out_b, g.reshape(1, d), b.reshape(1, d), rw_hi, rw_lo, rb)


def _moe_rank_kernel(e_ref, rank_ref, cnt_ref, run_scr, *, tm):
    @pl.when(pl.program_id(0) == 0)
    def _():
        run_scr[...] = jnp.zeros(run_scr.shape, F32)

    lane = lax.broadcasted_iota(jnp.int32, (tm, LANES), 1)
    onehot = e_ref[...] == lane
    oh = jnp.where(onehot, 1.0, 0.0)
    ri = lax.broadcasted_iota(jnp.int32, (tm, tm), 0)
    ci = lax.broadcasted_iota(jnp.int32, (tm, tm), 1)
    before = jnp.where(ri > ci, 1.0, 0.0).astype(BF16)
    earlier = jnp.dot(before, oh.astype(BF16), preferred_element_type=F32) + run_scr[...]
    rank_ref[...] = jnp.sum(jnp.where(onehot, earlier, 0.0), axis=1, keepdims=True).astype(jnp.int32)
    run_scr[...] += jnp.sum(oh, axis=0, keepdims=True)
    cnt_ref[...] = run_scr[...]


def moe_rank(e_col, tm):
    n = e_col.shape[0]
    return pl.pallas_call(
        functools.partial(_moe_rank_kernel, tm=tm),
        grid=(n // tm,),
        in_specs=[pl.BlockSpec((tm, 1), lambda i: (i, 0))],
        out_specs=[pl.BlockSpec((tm, 1), lambda i: (i, 0)), pl.BlockSpec((1, LANES), lambda i: (0, 0))],
        out_shape=[jax.ShapeDtypeStruct((n, 1), jnp.int32), jax.ShapeDtypeStruct((1, LANES), F32)],
        scratch_shapes=[pltpu.VMEM((1, LANES), F32)],
        compiler_params=_cparams(("arbitrary",)),
        name="moe_rank",
    )(e_col)


def _row_copy(src_hbm, row, dst, r, sem):
    return pltpu.make_async_copy(src_hbm.at[pl.ds(row, 1)], dst.at[pl.ds(r, 1)], sem)


def _moe_gather_kernel(src_ref, act_ref, x_hbm, o_ref, buf, sem, *, tm):
    j = pl.program_id(0)

    @pl.when(act_ref[j] > 0)
    def _():
        def issue(r, carry):
            _row_copy(x_hbm, src_ref[j * tm + r], buf, r, sem.at[0]).start()
            return carry

        def wait(r, carry):
            _row_copy(x_hbm, 0, buf, r, sem.at[0]).wait()
            return carry

        lax.fori_loop(0, tm, issue, 0, unroll=8)
        lax.fori_loop(0, tm, wait, 0, unroll=8)
        o_ref[...] = buf[...].astype(BF16)

    @pl.when(act_ref[j] == 0)
    def _():
        o_ref[...] = jnp.zeros(o_ref.shape, BF16)


def moe_gather(src_tok, tile_active, x, n_rows, tm):
    d = x.shape[1]
    return pl.pallas_call(
        functools.partial(_moe_gather_kernel, tm=tm),
        grid_spec=pltpu.PrefetchScalarGridSpec(
            num_scalar_prefetch=2,
            grid=(n_rows // tm,),
            in_specs=[pl.BlockSpec(memory_space=pl.ANY)],
            out_specs=pl.BlockSpec((tm, d), lambda j, src, act: (j, 0)),
            scratch_shapes=[pltpu.VMEM((tm, d), F32), pltpu.SemaphoreType.DMA((1,))],
        ),
        out_shape=jax.ShapeDtypeStruct((n_rows, d), BF16),
        compiler_params=_cparams(("arbitrary",)),
        name="moe_gather",
    )(src_tok, tile_active, x)


def _moe_expert_kernel(te_ref, act_ref, x_ref, wg_ref, wu_ref, wd_ref, o_ref):
    j = pl.program_id(0)

    @pl.when(act_ref[j] > 0)
    def _():
        x = x_ref[...]
        hg = jnp.dot(x, wg_ref[0].astype(BF16), preferred_element_type=F32)
        hu = jnp.dot(x, wu_ref[0].astype(BF16), preferred_element_type=F32)
        hid = hg * _sigmoid(hg) * hu
        o_ref[...] = jnp.dot(hid.astype(BF16), wd_ref[0].astype(BF16), preferred_element_type=F32)

    @pl.when(act_ref[j] == 0)
    def _():
        o_ref[...] = jnp.zeros(o_ref.shape, F32)


def moe_experts(tile_expert, tile_active, xs, wg, wu, wd, tm):
    n_rows, d = xs.shape
    f = wg.shape[-1]
    return pl.pallas_call(
        _moe_expert_kernel,
        grid_spec=pltpu.PrefetchScalarGridSpec(
            num_scalar_prefetch=2,
            grid=(n_rows // tm,),
            in_specs=[
                pl.BlockSpec((tm, d), lambda j, te, act: (j, 0)),
                pl.BlockSpec((1, d, f), lambda j, te, act: (te[j], 0, 0)),
                pl.BlockSpec((1, d, f), lambda j, te, act: (te[j], 0, 0)),
                pl.BlockSpec((1, f, d), lambda j, te, act: (te[j], 0, 0)),
            ],
            out_specs=pl.BlockSpec((tm, d), lambda j, te, act: (j, 0)),
        ),
        out_shape=jax.ShapeDtypeStruct((n_rows, d), F32),
        compiler_params=pltpu.CompilerParams(dimension_semantics=("arbitrary",),
                                             vmem_limit_bytes=MOE_VMEM_LIMIT),
        name="moe_experts",
    )(tile_expert, tile_active, xs, wg, wu, wd)


def _final_kernel(p1_ref, p2_ref, y_hbm, h1_ref, info_ref, p_ref, g_ref, b_ref, pw_ref, pg_ref, gw_ref,
                  o_ref, ybuf, sem, *, tm):
    i = pl.program_id(0)

    def issue(r, carry):
        _row_copy(y_hbm, p1_ref[i * tm + r], ybuf.at[0], r, sem.at[0]).start()
        _row_copy(y_hbm, p2_ref[i * tm + r], ybuf.at[1], r, sem.at[0]).start()
        return carry

    def wait(r, carry):
        _row_copy(y_hbm, 0, ybuf.at[0], r, sem.at[0]).wait()
        _row_copy(y_hbm, 0, ybuf.at[1], r, sem.at[0]).wait()
        return carry

    lax.fori_loop(0, tm, issue, 0, unroll=8)
    lax.fori_loop(0, tm, wait, 0, unroll=8)

    info = info_ref[...]
    ffn = info[:, 2:3] * ybuf[0] + info[:, 3:4] * ybuf[1]
    h2 = _layer_norm(DEEPNORM_ALPHA * h1_ref[...] + ffn, g_ref[...], b_ref[...])
    pe = jnp.dot(p_ref[...].astype(BF16), pw_ref[...], preferred_element_type=F32)
    ple = pe * lax.rsqrt(jnp.mean(pe * pe, axis=-1, keepdims=True) + LN_EPS) * pg_ref[...]
    gate = _sigmoid(jnp.dot(h2.astype(BF16), gw_ref[...], preferred_element_type=F32))
    o_ref[...] = h2 + gate * ple


def final_block(pos1, pos2, ys, h1, info, p, g, b, ple_w, ple_g, gate_w, tm):
    t, d = h1.shape
    dp = p.shape[1]
    tile = lambda n: pl.BlockSpec((tm, n), lambda i, p1, p2: (i, 0))
    full = lambda shp: pl.BlockSpec(shp, lambda i, p1, p2: (0, 0))
    return pl.pallas_call(
        functools.partial(_final_kernel, tm=tm),
        grid_spec=pltpu.PrefetchScalarGridSpec(
            num_scalar_prefetch=2,
            grid=(t // tm,),
            in_specs=[pl.BlockSpec(memory_space=pl.ANY), tile(d), tile(ROUTER_LANES), tile(dp),
                      full((1, d)), full((1, d)), full((dp, d)), full((1, d)), full((d, d))],
            out_specs=tile(d),
            scratch_shapes=[pltpu.VMEM((2, tm, d), F32), pltpu.SemaphoreType.DMA((1,))],
        ),
        out_shape=jax.ShapeDtypeStruct((t, d), F32),
        compiler_params=_cparams(("arbitrary",)),
        name="final_block",
    )(pos1, pos2, ys, h1, info, p, g.reshape(1, d), b.reshape(1, d), ple_w, ple_g.reshape(1, d), gate_w)


def _dispatch_plan(e_all, rank, counts, n_tok, tm, n_rows):
    n_tiles = n_rows // tm
    cnt = counts[0, :N_EXPERTS].astype(jnp.int32)
    padded = ((cnt + tm - 1) // tm) * tm
    ends = jnp.cumsum(padded)
    starts = ends - padded
    pos = starts[e_all] + rank
    src_tok = jnp.zeros((n_rows,), jnp.int32).at[pos].set(
        jnp.arange(2 * n_tok, dtype=jnp.int32) % n_tok, unique_indices=True)
    n_active = ends[-1] // tm
    tile_ids = jnp.arange(n_tiles, dtype=jnp.int32)
    tile_active = (tile_ids < n_active).astype(jnp.int32)
    tile_expert = jnp.searchsorted(ends, jnp.minimum(tile_ids, n_active - 1) * tm, side="right")
    tile_expert = jnp.minimum(tile_expert, N_EXPERTS - 1).astype(jnp.int32)
    return pos, src_tok, tile_expert, tile_active


def _t5_bucket(rel):
    n = jnp.maximum(rel, 0)
    max_exact = REL_BUCKETS // 2
    nf = jnp.maximum(n, 1).astype(F32)
    large = max_exact + (jnp.log(nf / max_exact) / math.log(REL_MAX_DIST / max_exact)
                         * (REL_BUCKETS - max_exact)).astype(jnp.int32)
    large = jnp.minimum(large, REL_BUCKETS - 1)
    return jnp.where(n < max_exact, n, large)


def _bias_tables(rel_bias, tq):
    n = 3 * tq
    rel = 2 * tq - 1 - jnp.arange(n, dtype=jnp.int32)
    g = rel_bias[_t5_bucket(rel)].astype(F32).T
    skew = jnp.tile(g, (1, tq))[:, :tq * (n - 1)].reshape(-1, tq, n - 1)
    tiles = jnp.stack([skew[:, :, 2 * tq - 1:3 * tq - 1], skew[:, :, tq - 1:2 * tq - 1]], axis=1)
    far = rel_bias[_t5_bucket(jnp.full((), 2 * tq, jnp.int32))].astype(F32)
    return tiles, far


def _head_selectors():
    lane = jnp.arange(RWKV_WIDTH, dtype=jnp.int32) // RWKV_HEAD_SIZE
    sel = (lane[:, None] == jnp.arange(LANES, dtype=jnp.int32)[None, :]).astype(BF16)
    return sel, sel.T


def kernel(x, p, ln_in_g, ln_in_b, rel_bias, w_in, diff_lam_q1, diff_lam_k1, diff_lam_q2, diff_lam_k2, diff_subln_g, rwkv_mu, rwkv_w0, rwkv_w2, rwkv_a0, rwkv_a2, rwkv_g2, rwkv_k_k, rwkv_k_a, rwkv_r_k, rwkv_lnx_g, rwkv_lnx_b, w_out, ln1_g, ln1_b, router_group_w, router_group_b, router_expert_w, router_expert_b, moe_w_gate, moe_w_up, moe_w_down, ln2_g, ln2_b, ple_w, ple_norm_g, ple_gate_w):
    bsz, seq, d = x.shape
    t = bsz * seq
    i = 0
    lambda_init = 0.8 - 0.6 * math.exp(-0.3 * i)
    tq = 256

    x2 = x.reshape(t, d)
    w_in_b = w_in[i].astype(BF16)
    w_qkv = w_in_b[:, :O_RWKV]
    w_rw = jnp.pad(w_in_b[:, O_RWKV:], ((0, 0), (0, RWKV_PROJ_PAD - RWKV_PROJ)))
    qkv, h = ln_matmul(x2, ln_in_g, ln_in_b, w_qkv, BF16, 512, 512)
    pr, _ = ln_matmul(x2, ln_in_g, ln_in_b, w_rw, F32, 512, 512)

    lam = (jnp.exp(jnp.sum(diff_lam_q1[i].astype(F32) * diff_lam_k1[i]))
           - jnp.exp(jnp.sum(diff_lam_q2[i].astype(F32) * diff_lam_k2[i])) + lambda_init)
    bias_tiles, far_bias = _bias_tables(rel_bias, tq)
    y_diff = diff_attention(qkv.reshape(bsz, seq, O_RWKV), bias_tiles, far_bias, lam.reshape(1),
                            diff_subln_g[i], lambda_init, tq, 8)

    sel, selt = _head_selectors()
    mu_p = jnp.pad(rwkv_mu[i], (0, RWKV_PROJ_PAD - RWKV_PROJ))
    zeros64 = jnp.zeros((LORA_DECAY, RWKV_WIDTH), F32)
    w2p = jnp.concatenate([rwkv_w2[i], zeros64], axis=0).astype(BF16)
    a2p = jnp.concatenate([zeros64, rwkv_a2[i]], axis=0).astype(BF16)
    g2p = jnp.pad(rwkv_g2[i], ((0, 2 * LANES - LORA_GATE), (0, 0))).astype(BF16)
    r, k2, v, lw, kn, bb, g, bonus = rwkv_prep(
        pr, mu_p, rwkv_w0[i], w2p, rwkv_a0[i], a2p, g2p, rwkv_k_k[i], rwkv_k_a[i],
        rwkv_r_k[i].reshape(-1), sel, selt, seq, 256)
    sh = lambda a: a.reshape(bsz, seq, RWKV_WIDTH)
    y_scan = rwkv_scan(sh(r), sh(k2), sh(v), sh(lw), sh(kn), sh(bb), 64, 8)
    y_rwkv = rwkv_post(y_scan.reshape(t, RWKV_WIDTH), bonus, g, rwkv_lnx_g[i], rwkv_lnx_b[i], sel, selt, 512)

    w_out_b = w_out[i].astype(BF16)
    rw = jnp.concatenate([router_group_w[i], router_expert_w[i]], axis=1)
    rw = jnp.pad(rw, ((0, 0), (0, ROUTER_LANES - rw.shape[1])))
    rw_hi = rw.astype(BF16)
    rw_lo = (rw - rw_hi.astype(F32)).astype(BF16)
    rb = jnp.pad(jnp.concatenate([router_group_b[i], router_expert_b[i]]),
                 (0, ROUTER_LANES - N_GROUPS - N_EXPERTS)).reshape(1, ROUTER_LANES)
    h1, info = outproj_ln_route(y_diff.reshape(t, DIFF_WIDTH), y_rwkv, h, w_out_b[:DIFF_WIDTH],
                                w_out_b[DIFF_WIDTH:], ln1_g[i], ln1_b[i], rw_hi, rw_lo, rb, 256)

    tm_e = MOE_ROW_TILE
    n_rows = 2 * t + N_EXPERTS * tm_e
    e_all = jnp.concatenate([info[:, 0], info[:, 1]]).astype(jnp.int32)
    rank, counts = moe_rank(e_all.reshape(-1, 1), 1024)
    pos, src_tok, tile_expert, tile_active = _dispatch_plan(e_all, rank[:, 0], counts, t, tm_e, n_rows)
    f = moe_w_gate.shape[-1]
    xs = moe_gather(src_tok, tile_active, h1, n_rows, tm_e)
    ys = moe_experts(tile_expert, tile_active, xs, moe_w_gate[i].reshape(N_EXPERTS, d, f),
                     moe_w_up[i].reshape(N_EXPERTS, d, f), moe_w_down[i].reshape(N_EXPERTS, f, d), tm_e)

    out = final_block(pos[:t], pos[t:], ys, h1, info, p[i].reshape(t, -1), ln2_g[i], ln2_b[i],
                      ple_w[i].astype(BF16), ple_norm_g[i], ple_gate_w[i].astype(BF16), 256)
    return out.reshape(bsz, seq, d)
```

```python
import functools
import math

import jax
import jax.numpy as jnp
from jax import lax
from jax.experimental import pallas as pl
from jax.experimental.pallas import tpu as pltpu

F32 = jnp.float32
BF16 = jnp.bfloat16

DIFF_HEADS = 8
DIFF_HEAD_DIM = 64
DIFF_V_DIM = 128
DIFF_WIDTH = DIFF_HEADS * DIFF_V_DIM
DIFF_QK_WIDTH = DIFF_HEADS * 2 * DIFF_HEAD_DIM
RWKV_HEADS = 16
RWKV_HEAD_SIZE = 64
RWKV_WIDTH = RWKV_HEADS * RWKV_HEAD_SIZE
LORA_DECAY = 64
LORA_AAA = 64
LORA_GATE = 160
RWKV_PROJ = 3 * RWKV_WIDTH + LORA_DECAY + LORA_AAA + LORA_GATE
O_RWKV = 2 * DIFF_QK_WIDTH + DIFF_WIDTH
REL_BUCKETS = 32
REL_MAX_DIST = 128
N_GROUPS = 4
EXPERTS_PER_GROUP = 8
N_EXPERTS = N_GROUPS * EXPERTS_PER_GROUP
LN_EPS = 1e-5
RWKV_GN_EPS = 64e-5
NEG_INF = -1e30
DEPTH = 1
DEEPNORM_ALPHA = (2 * DEPTH) ** 0.25

LANES = 128
RWKV_PROJ_PAD = 3584
ROUTER_LANES = LANES
VMEM_LIMIT = 48 * 1024 * 1024
MOE_VMEM_LIMIT = 56 * 1024 * 1024
MOE_ROW_TILE = 256


def _cparams(sem):
    return pltpu.CompilerParams(dimension_semantics=sem, vmem_limit_bytes=VMEM_LIMIT)


def _bdot(a, b):
    return jnp.dot(a.astype(BF16), b.astype(BF16), preferred_element_type=F32)


def _bdot_nt(a, b):
    return lax.dot_general(a.astype(BF16), b.astype(BF16), (((1,), (1,)), ((), ())),
                           preferred_element_type=F32)


def _bdot_tn(a, b):
    return lax.dot_general(a.astype(BF16), b.astype(BF16), (((0,), (0,)), ((), ())),
                           preferred_element_type=F32)


def _split3(x):
    hi = x.astype(BF16)
    r1 = x - hi.astype(F32)
    mid = r1.astype(BF16)
    lo = (r1 - mid.astype(F32)).astype(BF16)
    return hi, mid, lo


def _dot_x_sel(x, sel):
    hi, mid, lo = _split3(x)
    d = lambda p: jnp.dot(p, sel, preferred_element_type=F32)
    return d(hi) + d(mid) + d(lo)


def _dot_sel_x(sel, x):
    hi, mid, lo = _split3(x)
    d = lambda p: jnp.dot(sel, p, preferred_element_type=F32)
    return d(hi) + d(mid) + d(lo)


def _layer_norm(x, g, b):
    mu = jnp.mean(x, axis=-1, keepdims=True)
    xc = x - mu
    var = jnp.mean(xc * xc, axis=-1, keepdims=True)
    return xc * lax.rsqrt(var + LN_EPS) * g + b


def _sigmoid(z):
    return 1.0 / (1.0 + jnp.exp(-z))


def _ln_kernel(x_ref, g_ref, b_ref, h_ref, hb_ref):
    xn = _layer_norm(x_ref[...], g_ref[...], b_ref[...])
    h_ref[...] = xn
    hb_ref[...] = xn.astype(BF16)


def layer_norm_rows(x, g, b, tm):
    m, d = x.shape
    tile = pl.BlockSpec((tm, d), lambda i: (i, 0))
    row = pl.BlockSpec((1, d), lambda i: (0, 0))
    return pl.pallas_call(
        _ln_kernel,
        grid=(m // tm,),
        in_specs=[tile, row, row],
        out_specs=[tile, tile],
        out_shape=[jax.ShapeDtypeStruct((m, d), F32), jax.ShapeDtypeStruct((m, d), BF16)],
        compiler_params=_cparams(("parallel",)),
        name="layer_norm_rows",
    )(x, g.reshape(1, d), b.reshape(1, d))


def _mm_kernel(x_ref, w_ref, o_ref):
    o_ref[...] = jnp.dot(x_ref[...], w_ref[...], preferred_element_type=F32).astype(o_ref.dtype)


def matmul(x, w, out_dtype, tm, tn):
    m, k = x.shape
    n = w.shape[1]
    return pl.pallas_call(
        _mm_kernel,
        grid=(m // tm, n // tn),
        in_specs=[pl.BlockSpec((tm, k), lambda i, j: (i, 0)), pl.BlockSpec((k, tn), lambda i, j: (0, j))],
        out_specs=pl.BlockSpec((tm, tn), lambda i, j: (i, j)),
        out_shape=jax.ShapeDtypeStruct((m, n), out_dtype),
        compiler_params=_cparams(("parallel", "parallel")),
        name="matmul",
    )(x, w)


def _attn_kernel(lam_ref, far_ref, q_ref, k_ref, v_ref, bias_ref, g_ref, o_ref,
                 m_scr, l_scr, acc_scr, *, tq, hps, scale, post_scale):
    hp = pl.program_id(1)
    qi = pl.program_id(2)
    lane = lax.broadcasted_iota(jnp.int32, (tq, LANES), 1)
    q_maps = []
    for hh in range(hps):
        qs = (q_ref[0, :, hh * LANES:(hh + 1) * LANES].astype(F32) * scale).astype(BF16)
        zero = jnp.zeros_like(qs)
        q_maps += [jnp.where(lane < DIFF_HEAD_DIM, qs, zero), jnp.where(lane >= DIFF_HEAD_DIM, qs, zero)]

    m_scr[...] = jnp.full(m_scr.shape, NEG_INF, F32)
    l_scr[...] = jnp.zeros(l_scr.shape, F32)
    acc_scr[...] = jnp.zeros(acc_scr.shape, F32)

    def step(kstart, bias_index, causal):
        for hh in range(hps):
            kb = k_ref[0, pl.ds(kstart, tq), hh * LANES:(hh + 1) * LANES]
            vb = v_ref[0, pl.ds(kstart, tq), hh * LANES:(hh + 1) * LANES]
            for c in range(2 * hh, 2 * hh + 2):
                s = lax.dot_general(q_maps[c], kb, (((1,), (1,)), ((), ())), preferred_element_type=F32)
                m_old = m_scr[c]
                if bias_index is None:
                    far = far_ref[hp * hps + hh]
                    m_new = jnp.maximum(m_old, jnp.max(s, axis=1, keepdims=True) + far)
                    shift = m_new - far
                else:
                    s = s + bias_ref[hh, bias_index]
                    if causal:
                        ri = lax.broadcasted_iota(jnp.int32, (tq, tq), 0)
                        ci = lax.broadcasted_iota(jnp.int32, (tq, tq), 1)
                        s = jnp.where(ri >= ci, s, NEG_INF)
                    m_new = jnp.maximum(m_old, jnp.max(s, axis=1, keepdims=True))
                    shift = m_new
                p = jnp.exp(s - jnp.concatenate([shift] * (tq // LANES), axis=1))
                corr = jnp.exp(m_old - m_new)
                l_scr[c] = corr * l_scr[c] + jnp.sum(p, axis=1, keepdims=True)
                acc_scr[c] = corr * acc_scr[c] + jnp.dot(p.astype(BF16), vb, preferred_element_type=F32)
                m_scr[c] = m_new

    def far_body(kb, carry):
        step(pl.multiple_of(kb * tq, tq), None, False)
        return carry

    lax.fori_loop(0, jnp.maximum(qi - 1, 0), far_body, 0)

    @pl.when(qi >= 1)
    def _():
        step(pl.multiple_of((qi - 1) * tq, tq), 1, False)

    step(pl.multiple_of(qi * tq, tq), 0, True)

    for hh in range(hps):
        c = 2 * hh
        o = acc_scr[c] / l_scr[c] - lam_ref[0] * (acc_scr[c + 1] / l_scr[c + 1])
        ms = jnp.mean(o * o, axis=1, keepdims=True)
        o_ref[0, :, hh * LANES:(hh + 1) * LANES] = (
            o * lax.rsqrt(ms + LN_EPS) * g_ref[...] * post_scale).astype(o_ref.dtype)


def diff_attention(qkv, bias_tiles, far_bias, lam, subln_g, lambda_init, tq, hps):
    bsz, s, _ = qkv.shape
    kern = functools.partial(_attn_kernel, tq=tq, hps=hps, scale=DIFF_HEAD_DIM ** -0.5,
                             post_scale=1.0 - lambda_init)
    hw = hps * LANES
    nqk = DIFF_QK_WIDTH // hw
    return pl.pallas_call(
        kern,
        grid=(bsz, DIFF_HEADS // hps, s // tq),
        in_specs=[
            pl.BlockSpec(memory_space=pltpu.SMEM),
            pl.BlockSpec(memory_space=pltpu.SMEM),
            pl.BlockSpec((1, tq, hw), lambda b, h, i: (b, i, h)),
            pl.BlockSpec((1, s, hw), lambda b, h, i: (b, 0, nqk + h)),
            pl.BlockSpec((1, s, hw), lambda b, h, i: (b, 0, 2 * nqk + h)),
            pl.BlockSpec((hps, 2, tq, tq), lambda b, h, i: (h, 0, 0, 0)),
            pl.BlockSpec((1, LANES), lambda b, h, i: (0, 0)),
        ],
        out_specs=pl.BlockSpec((1, tq, hw), lambda b, h, i: (b, i, h)),
        out_shape=jax.ShapeDtypeStruct((bsz, s, DIFF_WIDTH), BF16),
        scratch_shapes=[pltpu.VMEM((2 * hps, tq, LANES), F32), pltpu.VMEM((2 * hps, tq, LANES), F32),
                        pltpu.VMEM((2 * hps, tq, LANES), F32)],
        compiler_params=_cparams(("parallel", "parallel", "arbitrary")),
        name="diff_attention",
    )(lam, far_bias, qkv, qkv, qkv, bias_tiles, subln_g.reshape(1, LANES))


def _rwkv_prep_kernel(x_ref, prev_ref, mu_ref, w0_ref, w2_ref, a0_ref, a2_ref, g2_ref,
                      kk_ref, ka_ref, rk_ref, sel_ref, selt_ref,
                      r_out, k_out, v_out, lw_out, kn_out, b_out, g_out, bonus_out, *, tm, seq):
    i = pl.program_id(0)
    x = x_ref[...]
    w = RWKV_WIDTH
    row = lax.broadcasted_iota(jnp.int32, x.shape, 0)
    is_start = (i * tm) % seq == 0
    last_prev = jnp.where(is_start, 0.0, prev_ref[7:8, :])
    prev = jnp.where(row == 0, last_prev, pltpu.roll(x, 1, 0))
    xs = x + (prev - x) * mu_ref[...]

    r = xs[:, 0:w]
    k = xs[:, w:2 * w]
    v = xs[:, 2 * w:3 * w]
    xwa = xs[:, 3 * w:3 * w + 2 * LORA_DECAY]
    xg = xs[:, 3 * w + LANES:3 * w + LANES + 2 * LANES]

    z = w0_ref[...] + _bdot(jnp.tanh(xwa), w2_ref[...])
    softplus_neg_z = jnp.maximum(-z, 0.0) + jnp.log(1.0 + jnp.exp(-jnp.abs(z)))
    lw_out[...] = -jnp.exp(-softplus_neg_z - 0.5)
    a_lr = _sigmoid(a0_ref[...] + _bdot(xwa, a2_ref[...]))
    g_out[...] = _bdot(_sigmoid(xg), g2_ref[...])

    sel = sel_ref[...]
    selt = selt_ref[...]
    kk = k * kk_ref[...]
    ss = _dot_x_sel(kk * kk, sel)
    inv = 1.0 / jnp.maximum(jnp.sqrt(ss), 1e-12)
    kn = kk * _dot_x_sel(inv, selt)
    k2 = k * (1.0 + (a_lr - 1.0) * ka_ref[...])
    bs = _dot_x_sel(r * k2 * rk_ref[...], sel)
    r_out[...] = r
    k_out[...] = k2
    v_out[...] = v
    kn_out[...] = kn
    b_out[...] = kn * a_lr
    bonus_out[...] = _dot_x_sel(bs, selt) * v


def rwkv_prep(pr, mu, w0, w2p, a0, a2p, g2p, k_k, k_a, r_k, sel, selt, seq, tm):
    t, wp = pr.shape
    w = RWKV_WIDTH
    row = lambda a: a.reshape(1, -1)
    full = lambda shp: pl.BlockSpec(shp, lambda i: (0, 0))
    outs = [jax.ShapeDtypeStruct((t, w), F32)] * 8
    kern = functools.partial(_rwkv_prep_kernel, tm=tm, seq=seq)
    return pl.pallas_call(
        kern,
        grid=(t // tm,),
        in_specs=[
            pl.BlockSpec((tm, wp), lambda i: (i, 0)),
            pl.BlockSpec((8, wp), lambda i: (jnp.maximum(i * (tm // 8) - 1, 0), 0)),
            full((1, wp)), full((1, w)), full((LANES, w)), full((1, w)), full((LANES, w)),
            full((2 * LANES, w)), full((1, w)), full((1, w)), full((1, w)),
            full((w, LANES)), full((LANES, w)),
        ],
        out_specs=[pl.BlockSpec((tm, w), lambda i: (i, 0))] * 8,
        out_shape=outs,
        compiler_params=_cparams(("parallel",)),
        name="rwkv_prep",
    )(pr, pr, row(mu), row(w0), w2p, row(a0), a2p, g2p, row(k_k), row(k_a), row(r_k), sel, selt)


def _rwkv_scan_kernel(r_ref, k_ref, v_ref, lw_ref, kn_ref, b_ref, y_ref, s_scr, *, chunk, pairs):
    c2 = 2 * chunk

    @pl.when(pl.program_id(2) == 0)
    def _():
        s_scr[...] = jnp.zeros(s_scr.shape, F32)

    ri = lax.broadcasted_iota(jnp.int32, (chunk, chunk), 0)
    ci = lax.broadcasted_iota(jnp.int32, (chunk, chunk), 1)
    ltri = jnp.where(ri >= ci, 1.0, 0.0).astype(BF16)

    lw = lw_ref[0]
    cl = _dot_sel_x(ltri, lw)
    mid = chunk // 2 - 1
    clm = cl[mid:mid + 1, :]
    cle = cl[chunk - 1:chunk, :]
    w_mid = jnp.exp(cl - clm)
    w_mid_prev = jnp.exp(cl - lw - clm)
    w_mid_inv = jnp.exp(clm - cl)
    w_abs = jnp.exp(cl)
    w_abs_prev = jnp.exp(cl - lw)
    w_end = jnp.exp(cle - cl)
    w_tot = jnp.exp(cle)

    r = r_ref[0]
    k = k_ref[0]
    v = v_ref[0]
    a = -kn_ref[0]
    b = b_ref[0]
    a_mid = a * w_mid_prev
    r_mid = r * w_mid
    b_mid = b * w_mid_inv
    k_mid = k * w_mid_inv
    a_abs = a * w_abs_prev
    r_abs = r * w_abs
    b_end = b * w_end
    k_end = k * w_end

    lane = lax.broadcasted_iota(jnp.int32, (chunk, LANES), 1)
    first = lane < RWKV_HEAD_SIZE

    def stack(x):
        return jnp.concatenate([jnp.where(first, x, 0.0), jnp.where(first, 0.0, x)], axis=0)

    rr = lax.broadcasted_iota(jnp.int32, (c2, c2), 0)
    cc = lax.broadcasted_iota(jnp.int32, (c2, c2), 1)
    same = (rr >= chunk) == (cc >= chunk)
    strict = same & (rr > cc)
    incl = same & (rr >= cc)
    eye = jnp.where(rr == cc, 1.0, 0.0)

    ps = range(pairs)
    sls = [slice(p * LANES, (p + 1) * LANES) for p in ps]
    bf = lambda x: x.astype(BF16)
    a_m = [bf(stack(a_mid[:, sl])) for sl in sls]
    r_m = [bf(stack(r_mid[:, sl])) for sl in sls]
    b_m = [bf(stack(b_mid[:, sl])) for sl in sls]
    k_m = [bf(stack(k_mid[:, sl])) for sl in sls]
    v_s = [bf(stack(v[:, sl])) for sl in sls]

    l_ab = [jnp.where(strict, _bdot_nt(a_m[p], b_m[p]), 0.0) for p in ps]
    l_ak = [jnp.where(strict, _bdot_nt(a_m[p], k_m[p]), 0.0) for p in ps]
    m_rb = [bf(jnp.where(incl, _bdot_nt(r_m[p], b_m[p]), 0.0)) for p in ps]
    m_rk = [bf(jnp.where(incl, _bdot_nt(r_m[p], k_m[p]), 0.0)) for p in ps]

    t_inv = [eye + l_ab[p] for p in ps]
    lp = l_ab
    for _ in range(max(chunk.bit_length() - 2, 0)):
        lp = [_bdot(lp[p], lp[p]) for p in ps]
        t_inv = [t_inv[p] + _bdot(lp[p], t_inv[p]) for p in ps]
    t_inv = [bf(t) for t in t_inv]

    lv = [_bdot(l_ak[p], v_s[p]) for p in ps]
    a_p = [bf(_bdot(t_inv[p], stack(a_abs[:, sls[p]]))) for p in ps]
    v_p = [bf(_bdot(t_inv[p], lv[p])) for p in ps]
    r_p = [stack(r_abs[:, sls[p]]) + _bdot(m_rb[p], a_p[p]) for p in ps]
    y_p = [_bdot(m_rb[p], v_p[p]) + _bdot(m_rk[p], v_s[p]) for p in ps]

    s_old = [s_scr[p] for p in ps]
    y_s = [_bdot_nt(r_p[p], s_old[p]) + y_p[p] for p in ps]
    for p in ps:
        y_ref[0, :, sls[p]] = y_s[p][:chunk] + y_s[p][chunk:]
    b_e = [bf(stack(b_end[:, sl])) for sl in sls]
    k_e = [bf(stack(k_end[:, sl])) for sl in sls]
    ab = [_bdot_tn(a_p[p], b_e[p]) for p in ps]
    for p in ps:
        s_scr[p] = (s_old[p] * w_tot[:, sls[p]] + _bdot(s_old[p], ab[p])
                    + _bdot_tn(v_p[p], b_e[p]) + _bdot_tn(v_s[p], k_e[p]))


def rwkv_scan(r, k, v, lw, kn, b, chunk, pairs):
    bsz, s, w = r.shape
    pw = pairs * LANES
    spec = pl.BlockSpec((1, chunk, pw), lambda bi, hi, ci: (bi, ci, hi))
    kern = functools.partial(_rwkv_scan_kernel, chunk=chunk, pairs=pairs)
    return pl.pallas_call(
        kern,
        grid=(bsz, w // pw, s // chunk),
        in_specs=[spec] * 6,
        out_specs=spec,
        out_shape=jax.ShapeDtypeStruct((bsz, s, w), F32),
        scratch_shapes=[pltpu.VMEM((pairs, LANES, LANES), F32)],
        compiler_params=_cparams(("parallel", "parallel", "arbitrary")),
        name="rwkv_scan",
    )(r, k, v, lw, kn, b)


def _rwkv_out(y, bonus, g, lnx_g, lnx_b, sel, selt):
    inv_n = 1.0 / RWKV_HEAD_SIZE
    mu = _dot_x_sel(_dot_x_sel(y, sel) * inv_n, selt)
    yc = y - mu
    var = _dot_x_sel(_dot_x_sel(yc * yc, sel) * inv_n, selt)
    yn = yc * lax.rsqrt(var + RWKV_GN_EPS) * lnx_g + lnx_b
    return (yn + bonus) * g


def _route(logits):
    lane = lax.broadcasted_iota(jnp.int32, logits.shape, 1)
    big = jnp.int32(4 * LANES)
    gmask = lane < N_GROUPS
    gl = jnp.where(gmask, logits, NEG_INF)
    gmax = jnp.max(gl, axis=1, keepdims=True)
    gidx = jnp.min(jnp.where(gl == gmax, lane, big), axis=1, keepdims=True)
    gtop = 1.0 / jnp.sum(jnp.where(gmask, jnp.exp(gl - gmax), 0.0), axis=1, keepdims=True)
    lo = N_GROUPS + EXPERTS_PER_GROUP * gidx
    emask = (lane >= lo) & (lane < lo + EXPERTS_PER_GROUP)
    el = jnp.where(emask, logits, NEG_INF)
    e1 = jnp.max(el, axis=1, keepdims=True)
    i1 = jnp.min(jnp.where(el == e1, lane, big), axis=1, keepdims=True)
    el2 = jnp.where(lane == i1, NEG_INF, el)
    e2 = jnp.max(el2, axis=1, keepdims=True)
    i2 = jnp.min(jnp.where(el2 == e2, lane, big), axis=1, keepdims=True)
    t = jnp.exp(e2 - e1)
    w1 = gtop / (1.0 + t)
    w2 = gtop * t / (1.0 + t)
    id1 = (i1 - N_GROUPS).astype(F32)
    id2 = (i2 - N_GROUPS).astype(F32)
    return jnp.where(lane == 0, id1, jnp.where(lane == 1, id2, jnp.where(lane == 2, w1,
                                                                         jnp.where(lane == 3, w2, 0.0))))


def _outproj_kernel(yd_ref, ys_ref, bonus_ref, gate_ref, lg_ref, lb_ref, sel_ref, selt_ref, h_ref,
                    w1_ref, w2_ref, g_ref, b_ref, rw_hi_ref, rw_lo_ref, rb_ref, h1_ref, info_ref):
    yr = _rwkv_out(ys_ref[...], bonus_ref[...], gate_ref[...], lg_ref[...], lb_ref[...],
                   sel_ref[...], selt_ref[...])
    mix = (jnp.dot(yd_ref[...], w1_ref[...], preferred_element_type=F32)
           + jnp.dot(yr.astype(BF16), w2_ref[...], preferred_element_type=F32))
    h1 = _layer_norm(DEEPNORM_ALPHA * h_ref[...] + mix, g_ref[...], b_ref[...])
    h1_ref[...] = h1
    hi = h1.astype(BF16)
    lo = (h1 - hi.astype(F32)).astype(BF16)
    d = lambda a, b: jnp.dot(a, b, preferred_element_type=F32)
    logits = d(hi, rw_hi_ref[...]) + d(lo, rw_hi_ref[...]) + d(hi, rw_lo_ref[...]) + rb_ref[...]
    info_ref[...] = _route(logits)


def outproj_ln_route(yd, ys, bonus, gate, lnx_g, lnx_b, sel, selt, h, w_out_a, w_out_b, g, b,
                     rw_hi, rw_lo, rb, tm):
    t, d = h.shape
    ka = yd.shape[1]
    kb = ys.shape[1]
    full = lambda shp: pl.BlockSpec(shp, lambda i: (0, 0))
    tile = lambda n: pl.BlockSpec((tm, n), lambda i: (i, 0))
    return pl.pallas_call(
        _outproj_kernel,
        grid=(t // tm,),
        in_specs=[
            tile(ka), tile(kb), tile(kb), tile(kb), full((1, kb)), full((1, kb)),
            full((kb, LANES)), full((LANES, kb)), tile(d),
            full((ka, d)), full((kb, d)), full((1, d)), full((1, d)),
            full((d, ROUTER_LANES)), full((d, ROUTER_LANES)), full((1, ROUTER_LANES)),
        ],
        out_specs=[
            pl.BlockSpec((tm, d), lambda i: (i, 0)),
            pl.BlockSpec((tm, ROUTER_LANES), lambda i: (i, 0)),
        ],
        out_shape=[jax.ShapeDtypeStruct((t, d), F32), jax.ShapeDtypeStruct((t, ROUTER_LANES), F32)],
        compiler_params=_cparams(("parallel",)),
        name="outproj_ln_route",
    )(yd, ys, bonus, gate, lnx_g.reshape(1, kb), lnx_b.reshape(1, kb), sel, selt, h, w_out_a, w_out_b,
      g.reshape(1, d), b.reshape(1, d), rw_hi, rw_lo, rb)


def _moe_rank_kernel(e_ref, rank_ref, cnt_ref, run_scr, *, tm):
    @pl.when(pl.program_id(0) == 0)
    def _():
        run_scr[...] = jnp.zeros(run_scr.shape, F32)

    lane = lax.broadcasted_iota(jnp.int32, (tm, LANES), 1)
    onehot = e_ref[...] == lane
    oh = jnp.where(onehot, 1.0, 0.0)
    ri = lax.broadcasted_iota(jnp.int32, (tm, tm), 0)
    ci = lax.broadcasted_iota(jnp.int32, (tm, tm), 1)
    before = jnp.where(ri > ci, 1.0, 0.0).astype(BF16)
    earlier = jnp.dot(before, oh.astype(BF16), preferred_element_type=F32) + run_scr[...]
    rank_ref[...] = jnp.sum(jnp.where(onehot, earlier, 0.0), axis=1, keepdims=True).astype(jnp.int32)
    run_scr[...] += jnp.sum(oh, axis=0, keepdims=True)
    cnt_ref[...] = run_scr[...]


def moe_rank(e_col, tm):
    n = e_col.shape[0]
    return pl.pallas_call(
        functools.partial(_moe_rank_kernel, tm=tm),
        grid=(n // tm,),
        in_specs=[pl.BlockSpec((tm, 1), lambda i: (i, 0))],
        out_specs=[pl.BlockSpec((tm, 1), lambda i: (i, 0)), pl.BlockSpec((1, LANES), lambda i: (0, 0))],
        out_shape=[jax.ShapeDtypeStruct((n, 1), jnp.int32), jax.ShapeDtypeStruct((1, LANES), F32)],
        scratch_shapes=[pltpu.VMEM((1, LANES), F32)],
        compiler_params=_cparams(("arbitrary",)),
        name="moe_rank",
    )(e_col)


def _row_copy(src_hbm, row, dst, r, sem):
    return pltpu.make_async_copy(src_hbm.at[pl.ds(row, 1)], dst.at[pl.ds(r, 1)], sem)


def _moe_expert_kernel(src_ref, te_ref, nact_ref, x_hbm, wg_ref, wu_ref, wd_ref, o_ref,
                       xbuf0, xbuf1, sem, *, tm):
    j = pl.program_id(0)
    n_act = nact_ref[0]
    bufs = (xbuf0, xbuf1)

    def start_gather(tile, into):
        for r in range(tm):
            _row_copy(x_hbm, src_ref[tile * tm + r], bufs[into], r, sem.at[into]).start()

    def wait_gather(into):
        for r in range(tm):
            _row_copy(x_hbm, 0, bufs[into], r, sem.at[into]).wait()

    @pl.when(j == 0)
    def _():
        start_gather(0, 0)

    for slot in range(2):
        @pl.when((j < n_act) & (j % 2 == slot))
        def _():
            wait_gather(slot)
            start_gather(jnp.minimum(j + 1, n_act - 1), 1 - slot)
            x = bufs[slot][...].astype(BF16)
            hg = jnp.dot(x, wg_ref[0].astype(BF16), preferred_element_type=F32)
            hu = jnp.dot(x, wu_ref[0].astype(BF16), preferred_element_type=F32)
            hid = hg * _sigmoid(hg) * hu
            o_ref[...] = jnp.dot(hid.astype(BF16), wd_ref[0].astype(BF16), preferred_element_type=F32)

        @pl.when((j == n_act - 1) & (j % 2 == slot))
        def _():
            wait_gather(1 - slot)

    @pl.when(j >= n_act)
    def _():
        o_ref[...] = jnp.zeros(o_ref.shape, F32)


def moe_experts(src_tok, tile_expert, n_active, x, wg, wu, wd, n_rows, tm):
    d = x.shape[1]
    f = wg.shape[-1]
    return pl.pallas_call(
        functools.partial(_moe_expert_kernel, tm=tm),
        grid_spec=pltpu.PrefetchScalarGridSpec(
            num_scalar_prefetch=3,
            grid=(n_rows // tm,),
            in_specs=[
                pl.BlockSpec(memory_space=pl.ANY),
                pl.BlockSpec((1, d, f), lambda j, src, te, na: (te[j], 0, 0)),
                pl.BlockSpec((1, d, f), lambda j, src, te, na: (te[j], 0, 0)),
                pl.BlockSpec((1, f, d), lambda j, src, te, na: (te[j], 0, 0)),
            ],
            out_specs=pl.BlockSpec((tm, d), lambda j, src, te, na: (j, 0)),
            scratch_shapes=[pltpu.VMEM((tm, d), F32), pltpu.VMEM((tm, d), F32),
                            pltpu.SemaphoreType.DMA((2,))],
        ),
        out_shape=jax.ShapeDtypeStruct((n_rows, d), F32),
        compiler_params=pltpu.CompilerParams(dimension_semantics=("arbitrary",),
                                             vmem_limit_bytes=MOE_VMEM_LIMIT),
        name="moe_experts",
    )(src_tok, tile_expert, n_active, x, wg, wu, wd)


def _final_kernel(p1_ref, p2_ref, y_hbm, h1_ref, info_ref, p_ref, g_ref, b_ref, pw_ref, pg_ref, gw_ref,
                  o_ref, ybuf0, ybuf1, sem, *, tm, n_tiles):
    i = pl.program_id(0)
    bufs = (ybuf0, ybuf1)

    def start_gather(tile, into):
        for r in range(tm):
            _row_copy(y_hbm, p1_ref[tile * tm + r], bufs[into].at[0], r, sem.at[into]).start()
            _row_copy(y_hbm, p2_ref[tile * tm + r], bufs[into].at[1], r, sem.at[into]).start()

    def wait_gather(into):
        for r in range(tm):
            _row_copy(y_hbm, 0, bufs[into].at[0], r, sem.at[into]).wait()
            _row_copy(y_hbm, 0, bufs[into].at[1], r, sem.at[into]).wait()

    @pl.when(i == 0)
    def _():
        start_gather(0, 0)

    for slot in range(2):
        @pl.when(i % 2 == slot)
        def _():
            wait_gather(slot)
            start_gather(jnp.minimum(i + 1, n_tiles - 1), 1 - slot)
            info = info_ref[...]
            ffn = info[:, 2:3] * bufs[slot][0] + info[:, 3:4] * bufs[slot][1]
            h2 = _layer_norm(DEEPNORM_ALPHA * h1_ref[...] + ffn, g_ref[...], b_ref[...])
            pe = jnp.dot(p_ref[...].astype(BF16), pw_ref[...], preferred_element_type=F32)
            ple = pe * lax.rsqrt(jnp.mean(pe * pe, axis=-1, keepdims=True) + LN_EPS) * pg_ref[...]
            gate = _sigmoid(jnp.dot(h2.astype(BF16), gw_ref[...], preferred_element_type=F32))
            o_ref[...] = h2 + gate * ple

        @pl.when((i == n_tiles - 1) & (i % 2 == slot))
        def _():
            wait_gather(1 - slot)


def final_block(pos1, pos2, ys, h1, info, p, g, b, ple_w, ple_g, gate_w, tm):
    t, d = h1.shape
    dp = p.shape[1]
    tile = lambda n: pl.BlockSpec((tm, n), lambda i, p1, p2: (i, 0))
    full = lambda shp: pl.BlockSpec(shp, lambda i, p1, p2: (0, 0))
    return pl.pallas_call(
        functools.partial(_final_kernel, tm=tm, n_tiles=t // tm),
        grid_spec=pltpu.PrefetchScalarGridSpec(
            num_scalar_prefetch=2,
            grid=(t // tm,),
            in_specs=[pl.BlockSpec(memory_space=pl.ANY), tile(d), tile(ROUTER_LANES), tile(dp),
                      full((1, d)), full((1, d)), full((dp, d)), full((1, d)), full((d, d))],
            out_specs=tile(d),
            scratch_shapes=[pltpu.VMEM((2, tm, d), F32), pltpu.VMEM((2, tm, d), F32),
                            pltpu.SemaphoreType.DMA((2,))],
        ),
        out_shape=jax.ShapeDtypeStruct((t, d), F32),
        compiler_params=_cparams(("arbitrary",)),
        name="final_block",
    )(pos1, pos2, ys, h1, info, p, g.reshape(1, d), b.reshape(1, d), ple_w, ple_g.reshape(1, d), gate_w)


def _dispatch_plan(e_all, rank, counts, n_tok, tm, n_rows):
    n_tiles = n_rows // tm
    cnt = counts[0, :N_EXPERTS].astype(jnp.int32)
    padded = ((cnt + tm - 1) // tm) * tm
    ends = jnp.cumsum(padded)
    starts = ends - padded
    pos = starts[e_all] + rank
    src_tok = jnp.zeros((n_rows,), jnp.int32).at[pos].set(
        jnp.arange(2 * n_tok, dtype=jnp.int32) % n_tok, unique_indices=True)
    n_active = ends[-1] // tm
    tile_ids = jnp.arange(n_tiles, dtype=jnp.int32)
    tile_expert = jnp.searchsorted(ends, jnp.minimum(tile_ids, n_active - 1) * tm, side="right")
    tile_expert = jnp.minimum(tile_expert, N_EXPERTS - 1).astype(jnp.int32)
    return pos, src_tok, tile_expert, n_active.astype(jnp.int32).reshape(1)


def _t5_bucket(rel):
    n = jnp.maximum(rel, 0)
    max_exact = REL_BUCKETS // 2
    nf = jnp.maximum(n, 1).astype(F32)
    large = max_exact + (jnp.log(nf / max_exact) / math.log(REL_MAX_DIST / max_exact)
                         * (REL_BUCKETS - max_exact)).astype(jnp.int32)
    large = jnp.minimum(large, REL_BUCKETS - 1)
    return jnp.where(n < max_exact, n, large)


def _bias_tables(rel_bias, tq):
    n = 3 * tq
    rel = 2 * tq - 1 - jnp.arange(n, dtype=jnp.int32)
    g = rel_bias[_t5_bucket(rel)].astype(F32).T
    skew = jnp.tile(g, (1, tq))[:, :tq * (n - 1)].reshape(-1, tq, n - 1)
    tiles = jnp.stack([skew[:, :, 2 * tq - 1:3 * tq - 1], skew[:, :, tq - 1:2 * tq - 1]], axis=1)
    far = rel_bias[_t5_bucket(jnp.full((), 2 * tq, jnp.int32))].astype(F32)
    return tiles, far


def _head_selectors():
    lane = jnp.arange(RWKV_WIDTH, dtype=jnp.int32) // RWKV_HEAD_SIZE
    sel = (lane[:, None] == jnp.arange(LANES, dtype=jnp.int32)[None, :]).astype(BF16)
    return sel, sel.T


def kernel(x, p, ln_in_g, ln_in_b, rel_bias, w_in, diff_lam_q1, diff_lam_k1, diff_lam_q2, diff_lam_k2, diff_subln_g, rwkv_mu, rwkv_w0, rwkv_w2, rwkv_a0, rwkv_a2, rwkv_g2, rwkv_k_k, rwkv_k_a, rwkv_r_k, rwkv_lnx_g, rwkv_lnx_b, w_out, ln1_g, ln1_b, router_group_w, router_group_b, router_expert_w, router_expert_b, moe_w_gate, moe_w_up, moe_w_down, ln2_g, ln2_b, ple_w, ple_norm_g, ple_gate_w):
    bsz, seq, d = x.shape
    t = bsz * seq
    i = 0
    lambda_init = 0.8 - 0.6 * math.exp(-0.3 * i)
    tq = 256

    x2 = x.reshape(t, d)
    w_in_b = w_in[i].astype(BF16)
    w_qkv = w_in_b[:, :O_RWKV]
    w_rw = jnp.pad(w_in_b[:, O_RWKV:], ((0, 0), (0, RWKV_PROJ_PAD - RWKV_PROJ)))
    h, hb = layer_norm_rows(x2, ln_in_g, ln_in_b, 512)
    qkv = matmul(hb, w_qkv, BF16, 1024, 512)
    pr = matmul(hb, w_rw, F32, 1024, 512)

    lam = (jnp.exp(jnp.sum(diff_lam_q1[i].astype(F32) * diff_lam_k1[i]))
           - jnp.exp(jnp.sum(diff_lam_q2[i].astype(F32) * diff_lam_k2[i])) + lambda_init)
    bias_tiles, far_bias = _bias_tables(rel_bias, tq)
    y_diff = diff_attention(qkv.reshape(bsz, seq, O_RWKV), bias_tiles, far_bias, lam.reshape(1),
                            diff_subln_g[i], lambda_init, tq, 8)

    sel, selt = _head_selectors()
    mu_p = jnp.pad(rwkv_mu[i], (0, RWKV_PROJ_PAD - RWKV_PROJ))
    zeros64 = jnp.zeros((LORA_DECAY, RWKV_WIDTH), F32)
    w2p = jnp.concatenate([rwkv_w2[i], zeros64], axis=0).astype(BF16)
    a2p = jnp.concatenate([zeros64, rwkv_a2[i]], axis=0).astype(BF16)
    g2p = jnp.pad(rwkv_g2[i], ((0, 2 * LANES - LORA_GATE), (0, 0))).astype(BF16)
    r, k2, v, lw, kn, bb, g, bonus = rwkv_prep(
        pr, mu_p, rwkv_w0[i], w2p, rwkv_a0[i], a2p, g2p, rwkv_k_k[i], rwkv_k_a[i],
        rwkv_r_k[i].reshape(-1), sel, selt, seq, 256)
    sh = lambda a: a.reshape(bsz, seq, RWKV_WIDTH)
    y_scan = rwkv_scan(sh(r), sh(k2), sh(v), sh(lw), sh(kn), sh(bb), 64, 8)

    w_out_b = w_out[i].astype(BF16)
    rw = jnp.concatenate([router_group_w[i], router_expert_w[i]], axis=1)
    rw = jnp.pad(rw, ((0, 0), (0, ROUTER_LANES - rw.shape[1])))
    rw_hi = rw.astype(BF16)
    rw_lo = (rw - rw_hi.astype(F32)).astype(BF16)
    rb = jnp.pad(jnp.concatenate([router_group_b[i], router_expert_b[i]]),
                 (0, ROUTER_LANES - N_GROUPS - N_EXPERTS)).reshape(1, ROUTER_LANES)
    h1, info = outproj_ln_route(y_diff.reshape(t, DIFF_WIDTH), y_scan.reshape(t, RWKV_WIDTH), bonus, g,
                                rwkv_lnx_g[i], rwkv_lnx_b[i], sel, selt, h, w_out_b[:DIFF_WIDTH],
                                w_out_b[DIFF_WIDTH:], ln1_g[i], ln1_b[i], rw_hi, rw_lo, rb, 256)

    tm_e = MOE_ROW_TILE
    n_rows = 2 * t + N_EXPERTS * tm_e
    e_all = jnp.concatenate([info[:, 0], info[:, 1]]).astype(jnp.int32)
    rank, counts = moe_rank(e_all.reshape(-1, 1), 1024)
    pos, src_tok, tile_expert, n_active = _dispatch_plan(e_all, rank[:, 0], counts, t, tm_e, n_rows)
    f = moe_w_gate.shape[-1]
    ys = moe_experts(src_tok, tile_expert, n_active, h1, moe_w_gate[i].reshape(N_EXPERTS, d, f),
                     moe_w_up[i].reshape(N_EXPERTS, d, f), moe_w_down[i].reshape(N_EXPERTS, f, d),
                     n_rows, tm_e)

    out = final_block(pos[:t], pos[t:], ys, h1, info, p[i].reshape(t, -1), ln2_g[i], ln2_b[i],
                      ple_w[i].astype(BF16), ple_norm_g[i], ple_gate_w[i].astype(BF16), 256)
    return out.reshape(bsz, seq, d)
```

```python
import functools
import math

import jax
import jax.numpy as jnp
from jax import lax
from jax.experimental import pallas as pl
from jax.experimental.pallas import tpu as pltpu

F32 = jnp.float32
BF16 = jnp.bfloat16

DIFF_HEADS = 8
DIFF_HEAD_DIM = 64
DIFF_V_DIM = 128
DIFF_WIDTH = DIFF_HEADS * DIFF_V_DIM
DIFF_QK_WIDTH = DIFF_HEADS * 2 * DIFF_HEAD_DIM
RWKV_HEADS = 16
RWKV_HEAD_SIZE = 64
RWKV_WIDTH = RWKV_HEADS * RWKV_HEAD_SIZE
LORA_DECAY = 64
LORA_AAA = 64
LORA_GATE = 160
RWKV_PROJ = 3 * RWKV_WIDTH + LORA_DECAY + LORA_AAA + LORA_GATE
O_RWKV = 2 * DIFF_QK_WIDTH + DIFF_WIDTH
REL_BUCKETS = 32
REL_MAX_DIST = 128
N_GROUPS = 4
EXPERTS_PER_GROUP = 8
N_EXPERTS = N_GROUPS * EXPERTS_PER_GROUP
LN_EPS = 1e-5
RWKV_GN_EPS = 64e-5
NEG_INF = -1e30
DEPTH = 1
DEEPNORM_ALPHA = (2 * DEPTH) ** 0.25

LANES = 128
RWKV_PROJ_PAD = 3584
ROUTER_LANES = LANES
VMEM_LIMIT = 48 * 1024 * 1024
MOE_VMEM_LIMIT = 56 * 1024 * 1024
MOE_ROW_TILE = 256


def _cparams(sem):
    return pltpu.CompilerParams(dimension_semantics=sem, vmem_limit_bytes=VMEM_LIMIT)


def _bdot(a, b):
    return jnp.dot(a.astype(BF16), b.astype(BF16), preferred_element_type=F32)


def _bdot_nt(a, b):
    return lax.dot_general(a.astype(BF16), b.astype(BF16), (((1,), (1,)), ((), ())),
                           preferred_element_type=F32)


def _bdot_tn(a, b):
    return lax.dot_general(a.astype(BF16), b.astype(BF16), (((0,), (0,)), ((), ())),
                           preferred_element_type=F32)


def _split3(x):
    hi = x.astype(BF16)
    r1 = x - hi.astype(F32)
    mid = r1.astype(BF16)
    lo = (r1 - mid.astype(F32)).astype(BF16)
    return hi, mid, lo


def _head_sum(x):
    lane = lax.broadcasted_iota(jnp.int32, (x.shape[0], LANES), 1)
    first = lane < RWKV_HEAD_SIZE
    cols = []
    for c in range(x.shape[1] // LANES):
        blk = x[:, c * LANES:(c + 1) * LANES]
        lo = jnp.sum(jnp.where(first, blk, 0.0), axis=1, keepdims=True)
        hi = jnp.sum(jnp.where(first, 0.0, blk), axis=1, keepdims=True)
        cols.append(jnp.where(first, lo, hi))
    return jnp.concatenate(cols, axis=1)


def _dot_sel_x(sel, x):
    hi, mid, lo = _split3(x)
    d = lambda p: jnp.dot(sel, p, preferred_element_type=F32)
    return d(hi) + d(mid) + d(lo)


def _layer_norm(x, g, b):
    mu = jnp.mean(x, axis=-1, keepdims=True)
    xc = x - mu
    var = jnp.mean(xc * xc, axis=-1, keepdims=True)
    return xc * lax.rsqrt(var + LN_EPS) * g + b


def _sigmoid(z):
    return 1.0 / (1.0 + jnp.exp(-z))


def _ln_kernel(x_ref, g_ref, b_ref, h_ref, hb_ref):
    xn = _layer_norm(x_ref[...], g_ref[...], b_ref[...])
    h_ref[...] = xn
    hb_ref[...] = xn.astype(BF16)


def layer_norm_rows(x, g, b, tm):
    m, d = x.shape
    tile = pl.BlockSpec((tm, d), lambda i: (i, 0))
    row = pl.BlockSpec((1, d), lambda i: (0, 0))
    return pl.pallas_call(
        _ln_kernel,
        grid=(m // tm,),
        in_specs=[tile, row, row],
        out_specs=[tile, tile],
        out_shape=[jax.ShapeDtypeStruct((m, d), F32), jax.ShapeDtypeStruct((m, d), BF16)],
        compiler_params=_cparams(("parallel",)),
        name="layer_norm_rows",
    )(x, g.reshape(1, d), b.reshape(1, d))


def _mm_kernel(x_ref, w_ref, o_ref):
    o_ref[...] = jnp.dot(x_ref[...], w_ref[...], preferred_element_type=F32).astype(o_ref.dtype)


def matmul(x, w, out_dtype, tm, tn):
    m, k = x.shape
    n = w.shape[1]
    return pl.pallas_call(
        _mm_kernel,
        grid=(m // tm, n // tn),
        in_specs=[pl.BlockSpec((tm, k), lambda i, j: (i, 0)), pl.BlockSpec((k, tn), lambda i, j: (0, j))],
        out_specs=pl.BlockSpec((tm, tn), lambda i, j: (i, j)),
        out_shape=jax.ShapeDtypeStruct((m, n), out_dtype),
        compiler_params=_cparams(("parallel", "parallel")),
        name="matmul",
    )(x, w)


def _attn_kernel(lam_ref, far_ref, q_ref, k_ref, v_ref, bias_ref, g_ref, o_ref,
                 m_scr, l_scr, acc_scr, *, tq, hps, scale, post_scale):
    hp = pl.program_id(1)
    qi = pl.program_id(2)
    lane = lax.broadcasted_iota(jnp.int32, (tq, LANES), 1)
    q_maps = []
    for hh in range(hps):
        qs = (q_ref[0, :, hh * LANES:(hh + 1) * LANES].astype(F32) * scale).astype(BF16)
        zero = jnp.zeros_like(qs)
        q_maps += [jnp.where(lane < DIFF_HEAD_DIM, qs, zero), jnp.where(lane >= DIFF_HEAD_DIM, qs, zero)]

    m_scr[...] = jnp.full(m_scr.shape, NEG_INF, F32)
    l_scr[...] = jnp.zeros(l_scr.shape, F32)
    acc_scr[...] = jnp.zeros(acc_scr.shape, F32)

    def step(kstart, bias_index, causal):
        for hh in range(hps):
            kb = k_ref[0, pl.ds(kstart, tq), hh * LANES:(hh + 1) * LANES]
            vb = v_ref[0, pl.ds(kstart, tq), hh * LANES:(hh + 1) * LANES]
            for c in range(2 * hh, 2 * hh + 2):
                s = lax.dot_general(q_maps[c], kb, (((1,), (1,)), ((), ())), preferred_element_type=F32)
                m_old = m_scr[c]
                if bias_index is None:
                    far = far_ref[hp * hps + hh]
                    m_new = jnp.maximum(m_old, jnp.max(s, axis=1, keepdims=True) + far)
                    shift = m_new - far
                else:
                    s = s + bias_ref[hh, bias_index]
                    if causal:
                        ri = lax.broadcasted_iota(jnp.int32, (tq, tq), 0)
                        ci = lax.broadcasted_iota(jnp.int32, (tq, tq), 1)
                        s = jnp.where(ri >= ci, s, NEG_INF)
                    m_new = jnp.maximum(m_old, jnp.max(s, axis=1, keepdims=True))
                    shift = m_new
                p = jnp.exp(s - jnp.concatenate([shift] * (tq // LANES), axis=1))
                corr = jnp.exp(m_old - m_new)
                l_scr[c] = corr * l_scr[c] + jnp.sum(p, axis=1, keepdims=True)
                acc_scr[c] = corr * acc_scr[c] + jnp.dot(p.astype(BF16), vb, preferred_element_type=F32)
                m_scr[c] = m_new

    def far_body(kb, carry):
        step(pl.multiple_of(kb * tq, tq), None, False)
        return carry

    lax.fori_loop(0, jnp.maximum(qi - 1, 0), far_body, 0)

    @pl.when(qi >= 1)
    def _():
        step(pl.multiple_of((qi - 1) * tq, tq), 1, False)

    step(pl.multiple_of(qi * tq, tq), 0, True)

    for hh in range(hps):
        c = 2 * hh
        o = acc_scr[c] / l_scr[c] - lam_ref[0] * (acc_scr[c + 1] / l_scr[c + 1])
        ms = jnp.mean(o * o, axis=1, keepdims=True)
        o_ref[0, :, hh * LANES:(hh + 1) * LANES] = (
            o * lax.rsqrt(ms + LN_EPS) * g_ref[...] * post_scale).astype(o_ref.dtype)


def diff_attention(qkv, bias_tiles, far_bias, lam, subln_g, lambda_init, tq, hps):
    bsz, s, _ = qkv.shape
    kern = functools.partial(_attn_kernel, tq=tq, hps=hps, scale=DIFF_HEAD_DIM ** -0.5,
                             post_scale=1.0 - lambda_init)
    hw = hps * LANES
    nqk = DIFF_QK_WIDTH // hw
    return pl.pallas_call(
        kern,
        grid=(bsz, DIFF_HEADS // hps, s // tq),
        in_specs=[
            pl.BlockSpec(memory_space=pltpu.SMEM),
            pl.BlockSpec(memory_space=pltpu.SMEM),
            pl.BlockSpec((1, tq, hw), lambda b, h, i: (b, i, h)),
            pl.BlockSpec((1, s, hw), lambda b, h, i: (b, 0, nqk + h)),
            pl.BlockSpec((1, s, hw), lambda b, h, i: (b, 0, 2 * nqk + h)),
            pl.BlockSpec((hps, 2, tq, tq), lambda b, h, i: (h, 0, 0, 0)),
            pl.BlockSpec((1, LANES), lambda b, h, i: (0, 0)),
        ],
        out_specs=pl.BlockSpec((1, tq, hw), lambda b, h, i: (b, i, h)),
        out_shape=jax.ShapeDtypeStruct((bsz, s, DIFF_WIDTH), BF16),
        scratch_shapes=[pltpu.VMEM((2 * hps, tq, LANES), F32), pltpu.VMEM((2 * hps, tq, LANES), F32),
                        pltpu.VMEM((2 * hps, tq, LANES), F32)],
        compiler_params=_cparams(("parallel", "parallel", "arbitrary")),
        name="diff_attention",
    )(lam, far_bias, qkv, qkv, qkv, bias_tiles, subln_g.reshape(1, LANES))


def _rwkv_prep_kernel(x_ref, prev_ref, mu_ref, w0_ref, w2_ref, a0_ref, a2_ref, g2_ref,
                      kk_ref, ka_ref, rk_ref,
                      r_out, k_out, v_out, lw_out, kn_out, b_out, g_out, bonus_out, *, tm, seq):
    i = pl.program_id(0)
    x = x_ref[...]
    w = RWKV_WIDTH
    row = lax.broadcasted_iota(jnp.int32, x.shape, 0)
    is_start = (i * tm) % seq == 0
    last_prev = jnp.where(is_start, 0.0, prev_ref[7:8, :])
    prev = jnp.where(row == 0, last_prev, pltpu.roll(x, 1, 0))
    xs = x + (prev - x) * mu_ref[...]

    r = xs[:, 0:w]
    k = xs[:, w:2 * w]
    v = xs[:, 2 * w:3 * w]
    xwa = xs[:, 3 * w:3 * w + 2 * LORA_DECAY]
    xg = xs[:, 3 * w + LANES:3 * w + LANES + 2 * LANES]

    z = w0_ref[...] + _bdot(jnp.tanh(xwa), w2_ref[...])
    softplus_neg_z = jnp.maximum(-z, 0.0) + jnp.log(1.0 + jnp.exp(-jnp.abs(z)))
    lw_out[...] = -jnp.exp(-softplus_neg_z - 0.5)
    a_lr = _sigmoid(a0_ref[...] + _bdot(xwa, a2_ref[...]))
    g_out[...] = _bdot(_sigmoid(xg), g2_ref[...])

    kk = k * kk_ref[...]
    kn = kk / jnp.maximum(jnp.sqrt(_head_sum(kk * kk)), 1e-12)
    k2 = k * (1.0 + (a_lr - 1.0) * ka_ref[...])
    r_out[...] = r
    k_out[...] = k2
    v_out[...] = v
    kn_out[...] = kn
    b_out[...] = kn * a_lr
    bonus_out[...] = _head_sum(r * k2 * rk_ref[...]) * v


def rwkv_prep(pr, mu, w0, w2p, a0, a2p, g2p, k_k, k_a, r_k, seq, tm):
    t, wp = pr.shape
    w = RWKV_WIDTH
    row = lambda a: a.reshape(1, -1)
    full = lambda shp: pl.BlockSpec(shp, lambda i: (0, 0))
    outs = [jax.ShapeDtypeStruct((t, w), F32)] * 8
    kern = functools.partial(_rwkv_prep_kernel, tm=tm, seq=seq)
    return pl.pallas_call(
        kern,
        grid=(t // tm,),
        in_specs=[
            pl.BlockSpec((tm, wp), lambda i: (i, 0)),
            pl.BlockSpec((8, wp), lambda i: (jnp.maximum(i * (tm // 8) - 1, 0), 0)),
            full((1, wp)), full((1, w)), full((LANES, w)), full((1, w)), full((LANES, w)),
            full((2 * LANES, w)), full((1, w)), full((1, w)), full((1, w)),
        ],
        out_specs=[pl.BlockSpec((tm, w), lambda i: (i, 0))] * 8,
        out_shape=outs,
        compiler_params=_cparams(("parallel",)),
        name="rwkv_prep",
    )(pr, pr, row(mu), row(w0), w2p, row(a0), a2p, g2p, row(k_k), row(k_a), row(r_k))


def _rwkv_scan_kernel(r_ref, k_ref, v_ref, lw_ref, kn_ref, b_ref, y_ref, s_scr, *, chunk, pairs):
    c2 = 2 * chunk

    @pl.when(pl.program_id(2) == 0)
    def _():
        s_scr[...] = jnp.zeros(s_scr.shape, F32)

    ri = lax.broadcasted_iota(jnp.int32, (chunk, chunk), 0)
    ci = lax.broadcasted_iota(jnp.int32, (chunk, chunk), 1)
    ltri = jnp.where(ri >= ci, 1.0, 0.0).astype(BF16)

    lw = lw_ref[0]
    cl = _dot_sel_x(ltri, lw)
    mid = chunk // 2 - 1
    clm = cl[mid:mid + 1, :]
    cle = cl[chunk - 1:chunk, :]
    w_mid = jnp.exp(cl - clm)
    w_mid_prev = jnp.exp(cl - lw - clm)
    w_mid_inv = jnp.exp(clm - cl)
    w_abs = jnp.exp(cl)
    w_abs_prev = jnp.exp(cl - lw)
    w_end = jnp.exp(cle - cl)
    w_tot = jnp.exp(cle)

    r = r_ref[0]
    k = k_ref[0]
    v = v_ref[0]
    a = -kn_ref[0]
    b = b_ref[0]
    a_mid = a * w_mid_prev
    r_mid = r * w_mid
    b_mid = b * w_mid_inv
    k_mid = k * w_mid_inv
    a_abs = a * w_abs_prev
    r_abs = r * w_abs
    b_end = b * w_end
    k_end = k * w_end

    lane = lax.broadcasted_iota(jnp.int32, (chunk, LANES), 1)
    first = lane < RWKV_HEAD_SIZE

    def stack(x):
        return jnp.concatenate([jnp.where(first, x, 0.0), jnp.where(first, 0.0, x)], axis=0)

    rr = lax.broadcasted_iota(jnp.int32, (c2, c2), 0)
    cc = lax.broadcasted_iota(jnp.int32, (c2, c2), 1)
    same = (rr >= chunk) == (cc >= chunk)
    strict = same & (rr > cc)
    incl = same & (rr >= cc)
    eye = jnp.where(rr == cc, 1.0, 0.0)

    ps = range(pairs)
    sls = [slice(p * LANES, (p + 1) * LANES) for p in ps]
    bf = lambda x: x.astype(BF16)
    cat0 = lambda xs: jnp.concatenate(xs, axis=0)
    cat1 = lambda xs: jnp.concatenate(xs, axis=1)
    v_s = [bf(stack(v[:, sl])) for sl in sls]
    ar_m = [bf(cat0([stack(a_mid[:, sl]), stack(r_mid[:, sl])])) for sl in sls]
    bk_m = [bf(cat0([stack(b_mid[:, sl]), stack(k_mid[:, sl])])) for sl in sls]
    g_all = [_bdot_nt(ar_m[p], bk_m[p]) for p in ps]
    l_ab = [jnp.where(strict, g[:c2, :c2], 0.0) for g in g_all]
    l_ak = [bf(jnp.where(strict, g[:c2, c2:], 0.0)) for g in g_all]
    m_r = [bf(cat1([jnp.where(incl, g[c2:, :c2], 0.0), jnp.where(incl, g[c2:, c2:], 0.0)])) for g in g_all]

    t_inv = [eye + l_ab[p] for p in ps]
    n_fac = max(chunk.bit_length() - 2, 0)
    lp = [_bdot(l_ab[p], l_ab[p]) for p in ps] if n_fac else l_ab
    for _ in range(n_fac - 1):
        x = [_bdot(cat0([t_inv[p], lp[p]]), lp[p]) for p in ps]
        t_inv = [t_inv[p] + x[p][:c2] for p in ps]
        lp = [x[p][c2:] for p in ps]
    if n_fac:
        t_inv = [t_inv[p] + _bdot(t_inv[p], lp[p]) for p in ps]
    t_inv = [bf(t) for t in t_inv]

    lv = [_bdot(l_ak[p], v_s[p]) for p in ps]
    av = [bf(_bdot(t_inv[p], cat1([stack(a_abs[:, sls[p]]), lv[p]]))) for p in ps]
    zero = jnp.zeros((c2, LANES), BF16)
    ry = [_bdot(m_r[p], cat0([av[p], cat1([zero, v_s[p]])])) for p in ps]
    r_p = [stack(r_abs[:, sls[p]]) + ry[p][:, :LANES] for p in ps]

    s_old = [s_scr[p] for p in ps]
    y_s = [_bdot_nt(r_p[p], s_old[p]) + ry[p][:, LANES:] for p in ps]
    for p in ps:
        y_ref[0, :, sls[p]] = y_s[p][:chunk] + y_s[p][chunk:]
    b_e = [bf(stack(b_end[:, sl])) for sl in sls]
    k_e = [bf(stack(k_end[:, sl])) for sl in sls]
    ab = [_bdot_tn(av[p][:, :LANES], b_e[p]) for p in ps]
    q_t = [_bdot_tn(cat0([av[p][:, LANES:], v_s[p]]), cat0([b_e[p], k_e[p]])) for p in ps]
    for p in ps:
        s_scr[p] = s_old[p] * w_tot[:, sls[p]] + _bdot(s_old[p], ab[p]) + q_t[p]


def rwkv_scan(r, k, v, lw, kn, b, chunk, pairs):
    bsz, s, w = r.shape
    pw = pairs * LANES
    spec = pl.BlockSpec((1, chunk, pw), lambda bi, hi, ci: (bi, ci, hi))
    kern = functools.partial(_rwkv_scan_kernel, chunk=chunk, pairs=pairs)
    return pl.pallas_call(
        kern,
        grid=(bsz, w // pw, s // chunk),
        in_specs=[spec] * 6,
        out_specs=spec,
        out_shape=jax.ShapeDtypeStruct((bsz, s, w), F32),
        scratch_shapes=[pltpu.VMEM((pairs, LANES, LANES), F32)],
        compiler_params=_cparams(("parallel", "parallel", "arbitrary")),
        name="rwkv_scan",
    )(r, k, v, lw, kn, b)


def _rwkv_out(y, bonus, g, lnx_g, lnx_b):
    inv_n = 1.0 / RWKV_HEAD_SIZE
    mu = _head_sum(y) * inv_n
    yc = y - mu
    var = _head_sum(yc * yc) * inv_n
    yn = yc * lax.rsqrt(var + RWKV_GN_EPS) * lnx_g + lnx_b
    return (yn + bonus) * g


def _route(logits):
    lane = lax.broadcasted_iota(jnp.int32, logits.shape, 1)
    big = jnp.int32(4 * LANES)
    gmask = lane < N_GROUPS
    gl = jnp.where(gmask, logits, NEG_INF)
    gmax = jnp.max(gl, axis=1, keepdims=True)
    gidx = jnp.min(jnp.where(gl == gmax, lane, big), axis=1, keepdims=True)
    gtop = 1.0 / jnp.sum(jnp.where(gmask, jnp.exp(gl - gmax), 0.0), axis=1, keepdims=True)
    lo = N_GROUPS + EXPERTS_PER_GROUP * gidx
    emask = (lane >= lo) & (lane < lo + EXPERTS_PER_GROUP)
    el = jnp.where(emask, logits, NEG_INF)
    e1 = jnp.max(el, axis=1, keepdims=True)
    i1 = jnp.min(jnp.where(el == e1, lane, big), axis=1, keepdims=True)
    el2 = jnp.where(lane == i1, NEG_INF, el)
    e2 = jnp.max(el2, axis=1, keepdims=True)
    i2 = jnp.min(jnp.where(el2 == e2, lane, big), axis=1, keepdims=True)
    t = jnp.exp(e2 - e1)
    w1 = gtop / (1.0 + t)
    w2 = gtop * t / (1.0 + t)
    id1 = (i1 - N_GROUPS).astype(F32)
    id2 = (i2 - N_GROUPS).astype(F32)
    return jnp.where(lane == 0, id1, jnp.where(lane == 1, id2, jnp.where(lane == 2, w1,
                                                                         jnp.where(lane == 3, w2, 0.0))))


def _outproj_kernel(yd_ref, ys_ref, bonus_ref, gate_ref, lg_ref, lb_ref, h_ref,
                    w1_ref, w2_ref, g_ref, b_ref, rw_hi_ref, rw_lo_ref, rb_ref, h1_ref, info_ref):
    yr = _rwkv_out(ys_ref[...], bonus_ref[...], gate_ref[...], lg_ref[...], lb_ref[...])
    mix = (jnp.dot(yd_ref[...], w1_ref[...], preferred_element_type=F32)
           + jnp.dot(yr.astype(BF16), w2_ref[...], preferred_element_type=F32))
    h1 = _layer_norm(DEEPNORM_ALPHA * h_ref[...] + mix, g_ref[...], b_ref[...])
    h1_ref[...] = h1
    hi = h1.astype(BF16)
    lo = (h1 - hi.astype(F32)).astype(BF16)
    d = lambda a, b: jnp.dot(a, b, preferred_element_type=F32)
    logits = d(hi, rw_hi_ref[...]) + d(lo, rw_hi_ref[...]) + d(hi, rw_lo_ref[...]) + rb_ref[...]
    info_ref[...] = _route(logits)


def outproj_ln_route(yd, ys, bonus, gate, lnx_g, lnx_b, h, w_out_a, w_out_b, g, b, rw_hi, rw_lo, rb, tm):
    t, d = h.shape
    ka = yd.shape[1]
    kb = ys.shape[1]
    full = lambda shp: pl.BlockSpec(shp, lambda i: (0, 0))
    tile = lambda n: pl.BlockSpec((tm, n), lambda i: (i, 0))
    return pl.pallas_call(
        _outproj_kernel,
        grid=(t // tm,),
        in_specs=[
            tile(ka), tile(kb), tile(kb), tile(kb), full((1, kb)), full((1, kb)), tile(d),
            full((ka, d)), full((kb, d)), full((1, d)), full((1, d)),
            full((d, ROUTER_LANES)), full((d, ROUTER_LANES)), full((1, ROUTER_LANES)),
        ],
        out_specs=[
            pl.BlockSpec((tm, d), lambda i: (i, 0)),
            pl.BlockSpec((tm, ROUTER_LANES), lambda i: (i, 0)),
        ],
        out_shape=[jax.ShapeDtypeStruct((t, d), F32), jax.ShapeDtypeStruct((t, ROUTER_LANES), F32)],
        compiler_params=_cparams(("parallel",)),
        name="outproj_ln_route",
    )(yd, ys, bonus, gate, lnx_g.reshape(1, kb), lnx_b.reshape(1, kb), h, w_out_a, w_out_b,
      g.reshape(1, d), b.reshape(1, d), rw_hi, rw_lo, rb)


def _moe_rank_kernel(e_ref, pos_ref, ends_ref, run_scr, start_scr, *, tm, row_tile):
    ph = pl.program_id(0)
    i = pl.program_id(1)
    lane = lax.broadcasted_iota(jnp.int32, (tm, LANES), 1)
    onehot = e_ref[...] == lane
    oh = jnp.where(onehot, 1.0, 0.0)

    @pl.when((ph == 0) & (i == 0))
    def _():
        run_scr[...] = jnp.zeros(run_scr.shape, F32)

    @pl.when(ph == 0)
    def _():
        run_scr[...] += jnp.sum(oh, axis=0, keepdims=True)

    @pl.when((ph == 1) & (i == 0))
    def _():
        padded = jnp.floor((run_scr[...] + (row_tile - 1)) * (1.0 / row_tile)) * row_tile
        ri = lax.broadcasted_iota(jnp.int32, (LANES, LANES), 0)
        ci = lax.broadcasted_iota(jnp.int32, (LANES, LANES), 1)
        upper = jnp.where(ri < ci, 1.0, 0.0).astype(BF16)
        parts = _split3(jnp.broadcast_to(padded, (8, LANES)))
        starts = sum(jnp.dot(q, upper, preferred_element_type=F32) for q in parts)[0:1]
        start_scr[...] = starts
        ends_ref[...] = starts + padded
        run_scr[...] = jnp.zeros(run_scr.shape, F32)

    @pl.when(ph == 1)
    def _():
        ri = lax.broadcasted_iota(jnp.int32, (tm, tm), 0)
        ci = lax.broadcasted_iota(jnp.int32, (tm, tm), 1)
        before = jnp.where(ri > ci, 1.0, 0.0).astype(BF16)
        row = jnp.dot(before, oh.astype(BF16), preferred_element_type=F32) + (run_scr[...] + start_scr[...])
        pos_ref[...] = jnp.sum(jnp.where(onehot, row, 0.0), axis=1, keepdims=True).astype(jnp.int32)
        run_scr[...] += jnp.sum(oh, axis=0, keepdims=True)


def moe_rank(e_col, tm, row_tile):
    n = e_col.shape[0]
    return pl.pallas_call(
        functools.partial(_moe_rank_kernel, tm=tm, row_tile=row_tile),
        grid=(2, n // tm),
        in_specs=[pl.BlockSpec((tm, 1), lambda ph, i: (i, 0))],
        out_specs=[pl.BlockSpec((tm, 1), lambda ph, i: (i * ph, 0)),
                   pl.BlockSpec((1, LANES), lambda ph, i: (0, 0))],
        out_shape=[jax.ShapeDtypeStruct((n, 1), jnp.int32), jax.ShapeDtypeStruct((1, LANES), F32)],
        scratch_shapes=[pltpu.VMEM((1, LANES), F32), pltpu.VMEM((1, LANES), F32)],
        compiler_params=_cparams(("arbitrary", "arbitrary")),
        name="moe_rank",
    )(e_col)


def _row_copy(src_hbm, row, dst, r, sem):
    return pltpu.make_async_copy(src_hbm.at[pl.ds(row, 1)], dst.at[pl.ds(r, 1)], sem)


def _moe_expert_kernel(src_ref, te_ref, nact_ref, x_hbm, wg_ref, wu_ref, wd_ref, o_ref,
                       xbuf0, xbuf1, sem, *, tm):
    j = pl.program_id(0)
    n_act = nact_ref[0]
    bufs = (xbuf0, xbuf1)

    def start_gather(tile, into):
        for r in range(tm):
            _row_copy(x_hbm, src_ref[tile * tm + r], bufs[into], r, sem.at[into]).start(priority=1)

    def wait_gather(into):
        for r in range(tm):
            _row_copy(x_hbm, 0, bufs[into], r, sem.at[into]).wait()

    @pl.when(j == 0)
    def _():
        start_gather(0, 0)

    for slot in range(2):
        @pl.when((j < n_act) & (j % 2 == slot))
        def _():
            wait_gather(slot)
            start_gather(jnp.minimum(j + 1, n_act - 1), 1 - slot)
            x = bufs[slot][...].astype(BF16)
            hg = jnp.dot(x, wg_ref[0].astype(BF16), preferred_element_type=F32)
            hu = jnp.dot(x, wu_ref[0].astype(BF16), preferred_element_type=F32)
            hid = hg * _sigmoid(hg) * hu
            o_ref[...] = jnp.dot(hid.astype(BF16), wd_ref[0].astype(BF16), preferred_element_type=F32)

        @pl.when((j == n_act - 1) & (j % 2 == slot))
        def _():
            wait_gather(1 - slot)

    @pl.when(j >= n_act)
    def _():
        o_ref[...] = jnp.zeros(o_ref.shape, F32)


def moe_experts(src_tok, tile_expert, n_active, x, wg, wu, wd, n_rows, tm):
    d = x.shape[1]
    f = wg.shape[-1]
    return pl.pallas_call(
        functools.partial(_moe_expert_kernel, tm=tm),
        grid_spec=pltpu.PrefetchScalarGridSpec(
            num_scalar_prefetch=3,
            grid=(n_rows // tm,),
            in_specs=[
                pl.BlockSpec(memory_space=pl.ANY),
                pl.BlockSpec((1, d, f), lambda j, src, te, na: (te[j], 0, 0)),
                pl.BlockSpec((1, d, f), lambda j, src, te, na: (te[j], 0, 0)),
                pl.BlockSpec((1, f, d), lambda j, src, te, na: (te[j], 0, 0)),
            ],
            out_specs=pl.BlockSpec((tm, d), lambda j, src, te, na: (j, 0)),
            scratch_shapes=[pltpu.VMEM((tm, d), F32), pltpu.VMEM((tm, d), F32),
                            pltpu.SemaphoreType.DMA((2,))],
        ),
        out_shape=jax.ShapeDtypeStruct((n_rows, d), F32),
        compiler_params=pltpu.CompilerParams(dimension_semantics=("arbitrary",),
                                             vmem_limit_bytes=MOE_VMEM_LIMIT),
        name="moe_experts",
    )(src_tok, tile_expert, n_active, x, wg, wu, wd)


def _final_kernel(p1_ref, p2_ref, y_hbm, h1_ref, info_ref, p_ref, g_ref, b_ref, pw_ref, pg_ref, gw_ref,
                  o_ref, ybuf0, ybuf1, sem, *, tm, n_tiles):
    i = pl.program_id(0)
    bufs = (ybuf0, ybuf1)

    def start_gather(tile, into):
        for r in range(tm):
            _row_copy(y_hbm, p1_ref[tile * tm + r], bufs[into].at[0], r, sem.at[into]).start(priority=0)
            _row_copy(y_hbm, p2_ref[tile * tm + r], bufs[into].at[1], r, sem.at[into]).start(priority=1)

    def wait_gather(into):
        for r in range(tm):
            _row_copy(y_hbm, 0, bufs[into].at[0], r, sem.at[into]).wait()
            _row_copy(y_hbm, 0, bufs[into].at[1], r, sem.at[into]).wait()

    @pl.when(i == 0)
    def _():
        start_gather(0, 0)

    for slot in range(2):
        @pl.when(i % 2 == slot)
        def _():
            wait_gather(slot)
            start_gather(jnp.minimum(i + 1, n_tiles - 1), 1 - slot)
            info = info_ref[...]
            ffn = info[:, 2:3] * bufs[slot][0] + info[:, 3:4] * bufs[slot][1]
            h2 = _layer_norm(DEEPNORM_ALPHA * h1_ref[...] + ffn, g_ref[...], b_ref[...])
            pe = jnp.dot(p_ref[...].astype(BF16), pw_ref[...], preferred_element_type=F32)
            ple = pe * lax.rsqrt(jnp.mean(pe * pe, axis=-1, keepdims=True) + LN_EPS) * pg_ref[...]
            gate = _sigmoid(jnp.dot(h2.astype(BF16), gw_ref[...], preferred_element_type=F32))
            o_ref[...] = h2 + gate * ple

        @pl.when((i == n_tiles - 1) & (i % 2 == slot))
        def _():
            wait_gather(1 - slot)


def final_block(pos1, pos2, ys, h1, info, p, g, b, ple_w, ple_g, gate_w, tm):
    t, d = h1.shape
    dp = p.shape[1]
    tile = lambda n: pl.BlockSpec((tm, n), lambda i, p1, p2: (i, 0))
    full = lambda shp: pl.BlockSpec(shp, lambda i, p1, p2: (0, 0))
    return pl.pallas_call(
        functools.partial(_final_kernel, tm=tm, n_tiles=t // tm),
        grid_spec=pltpu.PrefetchScalarGridSpec(
            num_scalar_prefetch=2,
            grid=(t // tm,),
            in_specs=[pl.BlockSpec(memory_space=pl.ANY), tile(d), tile(ROUTER_LANES), tile(dp),
                      full((1, d)), full((1, d)), full((dp, d)), full((1, d)), full((d, d))],
            out_specs=tile(d),
            scratch_shapes=[pltpu.VMEM((2, tm, d), F32), pltpu.VMEM((2, tm, d), F32),
                            pltpu.SemaphoreType.DMA((2,))],
        ),
        out_shape=jax.ShapeDtypeStruct((t, d), F32),
        compiler_params=_cparams(("arbitrary",)),
        name="final_block",
    )(pos1, pos2, ys, h1, info, p, g.reshape(1, d), b.reshape(1, d), ple_w, ple_g.reshape(1, d), gate_w)


def _dispatch_plan(pos, ends_f, n_tok, tm, n_rows):
    n_tiles = n_rows // tm
    ends = ends_f[0, :N_EXPERTS].astype(jnp.int32)
    src_tok = jnp.zeros((n_rows,), jnp.int32).at[pos].set(
        jnp.arange(2 * n_tok, dtype=jnp.int32) % n_tok, unique_indices=True)
    n_active = ends[-1] // tm
    tile_ids = jnp.arange(n_tiles, dtype=jnp.int32)
    tile_expert = jnp.searchsorted(ends, jnp.minimum(tile_ids, n_active - 1) * tm, side="right")
    tile_expert = jnp.minimum(tile_expert, N_EXPERTS - 1).astype(jnp.int32)
    return src_tok, tile_expert, n_active.astype(jnp.int32).reshape(1)


def _t5_bucket(rel):
    n = jnp.maximum(rel, 0)
    max_exact = REL_BUCKETS // 2
    nf = jnp.maximum(n, 1).astype(F32)
    large = max_exact + (jnp.log(nf / max_exact) / math.log(REL_MAX_DIST / max_exact)
                         * (REL_BUCKETS - max_exact)).astype(jnp.int32)
    large = jnp.minimum(large, REL_BUCKETS - 1)
    return jnp.where(n < max_exact, n, large)


def _bias_tables(rel_bias, tq):
    n = 3 * tq
    rel = 2 * tq - 1 - jnp.arange(n, dtype=jnp.int32)
    g = rel_bias[_t5_bucket(rel)].astype(F32).T
    skew = jnp.tile(g, (1, tq))[:, :tq * (n - 1)].reshape(-1, tq, n - 1)
    tiles = jnp.stack([skew[:, :, 2 * tq - 1:3 * tq - 1], skew[:, :, tq - 1:2 * tq - 1]], axis=1)
    far = rel_bias[_t5_bucket(jnp.full((), 2 * tq, jnp.int32))].astype(F32)
    return tiles, far


def kernel(x, p, ln_in_g, ln_in_b, rel_bias, w_in, diff_lam_q1, diff_lam_k1, diff_lam_q2, diff_lam_k2, diff_subln_g, rwkv_mu, rwkv_w0, rwkv_w2, rwkv_a0, rwkv_a2, rwkv_g2, rwkv_k_k, rwkv_k_a, rwkv_r_k, rwkv_lnx_g, rwkv_lnx_b, w_out, ln1_g, ln1_b, router_group_w, router_group_b, router_expert_w, router_expert_b, moe_w_gate, moe_w_up, moe_w_down, ln2_g, ln2_b, ple_w, ple_norm_g, ple_gate_w):
    bsz, seq, d = x.shape
    t = bsz * seq
    i = 0
    lambda_init = 0.8 - 0.6 * math.exp(-0.3 * i)
    tq = 256

    x2 = x.reshape(t, d)
    w_in_b = w_in[i].astype(BF16)
    w_qkv = w_in_b[:, :O_RWKV]
    w_rw = jnp.pad(w_in_b[:, O_RWKV:], ((0, 0), (0, RWKV_PROJ_PAD - RWKV_PROJ)))
    h, hb = layer_norm_rows(x2, ln_in_g, ln_in_b, 512)
    qkv = matmul(hb, w_qkv, BF16, 1024, 512)
    pr = matmul(hb, w_rw, F32, 1024, 512)

    lam = (jnp.exp(jnp.sum(diff_lam_q1[i].astype(F32) * diff_lam_k1[i]))
           - jnp.exp(jnp.sum(diff_lam_q2[i].astype(F32) * diff_lam_k2[i])) + lambda_init)
    bias_tiles, far_bias = _bias_tables(rel_bias, tq)
    y_diff = diff_attention(qkv.reshape(bsz, seq, O_RWKV), bias_tiles, far_bias, lam.reshape(1),
                            diff_subln_g[i], lambda_init, tq, 8)

    mu_p = jnp.pad(rwkv_mu[i], (0, RWKV_PROJ_PAD - RWKV_PROJ))
    zeros64 = jnp.zeros((LORA_DECAY, RWKV_WIDTH), F32)
    w2p = jnp.concatenate([rwkv_w2[i], zeros64], axis=0).astype(BF16)
    a2p = jnp.concatenate([zeros64, rwkv_a2[i]], axis=0).astype(BF16)
    g2p = jnp.pad(rwkv_g2[i], ((0, 2 * LANES - LORA_GATE), (0, 0))).astype(BF16)
    r, k2, v, lw, kn, bb, g, bonus = rwkv_prep(
        pr, mu_p, rwkv_w0[i], w2p, rwkv_a0[i], a2p, g2p, rwkv_k_k[i], rwkv_k_a[i],
        rwkv_r_k[i].reshape(-1), seq, 256)
    sh = lambda a: a.reshape(bsz, seq, RWKV_WIDTH)
    y_scan = rwkv_scan(sh(r), sh(k2), sh(v), sh(lw), sh(kn), sh(bb), 64, 8)

    w_out_b = w_out[i].astype(BF16)
    rw = jnp.concatenate([router_group_w[i], router_expert_w[i]], axis=1)
    rw = jnp.pad(rw, ((0, 0), (0, ROUTER_LANES - rw.shape[1])))
    rw_hi = rw.astype(BF16)
    rw_lo = (rw - rw_hi.astype(F32)).astype(BF16)
    rb = jnp.pad(jnp.concatenate([router_group_b[i], router_expert_b[i]]),
                 (0, ROUTER_LANES - N_GROUPS - N_EXPERTS)).reshape(1, ROUTER_LANES)
    h1, info = outproj_ln_route(y_diff.reshape(t, DIFF_WIDTH), y_scan.reshape(t, RWKV_WIDTH), bonus, g,
                                rwkv_lnx_g[i], rwkv_lnx_b[i], h, w_out_b[:DIFF_WIDTH],
                                w_out_b[DIFF_WIDTH:], ln1_g[i], ln1_b[i], rw_hi, rw_lo, rb, 256)

    tm_e = MOE_ROW_TILE
    n_rows = 2 * t + N_EXPERTS * tm_e
    e_all = jnp.concatenate([info[:, 0], info[:, 1]]).astype(jnp.int32)
    pos, ends = moe_rank(e_all.reshape(-1, 1), 1024, tm_e)
    pos = pos[:, 0]
    src_tok, tile_expert, n_active = _dispatch_plan(pos, ends, t, tm_e, n_rows)
    f = moe_w_gate.shape[-1]
    ys = moe_experts(src_tok, tile_expert, n_active, h1, moe_w_gate[i].reshape(N_EXPERTS, d, f),
                     moe_w_up[i].reshape(N_EXPERTS, d, f), moe_w_down[i].reshape(N_EXPERTS, f, d),
                     n_rows, tm_e)

    out = final_block(pos[:t], pos[t:], ys, h1, info, p[i].reshape(t, -1), ln2_g[i], ln2_b[i],
                      ple_w[i].astype(BF16), ple_norm_g[i], ple_gate_w[i].astype(BF16), 256)
    return out.reshape(bsz, seq, d)
```

```python
import functools
import math

import jax
import jax.numpy as jnp
from jax import lax
from jax.experimental import pallas as pl
from jax.experimental.pallas import tpu as pltpu

F32 = jnp.float32
BF16 = jnp.bfloat16

DIFF_HEADS = 8
DIFF_HEAD_DIM = 64
DIFF_V_DIM = 128
DIFF_WIDTH = DIFF_HEADS * DIFF_V_DIM
DIFF_QK_WIDTH = DIFF_HEADS * 2 * DIFF_HEAD_DIM
RWKV_HEADS = 16
RWKV_HEAD_SIZE = 64
RWKV_WIDTH = RWKV_HEADS * RWKV_HEAD_SIZE
LORA_DECAY = 64
LORA_AAA = 64
LORA_GATE = 160
RWKV_PROJ = 3 * RWKV_WIDTH + LORA_DECAY + LORA_AAA + LORA_GATE
O_RWKV = 2 * DIFF_QK_WIDTH + DIFF_WIDTH
REL_BUCKETS = 32
REL_MAX_DIST = 128
N_GROUPS = 4
EXPERTS_PER_GROUP = 8
N_EXPERTS = N_GROUPS * EXPERTS_PER_GROUP
LN_EPS = 1e-5
RWKV_GN_EPS = 64e-5
NEG_INF = -1e30
DEPTH = 1
DEEPNORM_ALPHA = (2 * DEPTH) ** 0.25

LANES = 128
RWKV_PROJ_PAD = 3584
ROUTER_LANES = LANES
VMEM_LIMIT = 48 * 1024 * 1024
MOE_VMEM_LIMIT = 56 * 1024 * 1024
MOE_ROW_TILE = 256


def _cparams(sem):
    return pltpu.CompilerParams(dimension_semantics=sem, vmem_limit_bytes=VMEM_LIMIT)


def _bdot(a, b):
    return jnp.dot(a.astype(BF16), b.astype(BF16), preferred_element_type=F32)


def _bdot_nt(a, b):
    return lax.dot_general(a.astype(BF16), b.astype(BF16), (((1,), (1,)), ((), ())),
                           preferred_element_type=F32)


def _bdot_tn(a, b):
    return lax.dot_general(a.astype(BF16), b.astype(BF16), (((0,), (0,)), ((), ())),
                           preferred_element_type=F32)


def _split3(x):
    hi = x.astype(BF16)
    r1 = x - hi.astype(F32)
    mid = r1.astype(BF16)
    lo = (r1 - mid.astype(F32)).astype(BF16)
    return hi, mid, lo


def _head_sum(x):
    lane = lax.broadcasted_iota(jnp.int32, (x.shape[0], LANES), 1)
    first = lane < RWKV_HEAD_SIZE
    cols = []
    for c in range(x.shape[1] // LANES):
        blk = x[:, c * LANES:(c + 1) * LANES]
        lo = jnp.sum(jnp.where(first, blk, 0.0), axis=1, keepdims=True)
        hi = jnp.sum(jnp.where(first, 0.0, blk), axis=1, keepdims=True)
        cols.append(jnp.where(first, lo, hi))
    return jnp.concatenate(cols, axis=1)


def _dot_sel_x(sel, x):
    hi, mid, lo = _split3(x)
    d = lambda p: jnp.dot(sel, p, preferred_element_type=F32)
    return d(hi) + d(mid) + d(lo)


def _layer_norm(x, g, b):
    mu = jnp.mean(x, axis=-1, keepdims=True)
    xc = x - mu
    var = jnp.mean(xc * xc, axis=-1, keepdims=True)
    return xc * lax.rsqrt(var + LN_EPS) * g + b


def _sigmoid(z):
    return 1.0 / (1.0 + jnp.exp(-z))


def _ln_kernel(x_ref, g_ref, b_ref, h_ref, hb_ref):
    xn = _layer_norm(x_ref[...], g_ref[...], b_ref[...])
    h_ref[...] = xn
    hb_ref[...] = xn.astype(BF16)


def layer_norm_rows(x, g, b, tm):
    m, d = x.shape
    tile = pl.BlockSpec((tm, d), lambda i: (i, 0))
    row = pl.BlockSpec((1, d), lambda i: (0, 0))
    return pl.pallas_call(
        _ln_kernel,
        grid=(m // tm,),
        in_specs=[tile, row, row],
        out_specs=[tile, tile],
        out_shape=[jax.ShapeDtypeStruct((m, d), F32), jax.ShapeDtypeStruct((m, d), BF16)],
        compiler_params=_cparams(("parallel",)),
        name="layer_norm_rows",
    )(x, g.reshape(1, d), b.reshape(1, d))


def _mm_kernel(x_ref, w_ref, o_ref):
    o_ref[...] = jnp.dot(x_ref[...], w_ref[...], preferred_element_type=F32).astype(o_ref.dtype)


def matmul(x, w, out_dtype, tm, tn):
    m, k = x.shape
    n = w.shape[1]
    return pl.pallas_call(
        _mm_kernel,
        grid=(m // tm, n // tn),
        in_specs=[pl.BlockSpec((tm, k), lambda i, j: (i, 0)), pl.BlockSpec((k, tn), lambda i, j: (0, j))],
        out_specs=pl.BlockSpec((tm, tn), lambda i, j: (i, j)),
        out_shape=jax.ShapeDtypeStruct((m, n), out_dtype),
        compiler_params=_cparams(("parallel", "parallel")),
        name="matmul",
    )(x, w)


def _attn_kernel(lam_ref, far_ref, q_ref, k_ref, v_ref, bias_ref, g_ref, o_ref,
                 m_scr, l_scr, acc_scr, *, tq, hps, scale, post_scale):
    hp = pl.program_id(1)
    qi = pl.program_id(2)
    lane = lax.broadcasted_iota(jnp.int32, (tq, LANES), 1)
    q_maps = []
    for hh in range(hps):
        qs = (q_ref[0, :, hh * LANES:(hh + 1) * LANES].astype(F32) * scale).astype(BF16)
        zero = jnp.zeros_like(qs)
        q_maps += [jnp.where(lane < DIFF_HEAD_DIM, qs, zero), jnp.where(lane >= DIFF_HEAD_DIM, qs, zero)]

    m_scr[...] = jnp.full(m_scr.shape, NEG_INF, F32)
    l_scr[...] = jnp.zeros(l_scr.shape, F32)
    acc_scr[...] = jnp.zeros(acc_scr.shape, F32)

    def step(kstart, bias_index, causal):
        for hh in range(hps):
            kb = k_ref[0, pl.ds(kstart, tq), hh * LANES:(hh + 1) * LANES]
            vb = v_ref[0, pl.ds(kstart, tq), hh * LANES:(hh + 1) * LANES]
            for c in range(2 * hh, 2 * hh + 2):
                s = lax.dot_general(q_maps[c], kb, (((1,), (1,)), ((), ())), preferred_element_type=F32)
                m_old = m_scr[c]
                if bias_index is None:
                    far = far_ref[hp * hps + hh]
                    m_new = jnp.maximum(m_old, jnp.max(s, axis=1, keepdims=True) + far)
                    shift = m_new - far
                else:
                    s = s + bias_ref[hh, bias_index]
                    if causal:
                        ri = lax.broadcasted_iota(jnp.int32, (tq, tq), 0)
                        ci = lax.broadcasted_iota(jnp.int32, (tq, tq), 1)
                        s = jnp.where(ri >= ci, s, NEG_INF)
                    m_new = jnp.maximum(m_old, jnp.max(s, axis=1, keepdims=True))
                    shift = m_new
                p = jnp.exp(s - jnp.concatenate([shift] * (tq // LANES), axis=1))
                corr = jnp.exp(m_old - m_new)
                l_scr[c] = corr * l_scr[c] + jnp.sum(p, axis=1, keepdims=True)
                acc_scr[c] = corr * acc_scr[c] + jnp.dot(p.astype(BF16), vb, preferred_element_type=F32)
                m_scr[c] = m_new

    def far_body(kb, carry):
        step(pl.multiple_of(kb * tq, tq), None, False)
        return carry

    lax.fori_loop(0, jnp.maximum(qi - 1, 0), far_body, 0)

    @pl.when(qi >= 1)
    def _():
        step(pl.multiple_of((qi - 1) * tq, tq), 1, False)

    step(pl.multiple_of(qi * tq, tq), 0, True)

    for hh in range(hps):
        c = 2 * hh
        o = acc_scr[c] / l_scr[c] - lam_ref[0] * (acc_scr[c + 1] / l_scr[c + 1])
        ms = jnp.mean(o * o, axis=1, keepdims=True)
        o_ref[0, :, hh * LANES:(hh + 1) * LANES] = (
            o * lax.rsqrt(ms + LN_EPS) * g_ref[...] * post_scale).astype(o_ref.dtype)


def diff_attention(qkv, bias_tiles, far_bias, lam, subln_g, lambda_init, tq, hps):
    bsz, s, _ = qkv.shape
    kern = functools.partial(_attn_kernel, tq=tq, hps=hps, scale=DIFF_HEAD_DIM ** -0.5,
                             post_scale=1.0 - lambda_init)
    hw = hps * LANES
    nqk = DIFF_QK_WIDTH // hw
    return pl.pallas_call(
        kern,
        grid=(bsz, DIFF_HEADS // hps, s // tq),
        in_specs=[
            pl.BlockSpec(memory_space=pltpu.SMEM),
            pl.BlockSpec(memory_space=pltpu.SMEM),
            pl.BlockSpec((1, tq, hw), lambda b, h, i: (b, i, h)),
            pl.BlockSpec((1, s, hw), lambda b, h, i: (b, 0, nqk + h)),
            pl.BlockSpec((1, s, hw), lambda b, h, i: (b, 0, 2 * nqk + h)),
            pl.BlockSpec((hps, 2, tq, tq), lambda b, h, i: (h, 0, 0, 0)),
            pl.BlockSpec((1, LANES), lambda b, h, i: (0, 0)),
        ],
        out_specs=pl.BlockSpec((1, tq, hw), lambda b, h, i: (b, i, h)),
        out_shape=jax.ShapeDtypeStruct((bsz, s, DIFF_WIDTH), BF16),
        scratch_shapes=[pltpu.VMEM((2 * hps, tq, LANES), F32), pltpu.VMEM((2 * hps, tq, LANES), F32),
                        pltpu.VMEM((2 * hps, tq, LANES), F32)],
        compiler_params=_cparams(("parallel", "parallel", "arbitrary")),
        name="diff_attention",
    )(lam, far_bias, qkv, qkv, qkv, bias_tiles, subln_g.reshape(1, LANES))


def _rwkv_prep_kernel(x_ref, prev_ref, mu_ref, w0_ref, w2_ref, a0_ref, a2_ref, g2_ref,
                      kk_ref, ka_ref, rk_ref,
                      r_out, k_out, v_out, lw_out, kn_out, b_out, g_out, bonus_out, *, tm, seq):
    i = pl.program_id(0)
    x = x_ref[...]
    w = RWKV_WIDTH
    row = lax.broadcasted_iota(jnp.int32, x.shape, 0)
    is_start = (i * tm) % seq == 0
    last_prev = jnp.where(is_start, 0.0, prev_ref[7:8, :])
    prev = jnp.where(row == 0, last_prev, pltpu.roll(x, 1, 0))
    xs = x + (prev - x) * mu_ref[...]

    r = xs[:, 0:w]
    k = xs[:, w:2 * w]
    v = xs[:, 2 * w:3 * w]
    xwa = xs[:, 3 * w:3 * w + 2 * LORA_DECAY]
    xg = xs[:, 3 * w + LANES:3 * w + LANES + 2 * LANES]

    z = w0_ref[...] + _bdot(jnp.tanh(xwa), w2_ref[...])
    softplus_neg_z = jnp.maximum(-z, 0.0) + jnp.log(1.0 + jnp.exp(-jnp.abs(z)))
    lw_out[...] = -jnp.exp(-softplus_neg_z - 0.5)
    a_lr = _sigmoid(a0_ref[...] + _bdot(xwa, a2_ref[...]))
    g_out[...] = _bdot(_sigmoid(xg), g2_ref[...])

    kk = k * kk_ref[...]
    kn = kk / jnp.maximum(jnp.sqrt(_head_sum(kk * kk)), 1e-12)
    k2 = k * (1.0 + (a_lr - 1.0) * ka_ref[...])
    r_out[...] = r
    k_out[...] = k2
    v_out[...] = v
    kn_out[...] = kn
    b_out[...] = kn * a_lr
    bonus_out[...] = _head_sum(r * k2 * rk_ref[...]) * v


def rwkv_prep(pr, mu, w0, w2p, a0, a2p, g2p, k_k, k_a, r_k, seq, tm):
    t, wp = pr.shape
    w = RWKV_WIDTH
    row = lambda a: a.reshape(1, -1)
    full = lambda shp: pl.BlockSpec(shp, lambda i: (0, 0))
    outs = [jax.ShapeDtypeStruct((t, w), F32)] * 8
    kern = functools.partial(_rwkv_prep_kernel, tm=tm, seq=seq)
    return pl.pallas_call(
        kern,
        grid=(t // tm,),
        in_specs=[
            pl.BlockSpec((tm, wp), lambda i: (i, 0)),
            pl.BlockSpec((8, wp), lambda i: (jnp.maximum(i * (tm // 8) - 1, 0), 0)),
            full((1, wp)), full((1, w)), full((LANES, w)), full((1, w)), full((LANES, w)),
            full((2 * LANES, w)), full((1, w)), full((1, w)), full((1, w)),
        ],
        out_specs=[pl.BlockSpec((tm, w), lambda i: (i, 0))] * 8,
        out_shape=outs,
        compiler_params=_cparams(("parallel",)),
        name="rwkv_prep",
    )(pr, pr, row(mu), row(w0), w2p, row(a0), a2p, g2p, row(k_k), row(k_a), row(r_k))


def _rwkv_scan_kernel(r_ref, k_ref, v_ref, lw_ref, kn_ref, b_ref, y_ref, s_scr, *, chunk, pairs):
    c2 = 2 * chunk

    @pl.when(pl.program_id(2) == 0)
    def _():
        s_scr[...] = jnp.zeros(s_scr.shape, F32)

    ri = lax.broadcasted_iota(jnp.int32, (chunk, chunk), 0)
    ci = lax.broadcasted_iota(jnp.int32, (chunk, chunk), 1)
    ltri = jnp.where(ri >= ci, 1.0, 0.0).astype(BF16)

    lw = lw_ref[0]
    cl = _dot_sel_x(ltri, lw)
    mid = chunk // 2 - 1
    clm = cl[mid:mid + 1, :]
    cle = cl[chunk - 1:chunk, :]
    w_mid = jnp.exp(cl - clm)
    w_mid_prev = jnp.exp(cl - lw - clm)
    w_mid_inv = jnp.exp(clm - cl)
    w_abs = jnp.exp(cl)
    w_abs_prev = jnp.exp(cl - lw)
    w_end = jnp.exp(cle - cl)
    w_tot = jnp.exp(cle)

    r = r_ref[0]
    k = k_ref[0]
    v = v_ref[0]
    a = -kn_ref[0]
    b = b_ref[0]
    a_mid = a * w_mid_prev
    r_mid = r * w_mid
    b_mid = b * w_mid_inv
    k_mid = k * w_mid_inv
    a_abs = a * w_abs_prev
    r_abs = r * w_abs
    b_end = b * w_end
    k_end = k * w_end

    lane = lax.broadcasted_iota(jnp.int32, (chunk, LANES), 1)
    first = lane < RWKV_HEAD_SIZE

    def stack(x):
        return jnp.concatenate([jnp.where(first, x, 0.0), jnp.where(first, 0.0, x)], axis=0)

    rr = lax.broadcasted_iota(jnp.int32, (c2, c2), 0)
    cc = lax.broadcasted_iota(jnp.int32, (c2, c2), 1)
    same = (rr >= chunk) == (cc >= chunk)
    strict = same & (rr > cc)
    incl = same & (rr >= cc)
    eye = jnp.where(rr == cc, 1.0, 0.0)

    ps = range(pairs)
    sls = [slice(p * LANES, (p + 1) * LANES) for p in ps]
    bf = lambda x: x.astype(BF16)
    cat0 = lambda xs: jnp.concatenate(xs, axis=0)
    cat1 = lambda xs: jnp.concatenate(xs, axis=1)
    v_s = [bf(stack(v[:, sl])) for sl in sls]
    ar_m = [bf(cat0([stack(a_mid[:, sl]), stack(r_mid[:, sl])])) for sl in sls]
    bk_m = [bf(cat0([stack(b_mid[:, sl]), stack(k_mid[:, sl])])) for sl in sls]
    g_all = [_bdot_nt(ar_m[p], bk_m[p]) for p in ps]
    l_ab = [jnp.where(strict, g[:c2, :c2], 0.0) for g in g_all]
    l_ak = [bf(jnp.where(strict, g[:c2, c2:], 0.0)) for g in g_all]
    m_r = [bf(cat1([jnp.where(incl, g[c2:, :c2], 0.0), jnp.where(incl, g[c2:, c2:], 0.0)])) for g in g_all]

    t_inv = [eye + l_ab[p] for p in ps]
    n_fac = max(chunk.bit_length() - 2, 0)
    lp = [_bdot(l_ab[p], l_ab[p]) for p in ps] if n_fac else l_ab
    for _ in range(n_fac - 1):
        x = [_bdot(cat0([t_inv[p], lp[p]]), lp[p]) for p in ps]
        t_inv = [t_inv[p] + x[p][:c2] for p in ps]
        lp = [x[p][c2:] for p in ps]
    if n_fac:
        t_inv = [t_inv[p] + _bdot(t_inv[p], lp[p]) for p in ps]
    t_inv = [bf(t) for t in t_inv]

    lv = [_bdot(l_ak[p], v_s[p]) for p in ps]
    av = [bf(_bdot(t_inv[p], cat1([stack(a_abs[:, sls[p]]), lv[p]]))) for p in ps]
    zero = jnp.zeros((c2, LANES), BF16)
    ry = [_bdot(m_r[p], cat0([av[p], cat1([zero, v_s[p]])])) for p in ps]
    r_p = [stack(r_abs[:, sls[p]]) + ry[p][:, :LANES] for p in ps]

    s_old = [s_scr[p] for p in ps]
    y_s = [_bdot_nt(r_p[p], s_old[p]) + ry[p][:, LANES:] for p in ps]
    for p in ps:
        y_ref[0, :, sls[p]] = y_s[p][:chunk] + y_s[p][chunk:]
    b_e = [bf(stack(b_end[:, sl])) for sl in sls]
    k_e = [bf(stack(k_end[:, sl])) for sl in sls]
    ab = [_bdot_tn(av[p][:, :LANES], b_e[p]) for p in ps]
    q_t = [_bdot_tn(cat0([av[p][:, LANES:], v_s[p]]), cat0([b_e[p], k_e[p]])) for p in ps]
    for p in ps:
        s_scr[p] = s_old[p] * w_tot[:, sls[p]] + _bdot(s_old[p], ab[p]) + q_t[p]


def rwkv_scan(r, k, v, lw, kn, b, chunk, pairs):
    bsz, s, w = r.shape
    pw = pairs * LANES
    spec = pl.BlockSpec((1, chunk, pw), lambda bi, hi, ci: (bi, ci, hi))
    kern = functools.partial(_rwkv_scan_kernel, chunk=chunk, pairs=pairs)
    return pl.pallas_call(
        kern,
        grid=(bsz, w // pw, s // chunk),
        in_specs=[spec] * 6,
        out_specs=spec,
        out_shape=jax.ShapeDtypeStruct((bsz, s, w), F32),
        scratch_shapes=[pltpu.VMEM((pairs, LANES, LANES), F32)],
        compiler_params=_cparams(("parallel", "parallel", "arbitrary")),
        name="rwkv_scan",
    )(r, k, v, lw, kn, b)


def _rwkv_out(y, bonus, g, lnx_g, lnx_b):
    inv_n = 1.0 / RWKV_HEAD_SIZE
    mu = _head_sum(y) * inv_n
    yc = y - mu
    var = _head_sum(yc * yc) * inv_n
    yn = yc * lax.rsqrt(var + RWKV_GN_EPS) * lnx_g + lnx_b
    return (yn + bonus) * g


def _route(logits):
    lane = lax.broadcasted_iota(jnp.int32, logits.shape, 1)
    big = jnp.int32(4 * LANES)
    gmask = lane < N_GROUPS
    gl = jnp.where(gmask, logits, NEG_INF)
    gmax = jnp.max(gl, axis=1, keepdims=True)
    gidx = jnp.min(jnp.where(gl == gmax, lane, big), axis=1, keepdims=True)
    gtop = 1.0 / jnp.sum(jnp.where(gmask, jnp.exp(gl - gmax), 0.0), axis=1, keepdims=True)
    lo = N_GROUPS + EXPERTS_PER_GROUP * gidx
    emask = (lane >= lo) & (lane < lo + EXPERTS_PER_GROUP)
    el = jnp.where(emask, logits, NEG_INF)
    e1 = jnp.max(el, axis=1, keepdims=True)
    i1 = jnp.min(jnp.where(el == e1, lane, big), axis=1, keepdims=True)
    el2 = jnp.where(lane == i1, NEG_INF, el)
    e2 = jnp.max(el2, axis=1, keepdims=True)
    i2 = jnp.min(jnp.where(el2 == e2, lane, big), axis=1, keepdims=True)
    t = jnp.exp(e2 - e1)
    w1 = gtop / (1.0 + t)
    w2 = gtop * t / (1.0 + t)
    id1 = (i1 - N_GROUPS).astype(F32)
    id2 = (i2 - N_GROUPS).astype(F32)
    return jnp.where(lane == 0, id1, jnp.where(lane == 1, id2, jnp.where(lane == 2, w1,
                                                                         jnp.where(lane == 3, w2, 0.0))))


def _outproj_kernel(yd_ref, ys_ref, bonus_ref, gate_ref, lg_ref, lb_ref, h_ref,
                    w1_ref, w2_ref, g_ref, b_ref, rw_hi_ref, rw_lo_ref, rb_ref, h1_ref, info_ref):
    yr = _rwkv_out(ys_ref[...], bonus_ref[...], gate_ref[...], lg_ref[...], lb_ref[...])
    mix = (jnp.dot(yd_ref[...], w1_ref[...], preferred_element_type=F32)
           + jnp.dot(yr.astype(BF16), w2_ref[...], preferred_element_type=F32))
    h1 = _layer_norm(DEEPNORM_ALPHA * h_ref[...] + mix, g_ref[...], b_ref[...])
    h1_ref[...] = h1
    hi = h1.astype(BF16)
    lo = (h1 - hi.astype(F32)).astype(BF16)
    d = lambda a, b: jnp.dot(a, b, preferred_element_type=F32)
    logits = d(hi, rw_hi_ref[...]) + d(lo, rw_hi_ref[...]) + d(hi, rw_lo_ref[...]) + rb_ref[...]
    info_ref[...] = _route(logits)


def outproj_ln_route(yd, ys, bonus, gate, lnx_g, lnx_b, h, w_out_a, w_out_b, g, b, rw_hi, rw_lo, rb, tm):
    t, d = h.shape
    ka = yd.shape[1]
    kb = ys.shape[1]
    full = lambda shp: pl.BlockSpec(shp, lambda i: (0, 0))
    tile = lambda n: pl.BlockSpec((tm, n), lambda i: (i, 0))
    return pl.pallas_call(
        _outproj_kernel,
        grid=(t // tm,),
        in_specs=[
            tile(ka), tile(kb), tile(kb), tile(kb), full((1, kb)), full((1, kb)), tile(d),
            full((ka, d)), full((kb, d)), full((1, d)), full((1, d)),
            full((d, ROUTER_LANES)), full((d, ROUTER_LANES)), full((1, ROUTER_LANES)),
        ],
        out_specs=[
            pl.BlockSpec((tm, d), lambda i: (i, 0)),
            pl.BlockSpec((tm, ROUTER_LANES), lambda i: (i, 0)),
        ],
        out_shape=[jax.ShapeDtypeStruct((t, d), F32), jax.ShapeDtypeStruct((t, ROUTER_LANES), F32)],
        compiler_params=_cparams(("parallel",)),
        name="outproj_ln_route",
    )(yd, ys, bonus, gate, lnx_g.reshape(1, kb), lnx_b.reshape(1, kb), h, w_out_a, w_out_b,
      g.reshape(1, d), b.reshape(1, d), rw_hi, rw_lo, rb)


def _moe_rank_kernel(e_ref, pos_ref, ends_ref, run_scr, start_scr, *, tm, row_tile):
    ph = pl.program_id(0)
    i = pl.program_id(1)
    lane = lax.broadcasted_iota(jnp.int32, (tm, LANES), 1)
    onehot = e_ref[...] == lane
    oh = jnp.where(onehot, 1.0, 0.0)

    @pl.when((ph == 0) & (i == 0))
    def _():
        run_scr[...] = jnp.zeros(run_scr.shape, F32)

    @pl.when(ph == 0)
    def _():
        run_scr[...] += jnp.sum(oh, axis=0, keepdims=True)

    @pl.when((ph == 1) & (i == 0))
    def _():
        padded = jnp.floor((run_scr[...] + (row_tile - 1)) * (1.0 / row_tile)) * row_tile
        ri = lax.broadcasted_iota(jnp.int32, (LANES, LANES), 0)
        ci = lax.broadcasted_iota(jnp.int32, (LANES, LANES), 1)
        upper = jnp.where(ri < ci, 1.0, 0.0).astype(BF16)
        parts = _split3(jnp.broadcast_to(padded, (8, LANES)))
        starts = sum(jnp.dot(q, upper, preferred_element_type=F32) for q in parts)[0:1]
        start_scr[...] = starts
        ends_ref[...] = starts + padded
        run_scr[...] = jnp.zeros(run_scr.shape, F32)

    @pl.when(ph == 1)
    def _():
        ri = lax.broadcasted_iota(jnp.int32, (tm, tm), 0)
        ci = lax.broadcasted_iota(jnp.int32, (tm, tm), 1)
        before = jnp.where(ri > ci, 1.0, 0.0).astype(BF16)
        row = jnp.dot(before, oh.astype(BF16), preferred_element_type=F32) + (run_scr[...] + start_scr[...])
        pos_ref[...] = jnp.sum(jnp.where(onehot, row, 0.0), axis=1, keepdims=True).astype(jnp.int32)
        run_scr[...] += jnp.sum(oh, axis=0, keepdims=True)


def moe_rank(e_col, tm, row_tile):
    n = e_col.shape[0]
    return pl.pallas_call(
        functools.partial(_moe_rank_kernel, tm=tm, row_tile=row_tile),
        grid=(2, n // tm),
        in_specs=[pl.BlockSpec((tm, 1), lambda ph, i: (i, 0))],
        out_specs=[pl.BlockSpec((tm, 1), lambda ph, i: (i * ph, 0)),
                   pl.BlockSpec((1, LANES), lambda ph, i: (0, 0))],
        out_shape=[jax.ShapeDtypeStruct((n, 1), jnp.int32), jax.ShapeDtypeStruct((1, LANES), F32)],
        scratch_shapes=[pltpu.VMEM((1, LANES), F32), pltpu.VMEM((1, LANES), F32)],
        compiler_params=_cparams(("arbitrary", "arbitrary")),
        name="moe_rank",
    )(e_col)


def _row_copy(src_hbm, row, dst, r, sem):
    return pltpu.make_async_copy(src_hbm.at[pl.ds(row, 1)], dst.at[pl.ds(r, 1)], sem)


def _moe_expert_kernel(src_ref, te_ref, first_ref, par_ref, nxt_ref, nact_ref,
                       x_hbm, wg_hbm, wu_hbm, wd_hbm, o_ref,
                       xbuf0, xbuf1, wgf, wuf, wdf, wgb, wub, wdb, sem, wsem, *, tm):
    j = pl.program_id(0)
    n_act = nact_ref[0]
    bufs = (xbuf0, xbuf1)

    def start_gather(tile, into):
        for r in range(tm):
            _row_copy(x_hbm, src_ref[tile * tm + r], bufs[into], r, sem.at[into]).start(priority=1)

    def wait_gather(into):
        for r in range(tm):
            _row_copy(x_hbm, 0, bufs[into], r, sem.at[into]).wait()

    def weight_copies(e, par):
        return (pltpu.make_async_copy(wg_hbm.at[e], wgf.at[par], wsem.at[par]),
                pltpu.make_async_copy(wu_hbm.at[e], wuf.at[par], wsem.at[par]),
                pltpu.make_async_copy(wd_hbm.at[e], wdf.at[par], wsem.at[par]))

    @pl.when(j == 0)
    def _():
        start_gather(0, 0)
        for c in weight_copies(te_ref[0], 0):
            c.start()

    @pl.when((j < n_act) & (first_ref[j] > 0))
    def _():
        par = par_ref[j]
        for c in weight_copies(te_ref[j], par):
            c.wait()

        @pl.when(nxt_ref[j] >= 0)
        def _():
            for c in weight_copies(nxt_ref[j], 1 - par):
                c.start()

        wgb[...] = wgf[par].astype(BF16)
        wub[...] = wuf[par].astype(BF16)
        wdb[...] = wdf[par].astype(BF16)

    for slot in range(2):
        @pl.when((j < n_act) & (j % 2 == slot))
        def _():
            wait_gather(slot)
            start_gather(jnp.minimum(j + 1, n_act - 1), 1 - slot)
            x = bufs[slot][...].astype(BF16)
            hg = jnp.dot(x, wgb[...], preferred_element_type=F32)
            hu = jnp.dot(x, wub[...], preferred_element_type=F32)
            hid = hg * _sigmoid(hg) * hu
            o_ref[...] = jnp.dot(hid.astype(BF16), wdb[...], preferred_element_type=F32)

        @pl.when((j == n_act - 1) & (j % 2 == slot))
        def _():
            wait_gather(1 - slot)

    @pl.when(j >= n_act)
    def _():
        o_ref[...] = jnp.zeros(o_ref.shape, F32)


def moe_experts(src_tok, plan, x, wg, wu, wd, n_rows, tm):
    d = x.shape[1]
    f = wg.shape[-1]
    hbm = pl.BlockSpec(memory_space=pl.ANY)
    return pl.pallas_call(
        functools.partial(_moe_expert_kernel, tm=tm),
        grid_spec=pltpu.PrefetchScalarGridSpec(
            num_scalar_prefetch=6,
            grid=(n_rows // tm,),
            in_specs=[hbm, hbm, hbm, hbm],
            out_specs=pl.BlockSpec((tm, d), lambda j, *_: (j, 0)),
            scratch_shapes=[pltpu.VMEM((tm, d), F32), pltpu.VMEM((tm, d), F32),
                            pltpu.VMEM((2, d, f), F32), pltpu.VMEM((2, d, f), F32), pltpu.VMEM((2, f, d), F32),
                            pltpu.VMEM((d, f), BF16), pltpu.VMEM((d, f), BF16), pltpu.VMEM((f, d), BF16),
                            pltpu.SemaphoreType.DMA((2,)), pltpu.SemaphoreType.DMA((2,))],
        ),
        out_shape=jax.ShapeDtypeStruct((n_rows, d), F32),
        compiler_params=pltpu.CompilerParams(dimension_semantics=("arbitrary",),
                                             vmem_limit_bytes=MOE_VMEM_LIMIT),
        name="moe_experts",
    )(src_tok, *plan, x, wg, wu, wd)


def _final_kernel(p1_ref, p2_ref, y_hbm, h1_ref, info_ref, p_ref, g_ref, b_ref, pw_ref, pg_ref, gw_ref,
                  o_ref, ybuf0, ybuf1, sem, *, tm, n_tiles):
    i = pl.program_id(0)
    bufs = (ybuf0, ybuf1)

    def start_gather(tile, into):
        for r in range(tm):
            _row_copy(y_hbm, p1_ref[tile * tm + r], bufs[into].at[0], r, sem.at[into]).start(priority=0)
            _row_copy(y_hbm, p2_ref[tile * tm + r], bufs[into].at[1], r, sem.at[into]).start(priority=1)

    def wait_gather(into):
        for r in range(tm):
            _row_copy(y_hbm, 0, bufs[into].at[0], r, sem.at[into]).wait()
            _row_copy(y_hbm, 0, bufs[into].at[1], r, sem.at[into]).wait()

    @pl.when(i == 0)
    def _():
        start_gather(0, 0)

    for slot in range(2):
        @pl.when(i % 2 == slot)
        def _():
            wait_gather(slot)
            start_gather(jnp.minimum(i + 1, n_tiles - 1), 1 - slot)
            info = info_ref[...]
            ffn = info[:, 2:3] * bufs[slot][0] + info[:, 3:4] * bufs[slot][1]
            h2 = _layer_norm(DEEPNORM_ALPHA * h1_ref[...] + ffn, g_ref[...], b_ref[...])
            pe = jnp.dot(p_ref[...].astype(BF16), pw_ref[...], preferred_element_type=F32)
            ple = pe * lax.rsqrt(jnp.mean(pe * pe, axis=-1, keepdims=True) + LN_EPS) * pg_ref[...]
            gate = _sigmoid(jnp.dot(h2.astype(BF16), gw_ref[...], preferred_element_type=F32))
            o_ref[...] = h2 + gate * ple

        @pl.when((i == n_tiles - 1) & (i % 2 == slot))
        def _():
            wait_gather(1 - slot)


def final_block(pos1, pos2, ys, h1, info, p, g, b, ple_w, ple_g, gate_w, tm):
    t, d = h1.shape
    dp = p.shape[1]
    tile = lambda n: pl.BlockSpec((tm, n), lambda i, p1, p2: (i, 0))
    full = lambda shp: pl.BlockSpec(shp, lambda i, p1, p2: (0, 0))
    return pl.pallas_call(
        functools.partial(_final_kernel, tm=tm, n_tiles=t // tm),
        grid_spec=pltpu.PrefetchScalarGridSpec(
            num_scalar_prefetch=2,
            grid=(t // tm,),
            in_specs=[pl.BlockSpec(memory_space=pl.ANY), tile(d), tile(ROUTER_LANES), tile(dp),
                      full((1, d)), full((1, d)), full((dp, d)), full((1, d)), full((d, d))],
            out_specs=tile(d),
            scratch_shapes=[pltpu.VMEM((2, tm, d), F32), pltpu.VMEM((2, tm, d), F32),
                            pltpu.SemaphoreType.DMA((2,))],
        ),
        out_shape=jax.ShapeDtypeStruct((t, d), F32),
        compiler_params=_cparams(("arbitrary",)),
        name="final_block",
    )(pos1, pos2, ys, h1, info, p, g.reshape(1, d), b.reshape(1, d), ple_w, ple_g.reshape(1, d), gate_w)


def _dispatch_plan(pos, ends_f, n_tok, tm, n_rows):
    n_tiles = n_rows // tm
    ends = ends_f[0, :N_EXPERTS].astype(jnp.int32)
    src_tok = jnp.zeros((n_rows,), jnp.int32).at[pos].set(
        jnp.arange(2 * n_tok, dtype=jnp.int32) % n_tok, unique_indices=True)
    n_active = ends[-1] // tm
    tile_ids = jnp.arange(n_tiles, dtype=jnp.int32)
    tile_expert = jnp.searchsorted(ends, jnp.minimum(tile_ids, n_active - 1) * tm, side="right")
    tile_expert = jnp.minimum(tile_expert, N_EXPERTS - 1).astype(jnp.int32)
    first = jnp.concatenate([jnp.ones((1,), jnp.int32),
                             (tile_expert[1:] != tile_expert[:-1]).astype(jnp.int32)])
    parity = (jnp.cumsum(first) - 1) % 2
    ids = jnp.arange(N_EXPERTS, dtype=jnp.int32)
    present = ends > jnp.concatenate([jnp.zeros((1,), jnp.int32), ends[:-1]])
    later = present[None, :] & (ids[None, :] > ids[:, None])
    next_of = jnp.min(jnp.where(later, ids[None, :], N_EXPERTS), axis=1)
    next_of = jnp.where(next_of == N_EXPERTS, -1, next_of)
    plan = (tile_expert, first, parity.astype(jnp.int32), next_of[tile_expert].astype(jnp.int32),
            n_active.astype(jnp.int32).reshape(1))
    return src_tok, plan


def _t5_bucket(rel):
    n = jnp.maximum(rel, 0)
    max_exact = REL_BUCKETS // 2
    nf = jnp.maximum(n, 1).astype(F32)
    large = max_exact + (jnp.log(nf / max_exact) / math.log(REL_MAX_DIST / max_exact)
                         * (REL_BUCKETS - max_exact)).astype(jnp.int32)
    large = jnp.minimum(large, REL_BUCKETS - 1)
    return jnp.where(n < max_exact, n, large)


def _bias_tables(rel_bias, tq):
    n = 3 * tq
    rel = 2 * tq - 1 - jnp.arange(n, dtype=jnp.int32)
    g = rel_bias[_t5_bucket(rel)].astype(F32).T
    skew = jnp.tile(g, (1, tq))[:, :tq * (n - 1)].reshape(-1, tq, n - 1)
    tiles = jnp.stack([skew[:, :, 2 * tq - 1:3 * tq - 1], skew[:, :, tq - 1:2 * tq - 1]], axis=1)
    far = rel_bias[_t5_bucket(jnp.full((), 2 * tq, jnp.int32))].astype(F32)
    return tiles, far


def kernel(x, p, ln_in_g, ln_in_b, rel_bias, w_in, diff_lam_q1, diff_lam_k1, diff_lam_q2, diff_lam_k2, diff_subln_g, rwkv_mu, rwkv_w0, rwkv_w2, rwkv_a0, rwkv_a2, rwkv_g2, rwkv_k_k, rwkv_k_a, rwkv_r_k, rwkv_lnx_g, rwkv_lnx_b, w_out, ln1_g, ln1_b, router_group_w, router_group_b, router_expert_w, router_expert_b, moe_w_gate, moe_w_up, moe_w_down, ln2_g, ln2_b, ple_w, ple_norm_g, ple_gate_w):
    bsz, seq, d = x.shape
    t = bsz * seq
    i = 0
    lambda_init = 0.8 - 0.6 * math.exp(-0.3 * i)
    tq = 256

    x2 = x.reshape(t, d)
    w_in_b = w_in[i].astype(BF16)
    w_qkv = w_in_b[:, :O_RWKV]
    w_rw = jnp.pad(w_in_b[:, O_RWKV:], ((0, 0), (0, RWKV_PROJ_PAD - RWKV_PROJ)))
    h, hb = layer_norm_rows(x2, ln_in_g, ln_in_b, 512)
    qkv = matmul(hb, w_qkv, BF16, 1024, 512)
    pr = matmul(hb, w_rw, F32, 1024, 512)

    lam = (jnp.exp(jnp.sum(diff_lam_q1[i].astype(F32) * diff_lam_k1[i]))
           - jnp.exp(jnp.sum(diff_lam_q2[i].astype(F32) * diff_lam_k2[i])) + lambda_init)
    bias_tiles, far_bias = _bias_tables(rel_bias, tq)
    y_diff = diff_attention(qkv.reshape(bsz, seq, O_RWKV), bias_tiles, far_bias, lam.reshape(1),
                            diff_subln_g[i], lambda_init, tq, 8)

    mu_p = jnp.pad(rwkv_mu[i], (0, RWKV_PROJ_PAD - RWKV_PROJ))
    zeros64 = jnp.zeros((LORA_DECAY, RWKV_WIDTH), F32)
    w2p = jnp.concatenate([rwkv_w2[i], zeros64], axis=0).astype(BF16)
    a2p = jnp.concatenate([zeros64, rwkv_a2[i]], axis=0).astype(BF16)
    g2p = jnp.pad(rwkv_g2[i], ((0, 2 * LANES - LORA_GATE), (0, 0))).astype(BF16)
    r, k2, v, lw, kn, bb, g, bonus = rwkv_prep(
        pr, mu_p, rwkv_w0[i], w2p, rwkv_a0[i], a2p, g2p, rwkv_k_k[i], rwkv_k_a[i],
        rwkv_r_k[i].reshape(-1), seq, 256)
    sh = lambda a: a.reshape(bsz, seq, RWKV_WIDTH)
    y_scan = rwkv_scan(sh(r), sh(k2), sh(v), sh(lw), sh(kn), sh(bb), 64, 8)

    w_out_b = w_out[i].astype(BF16)
    rw = jnp.concatenate([router_group_w[i], router_expert_w[i]], axis=1)
    rw = jnp.pad(rw, ((0, 0), (0, ROUTER_LANES - rw.shape[1])))
    rw_hi = rw.astype(BF16)
    rw_lo = (rw - rw_hi.astype(F32)).astype(BF16)
    rb = jnp.pad(jnp.concatenate([router_group_b[i], router_expert_b[i]]),
                 (0, ROUTER_LANES - N_GROUPS - N_EXPERTS)).reshape(1, ROUTER_LANES)
    h1, info = outproj_ln_route(y_diff.reshape(t, DIFF_WIDTH), y_scan.reshape(t, RWKV_WIDTH), bonus, g,
                                rwkv_lnx_g[i], rwkv_lnx_b[i], h, w_out_b[:DIFF_WIDTH],
                                w_out_b[DIFF_WIDTH:], ln1_g[i], ln1_b[i], rw_hi, rw_lo, rb, 256)

    tm_e = MOE_ROW_TILE
    n_rows = 2 * t + N_EXPERTS * tm_e
    e_all = jnp.concatenate([info[:, 0], info[:, 1]]).astype(jnp.int32)
    pos, ends = moe_rank(e_all.reshape(-1, 1), 1024, tm_e)
    pos = pos[:, 0]
    src_tok, plan = _dispatch_plan(pos, ends, t, tm_e, n_rows)
    f = moe_w_gate.shape[-1]
    ys = moe_experts(src_tok, plan, h1, moe_w_gate[i].reshape(N_EXPERTS, d, f),
                     moe_w_up[i].reshape(N_EXPERTS, d, f), moe_w_down[i].reshape(N_EXPERTS, f, d),
                     n_rows, tm_e)

    out = final_block(pos[:t], pos[t:], ys, h1, info, p[i].reshape(t, -1), ln2_g[i], ln2_b[i],
                      ple_w[i].astype(BF16), ple_norm_g[i], ple_gate_w[i].astype(BF16), 256)
    return out.reshape(bsz, seq, d)
```

```python
import functools
import math

import jax
import jax.numpy as jnp
from jax import lax
from jax.experimental import pallas as pl
from jax.experimental.pallas import tpu as pltpu

F32 = jnp.float32
BF16 = jnp.bfloat16

DIFF_HEADS = 8
DIFF_HEAD_DIM = 64
DIFF_V_DIM = 128
DIFF_WIDTH = DIFF_HEADS * DIFF_V_DIM
DIFF_QK_WIDTH = DIFF_HEADS * 2 * DIFF_HEAD_DIM
RWKV_HEADS = 16
RWKV_HEAD_SIZE = 64
RWKV_WIDTH = RWKV_HEADS * RWKV_HEAD_SIZE
LORA_DECAY = 64
LORA_AAA = 64
LORA_GATE = 160
RWKV_PROJ = 3 * RWKV_WIDTH + LORA_DECAY + LORA_AAA + LORA_GATE
O_RWKV = 2 * DIFF_QK_WIDTH + DIFF_WIDTH
REL_BUCKETS = 32
REL_MAX_DIST = 128
N_GROUPS = 4
EXPERTS_PER_GROUP = 8
N_EXPERTS = N_GROUPS * EXPERTS_PER_GROUP
LN_EPS = 1e-5
RWKV_GN_EPS = 64e-5
NEG_INF = -1e30
DEPTH = 1
DEEPNORM_ALPHA = (2 * DEPTH) ** 0.25

LANES = 128
RWKV_PROJ_PAD = 3584
ROUTER_LANES = LANES
VMEM_LIMIT = 48 * 1024 * 1024
MOE_VMEM_LIMIT = 56 * 1024 * 1024
MOE_ROW_TILE = 256


def _cparams(sem):
    return pltpu.CompilerParams(dimension_semantics=sem, vmem_limit_bytes=VMEM_LIMIT)


def _bdot(a, b):
    return jnp.dot(a.astype(BF16), b.astype(BF16), preferred_element_type=F32)


def _bdot_nt(a, b):
    return lax.dot_general(a.astype(BF16), b.astype(BF16), (((1,), (1,)), ((), ())),
                           preferred_element_type=F32)


def _bdot_tn(a, b):
    return lax.dot_general(a.astype(BF16), b.astype(BF16), (((0,), (0,)), ((), ())),
                           preferred_element_type=F32)


def _split3(x):
    hi = x.astype(BF16)
    r1 = x - hi.astype(F32)
    mid = r1.astype(BF16)
    lo = (r1 - mid.astype(F32)).astype(BF16)
    return hi, mid, lo


def _head_sum(x):
    lane = lax.broadcasted_iota(jnp.int32, (x.shape[0], LANES), 1)
    first = lane < RWKV_HEAD_SIZE
    cols = []
    for c in range(x.shape[1] // LANES):
        blk = x[:, c * LANES:(c + 1) * LANES]
        lo = jnp.sum(jnp.where(first, blk, 0.0), axis=1, keepdims=True)
        hi = jnp.sum(jnp.where(first, 0.0, blk), axis=1, keepdims=True)
        cols.append(jnp.where(first, lo, hi))
    return jnp.concatenate(cols, axis=1)


def _dot_sel_x(sel, x):
    hi, mid, lo = _split3(x)
    d = lambda p: jnp.dot(sel, p, preferred_element_type=F32)
    return d(hi) + d(mid) + d(lo)


def _layer_norm(x, g, b):
    mu = jnp.mean(x, axis=-1, keepdims=True)
    xc = x - mu
    var = jnp.mean(xc * xc, axis=-1, keepdims=True)
    return xc * lax.rsqrt(var + LN_EPS) * g + b


def _sigmoid(z):
    return 1.0 / (1.0 + jnp.exp(-z))


def _ln_kernel(x_ref, g_ref, b_ref, h_ref, hb_ref):
    xn = _layer_norm(x_ref[...], g_ref[...], b_ref[...])
    h_ref[...] = xn
    hb_ref[...] = xn.astype(BF16)


def layer_norm_rows(x, g, b, tm):
    m, d = x.shape
    tile = pl.BlockSpec((tm, d), lambda i: (i, 0))
    row = pl.BlockSpec((1, d), lambda i: (0, 0))
    return pl.pallas_call(
        _ln_kernel,
        grid=(m // tm,),
        in_specs=[tile, row, row],
        out_specs=[tile, tile],
        out_shape=[jax.ShapeDtypeStruct((m, d), F32), jax.ShapeDtypeStruct((m, d), BF16)],
        compiler_params=_cparams(("parallel",)),
        name="layer_norm_rows",
    )(x, g.reshape(1, d), b.reshape(1, d))


def _mm_kernel(x_ref, w_ref, o_ref):
    o_ref[...] = jnp.dot(x_ref[...], w_ref[...], preferred_element_type=F32).astype(o_ref.dtype)


def matmul(x, w, out_dtype, tm, tn):
    m, k = x.shape
    n = w.shape[1]
    return pl.pallas_call(
        _mm_kernel,
        grid=(m // tm, n // tn),
        in_specs=[pl.BlockSpec((tm, k), lambda i, j: (i, 0)), pl.BlockSpec((k, tn), lambda i, j: (0, j))],
        out_specs=pl.BlockSpec((tm, tn), lambda i, j: (i, j)),
        out_shape=jax.ShapeDtypeStruct((m, n), out_dtype),
        compiler_params=_cparams(("parallel", "parallel")),
        name="matmul",
    )(x, w)


def _attn_kernel(lam_ref, far_ref, q_ref, k_ref, v_ref, bias_ref, g_ref, o_ref,
                 m_scr, l_scr, acc_scr, *, tq, hps, scale, post_scale):
    hp = pl.program_id(1)
    qi = pl.program_id(2)
    lane = lax.broadcasted_iota(jnp.int32, (tq, LANES), 1)
    q_maps = []
    for hh in range(hps):
        qs = (q_ref[0, :, hh * LANES:(hh + 1) * LANES].astype(F32) * scale).astype(BF16)
        zero = jnp.zeros_like(qs)
        q_maps += [jnp.where(lane < DIFF_HEAD_DIM, qs, zero), jnp.where(lane >= DIFF_HEAD_DIM, qs, zero)]

    m_scr[...] = jnp.full(m_scr.shape, NEG_INF, F32)
    l_scr[...] = jnp.zeros(l_scr.shape, F32)
    acc_scr[...] = jnp.zeros(acc_scr.shape, F32)

    def step(kstart, bias_index, causal):
        for hh in range(hps):
            kb = k_ref[0, pl.ds(kstart, tq), hh * LANES:(hh + 1) * LANES]
            vb = v_ref[0, pl.ds(kstart, tq), hh * LANES:(hh + 1) * LANES]
            for c in range(2 * hh, 2 * hh + 2):
                s = lax.dot_general(q_maps[c], kb, (((1,), (1,)), ((), ())), preferred_element_type=F32)
                m_old = m_scr[c]
                if bias_index is None:
                    far = far_ref[hp * hps + hh]
                    m_new = jnp.maximum(m_old, jnp.max(s, axis=1, keepdims=True) + far)
                    shift = m_new - far
                else:
                    s = s + bias_ref[hh, bias_index]
                    if causal:
                        ri = lax.broadcasted_iota(jnp.int32, (tq, tq), 0)
                        ci = lax.broadcasted_iota(jnp.int32, (tq, tq), 1)
                        s = jnp.where(ri >= ci, s, NEG_INF)
                    m_new = jnp.maximum(m_old, jnp.max(s, axis=1, keepdims=True))
                    shift = m_new
                p = jnp.exp(s - jnp.concatenate([shift] * (tq // LANES), axis=1))
                corr = jnp.exp(m_old - m_new)
                l_scr[c] = corr * l_scr[c] + jnp.sum(p, axis=1, keepdims=True)
                acc_scr[c] = corr * acc_scr[c] + jnp.dot(p.astype(BF16), vb, preferred_element_type=F32)
                m_scr[c] = m_new

    def far_body(kb, carry):
        step(pl.multiple_of(kb * tq, tq), None, False)
        return carry

    lax.fori_loop(0, jnp.maximum(qi - 1, 0), far_body, 0)

    @pl.when(qi >= 1)
    def _():
        step(pl.multiple_of((qi - 1) * tq, tq), 1, False)

    step(pl.multiple_of(qi * tq, tq), 0, True)

    for hh in range(hps):
        c = 2 * hh
        o = acc_scr[c] / l_scr[c] - lam_ref[0] * (acc_scr[c + 1] / l_scr[c + 1])
        ms = jnp.mean(o * o, axis=1, keepdims=True)
        o_ref[0, :, hh * LANES:(hh + 1) * LANES] = (
            o * lax.rsqrt(ms + LN_EPS) * g_ref[...] * post_scale).astype(o_ref.dtype)


def diff_attention(qkv, bias_tiles, far_bias, lam, subln_g, lambda_init, tq, hps):
    bsz, s, _ = qkv.shape
    kern = functools.partial(_attn_kernel, tq=tq, hps=hps, scale=DIFF_HEAD_DIM ** -0.5,
                             post_scale=1.0 - lambda_init)
    hw = hps * LANES
    nqk = DIFF_QK_WIDTH // hw
    return pl.pallas_call(
        kern,
        grid=(bsz, DIFF_HEADS // hps, s // tq),
        in_specs=[
            pl.BlockSpec(memory_space=pltpu.SMEM),
            pl.BlockSpec(memory_space=pltpu.SMEM),
            pl.BlockSpec((1, tq, hw), lambda b, h, i: (b, i, h)),
            pl.BlockSpec((1, s, hw), lambda b, h, i: (b, 0, nqk + h)),
            pl.BlockSpec((1, s, hw), lambda b, h, i: (b, 0, 2 * nqk + h)),
            pl.BlockSpec((hps, 2, tq, tq), lambda b, h, i: (h, 0, 0, 0)),
            pl.BlockSpec((1, LANES), lambda b, h, i: (0, 0)),
        ],
        out_specs=pl.BlockSpec((1, tq, hw), lambda b, h, i: (b, i, h)),
        out_shape=jax.ShapeDtypeStruct((bsz, s, DIFF_WIDTH), BF16),
        scratch_shapes=[pltpu.VMEM((2 * hps, tq, LANES), F32), pltpu.VMEM((2 * hps, tq, LANES), F32),
                        pltpu.VMEM((2 * hps, tq, LANES), F32)],
        compiler_params=_cparams(("parallel", "parallel", "arbitrary")),
        name="diff_attention",
    )(lam, far_bias, qkv, qkv, qkv, bias_tiles, subln_g.reshape(1, LANES))


def _rwkv_prep_kernel(x_ref, prev_ref, mu_ref, w0_ref, w2_ref, a0_ref, a2_ref, g2_ref,
                      kk_ref, ka_ref, rk_ref,
                      r_out, k_out, v_out, lw_out, kn_out, b_out, g_out, bonus_out, *, tm, seq):
    i = pl.program_id(0)
    x = x_ref[...]
    w = RWKV_WIDTH
    row = lax.broadcasted_iota(jnp.int32, x.shape, 0)
    is_start = (i * tm) % seq == 0
    last_prev = jnp.where(is_start, 0.0, prev_ref[7:8, :])
    prev = jnp.where(row == 0, last_prev, pltpu.roll(x, 1, 0))
    xs = x + (prev - x) * mu_ref[...]

    r = xs[:, 0:w]
    k = xs[:, w:2 * w]
    v = xs[:, 2 * w:3 * w]
    xwa = xs[:, 3 * w:3 * w + 2 * LORA_DECAY]
    xg = xs[:, 3 * w + LANES:3 * w + LANES + 2 * LANES]

    z = w0_ref[...] + _bdot(jnp.tanh(xwa), w2_ref[...])
    softplus_neg_z = jnp.maximum(-z, 0.0) + jnp.log(1.0 + jnp.exp(-jnp.abs(z)))
    lw_out[...] = -jnp.exp(-softplus_neg_z - 0.5)
    a_lr = _sigmoid(a0_ref[...] + _bdot(xwa, a2_ref[...]))
    g_out[...] = _bdot(_sigmoid(xg), g2_ref[...])

    kk = k * kk_ref[...]
    kn = kk / jnp.maximum(jnp.sqrt(_head_sum(kk * kk)), 1e-12)
    k2 = k * (1.0 + (a_lr - 1.0) * ka_ref[...])
    r_out[...] = r
    k_out[...] = k2
    v_out[...] = v
    kn_out[...] = kn
    b_out[...] = kn * a_lr
    bonus_out[...] = _head_sum(r * k2 * rk_ref[...]) * v


def rwkv_prep(pr, mu, w0, w2p, a0, a2p, g2p, k_k, k_a, r_k, seq, tm):
    t, wp = pr.shape
    w = RWKV_WIDTH
    row = lambda a: a.reshape(1, -1)
    full = lambda shp: pl.BlockSpec(shp, lambda i: (0, 0))
    outs = [jax.ShapeDtypeStruct((t, w), F32)] * 8
    kern = functools.partial(_rwkv_prep_kernel, tm=tm, seq=seq)
    return pl.pallas_call(
        kern,
        grid=(t // tm,),
        in_specs=[
            pl.BlockSpec((tm, wp), lambda i: (i, 0)),
            pl.BlockSpec((8, wp), lambda i: (jnp.maximum(i * (tm // 8) - 1, 0), 0)),
            full((1, wp)), full((1, w)), full((LANES, w)), full((1, w)), full((LANES, w)),
            full((2 * LANES, w)), full((1, w)), full((1, w)), full((1, w)),
        ],
        out_specs=[pl.BlockSpec((tm, w), lambda i: (i, 0))] * 8,
        out_shape=outs,
        compiler_params=_cparams(("parallel",)),
        name="rwkv_prep",
    )(pr, pr, row(mu), row(w0), w2p, row(a0), a2p, g2p, row(k_k), row(k_a), row(r_k))


def _rwkv_scan_kernel(r_ref, k_ref, v_ref, lw_ref, kn_ref, b_ref, y_ref, s_scr, *, chunk, pairs):
    c2 = 2 * chunk

    @pl.when(pl.program_id(2) == 0)
    def _():
        s_scr[...] = jnp.zeros(s_scr.shape, F32)

    ri = lax.broadcasted_iota(jnp.int32, (chunk, chunk), 0)
    ci = lax.broadcasted_iota(jnp.int32, (chunk, chunk), 1)
    ltri = jnp.where(ri >= ci, 1.0, 0.0).astype(BF16)

    lw = lw_ref[0]
    cl = _dot_sel_x(ltri, lw)
    mid = chunk // 2 - 1
    clm = cl[mid:mid + 1, :]
    cle = cl[chunk - 1:chunk, :]
    w_mid = jnp.exp(cl - clm)
    w_mid_prev = jnp.exp(cl - lw - clm)
    w_mid_inv = jnp.exp(clm - cl)
    w_abs = jnp.exp(cl)
    w_abs_prev = jnp.exp(cl - lw)
    w_end = jnp.exp(cle - cl)
    w_tot = jnp.exp(cle)

    r = r_ref[0]
    k = k_ref[0]
    v = v_ref[0]
    a = -kn_ref[0]
    b = b_ref[0]
    a_mid = a * w_mid_prev
    r_mid = r * w_mid
    b_mid = b * w_mid_inv
    k_mid = k * w_mid_inv
    a_abs = a * w_abs_prev
    r_abs = r * w_abs
    b_end = b * w_end
    k_end = k * w_end

    lane = lax.broadcasted_iota(jnp.int32, (chunk, LANES), 1)
    first = lane < RWKV_HEAD_SIZE

    def stack(x):
        return jnp.concatenate([jnp.where(first, x, 0.0), jnp.where(first, 0.0, x)], axis=0)

    rr = lax.broadcasted_iota(jnp.int32, (c2, c2), 0)
    cc = lax.broadcasted_iota(jnp.int32, (c2, c2), 1)
    same = (rr >= chunk) == (cc >= chunk)
    strict = same & (rr > cc)
    incl = same & (rr >= cc)
    eye = jnp.where(rr == cc, 1.0, 0.0)

    ps = range(pairs)
    sls = [slice(p * LANES, (p + 1) * LANES) for p in ps]
    bf = lambda x: x.astype(BF16)
    cat0 = lambda xs: jnp.concatenate(xs, axis=0)
    cat1 = lambda xs: jnp.concatenate(xs, axis=1)
    v_s = [bf(stack(v[:, sl])) for sl in sls]
    ar_m = [bf(cat0([stack(a_mid[:, sl]), stack(r_mid[:, sl])])) for sl in sls]
    bk_m = [bf(cat0([stack(b_mid[:, sl]), stack(k_mid[:, sl])])) for sl in sls]
    g_all = [_bdot_nt(ar_m[p], bk_m[p]) for p in ps]
    l_ab = [jnp.where(strict, g[:c2, :c2], 0.0) for g in g_all]
    l_ak = [bf(jnp.where(strict, g[:c2, c2:], 0.0)) for g in g_all]
    m_r = [bf(cat1([jnp.where(incl, g[c2:, :c2], 0.0), jnp.where(incl, g[c2:, c2:], 0.0)])) for g in g_all]

    t_inv = [eye + l_ab[p] for p in ps]
    n_fac = max(chunk.bit_length() - 2, 0)
    lp = [_bdot(l_ab[p], l_ab[p]) for p in ps] if n_fac else l_ab
    for _ in range(n_fac - 1):
        x = [_bdot(cat0([t_inv[p], lp[p]]), lp[p]) for p in ps]
        t_inv = [t_inv[p] + x[p][:c2] for p in ps]
        lp = [x[p][c2:] for p in ps]
    if n_fac:
        t_inv = [t_inv[p] + _bdot(t_inv[p], lp[p]) for p in ps]
    t_inv = [bf(t) for t in t_inv]

    lv = [_bdot(l_ak[p], v_s[p]) for p in ps]
    av = [bf(_bdot(t_inv[p], cat1([stack(a_abs[:, sls[p]]), lv[p]]))) for p in ps]
    zero = jnp.zeros((c2, LANES), BF16)
    ry = [_bdot(m_r[p], cat0([av[p], cat1([zero, v_s[p]])])) for p in ps]
    r_p = [stack(r_abs[:, sls[p]]) + ry[p][:, :LANES] for p in ps]

    s_old = [s_scr[p] for p in ps]
    y_s = [_bdot_nt(r_p[p], s_old[p]) + ry[p][:, LANES:] for p in ps]
    for p in ps:
        y_ref[0, :, sls[p]] = y_s[p][:chunk] + y_s[p][chunk:]
    b_e = [bf(stack(b_end[:, sl])) for sl in sls]
    k_e = [bf(stack(k_end[:, sl])) for sl in sls]
    ab = [_bdot_tn(av[p][:, :LANES], b_e[p]) for p in ps]
    q_t = [_bdot_tn(cat0([av[p][:, LANES:], v_s[p]]), cat0([b_e[p], k_e[p]])) for p in ps]
    for p in ps:
        s_scr[p] = s_old[p] * w_tot[:, sls[p]] + _bdot(s_old[p], ab[p]) + q_t[p]


def rwkv_scan(r, k, v, lw, kn, b, chunk, pairs):
    bsz, s, w = r.shape
    pw = pairs * LANES
    spec = pl.BlockSpec((1, chunk, pw), lambda bi, hi, ci: (bi, ci, hi))
    kern = functools.partial(_rwkv_scan_kernel, chunk=chunk, pairs=pairs)
    return pl.pallas_call(
        kern,
        grid=(bsz, w // pw, s // chunk),
        in_specs=[spec] * 6,
        out_specs=spec,
        out_shape=jax.ShapeDtypeStruct((bsz, s, w), F32),
        scratch_shapes=[pltpu.VMEM((pairs, LANES, LANES), F32)],
        compiler_params=_cparams(("parallel", "parallel", "arbitrary")),
        name="rwkv_scan",
    )(r, k, v, lw, kn, b)


def _rwkv_out(y, bonus, g, lnx_g, lnx_b):
    inv_n = 1.0 / RWKV_HEAD_SIZE
    mu = _head_sum(y) * inv_n
    yc = y - mu
    var = _head_sum(yc * yc) * inv_n
    yn = yc * lax.rsqrt(var + RWKV_GN_EPS) * lnx_g + lnx_b
    return (yn + bonus) * g


def _route(logits):
    lane = lax.broadcasted_iota(jnp.int32, logits.shape, 1)
    big = jnp.int32(4 * LANES)
    gmask = lane < N_GROUPS
    gl = jnp.where(gmask, logits, NEG_INF)
    gmax = jnp.max(gl, axis=1, keepdims=True)
    gidx = jnp.min(jnp.where(gl == gmax, lane, big), axis=1, keepdims=True)
    gtop = 1.0 / jnp.sum(jnp.where(gmask, jnp.exp(gl - gmax), 0.0), axis=1, keepdims=True)
    lo = N_GROUPS + EXPERTS_PER_GROUP * gidx
    emask = (lane >= lo) & (lane < lo + EXPERTS_PER_GROUP)
    el = jnp.where(emask, logits, NEG_INF)
    e1 = jnp.max(el, axis=1, keepdims=True)
    i1 = jnp.min(jnp.where(el == e1, lane, big), axis=1, keepdims=True)
    el2 = jnp.where(lane == i1, NEG_INF, el)
    e2 = jnp.max(el2, axis=1, keepdims=True)
    i2 = jnp.min(jnp.where(el2 == e2, lane, big), axis=1, keepdims=True)
    t = jnp.exp(e2 - e1)
    w1 = gtop / (1.0 + t)
    w2 = gtop * t / (1.0 + t)
    id1 = (i1 - N_GROUPS).astype(F32)
    id2 = (i2 - N_GROUPS).astype(F32)
    return jnp.where(lane == 0, id1, jnp.where(lane == 1, id2, jnp.where(lane == 2, w1,
                                                                         jnp.where(lane == 3, w2, 0.0))))


def _outproj_kernel(yd_ref, ys_ref, bonus_ref, gate_ref, lg_ref, lb_ref, h_ref,
                    w1_ref, w2_ref, g_ref, b_ref, rw_hi_ref, rw_lo_ref, rb_ref, h1_ref, info_ref):
    yr = _rwkv_out(ys_ref[...], bonus_ref[...], gate_ref[...], lg_ref[...], lb_ref[...])
    mix = (jnp.dot(yd_ref[...], w1_ref[...], preferred_element_type=F32)
           + jnp.dot(yr.astype(BF16), w2_ref[...], preferred_element_type=F32))
    h1 = _layer_norm(DEEPNORM_ALPHA * h_ref[...] + mix, g_ref[...], b_ref[...])
    h1_ref[...] = h1
    hi = h1.astype(BF16)
    lo = (h1 - hi.astype(F32)).astype(BF16)
    d = lambda a, b: jnp.dot(a, b, preferred_element_type=F32)
    logits = d(hi, rw_hi_ref[...]) + d(lo, rw_hi_ref[...]) + d(hi, rw_lo_ref[...]) + rb_ref[...]
    info_ref[...] = _route(logits)


def outproj_ln_route(yd, ys, bonus, gate, lnx_g, lnx_b, h, w_out_a, w_out_b, g, b, rw_hi, rw_lo, rb, tm):
    t, d = h.shape
    ka = yd.shape[1]
    kb = ys.shape[1]
    full = lambda shp: pl.BlockSpec(shp, lambda i: (0, 0))
    tile = lambda n: pl.BlockSpec((tm, n), lambda i: (i, 0))
    return pl.pallas_call(
        _outproj_kernel,
        grid=(t // tm,),
        in_specs=[
            tile(ka), tile(kb), tile(kb), tile(kb), full((1, kb)), full((1, kb)), tile(d),
            full((ka, d)), full((kb, d)), full((1, d)), full((1, d)),
            full((d, ROUTER_LANES)), full((d, ROUTER_LANES)), full((1, ROUTER_LANES)),
        ],
        out_specs=[
            pl.BlockSpec((tm, d), lambda i: (i, 0)),
            pl.BlockSpec((tm, ROUTER_LANES), lambda i: (i, 0)),
        ],
        out_shape=[jax.ShapeDtypeStruct((t, d), F32), jax.ShapeDtypeStruct((t, ROUTER_LANES), F32)],
        compiler_params=_cparams(("parallel",)),
        name="outproj_ln_route",
    )(yd, ys, bonus, gate, lnx_g.reshape(1, kb), lnx_b.reshape(1, kb), h, w_out_a, w_out_b,
      g.reshape(1, d), b.reshape(1, d), rw_hi, rw_lo, rb)


def _moe_rank_kernel(e_ref, pos_ref, ends_ref, run_scr, start_scr, *, tm, row_tile):
    ph = pl.program_id(0)
    i = pl.program_id(1)
    lane = lax.broadcasted_iota(jnp.int32, (tm, LANES), 1)
    onehot = e_ref[...] == lane
    oh = jnp.where(onehot, 1.0, 0.0)

    @pl.when((ph == 0) & (i == 0))
    def _():
        run_scr[...] = jnp.zeros(run_scr.shape, F32)

    @pl.when(ph == 0)
    def _():
        run_scr[...] += jnp.sum(oh, axis=0, keepdims=True)

    @pl.when((ph == 1) & (i == 0))
    def _():
        padded = jnp.floor((run_scr[...] + (row_tile - 1)) * (1.0 / row_tile)) * row_tile
        ri = lax.broadcasted_iota(jnp.int32, (LANES, LANES), 0)
        ci = lax.broadcasted_iota(jnp.int32, (LANES, LANES), 1)
        upper = jnp.where(ri < ci, 1.0, 0.0).astype(BF16)
        parts = _split3(jnp.broadcast_to(padded, (8, LANES)))
        starts = sum(jnp.dot(q, upper, preferred_element_type=F32) for q in parts)[0:1]
        start_scr[...] = starts
        ends_ref[...] = starts + padded
        run_scr[...] = jnp.zeros(run_scr.shape, F32)

    @pl.when(ph == 1)
    def _():
        ri = lax.broadcasted_iota(jnp.int32, (tm, tm), 0)
        ci = lax.broadcasted_iota(jnp.int32, (tm, tm), 1)
        before = jnp.where(ri > ci, 1.0, 0.0).astype(BF16)
        row = jnp.dot(before, oh.astype(BF16), preferred_element_type=F32) + (run_scr[...] + start_scr[...])
        pos_ref[...] = jnp.sum(jnp.where(onehot, row, 0.0), axis=1, keepdims=True).astype(jnp.int32)
        run_scr[...] += jnp.sum(oh, axis=0, keepdims=True)


def moe_rank(e_col, tm, row_tile):
    n = e_col.shape[0]
    return pl.pallas_call(
        functools.partial(_moe_rank_kernel, tm=tm, row_tile=row_tile),
        grid=(2, n // tm),
        in_specs=[pl.BlockSpec((tm, 1), lambda ph, i: (i, 0))],
        out_specs=[pl.BlockSpec((tm, 1), lambda ph, i: (i * ph, 0)),
                   pl.BlockSpec((1, LANES), lambda ph, i: (0, 0))],
        out_shape=[jax.ShapeDtypeStruct((n, 1), jnp.int32), jax.ShapeDtypeStruct((1, LANES), F32)],
        scratch_shapes=[pltpu.VMEM((1, LANES), F32), pltpu.VMEM((1, LANES), F32)],
        compiler_params=_cparams(("arbitrary", "arbitrary")),
        name="moe_rank",
    )(e_col)


def _row_copy(src_hbm, row, dst, r, sem):
    return pltpu.make_async_copy(src_hbm.at[pl.ds(row, 1)], dst.at[pl.ds(r, 1)], sem)


def _moe_expert_kernel(src_ref, te_ref, first_ref, par_ref, nxt_ref, nact_ref,
                       x_hbm, wg_hbm, wu_hbm, wd_hbm, o_ref,
                       xbuf0, xbuf1, wgf, wuf, wdf, wgb, wub, wdb, sem, wsem, *, tm):
    j = pl.program_id(0)
    n_act = nact_ref[0]
    bufs = (xbuf0, xbuf1)

    def token_copy(tok, into, r):
        return pltpu.make_async_copy(x_hbm.at[tok], bufs[into].at[:, r, :], sem.at[into])

    def start_gather(tile, into):
        for r in range(tm):
            token_copy(src_ref[tile * tm + r], into, r).start(priority=0)

    def wait_gather(into):
        for r in range(tm):
            token_copy(0, into, r).wait()

    def weight_copies(e, par):
        return (pltpu.make_async_copy(wg_hbm.at[e], wgf.at[par], wsem.at[par]),
                pltpu.make_async_copy(wu_hbm.at[e], wuf.at[par], wsem.at[par]),
                pltpu.make_async_copy(wd_hbm.at[e], wdf.at[par], wsem.at[par]))

    @pl.when(j == 0)
    def _():
        start_gather(0, 0)
        for c in weight_copies(te_ref[0], 0):
            c.start(priority=1)

    @pl.when((j < n_act) & (first_ref[j] > 0))
    def _():
        par = par_ref[j]
        for c in weight_copies(te_ref[j], par):
            c.wait()

        @pl.when(nxt_ref[j] >= 0)
        def _():
            for c in weight_copies(nxt_ref[j], 1 - par):
                c.start(priority=1)

        wgb[...] = wgf[par].astype(BF16)
        wub[...] = wuf[par].astype(BF16)
        wdb[...] = wdf[par].astype(BF16)

    for slot in range(2):
        @pl.when((j < n_act) & (j % 2 == slot))
        def _():
            wait_gather(slot)
            start_gather(jnp.minimum(j + 1, n_act - 1), 1 - slot)
            x = jnp.concatenate([bufs[slot][c] for c in range(bufs[slot].shape[0])], axis=1).astype(BF16)
            hg = jnp.dot(x, wgb[...], preferred_element_type=F32)
            hu = jnp.dot(x, wub[...], preferred_element_type=F32)
            hid = hg * _sigmoid(hg) * hu
            o_ref[...] = jnp.dot(hid.astype(BF16), wdb[...], preferred_element_type=F32)

        @pl.when((j == n_act - 1) & (j % 2 == slot))
        def _():
            wait_gather(1 - slot)

    @pl.when(j >= n_act)
    def _():
        o_ref[...] = jnp.zeros(o_ref.shape, F32)


def moe_experts(src_tok, plan, x, wg, wu, wd, n_rows, tm):
    d, f = wg.shape[1:]
    hbm = pl.BlockSpec(memory_space=pl.ANY)
    return pl.pallas_call(
        functools.partial(_moe_expert_kernel, tm=tm),
        grid_spec=pltpu.PrefetchScalarGridSpec(
            num_scalar_prefetch=6,
            grid=(n_rows // tm,),
            in_specs=[hbm, hbm, hbm, hbm],
            out_specs=pl.BlockSpec((tm, d), lambda j, *_: (j, 0)),
            scratch_shapes=[pltpu.VMEM((d // LANES, tm, LANES), F32), pltpu.VMEM((d // LANES, tm, LANES), F32),
                            pltpu.VMEM((2, d, f), F32), pltpu.VMEM((2, d, f), F32), pltpu.VMEM((2, f, d), F32),
                            pltpu.VMEM((d, f), BF16), pltpu.VMEM((d, f), BF16), pltpu.VMEM((f, d), BF16),
                            pltpu.SemaphoreType.DMA((2,)), pltpu.SemaphoreType.DMA((2,))],
        ),
        out_shape=jax.ShapeDtypeStruct((n_rows, d), F32),
        compiler_params=pltpu.CompilerParams(dimension_semantics=("arbitrary",),
                                             vmem_limit_bytes=MOE_VMEM_LIMIT),
        name="moe_experts",
    )(src_tok, *plan, x, wg, wu, wd)


def _final_kernel(p1_ref, p2_ref, y_hbm, h1_ref, info_ref, p_ref, g_ref, b_ref, pw_ref, pg_ref, gw_ref,
                  o_ref, ybuf0, ybuf1, sem, *, tm, n_tiles):
    i = pl.program_id(0)
    bufs = (ybuf0, ybuf1)

    def start_gather(tile, into):
        for r in range(tm):
            _row_copy(y_hbm, p1_ref[tile * tm + r], bufs[into].at[0], r, sem.at[into]).start(priority=0)
            _row_copy(y_hbm, p2_ref[tile * tm + r], bufs[into].at[1], r, sem.at[into]).start(priority=1)

    def wait_gather(into):
        for r in range(tm):
            _row_copy(y_hbm, 0, bufs[into].at[0], r, sem.at[into]).wait()
            _row_copy(y_hbm, 0, bufs[into].at[1], r, sem.at[into]).wait()

    @pl.when(i == 0)
    def _():
        start_gather(0, 0)

    for slot in range(2):
        @pl.when(i % 2 == slot)
        def _():
            wait_gather(slot)
            start_gather(jnp.minimum(i + 1, n_tiles - 1), 1 - slot)
            info = info_ref[...]
            ffn = info[:, 2:3] * bufs[slot][0] + info[:, 3:4] * bufs[slot][1]
            h2 = _layer_norm(DEEPNORM_ALPHA * h1_ref[...] + ffn, g_ref[...], b_ref[...])
            pe = jnp.dot(p_ref[...].astype(BF16), pw_ref[...], preferred_element_type=F32)
            ple = pe * lax.rsqrt(jnp.mean(pe * pe, axis=-1, keepdims=True) + LN_EPS) * pg_ref[...]
            gate = _sigmoid(jnp.dot(h2.astype(BF16), gw_ref[...], preferred_element_type=F32))
            o_ref[...] = h2 + gate * ple

        @pl.when((i == n_tiles - 1) & (i % 2 == slot))
        def _():
            wait_gather(1 - slot)


def final_block(pos1, pos2, ys, h1, info, p, g, b, ple_w, ple_g, gate_w, tm):
    t, d = h1.shape
    dp = p.shape[1]
    tile = lambda n: pl.BlockSpec((tm, n), lambda i, p1, p2: (i, 0))
    full = lambda shp: pl.BlockSpec(shp, lambda i, p1, p2: (0, 0))
    return pl.pallas_call(
        functools.partial(_final_kernel, tm=tm, n_tiles=t // tm),
        grid_spec=pltpu.PrefetchScalarGridSpec(
            num_scalar_prefetch=2,
            grid=(t // tm,),
            in_specs=[pl.BlockSpec(memory_space=pl.ANY), tile(d), tile(ROUTER_LANES), tile(dp),
                      full((1, d)), full((1, d)), full((dp, d)), full((1, d)), full((d, d))],
            out_specs=tile(d),
            scratch_shapes=[pltpu.VMEM((2, tm, d), F32), pltpu.VMEM((2, tm, d), F32),
                            pltpu.SemaphoreType.DMA((2,))],
        ),
        out_shape=jax.ShapeDtypeStruct((t, d), F32),
        compiler_params=_cparams(("arbitrary",)),
        name="final_block",
    )(pos1, pos2, ys, h1, info, p, g.reshape(1, d), b.reshape(1, d), ple_w, ple_g.reshape(1, d), gate_w)


def _dispatch_plan(pos, ends_f, n_tok, tm, n_rows):
    n_tiles = n_rows // tm
    ends = ends_f[0, :N_EXPERTS].astype(jnp.int32)
    src_tok = jnp.zeros((n_rows,), jnp.int32).at[pos].set(
        jnp.arange(2 * n_tok, dtype=jnp.int32) % n_tok, unique_indices=True)
    n_active = ends[-1] // tm
    tile_ids = jnp.arange(n_tiles, dtype=jnp.int32)
    tile_expert = jnp.searchsorted(ends, jnp.minimum(tile_ids, n_active - 1) * tm, side="right")
    tile_expert = jnp.minimum(tile_expert, N_EXPERTS - 1).astype(jnp.int32)
    first = jnp.concatenate([jnp.ones((1,), jnp.int32),
                             (tile_expert[1:] != tile_expert[:-1]).astype(jnp.int32)])
    parity = (jnp.cumsum(first) - 1) % 2
    ids = jnp.arange(N_EXPERTS, dtype=jnp.int32)
    present = ends > jnp.concatenate([jnp.zeros((1,), jnp.int32), ends[:-1]])
    later = present[None, :] & (ids[None, :] > ids[:, None])
    next_of = jnp.min(jnp.where(later, ids[None, :], N_EXPERTS), axis=1)
    next_of = jnp.where(next_of == N_EXPERTS, -1, next_of)
    plan = (tile_expert, first, parity.astype(jnp.int32), next_of[tile_expert].astype(jnp.int32),
            n_active.astype(jnp.int32).reshape(1))
    return src_tok, plan


def _t5_bucket(rel):
    n = jnp.maximum(rel, 0)
    max_exact = REL_BUCKETS // 2
    nf = jnp.maximum(n, 1).astype(F32)
    large = max_exact + (jnp.log(nf / max_exact) / math.log(REL_MAX_DIST / max_exact)
                         * (REL_BUCKETS - max_exact)).astype(jnp.int32)
    large = jnp.minimum(large, REL_BUCKETS - 1)
    return jnp.where(n < max_exact, n, large)


def _bias_tables(rel_bias, tq):
    n = 3 * tq
    rel = 2 * tq - 1 - jnp.arange(n, dtype=jnp.int32)
    g = rel_bias[_t5_bucket(rel)].astype(F32).T
    skew = jnp.tile(g, (1, tq))[:, :tq * (n - 1)].reshape(-1, tq, n - 1)
    tiles = jnp.stack([skew[:, :, 2 * tq - 1:3 * tq - 1], skew[:, :, tq - 1:2 * tq - 1]], axis=1)
    far = rel_bias[_t5_bucket(jnp.full((), 2 * tq, jnp.int32))].astype(F32)
    return tiles, far


def kernel(x, p, ln_in_g, ln_in_b, rel_bias, w_in, diff_lam_q1, diff_lam_k1, diff_lam_q2, diff_lam_k2, diff_subln_g, rwkv_mu, rwkv_w0, rwkv_w2, rwkv_a0, rwkv_a2, rwkv_g2, rwkv_k_k, rwkv_k_a, rwkv_r_k, rwkv_lnx_g, rwkv_lnx_b, w_out, ln1_g, ln1_b, router_group_w, router_group_b, router_expert_w, router_expert_b, moe_w_gate, moe_w_up, moe_w_down, ln2_g, ln2_b, ple_w, ple_norm_g, ple_gate_w):
    bsz, seq, d = x.shape
    t = bsz * seq
    i = 0
    lambda_init = 0.8 - 0.6 * math.exp(-0.3 * i)
    tq = 256

    x2 = x.reshape(t, d)
    w_in_b = w_in[i].astype(BF16)
    w_qkv = w_in_b[:, :O_RWKV]
    w_rw = jnp.pad(w_in_b[:, O_RWKV:], ((0, 0), (0, RWKV_PROJ_PAD - RWKV_PROJ)))
    h, hb = layer_norm_rows(x2, ln_in_g, ln_in_b, 512)
    qkv = matmul(hb, w_qkv, BF16, 1024, 512)
    pr = matmul(hb, w_rw, F32, 1024, 512)

    lam = (jnp.exp(jnp.sum(diff_lam_q1[i].astype(F32) * diff_lam_k1[i]))
           - jnp.exp(jnp.sum(diff_lam_q2[i].astype(F32) * diff_lam_k2[i])) + lambda_init)
    bias_tiles, far_bias = _bias_tables(rel_bias, tq)
    y_diff = diff_attention(qkv.reshape(bsz, seq, O_RWKV), bias_tiles, far_bias, lam.reshape(1),
                            diff_subln_g[i], lambda_init, tq, 8)

    mu_p = jnp.pad(rwkv_mu[i], (0, RWKV_PROJ_PAD - RWKV_PROJ))
    zeros64 = jnp.zeros((LORA_DECAY, RWKV_WIDTH), F32)
    w2p = jnp.concatenate([rwkv_w2[i], zeros64], axis=0).astype(BF16)
    a2p = jnp.concatenate([zeros64, rwkv_a2[i]], axis=0).astype(BF16)
    g2p = jnp.pad(rwkv_g2[i], ((0, 2 * LANES - LORA_GATE), (0, 0))).astype(BF16)
    r, k2, v, lw, kn, bb, g, bonus = rwkv_prep(
        pr, mu_p, rwkv_w0[i], w2p, rwkv_a0[i], a2p, g2p, rwkv_k_k[i], rwkv_k_a[i],
        rwkv_r_k[i].reshape(-1), seq, 256)
    sh = lambda a: a.reshape(bsz, seq, RWKV_WIDTH)
    y_scan = rwkv_scan(sh(r), sh(k2), sh(v), sh(lw), sh(kn), sh(bb), 64, 8)

    w_out_b = w_out[i].astype(BF16)
    rw = jnp.concatenate([router_group_w[i], router_expert_w[i]], axis=1)
    rw = jnp.pad(rw, ((0, 0), (0, ROUTER_LANES - rw.shape[1])))
    rw_hi = rw.astype(BF16)
    rw_lo = (rw - rw_hi.astype(F32)).astype(BF16)
    rb = jnp.pad(jnp.concatenate([router_group_b[i], router_expert_b[i]]),
                 (0, ROUTER_LANES - N_GROUPS - N_EXPERTS)).reshape(1, ROUTER_LANES)
    h1, info = outproj_ln_route(y_diff.reshape(t, DIFF_WIDTH), y_scan.reshape(t, RWKV_WIDTH), bonus, g,
                                rwkv_lnx_g[i], rwkv_lnx_b[i], h, w_out_b[:DIFF_WIDTH],
                                w_out_b[DIFF_WIDTH:], ln1_g[i], ln1_b[i], rw_hi, rw_lo, rb, 256)

    tm_e = MOE_ROW_TILE
    n_rows = 2 * t + N_EXPERTS * tm_e
    e_all = jnp.concatenate([info[:, 0], info[:, 1]]).astype(jnp.int32)
    pos, ends = moe_rank(e_all.reshape(-1, 1), 1024, tm_e)
    pos = pos[:, 0]
    src_tok, plan = _dispatch_plan(pos, ends, t, tm_e, n_rows)
    f = moe_w_gate.shape[-1]
    ys = moe_experts(src_tok, plan, h1.reshape(t, d // LANES, LANES), moe_w_gate[i].reshape(N_EXPERTS, d, f),
                     moe_w_up[i].reshape(N_EXPERTS, d, f), moe_w_down[i].reshape(N_EXPERTS, f, d),
                     n_rows, tm_e)

    out = final_block(pos[:t], pos[t:], ys, h1, info, p[i].reshape(t, -1), ln2_g[i], ln2_b[i],
                      ple_w[i].astype(BF16), ple_norm_g[i], ple_gate_w[i].astype(BF16), 256)
    return out.reshape(bsz, seq, d)
```

```python
import functools
import math

import jax
import jax.numpy as jnp
from jax import lax
from jax.experimental import pallas as pl
from jax.experimental.pallas import tpu as pltpu

F32 = jnp.float32
BF16 = jnp.bfloat16

DIFF_HEADS = 8
DIFF_HEAD_DIM = 64
DIFF_V_DIM = 128
DIFF_WIDTH = DIFF_HEADS * DIFF_V_DIM
DIFF_QK_WIDTH = DIFF_HEADS * 2 * DIFF_HEAD_DIM
RWKV_HEADS = 16
RWKV_HEAD_SIZE = 64
RWKV_WIDTH = RWKV_HEADS * RWKV_HEAD_SIZE
LORA_DECAY = 64
LORA_AAA = 64
LORA_GATE = 160
RWKV_PROJ = 3 * RWKV_WIDTH + LORA_DECAY + LORA_AAA + LORA_GATE
O_RWKV = 2 * DIFF_QK_WIDTH + DIFF_WIDTH
REL_BUCKETS = 32
REL_MAX_DIST = 128
N_GROUPS = 4
EXPERTS_PER_GROUP = 8
N_EXPERTS = N_GROUPS * EXPERTS_PER_GROUP
LN_EPS = 1e-5
RWKV_GN_EPS = 64e-5
NEG_INF = -1e30
DEPTH = 1
DEEPNORM_ALPHA = (2 * DEPTH) ** 0.25
LOG2_E = math.log2(math.e)

LANES = 128
RWKV_PROJ_PAD = 3584
ROUTER_LANES = LANES
VMEM_LIMIT = 48 * 1024 * 1024
MOE_VMEM_LIMIT = 56 * 1024 * 1024
MOE_ROW_TILE = 256


def _cparams(sem):
    return pltpu.CompilerParams(dimension_semantics=sem, vmem_limit_bytes=VMEM_LIMIT)


def _bdot(a, b):
    return jnp.dot(a.astype(BF16), b.astype(BF16), preferred_element_type=F32)


def _bdot_nt(a, b):
    return lax.dot_general(a.astype(BF16), b.astype(BF16), (((1,), (1,)), ((), ())),
                           preferred_element_type=F32)


def _bdot_tn(a, b):
    return lax.dot_general(a.astype(BF16), b.astype(BF16), (((0,), (0,)), ((), ())),
                           preferred_element_type=F32)


def _split3(x):
    hi = x.astype(BF16)
    r1 = x - hi.astype(F32)
    mid = r1.astype(BF16)
    lo = (r1 - mid.astype(F32)).astype(BF16)
    return hi, mid, lo


def _head_sum(x):
    lane = lax.broadcasted_iota(jnp.int32, (x.shape[0], LANES), 1)
    first = lane < RWKV_HEAD_SIZE
    cols = []
    for c in range(x.shape[1] // LANES):
        blk = x[:, c * LANES:(c + 1) * LANES]
        lo = jnp.sum(jnp.where(first, blk, 0.0), axis=1, keepdims=True)
        hi = jnp.sum(jnp.where(first, 0.0, blk), axis=1, keepdims=True)
        cols.append(jnp.where(first, lo, hi))
    return jnp.concatenate(cols, axis=1)


def _dot_sel_x(sel, x):
    hi, mid, lo = _split3(x)
    d = lambda p: jnp.dot(sel, p, preferred_element_type=F32)
    return d(hi) + d(mid) + d(lo)


def _layer_norm(x, g, b):
    mu = jnp.mean(x, axis=-1, keepdims=True)
    xc = x - mu
    var = jnp.mean(xc * xc, axis=-1, keepdims=True)
    return xc * lax.rsqrt(var + LN_EPS) * g + b


def _sigmoid(z):
    return 1.0 / (1.0 + jnp.exp(-z))


def _ln_kernel(x_ref, g_ref, b_ref, h_ref, hb_ref):
    xn = _layer_norm(x_ref[...], g_ref[...], b_ref[...])
    h_ref[...] = xn
    hb_ref[...] = xn.astype(BF16)


def layer_norm_rows(x, g, b, tm):
    m, d = x.shape
    tile = pl.BlockSpec((tm, d), lambda i: (i, 0))
    row = pl.BlockSpec((1, d), lambda i: (0, 0))
    return pl.pallas_call(
        _ln_kernel,
        grid=(m // tm,),
        in_specs=[tile, row, row],
        out_specs=[tile, tile],
        out_shape=[jax.ShapeDtypeStruct((m, d), F32), jax.ShapeDtypeStruct((m, d), BF16)],
        compiler_params=_cparams(("parallel",)),
        name="layer_norm_rows",
    )(x, g.reshape(1, d), b.reshape(1, d))


def _mm_nt_kernel(x_ref, wt_ref, o_ref, *, n_valid, tn):
    y = lax.dot_general(x_ref[...], wt_ref[...].astype(BF16), (((1,), (1,)), ((), ())),
                        preferred_element_type=F32)
    if n_valid % tn:
        col = pl.program_id(1) * tn + lax.broadcasted_iota(jnp.int32, y.shape, 1)
        y = jnp.where(col < n_valid, y, 0.0)
    o_ref[...] = y.astype(o_ref.dtype)


def matmul_nt(x, wt, row0, n_valid, n_out, out_dtype, tm, tn):
    m, k = x.shape
    blk0 = row0 // tn
    return pl.pallas_call(
        functools.partial(_mm_nt_kernel, n_valid=n_valid, tn=tn),
        grid=(m // tm, n_out // tn),
        in_specs=[pl.BlockSpec((tm, k), lambda i, j: (i, 0)),
                  pl.BlockSpec((tn, k), lambda i, j: (blk0 + j, 0))],
        out_specs=pl.BlockSpec((tm, tn), lambda i, j: (i, j)),
        out_shape=jax.ShapeDtypeStruct((m, n_out), out_dtype),
        compiler_params=_cparams(("parallel", "parallel")),
        name="matmul_nt",
    )(x, wt)


def _attn_kernel(lam_ref, far_ref, q_ref, k_ref, v_ref, bias_ref, g_ref, o_ref,
                 m_scr, l_scr, acc_scr, *, tq, hps, scale, post_scale):
    hp = pl.program_id(1)
    qi = pl.program_id(2)
    lane = lax.broadcasted_iota(jnp.int32, (tq, LANES), 1)
    q_maps = []
    for hh in range(hps):
        qs = (q_ref[0, :, hh * LANES:(hh + 1) * LANES].astype(F32) * scale).astype(BF16)
        zero = jnp.zeros_like(qs)
        q_maps += [jnp.where(lane < DIFF_HEAD_DIM, qs, zero), jnp.where(lane >= DIFF_HEAD_DIM, qs, zero)]

    m_scr[...] = jnp.full(m_scr.shape, NEG_INF, F32)
    l_scr[...] = jnp.zeros(l_scr.shape, F32)
    acc_scr[...] = jnp.zeros(acc_scr.shape, F32)

    def step(kstart, bias_index, causal):
        for hh in range(hps):
            kb = k_ref[0, pl.ds(kstart, tq), hh * LANES:(hh + 1) * LANES]
            vb = v_ref[0, pl.ds(kstart, tq), hh * LANES:(hh + 1) * LANES]
            for c in range(2 * hh, 2 * hh + 2):
                s = lax.dot_general(q_maps[c], kb, (((1,), (1,)), ((), ())), preferred_element_type=F32)
                m_old = m_scr[c]
                if bias_index is None:
                    far = far_ref[hp * hps + hh]
                    m_new = jnp.maximum(m_old, jnp.max(s, axis=1, keepdims=True) + far)
                    shift = m_new - far
                else:
                    s = s + bias_ref[hh, bias_index]
                    if causal:
                        ri = lax.broadcasted_iota(jnp.int32, (tq, tq), 0)
                        ci = lax.broadcasted_iota(jnp.int32, (tq, tq), 1)
                        s = jnp.where(ri >= ci, s, NEG_INF)
                    m_new = jnp.maximum(m_old, jnp.max(s, axis=1, keepdims=True))
                    shift = m_new
                p = jnp.exp2(s - jnp.concatenate([shift] * (tq // LANES), axis=1))
                corr = jnp.exp2(m_old - m_new)
                l_scr[c] = corr * l_scr[c] + jnp.sum(p, axis=1, keepdims=True)
                acc_scr[c] = corr * acc_scr[c] + jnp.dot(p.astype(BF16), vb, preferred_element_type=F32)
                m_scr[c] = m_new

    def far_body(kb, carry):
        step(pl.multiple_of(kb * tq, tq), None, False)
        return carry

    lax.fori_loop(0, jnp.maximum(qi - 1, 0), far_body, 0)

    @pl.when(qi >= 1)
    def _():
        step(pl.multiple_of((qi - 1) * tq, tq), 1, False)

    step(pl.multiple_of(qi * tq, tq), 0, True)

    for hh in range(hps):
        c = 2 * hh
        o = acc_scr[c] / l_scr[c] - lam_ref[0] * (acc_scr[c + 1] / l_scr[c + 1])
        ms = jnp.mean(o * o, axis=1, keepdims=True)
        o_ref[0, :, hh * LANES:(hh + 1) * LANES] = (
            o * lax.rsqrt(ms + LN_EPS) * g_ref[...] * post_scale).astype(o_ref.dtype)


def diff_attention(qkv, bias_tiles, far_bias, lam, subln_g, lambda_init, tq, hps):
    bsz, s, _ = qkv.shape
    kern = functools.partial(_attn_kernel, tq=tq, hps=hps, scale=DIFF_HEAD_DIM ** -0.5 * LOG2_E,
                             post_scale=1.0 - lambda_init)
    hw = hps * LANES
    nqk = DIFF_QK_WIDTH // hw
    return pl.pallas_call(
        kern,
        grid=(bsz, DIFF_HEADS // hps, s // tq),
        in_specs=[
            pl.BlockSpec(memory_space=pltpu.SMEM),
            pl.BlockSpec(memory_space=pltpu.SMEM),
            pl.BlockSpec((1, tq, hw), lambda b, h, i: (b, i, h)),
            pl.BlockSpec((1, s, hw), lambda b, h, i: (b, 0, nqk + h)),
            pl.BlockSpec((1, s, hw), lambda b, h, i: (b, 0, 2 * nqk + h)),
            pl.BlockSpec((hps, 2, tq, tq), lambda b, h, i: (h, 0, 0, 0)),
            pl.BlockSpec((1, LANES), lambda b, h, i: (0, 0)),
        ],
        out_specs=pl.BlockSpec((1, tq, hw), lambda b, h, i: (b, i, h)),
        out_shape=jax.ShapeDtypeStruct((bsz, s, DIFF_WIDTH), BF16),
        scratch_shapes=[pltpu.VMEM((2 * hps, tq, LANES), F32), pltpu.VMEM((2 * hps, tq, LANES), F32),
                        pltpu.VMEM((2 * hps, tq, LANES), F32)],
        compiler_params=_cparams(("parallel", "parallel", "arbitrary")),
        name="diff_attention",
    )(lam, far_bias, qkv, qkv, qkv, bias_tiles, subln_g.reshape(1, LANES))


def _rwkv_prep_kernel(x_ref, prev_ref, mu_ref, w0_ref, w2_ref, a0_ref, a2_ref, g2_ref,
                      kk_ref, ka_ref, rk_ref,
                      r_out, k_out, v_out, lw_out, kn_out, b_out, g_out, bonus_out, *, tm, seq):
    i = pl.program_id(0)
    x = x_ref[...]
    w = RWKV_WIDTH
    row = lax.broadcasted_iota(jnp.int32, x.shape, 0)
    is_start = (i * tm) % seq == 0
    last_prev = jnp.where(is_start, 0.0, prev_ref[7:8, :])
    prev = jnp.where(row == 0, last_prev, pltpu.roll(x, 1, 0))
    xs = x + (prev - x) * mu_ref[...]

    r = xs[:, 0:w]
    k = xs[:, w:2 * w]
    v = xs[:, 2 * w:3 * w]
    xwa = xs[:, 3 * w:3 * w + 2 * LORA_DECAY]
    xg = xs[:, 3 * w + LANES:3 * w + LANES + 2 * LANES]

    z = w0_ref[...] + _bdot(jnp.tanh(xwa), w2_ref[...])
    softplus_neg_z = jnp.maximum(-z, 0.0) + jnp.log(1.0 + jnp.exp(-jnp.abs(z)))
    lw_out[...] = -jnp.exp(-softplus_neg_z - 0.5)
    a_lr = _sigmoid(a0_ref[...] + _bdot(xwa, a2_ref[...]))
    g_out[...] = _bdot(_sigmoid(xg), g2_ref[...])

    kk = k * kk_ref[...]
    kn = kk / jnp.maximum(jnp.sqrt(_head_sum(kk * kk)), 1e-12)
    k2 = k * (1.0 + (a_lr - 1.0) * ka_ref[...])
    r_out[...] = r
    k_out[...] = k2
    v_out[...] = v
    kn_out[...] = kn
    b_out[...] = kn * a_lr
    bonus_out[...] = _head_sum(r * k2 * rk_ref[...]) * v


def rwkv_prep(pr, mu, w0, w2p, a0, a2p, g2p, k_k, k_a, r_k, seq, tm):
    t, wp = pr.shape
    w = RWKV_WIDTH
    row = lambda a: a.reshape(1, -1)
    full = lambda shp: pl.BlockSpec(shp, lambda i: (0, 0))
    outs = [jax.ShapeDtypeStruct((t, w), F32)] * 8
    kern = functools.partial(_rwkv_prep_kernel, tm=tm, seq=seq)
    return pl.pallas_call(
        kern,
        grid=(t // tm,),
        in_specs=[
            pl.BlockSpec((tm, wp), lambda i: (i, 0)),
            pl.BlockSpec((8, wp), lambda i: (jnp.maximum(i * (tm // 8) - 1, 0), 0)),
            full((1, wp)), full((1, w)), full((LANES, w)), full((1, w)), full((LANES, w)),
            full((2 * LANES, w)), full((1, w)), full((1, w)), full((1, w)),
        ],
        out_specs=[pl.BlockSpec((tm, w), lambda i: (i, 0))] * 8,
        out_shape=outs,
        compiler_params=_cparams(("parallel",)),
        name="rwkv_prep",
    )(pr, pr, row(mu), row(w0), w2p, row(a0), a2p, g2p, row(k_k), row(k_a), row(r_k))


def _rwkv_scan_kernel(r_ref, k_ref, v_ref, lw_ref, kn_ref, b_ref, y_ref, s_scr, *, chunk, pairs):
    c2 = 2 * chunk

    @pl.when(pl.program_id(2) == 0)
    def _():
        s_scr[...] = jnp.zeros(s_scr.shape, F32)

    ri = lax.broadcasted_iota(jnp.int32, (chunk, chunk), 0)
    ci = lax.broadcasted_iota(jnp.int32, (chunk, chunk), 1)
    ltri = jnp.where(ri >= ci, 1.0, 0.0).astype(BF16)

    lw = lw_ref[0]
    cl = _dot_sel_x(ltri, lw)
    mid = chunk // 2 - 1
    clm = cl[mid:mid + 1, :]
    cle = cl[chunk - 1:chunk, :]
    w_mid = jnp.exp(cl - clm)
    w_mid_prev = jnp.exp(cl - lw - clm)
    w_mid_inv = jnp.exp(clm - cl)
    w_abs = jnp.exp(cl)
    w_abs_prev = jnp.exp(cl - lw)
    w_end = jnp.exp(cle - cl)
    w_tot = jnp.exp(cle)

    r = r_ref[0]
    k = k_ref[0]
    v = v_ref[0]
    a = -kn_ref[0]
    b = b_ref[0]
    a_mid = a * w_mid_prev
    r_mid = r * w_mid
    b_mid = b * w_mid_inv
    k_mid = k * w_mid_inv
    a_abs = a * w_abs_prev
    r_abs = r * w_abs
    b_end = b * w_end
    k_end = k * w_end

    lane = lax.broadcasted_iota(jnp.int32, (chunk, LANES), 1)
    first = lane < RWKV_HEAD_SIZE

    def stack(x):
        return jnp.concatenate([jnp.where(first, x, 0.0), jnp.where(first, 0.0, x)], axis=0)

    rr = lax.broadcasted_iota(jnp.int32, (c2, c2), 0)
    cc = lax.broadcasted_iota(jnp.int32, (c2, c2), 1)
    same = (rr >= chunk) == (cc >= chunk)
    strict = same & (rr > cc)
    incl = same & (rr >= cc)
    eye = jnp.where(rr == cc, 1.0, 0.0)

    ps = range(pairs)
    sls = [slice(p * LANES, (p + 1) * LANES) for p in ps]
    bf = lambda x: x.astype(BF16)
    cat0 = lambda xs: jnp.concatenate(xs, axis=0)
    cat1 = lambda xs: jnp.concatenate(xs, axis=1)
    v_s = [bf(stack(v[:, sl])) for sl in sls]
    ar_m = [bf(cat0([stack(a_mid[:, sl]), stack(r_mid[:, sl])])) for sl in sls]
    bk_m = [bf(cat0([stack(b_mid[:, sl]), stack(k_mid[:, sl])])) for sl in sls]
    g_all = [_bdot_nt(ar_m[p], bk_m[p]) for p in ps]
    l_ab = [jnp.where(strict, g[:c2, :c2], 0.0) for g in g_all]
    l_ak = [bf(jnp.where(strict, g[:c2, c2:], 0.0)) for g in g_all]
    m_r = [bf(cat1([jnp.where(incl, g[c2:, :c2], 0.0), jnp.where(incl, g[c2:, c2:], 0.0)])) for g in g_all]

    t_inv = [eye + l_ab[p] for p in ps]
    n_fac = max(chunk.bit_length() - 2, 0)
    lp = [_bdot(l_ab[p], l_ab[p]) for p in ps] if n_fac else l_ab
    for _ in range(n_fac - 1):
        x = [_bdot(cat0([t_inv[p], lp[p]]), lp[p]) for p in ps]
        t_inv = [t_inv[p] + x[p][:c2] for p in ps]
        lp = [x[p][c2:] for p in ps]
    if n_fac:
        t_inv = [t_inv[p] + _bdot(t_inv[p], lp[p]) for p in ps]
    t_inv = [bf(t) for t in t_inv]

    lv = [_bdot(l_ak[p], v_s[p]) for p in ps]
    av = [bf(_bdot(t_inv[p], cat1([stack(a_abs[:, sls[p]]), lv[p]]))) for p in ps]
    zero = jnp.zeros((c2, LANES), BF16)
    ry = [_bdot(m_r[p], cat0([av[p], cat1([zero, v_s[p]])])) for p in ps]
    r_p = [stack(r_abs[:, sls[p]]) + ry[p][:, :LANES] for p in ps]

    s_old = [s_scr[p] for p in ps]
    y_s = [_bdot_nt(r_p[p], s_old[p]) + ry[p][:, LANES:] for p in ps]
    for p in ps:
        y_ref[0, :, sls[p]] = y_s[p][:chunk] + y_s[p][chunk:]
    b_e = [bf(stack(b_end[:, sl])) for sl in sls]
    k_e = [bf(stack(k_end[:, sl])) for sl in sls]
    ab = [_bdot_tn(av[p][:, :LANES], b_e[p]) for p in ps]
    q_t = [_bdot_tn(cat0([av[p][:, LANES:], v_s[p]]), cat0([b_e[p], k_e[p]])) for p in ps]
    for p in ps:
        s_scr[p] = s_old[p] * w_tot[:, sls[p]] + _bdot(s_old[p], ab[p]) + q_t[p]


def rwkv_scan(r, k, v, lw, kn, b, chunk, pairs):
    bsz, s, w = r.shape
    pw = pairs * LANES
    spec = pl.BlockSpec((1, chunk, pw), lambda bi, hi, ci: (bi, ci, hi))
    kern = functools.partial(_rwkv_scan_kernel, chunk=chunk, pairs=pairs)
    return pl.pallas_call(
        kern,
        grid=(bsz, w // pw, s // chunk),
        in_specs=[spec] * 6,
        out_specs=spec,
        out_shape=jax.ShapeDtypeStruct((bsz, s, w), F32),
        scratch_shapes=[pltpu.VMEM((pairs, LANES, LANES), F32)],
        compiler_params=_cparams(("parallel", "parallel", "arbitrary")),
        name="rwkv_scan",
    )(r, k, v, lw, kn, b)


def _rwkv_out(y, bonus, g, lnx_g, lnx_b):
    inv_n = 1.0 / RWKV_HEAD_SIZE
    mu = _head_sum(y) * inv_n
    yc = y - mu
    var = _head_sum(yc * yc) * inv_n
    yn = yc * lax.rsqrt(var + RWKV_GN_EPS) * lnx_g + lnx_b
    return (yn + bonus) * g


def _route(logits):
    lane = lax.broadcasted_iota(jnp.int32, logits.shape, 1)
    big = jnp.int32(4 * LANES)
    gmask = lane < N_GROUPS
    gl = jnp.where(gmask, logits, NEG_INF)
    gmax = jnp.max(gl, axis=1, keepdims=True)
    gidx = jnp.min(jnp.where(gl == gmax, lane, big), axis=1, keepdims=True)
    gtop = 1.0 / jnp.sum(jnp.where(gmask, jnp.exp(gl - gmax), 0.0), axis=1, keepdims=True)
    lo = N_GROUPS + EXPERTS_PER_GROUP * gidx
    emask = (lane >= lo) & (lane < lo + EXPERTS_PER_GROUP)
    el = jnp.where(emask, logits, NEG_INF)
    e1 = jnp.max(el, axis=1, keepdims=True)
    i1 = jnp.min(jnp.where(el == e1, lane, big), axis=1, keepdims=True)
    el2 = jnp.where(lane == i1, NEG_INF, el)
    e2 = jnp.max(el2, axis=1, keepdims=True)
    i2 = jnp.min(jnp.where(el2 == e2, lane, big), axis=1, keepdims=True)
    t = jnp.exp(e2 - e1)
    w1 = gtop / (1.0 + t)
    w2 = gtop * t / (1.0 + t)
    id1 = (i1 - N_GROUPS).astype(F32)
    id2 = (i2 - N_GROUPS).astype(F32)
    return jnp.where(lane == 0, id1, jnp.where(lane == 1, id2, jnp.where(lane == 2, w1,
                                                                         jnp.where(lane == 3, w2, 0.0))))


def _outproj_kernel(yd_ref, ys_ref, bonus_ref, gate_ref, lg_ref, lb_ref, h_ref,
                    w1_ref, w2_ref, g_ref, b_ref, rw_hi_ref, rw_lo_ref, rb_ref, h1_ref, info_ref):
    yr = _rwkv_out(ys_ref[...], bonus_ref[...], gate_ref[...], lg_ref[...], lb_ref[...])
    mix = (jnp.dot(yd_ref[...], w1_ref[...], preferred_element_type=F32)
           + jnp.dot(yr.astype(BF16), w2_ref[...], preferred_element_type=F32))
    h1 = _layer_norm(DEEPNORM_ALPHA * h_ref[...] + mix, g_ref[...], b_ref[...])
    h1_ref[...] = h1
    hi = h1.astype(BF16)
    lo = (h1 - hi.astype(F32)).astype(BF16)
    d = lambda a, b: jnp.dot(a, b, preferred_element_type=F32)
    logits = d(hi, rw_hi_ref[...]) + d(lo, rw_hi_ref[...]) + d(hi, rw_lo_ref[...]) + rb_ref[...]
    info_ref[...] = _route(logits)


def outproj_ln_route(yd, ys, bonus, gate, lnx_g, lnx_b, h, w_out_a, w_out_b, g, b, rw_hi, rw_lo, rb, tm):
    t, d = h.shape
    ka = yd.shape[1]
    kb = ys.shape[1]
    full = lambda shp: pl.BlockSpec(shp, lambda i: (0, 0))
    tile = lambda n: pl.BlockSpec((tm, n), lambda i: (i, 0))
    return pl.pallas_call(
        _outproj_kernel,
        grid=(t // tm,),
        in_specs=[
            tile(ka), tile(kb), tile(kb), tile(kb), full((1, kb)), full((1, kb)), tile(d),
            full((ka, d)), full((kb, d)), full((1, d)), full((1, d)),
            full((d, ROUTER_LANES)), full((d, ROUTER_LANES)), full((1, ROUTER_LANES)),
        ],
        out_specs=[
            pl.BlockSpec((tm, d), lambda i: (i, 0)),
            pl.BlockSpec((tm, ROUTER_LANES), lambda i: (i, 0)),
        ],
        out_shape=[jax.ShapeDtypeStruct((t, d), F32), jax.ShapeDtypeStruct((t, ROUTER_LANES), F32)],
        compiler_params=_cparams(("parallel",)),
        name="outproj_ln_route",
    )(yd, ys, bonus, gate, lnx_g.reshape(1, kb), lnx_b.reshape(1, kb), h, w_out_a, w_out_b,
      g.reshape(1, d), b.reshape(1, d), rw_hi, rw_lo, rb)


def _moe_rank_kernel(e_ref, pos_ref, ends_ref, run_scr, start_scr, *, tm, row_tile):
    ph = pl.program_id(0)
    i = pl.program_id(1)
    lane = lax.broadcasted_iota(jnp.int32, (tm, LANES), 1)
    onehot = e_ref[...] == lane
    oh = jnp.where(onehot, 1.0, 0.0)

    @pl.when((ph == 0) & (i == 0))
    def _():
        run_scr[...] = jnp.zeros(run_scr.shape, F32)

    @pl.when(ph == 0)
    def _():
        run_scr[...] += jnp.sum(oh, axis=0, keepdims=True)

    @pl.when((ph == 1) & (i == 0))
    def _():
        padded = jnp.floor((run_scr[...] + (row_tile - 1)) * (1.0 / row_tile)) * row_tile
        ri = lax.broadcasted_iota(jnp.int32, (LANES, LANES), 0)
        ci = lax.broadcasted_iota(jnp.int32, (LANES, LANES), 1)
        upper = jnp.where(ri < ci, 1.0, 0.0).astype(BF16)
        parts = _split3(jnp.broadcast_to(padded, (8, LANES)))
        starts = sum(jnp.dot(q, upper, preferred_element_type=F32) for q in parts)[0:1]
        start_scr[...] = starts
        ends_ref[...] = starts + padded
        run_scr[...] = jnp.zeros(run_scr.shape, F32)

    @pl.when(ph == 1)
    def _():
        ri = lax.broadcasted_iota(jnp.int32, (tm, tm), 0)
        ci = lax.broadcasted_iota(jnp.int32, (tm, tm), 1)
        before = jnp.where(ri > ci, 1.0, 0.0).astype(BF16)
        row = jnp.dot(before, oh.astype(BF16), preferred_element_type=F32) + (run_scr[...] + start_scr[...])
        pos_ref[...] = jnp.sum(jnp.where(onehot, row, 0.0), axis=1, keepdims=True).astype(jnp.int32)
        run_scr[...] += jnp.sum(oh, axis=0, keepdims=True)


def moe_rank(e_col, tm, row_tile):
    n = e_col.shape[0]
    return pl.pallas_call(
        functools.partial(_moe_rank_kernel, tm=tm, row_tile=row_tile),
        grid=(2, n // tm),
        in_specs=[pl.BlockSpec((tm, 1), lambda ph, i: (i, 0))],
        out_specs=[pl.BlockSpec((tm, 1), lambda ph, i: (i * ph, 0)),
                   pl.BlockSpec((1, LANES), lambda ph, i: (0, 0))],
        out_shape=[jax.ShapeDtypeStruct((n, 1), jnp.int32), jax.ShapeDtypeStruct((1, LANES), F32)],
        scratch_shapes=[pltpu.VMEM((1, LANES), F32), pltpu.VMEM((1, LANES), F32)],
        compiler_params=_cparams(("arbitrary", "arbitrary")),
        name="moe_rank",
    )(e_col)


def _row_copy(src_hbm, row, dst, r, sem):
    return pltpu.make_async_copy(src_hbm.at[pl.ds(row, 1)], dst.at[pl.ds(r, 1)], sem)


def _moe_expert_kernel(src_ref, te_ref, first_ref, par_ref, nxt_ref, nact_ref,
                       x_hbm, wg_hbm, wu_hbm, wd_hbm, o_ref,
                       xbuf0, xbuf1, wgf, wuf, wdf, wgb, wub, wdb, sem, wsem, *, tm):
    j = pl.program_id(0)
    n_act = nact_ref[0]
    bufs = (xbuf0, xbuf1)

    def token_copy(tok, into, r):
        return pltpu.make_async_copy(x_hbm.at[tok], bufs[into].at[r], sem.at[into])

    def start_gather(tile, into):
        for r in range(tm):
            token_copy(src_ref[tile * tm + r], into, r).start(priority=0)

    def wait_gather(into):
        for r in range(tm):
            token_copy(0, into, r).wait()

    def weight_copies(e, par):
        return (pltpu.make_async_copy(wg_hbm.at[e], wgf.at[par], wsem.at[par]),
                pltpu.make_async_copy(wu_hbm.at[e], wuf.at[par], wsem.at[par]),
                pltpu.make_async_copy(wd_hbm.at[e], wdf.at[par], wsem.at[par]))

    @pl.when(j == 0)
    def _():
        start_gather(0, 0)
        for c in weight_copies(te_ref[0], 0):
            c.start(priority=1)

    @pl.when((j < n_act) & (first_ref[j] > 0))
    def _():
        par = par_ref[j]
        for c in weight_copies(te_ref[j], par):
            c.wait()

        @pl.when(nxt_ref[j] >= 0)
        def _():
            for c in weight_copies(nxt_ref[j], 1 - par):
                c.start(priority=1)

        wgb[...] = wgf[par].astype(BF16)
        wub[...] = wuf[par].astype(BF16)
        wdb[...] = wdf[par].astype(BF16)

    for slot in range(2):
        @pl.when((j < n_act) & (j % 2 == slot))
        def _():
            wait_gather(slot)
            start_gather(jnp.minimum(j + 1, n_act - 1), 1 - slot)
            x = jnp.concatenate([bufs[slot][:, c, :] for c in range(bufs[slot].shape[1])], axis=1).astype(BF16)
            hg = jnp.dot(x, wgb[...], preferred_element_type=F32)
            hu = jnp.dot(x, wub[...], preferred_element_type=F32)
            hid = hg * _sigmoid(hg) * hu
            o_ref[...] = jnp.dot(hid.astype(BF16), wdb[...], preferred_element_type=F32)

        @pl.when((j == n_act - 1) & (j % 2 == slot))
        def _():
            wait_gather(1 - slot)

    @pl.when(j >= n_act)
    def _():
        o_ref[...] = jnp.zeros(o_ref.shape, F32)


def moe_experts(src_tok, plan, x, wg, wu, wd, n_rows, tm):
    d, f = wg.shape[1:]
    hbm = pl.BlockSpec(memory_space=pl.ANY)
    return pl.pallas_call(
        functools.partial(_moe_expert_kernel, tm=tm),
        grid_spec=pltpu.PrefetchScalarGridSpec(
            num_scalar_prefetch=6,
            grid=(n_rows // tm,),
            in_specs=[hbm, hbm, hbm, hbm],
            out_specs=pl.BlockSpec((tm, d), lambda j, *_: (j, 0)),
            scratch_shapes=[pltpu.VMEM((tm, d // LANES, LANES), F32), pltpu.VMEM((tm, d // LANES, LANES), F32),
                            pltpu.VMEM((2, d, f), F32), pltpu.VMEM((2, d, f), F32), pltpu.VMEM((2, f, d), F32),
                            pltpu.VMEM((d, f), BF16), pltpu.VMEM((d, f), BF16), pltpu.VMEM((f, d), BF16),
                            pltpu.SemaphoreType.DMA((2,)), pltpu.SemaphoreType.DMA((2,))],
        ),
        out_shape=jax.ShapeDtypeStruct((n_rows, d), F32),
        compiler_params=pltpu.CompilerParams(dimension_semantics=("arbitrary",),
                                             vmem_limit_bytes=MOE_VMEM_LIMIT),
        name="moe_experts",
    )(src_tok, *plan, x, wg, wu, wd)


def _final_kernel(p1_ref, p2_ref, y_hbm, h1_ref, info_ref, p_ref, g_ref, b_ref, pw_ref, pg_ref, gw_ref,
                  o_ref, ybuf0, ybuf1, sem, *, tm, n_tiles):
    i = pl.program_id(0)
    bufs = (ybuf0, ybuf1)

    def start_gather(tile, into):
        for r in range(tm):
            _row_copy(y_hbm, p1_ref[tile * tm + r], bufs[into].at[0], r, sem.at[into]).start(priority=0)
            _row_copy(y_hbm, p2_ref[tile * tm + r], bufs[into].at[1], r, sem.at[into]).start(priority=1)

    def wait_gather(into):
        for r in range(tm):
            _row_copy(y_hbm, 0, bufs[into].at[0], r, sem.at[into]).wait()
            _row_copy(y_hbm, 0, bufs[into].at[1], r, sem.at[into]).wait()

    @pl.when(i == 0)
    def _():
        start_gather(0, 0)

    for slot in range(2):
        @pl.when(i % 2 == slot)
        def _():
            wait_gather(slot)
            start_gather(jnp.minimum(i + 1, n_tiles - 1), 1 - slot)
            info = info_ref[...]
            ffn = info[:, 2:3] * bufs[slot][0] + info[:, 3:4] * bufs[slot][1]
            h2 = _layer_norm(DEEPNORM_ALPHA * h1_ref[...] + ffn, g_ref[...], b_ref[...])
            pe = jnp.dot(p_ref[...].astype(BF16), pw_ref[...], preferred_element_type=F32)
            ple = pe * lax.rsqrt(jnp.mean(pe * pe, axis=-1, keepdims=True) + LN_EPS) * pg_ref[...]
            gate = _sigmoid(jnp.dot(h2.astype(BF16), gw_ref[...], preferred_element_type=F32))
            o_ref[...] = h2 + gate * ple

        @pl.when((i == n_tiles - 1) & (i % 2 == slot))
        def _():
            wait_gather(1 - slot)


def final_block(pos1, pos2, ys, h1, info, p, g, b, ple_w, ple_g, gate_w, tm):
    t, d = h1.shape
    dp = p.shape[1]
    tile = lambda n: pl.BlockSpec((tm, n), lambda i, p1, p2: (i, 0))
    full = lambda shp: pl.BlockSpec(shp, lambda i, p1, p2: (0, 0))
    return pl.pallas_call(
        functools.partial(_final_kernel, tm=tm, n_tiles=t // tm),
        grid_spec=pltpu.PrefetchScalarGridSpec(
            num_scalar_prefetch=2,
            grid=(t // tm,),
            in_specs=[pl.BlockSpec(memory_space=pl.ANY), tile(d), tile(ROUTER_LANES), tile(dp),
                      full((1, d)), full((1, d)), full((dp, d)), full((1, d)), full((d, d))],
            out_specs=tile(d),
            scratch_shapes=[pltpu.VMEM((2, tm, d), F32), pltpu.VMEM((2, tm, d), F32),
                            pltpu.SemaphoreType.DMA((2,))],
        ),
        out_shape=jax.ShapeDtypeStruct((t, d), F32),
        compiler_params=_cparams(("arbitrary",)),
        name="final_block",
    )(pos1, pos2, ys, h1, info, p, g.reshape(1, d), b.reshape(1, d), ple_w, ple_g.reshape(1, d), gate_w)


def _dispatch_plan(pos, ends_f, n_tok, tm, n_rows):
    n_tiles = n_rows // tm
    ends = ends_f[0, :N_EXPERTS].astype(jnp.int32)
    src_tok = jnp.zeros((n_rows,), jnp.int32).at[pos].set(
        jnp.arange(2 * n_tok, dtype=jnp.int32) % n_tok, unique_indices=True)
    n_active = ends[-1] // tm
    tile_ids = jnp.arange(n_tiles, dtype=jnp.int32)
    tile_expert = jnp.searchsorted(ends, jnp.minimum(tile_ids, n_active - 1) * tm, side="right")
    tile_expert = jnp.minimum(tile_expert, N_EXPERTS - 1).astype(jnp.int32)
    first = jnp.concatenate([jnp.ones((1,), jnp.int32),
                             (tile_expert[1:] != tile_expert[:-1]).astype(jnp.int32)])
    parity = (jnp.cumsum(first) - 1) % 2
    ids = jnp.arange(N_EXPERTS, dtype=jnp.int32)
    present = ends > jnp.concatenate([jnp.zeros((1,), jnp.int32), ends[:-1]])
    later = present[None, :] & (ids[None, :] > ids[:, None])
    next_of = jnp.min(jnp.where(later, ids[None, :], N_EXPERTS), axis=1)
    next_of = jnp.where(next_of == N_EXPERTS, -1, next_of)
    plan = (tile_expert, first, parity.astype(jnp.int32), next_of[tile_expert].astype(jnp.int32),
            n_active.astype(jnp.int32).reshape(1))
    return src_tok, plan


def _t5_bucket(rel):
    n = jnp.maximum(rel, 0)
    max_exact = REL_BUCKETS // 2
    nf = jnp.maximum(n, 1).astype(F32)
    large = max_exact + (jnp.log(nf / max_exact) / math.log(REL_MAX_DIST / max_exact)
                         * (REL_BUCKETS - max_exact)).astype(jnp.int32)
    large = jnp.minimum(large, REL_BUCKETS - 1)
    return jnp.where(n < max_exact, n, large)


def _bias_tables(rel_bias, tq):
    n = 3 * tq
    rel = 2 * tq - 1 - jnp.arange(n, dtype=jnp.int32)
    g = rel_bias[_t5_bucket(rel)].astype(F32).T
    skew = jnp.tile(g, (1, tq))[:, :tq * (n - 1)].reshape(-1, tq, n - 1)
    tiles = jnp.stack([skew[:, :, 2 * tq - 1:3 * tq - 1], skew[:, :, tq - 1:2 * tq - 1]], axis=1)
    far = rel_bias[_t5_bucket(jnp.full((), 2 * tq, jnp.int32))].astype(F32)
    return tiles * LOG2_E, far * LOG2_E


def kernel(x, p, ln_in_g, ln_in_b, rel_bias, w_in, diff_lam_q1, diff_lam_k1, diff_lam_q2, diff_lam_k2, diff_subln_g, rwkv_mu, rwkv_w0, rwkv_w2, rwkv_a0, rwkv_a2, rwkv_g2, rwkv_k_k, rwkv_k_a, rwkv_r_k, rwkv_lnx_g, rwkv_lnx_b, w_out, ln1_g, ln1_b, router_group_w, router_group_b, router_expert_w, router_expert_b, moe_w_gate, moe_w_up, moe_w_down, ln2_g, ln2_b, ple_w, ple_norm_g, ple_gate_w):
    bsz, seq, d = x.shape
    t = bsz * seq
    i = 0
    lambda_init = 0.8 - 0.6 * math.exp(-0.3 * i)
    tq = 256

    x2 = x.reshape(t, d)
    w_in_t = jnp.swapaxes(w_in[i], 0, 1)
    h, hb = layer_norm_rows(x2, ln_in_g, ln_in_b, 512)
    qkv = matmul_nt(hb, w_in_t, 0, O_RWKV, O_RWKV, BF16, 1024, 512)
    pr = matmul_nt(hb, w_in_t, O_RWKV, RWKV_PROJ, RWKV_PROJ_PAD, F32, 1024, 512)

    lam = (jnp.exp(jnp.sum(diff_lam_q1[i].astype(F32) * diff_lam_k1[i]))
           - jnp.exp(jnp.sum(diff_lam_q2[i].astype(F32) * diff_lam_k2[i])) + lambda_init)
    bias_tiles, far_bias = _bias_tables(rel_bias, tq)
    y_diff = diff_attention(qkv.reshape(bsz, seq, O_RWKV), bias_tiles, far_bias, lam.reshape(1),
                            diff_subln_g[i], lambda_init, tq, 8)

    mu_p = jnp.pad(rwkv_mu[i], (0, RWKV_PROJ_PAD - RWKV_PROJ))
    zeros64 = jnp.zeros((LORA_DECAY, RWKV_WIDTH), F32)
    w2p = jnp.concatenate([rwkv_w2[i], zeros64], axis=0).astype(BF16)
    a2p = jnp.concatenate([zeros64, rwkv_a2[i]], axis=0).astype(BF16)
    g2p = jnp.pad(rwkv_g2[i], ((0, 2 * LANES - LORA_GATE), (0, 0))).astype(BF16)
    r, k2, v, lw, kn, bb, g, bonus = rwkv_prep(
        pr, mu_p, rwkv_w0[i], w2p, rwkv_a0[i], a2p, g2p, rwkv_k_k[i], rwkv_k_a[i],
        rwkv_r_k[i].reshape(-1), seq, 256)
    sh = lambda a: a.reshape(bsz, seq, RWKV_WIDTH)
    y_scan = rwkv_scan(sh(r), sh(k2), sh(v), sh(lw), sh(kn), sh(bb), 64, 8)

    w_out_b = w_out[i].astype(BF16)
    rw = jnp.concatenate([router_group_w[i], router_expert_w[i]], axis=1)
    rw = jnp.pad(rw, ((0, 0), (0, ROUTER_LANES - rw.shape[1])))
    rw_hi = rw.astype(BF16)
    rw_lo = (rw - rw_hi.astype(F32)).astype(BF16)
    rb = jnp.pad(jnp.concatenate([router_group_b[i], router_expert_b[i]]),
                 (0, ROUTER_LANES - N_GROUPS - N_EXPERTS)).reshape(1, ROUTER_LANES)
    h1, info = outproj_ln_route(y_diff.reshape(t, DIFF_WIDTH), y_scan.reshape(t, RWKV_WIDTH), bonus, g,
                                rwkv_lnx_g[i], rwkv_lnx_b[i], h, w_out_b[:DIFF_WIDTH],
                                w_out_b[DIFF_WIDTH:], ln1_g[i], ln1_b[i], rw_hi, rw_lo, rb, 256)

    tm_e = MOE_ROW_TILE
    n_rows = 2 * t + N_EXPERTS * tm_e
    e_all = jnp.concatenate([info[:, 0], info[:, 1]]).astype(jnp.int32)
    pos, ends = moe_rank(e_all.reshape(-1, 1), 1024, tm_e)
    pos = pos[:, 0]
    src_tok, plan = _dispatch_plan(pos, ends, t, tm_e, n_rows)
    f = moe_w_gate.shape[-1]
    ys = moe_experts(src_tok, plan, h1.reshape(t, d // LANES, LANES), moe_w_gate[i].reshape(N_EXPERTS, d, f),
                     moe_w_up[i].reshape(N_EXPERTS, d, f), moe_w_down[i].reshape(N_EXPERTS, f, d),
                     n_rows, tm_e)

    out = final_block(pos[:t], pos[t:], ys, h1, info, p[i].reshape(t, -1), ln2_g[i], ln2_b[i],
                      ple_w[i].astype(BF16), ple_norm_g[i], ple_gate_w[i].astype(BF16), 256)
    return out.reshape(bsz, seq, d)
```

```python
import functools
import math

import jax
import jax.numpy as jnp
from jax import lax
from jax.experimental import pallas as pl
from jax.experimental.pallas import tpu as pltpu

F32 = jnp.float32
BF16 = jnp.bfloat16

DIFF_HEADS = 8
DIFF_HEAD_DIM = 64
DIFF_V_DIM = 128
DIFF_WIDTH = DIFF_HEADS * DIFF_V_DIM
DIFF_QK_WIDTH = DIFF_HEADS * 2 * DIFF_HEAD_DIM
RWKV_HEADS = 16
RWKV_HEAD_SIZE = 64
RWKV_WIDTH = RWKV_HEADS * RWKV_HEAD_SIZE
LORA_DECAY = 64
LORA_AAA = 64
LORA_GATE = 160
RWKV_PROJ = 3 * RWKV_WIDTH + LORA_DECAY + LORA_AAA + LORA_GATE
O_RWKV = 2 * DIFF_QK_WIDTH + DIFF_WIDTH
REL_BUCKETS = 32
REL_MAX_DIST = 128
N_GROUPS = 4
EXPERTS_PER_GROUP = 8
N_EXPERTS = N_GROUPS * EXPERTS_PER_GROUP
LN_EPS = 1e-5
RWKV_GN_EPS = 64e-5
NEG_INF = -1e30
DEPTH = 1
DEEPNORM_ALPHA = (2 * DEPTH) ** 0.25
LOG2_E = math.log2(math.e)

LANES = 128
RWKV_PROJ_PAD = 3584
ROUTER_LANES = LANES
VMEM_LIMIT = 48 * 1024 * 1024
MOE_VMEM_LIMIT = 56 * 1024 * 1024
MOE_ROW_TILE = 256


def _cparams(sem):
    return pltpu.CompilerParams(dimension_semantics=sem, vmem_limit_bytes=VMEM_LIMIT)


def _bdot(a, b):
    return jnp.dot(a.astype(BF16), b.astype(BF16), preferred_element_type=F32)


def _bdot_nt(a, b):
    return lax.dot_general(a.astype(BF16), b.astype(BF16), (((1,), (1,)), ((), ())),
                           preferred_element_type=F32)


def _bdot_tn(a, b):
    return lax.dot_general(a.astype(BF16), b.astype(BF16), (((0,), (0,)), ((), ())),
                           preferred_element_type=F32)


def _split3(x):
    hi = x.astype(BF16)
    r1 = x - hi.astype(F32)
    mid = r1.astype(BF16)
    lo = (r1 - mid.astype(F32)).astype(BF16)
    return hi, mid, lo


def _head_sum(x):
    lane = lax.broadcasted_iota(jnp.int32, (x.shape[0], LANES), 1)
    first = lane < RWKV_HEAD_SIZE
    cols = []
    for c in range(x.shape[1] // LANES):
        blk = x[:, c * LANES:(c + 1) * LANES]
        lo = jnp.sum(jnp.where(first, blk, 0.0), axis=1, keepdims=True)
        hi = jnp.sum(jnp.where(first, 0.0, blk), axis=1, keepdims=True)
        cols.append(jnp.where(first, lo, hi))
    return jnp.concatenate(cols, axis=1)


def _dot_sel_x(sel, x):
    hi, mid, lo = _split3(x)
    d = lambda p: jnp.dot(sel, p, preferred_element_type=F32)
    return d(hi) + d(mid) + d(lo)


def _layer_norm(x, g, b):
    mu = jnp.mean(x, axis=-1, keepdims=True)
    xc = x - mu
    var = jnp.mean(xc * xc, axis=-1, keepdims=True)
    return xc * lax.rsqrt(var + LN_EPS) * g + b


def _sigmoid(z):
    return 1.0 / (1.0 + jnp.exp(-z))


def _ln_kernel(x_ref, g_ref, b_ref, h_ref, hb_ref):
    xn = _layer_norm(x_ref[...], g_ref[...], b_ref[...])
    h_ref[...] = xn
    hb_ref[...] = xn.astype(BF16)


def layer_norm_rows(x, g, b, tm):
    m, d = x.shape
    tile = pl.BlockSpec((tm, d), lambda i: (i, 0))
    row = pl.BlockSpec((1, d), lambda i: (0, 0))
    return pl.pallas_call(
        _ln_kernel,
        grid=(m // tm,),
        in_specs=[tile, row, row],
        out_specs=[tile, tile],
        out_shape=[jax.ShapeDtypeStruct((m, d), F32), jax.ShapeDtypeStruct((m, d), BF16)],
        compiler_params=_cparams(("parallel",)),
        name="layer_norm_rows",
    )(x, g.reshape(1, d), b.reshape(1, d))


def _mm_nt_kernel(x_ref, wt_ref, o_ref, *, n_valid, tn):
    y = lax.dot_general(x_ref[...], wt_ref[...].astype(BF16), (((1,), (1,)), ((), ())),
                        preferred_element_type=F32)
    if n_valid % tn:
        col = pl.program_id(1) * tn + lax.broadcasted_iota(jnp.int32, y.shape, 1)
        y = jnp.where(col < n_valid, y, 0.0)
    o_ref[...] = y.astype(o_ref.dtype)


def matmul_nt(x, wt, row0, n_valid, n_out, out_dtype, tm, tn):
    m, k = x.shape
    blk0 = row0 // tn
    return pl.pallas_call(
        functools.partial(_mm_nt_kernel, n_valid=n_valid, tn=tn),
        grid=(m // tm, n_out // tn),
        in_specs=[pl.BlockSpec((tm, k), lambda i, j: (i, 0)),
                  pl.BlockSpec((tn, k), lambda i, j: (blk0 + j, 0))],
        out_specs=pl.BlockSpec((tm, tn), lambda i, j: (i, j)),
        out_shape=jax.ShapeDtypeStruct((m, n_out), out_dtype),
        compiler_params=_cparams(("parallel", "parallel")),
        name="matmul_nt",
    )(x, wt)


def _attn_kernel(lam_ref, far_ref, q_ref, k_ref, v_ref, bias_ref, g_ref, o_ref,
                 m_scr, l_scr, acc_scr, *, tq, hps, scale, post_scale):
    hp = pl.program_id(1)
    qi = pl.program_id(2)
    lane = lax.broadcasted_iota(jnp.int32, (tq, LANES), 1)
    q_maps = []
    for hh in range(hps):
        qs = (q_ref[0, :, hh * LANES:(hh + 1) * LANES].astype(F32) * scale).astype(BF16)
        zero = jnp.zeros_like(qs)
        q_maps += [jnp.where(lane < DIFF_HEAD_DIM, qs, zero), jnp.where(lane >= DIFF_HEAD_DIM, qs, zero)]

    m_scr[...] = jnp.full(m_scr.shape, NEG_INF, F32)
    l_scr[...] = jnp.zeros(l_scr.shape, F32)
    acc_scr[...] = jnp.zeros(acc_scr.shape, F32)

    def step(kstart, bias_index, causal):
        for hh in range(hps):
            kb = k_ref[0, pl.ds(kstart, tq), hh * LANES:(hh + 1) * LANES]
            vb = v_ref[0, pl.ds(kstart, tq), hh * LANES:(hh + 1) * LANES]
            for c in range(2 * hh, 2 * hh + 2):
                s = lax.dot_general(q_maps[c], kb, (((1,), (1,)), ((), ())), preferred_element_type=F32)
                m_old = m_scr[c]
                if bias_index is None:
                    far = far_ref[hp * hps + hh]
                    m_new = jnp.maximum(m_old, jnp.max(s, axis=1, keepdims=True) + far)
                    shift = m_new - far
                else:
                    s = s + bias_ref[hh, bias_index]
                    if causal:
                        ri = lax.broadcasted_iota(jnp.int32, (tq, tq), 0)
                        ci = lax.broadcasted_iota(jnp.int32, (tq, tq), 1)
                        s = jnp.where(ri >= ci, s, NEG_INF)
                    m_new = jnp.maximum(m_old, jnp.max(s, axis=1, keepdims=True))
                    shift = m_new
                p = jnp.exp2(s - jnp.concatenate([shift] * (tq // LANES), axis=1))
                corr = jnp.exp2(m_old - m_new)
                l_scr[c] = corr * l_scr[c] + jnp.sum(p, axis=1, keepdims=True)
                acc_scr[c] = corr * acc_scr[c] + jnp.dot(p.astype(BF16), vb, preferred_element_type=F32)
                m_scr[c] = m_new

    def far_body(kb, carry):
        step(pl.multiple_of(kb * tq, tq), None, False)
        return carry

    lax.fori_loop(0, jnp.maximum(qi - 1, 0), far_body, 0)

    @pl.when(qi >= 1)
    def _():
        step(pl.multiple_of((qi - 1) * tq, tq), 1, False)

    step(pl.multiple_of(qi * tq, tq), 0, True)

    for hh in range(hps):
        c = 2 * hh
        o = acc_scr[c] / l_scr[c] - lam_ref[0] * (acc_scr[c + 1] / l_scr[c + 1])
        ms = jnp.mean(o * o, axis=1, keepdims=True)
        o_ref[0, :, hh * LANES:(hh + 1) * LANES] = (
            o * lax.rsqrt(ms + LN_EPS) * g_ref[...] * post_scale).astype(o_ref.dtype)


def diff_attention(qkv, bias_tiles, far_bias, lam, subln_g, lambda_init, tq, hps):
    bsz, s, _ = qkv.shape
    kern = functools.partial(_attn_kernel, tq=tq, hps=hps, scale=DIFF_HEAD_DIM ** -0.5 * LOG2_E,
                             post_scale=1.0 - lambda_init)
    hw = hps * LANES
    nqk = DIFF_QK_WIDTH // hw
    return pl.pallas_call(
        kern,
        grid=(bsz, DIFF_HEADS // hps, s // tq),
        in_specs=[
            pl.BlockSpec(memory_space=pltpu.SMEM),
            pl.BlockSpec(memory_space=pltpu.SMEM),
            pl.BlockSpec((1, tq, hw), lambda b, h, i: (b, i, h)),
            pl.BlockSpec((1, s, hw), lambda b, h, i: (b, 0, nqk + h)),
            pl.BlockSpec((1, s, hw), lambda b, h, i: (b, 0, 2 * nqk + h)),
            pl.BlockSpec((hps, 2, tq, tq), lambda b, h, i: (h, 0, 0, 0)),
            pl.BlockSpec((1, LANES), lambda b, h, i: (0, 0)),
        ],
        out_specs=pl.BlockSpec((1, tq, hw), lambda b, h, i: (b, i, h)),
        out_shape=jax.ShapeDtypeStruct((bsz, s, DIFF_WIDTH), BF16),
        scratch_shapes=[pltpu.VMEM((2 * hps, tq, LANES), F32), pltpu.VMEM((2 * hps, tq, LANES), F32),
                        pltpu.VMEM((2 * hps, tq, LANES), F32)],
        compiler_params=_cparams(("parallel", "parallel", "arbitrary")),
        name="diff_attention",
    )(lam, far_bias, qkv, qkv, qkv, bias_tiles, subln_g.reshape(1, LANES))


def _rwkv_prep_kernel(x_ref, prev_ref, mu_ref, w0_ref, w2_ref, a0_ref, a2_ref, g2_ref,
                      kk_ref, ka_ref, rk_ref,
                      r_out, k_out, v_out, lw_out, kn_out, b_out, g_out, bonus_out, *, tm, seq):
    i = pl.program_id(0)
    x = x_ref[...]
    w = RWKV_WIDTH
    row = lax.broadcasted_iota(jnp.int32, x.shape, 0)
    is_start = (i * tm) % seq == 0
    last_prev = jnp.where(is_start, 0.0, prev_ref[7:8, :])
    prev = jnp.where(row == 0, last_prev, pltpu.roll(x, 1, 0))
    xs = x + (prev - x) * mu_ref[...]

    r = xs[:, 0:w]
    k = xs[:, w:2 * w]
    v = xs[:, 2 * w:3 * w]
    xwa = xs[:, 3 * w:3 * w + 2 * LORA_DECAY]
    xg = xs[:, 3 * w + LANES:3 * w + LANES + 2 * LANES]

    z = w0_ref[...] + _bdot(jnp.tanh(xwa), w2_ref[...])
    softplus_neg_z = jnp.maximum(-z, 0.0) + jnp.log(1.0 + jnp.exp(-jnp.abs(z)))
    lw_out[...] = -jnp.exp(-softplus_neg_z - 0.5)
    a_lr = _sigmoid(a0_ref[...] + _bdot(xwa, a2_ref[...]))
    g_out[...] = _bdot(_sigmoid(xg), g2_ref[...])

    kk = k * kk_ref[...]
    kn = kk / jnp.maximum(jnp.sqrt(_head_sum(kk * kk)), 1e-12)
    k2 = k * (1.0 + (a_lr - 1.0) * ka_ref[...])
    r_out[...] = r
    k_out[...] = k2
    v_out[...] = v
    kn_out[...] = kn
    b_out[...] = kn * a_lr
    bonus_out[...] = _head_sum(r * k2 * rk_ref[...]) * v


def rwkv_prep(pr, mu, w0, w2p, a0, a2p, g2p, k_k, k_a, r_k, seq, tm):
    t, wp = pr.shape
    w = RWKV_WIDTH
    row = lambda a: a.reshape(1, -1)
    full = lambda shp: pl.BlockSpec(shp, lambda i: (0, 0))
    outs = [jax.ShapeDtypeStruct((t, w), F32)] * 8
    kern = functools.partial(_rwkv_prep_kernel, tm=tm, seq=seq)
    return pl.pallas_call(
        kern,
        grid=(t // tm,),
        in_specs=[
            pl.BlockSpec((tm, wp), lambda i: (i, 0)),
            pl.BlockSpec((8, wp), lambda i: (jnp.maximum(i * (tm // 8) - 1, 0), 0)),
            full((1, wp)), full((1, w)), full((LANES, w)), full((1, w)), full((LANES, w)),
            full((2 * LANES, w)), full((1, w)), full((1, w)), full((1, w)),
        ],
        out_specs=[pl.BlockSpec((tm, w), lambda i: (i, 0))] * 8,
        out_shape=outs,
        compiler_params=_cparams(("parallel",)),
        name="rwkv_prep",
    )(pr, pr, row(mu), row(w0), w2p, row(a0), a2p, g2p, row(k_k), row(k_a), row(r_k))


def _rwkv_scan_kernel(r_ref, k_ref, v_ref, lw_ref, kn_ref, b_ref, y_ref, s_scr, *, chunk, pairs):
    c2 = 2 * chunk

    @pl.when(pl.program_id(2) == 0)
    def _():
        s_scr[...] = jnp.zeros(s_scr.shape, F32)

    ri = lax.broadcasted_iota(jnp.int32, (chunk, chunk), 0)
    ci = lax.broadcasted_iota(jnp.int32, (chunk, chunk), 1)
    ltri = jnp.where(ri >= ci, 1.0, 0.0).astype(BF16)

    lw = lw_ref[0]
    cl = _dot_sel_x(ltri, lw)
    mid = chunk // 2 - 1
    clm = cl[mid:mid + 1, :]
    cle = cl[chunk - 1:chunk, :]
    w_mid = jnp.exp(cl - clm)
    w_mid_prev = jnp.exp(cl - lw - clm)
    w_mid_inv = jnp.exp(clm - cl)
    w_abs = jnp.exp(cl)
    w_abs_prev = jnp.exp(cl - lw)
    w_end = jnp.exp(cle - cl)
    w_tot = jnp.exp(cle)

    r = r_ref[0]
    k = k_ref[0]
    v = v_ref[0]
    a = -kn_ref[0]
    b = b_ref[0]
    a_mid = a * w_mid_prev
    r_mid = r * w_mid
    b_mid = b * w_mid_inv
    k_mid = k * w_mid_inv
    a_abs = a * w_abs_prev
    r_abs = r * w_abs
    b_end = b * w_end
    k_end = k * w_end

    lane = lax.broadcasted_iota(jnp.int32, (chunk, LANES), 1)
    first = lane < RWKV_HEAD_SIZE

    def stack(x):
        return jnp.concatenate([jnp.where(first, x, 0.0), jnp.where(first, 0.0, x)], axis=0)

    rr = lax.broadcasted_iota(jnp.int32, (c2, c2), 0)
    cc = lax.broadcasted_iota(jnp.int32, (c2, c2), 1)
    same = (rr >= chunk) == (cc >= chunk)
    strict = same & (rr > cc)
    incl = same & (rr >= cc)
    eye = jnp.where(rr == cc, 1.0, 0.0)

    ps = range(pairs)
    sls = [slice(p * LANES, (p + 1) * LANES) for p in ps]
    bf = lambda x: x.astype(BF16)
    cat0 = lambda xs: jnp.concatenate(xs, axis=0)
    cat1 = lambda xs: jnp.concatenate(xs, axis=1)
    v_s = [bf(stack(v[:, sl])) for sl in sls]
    ar_m = [bf(cat0([stack(a_mid[:, sl]), stack(r_mid[:, sl])])) for sl in sls]
    bk_m = [bf(cat0([stack(b_mid[:, sl]), stack(k_mid[:, sl])])) for sl in sls]
    g_all = [_bdot_nt(ar_m[p], bk_m[p]) for p in ps]
    l_ab = [jnp.where(strict, g[:c2, :c2], 0.0) for g in g_all]
    l_ak = [bf(jnp.where(strict, g[:c2, c2:], 0.0)) for g in g_all]
    m_r = [bf(cat1([jnp.where(incl, g[c2:, :c2], 0.0), jnp.where(incl, g[c2:, c2:], 0.0)])) for g in g_all]

    t_inv = [eye + l_ab[p] for p in ps]
    n_fac = max(chunk.bit_length() - 2, 0)
    lp = [_bdot(l_ab[p], l_ab[p]) for p in ps] if n_fac else l_ab
    for _ in range(n_fac - 1):
        x = [_bdot(cat0([t_inv[p], lp[p]]), lp[p]) for p in ps]
        t_inv = [t_inv[p] + x[p][:c2] for p in ps]
        lp = [x[p][c2:] for p in ps]
    if n_fac:
        t_inv = [t_inv[p] + _bdot(t_inv[p], lp[p]) for p in ps]
    t_inv = [bf(t) for t in t_inv]

    lv = [_bdot(l_ak[p], v_s[p]) for p in ps]
    av = [bf(_bdot(t_inv[p], cat1([stack(a_abs[:, sls[p]]), lv[p]]))) for p in ps]
    zero = jnp.zeros((c2, LANES), BF16)
    ry = [_bdot(m_r[p], cat0([av[p], cat1([zero, v_s[p]])])) for p in ps]
    r_p = [stack(r_abs[:, sls[p]]) + ry[p][:, :LANES] for p in ps]

    s_old = [s_scr[p] for p in ps]
    y_s = [_bdot_nt(r_p[p], s_old[p]) + ry[p][:, LANES:] for p in ps]
    for p in ps:
        y_ref[0, :, sls[p]] = y_s[p][:chunk] + y_s[p][chunk:]
    b_e = [bf(stack(b_end[:, sl])) for sl in sls]
    k_e = [bf(stack(k_end[:, sl])) for sl in sls]
    ab = [_bdot_tn(av[p][:, :LANES], b_e[p]) for p in ps]
    q_t = [_bdot_tn(cat0([av[p][:, LANES:], v_s[p]]), cat0([b_e[p], k_e[p]])) for p in ps]
    for p in ps:
        s_scr[p] = s_old[p] * w_tot[:, sls[p]] + _bdot(s_old[p], ab[p]) + q_t[p]


def rwkv_scan(r, k, v, lw, kn, b, chunk, pairs):
    bsz, s, w = r.shape
    pw = pairs * LANES
    spec = pl.BlockSpec((1, chunk, pw), lambda bi, hi, ci: (bi, ci, hi))
    kern = functools.partial(_rwkv_scan_kernel, chunk=chunk, pairs=pairs)
    return pl.pallas_call(
        kern,
        grid=(bsz, w // pw, s // chunk),
        in_specs=[spec] * 6,
        out_specs=spec,
        out_shape=jax.ShapeDtypeStruct((bsz, s, w), F32),
        scratch_shapes=[pltpu.VMEM((pairs, LANES, LANES), F32)],
        compiler_params=_cparams(("parallel", "parallel", "arbitrary")),
        name="rwkv_scan",
    )(r, k, v, lw, kn, b)


def _rwkv_out(y, bonus, g, lnx_g, lnx_b):
    inv_n = 1.0 / RWKV_HEAD_SIZE
    mu = _head_sum(y) * inv_n
    yc = y - mu
    var = _head_sum(yc * yc) * inv_n
    yn = yc * lax.rsqrt(var + RWKV_GN_EPS) * lnx_g + lnx_b
    return (yn + bonus) * g


def _route(logits):
    lane = lax.broadcasted_iota(jnp.int32, logits.shape, 1)
    big = jnp.int32(4 * LANES)
    gmask = lane < N_GROUPS
    gl = jnp.where(gmask, logits, NEG_INF)
    gmax = jnp.max(gl, axis=1, keepdims=True)
    gidx = jnp.min(jnp.where(gl == gmax, lane, big), axis=1, keepdims=True)
    gtop = 1.0 / jnp.sum(jnp.where(gmask, jnp.exp(gl - gmax), 0.0), axis=1, keepdims=True)
    lo = N_GROUPS + EXPERTS_PER_GROUP * gidx
    emask = (lane >= lo) & (lane < lo + EXPERTS_PER_GROUP)
    el = jnp.where(emask, logits, NEG_INF)
    e1 = jnp.max(el, axis=1, keepdims=True)
    i1 = jnp.min(jnp.where(el == e1, lane, big), axis=1, keepdims=True)
    el2 = jnp.where(lane == i1, NEG_INF, el)
    e2 = jnp.max(el2, axis=1, keepdims=True)
    i2 = jnp.min(jnp.where(el2 == e2, lane, big), axis=1, keepdims=True)
    t = jnp.exp(e2 - e1)
    w1 = gtop / (1.0 + t)
    w2 = gtop * t / (1.0 + t)
    id1 = (i1 - N_GROUPS).astype(F32)
    id2 = (i2 - N_GROUPS).astype(F32)
    return jnp.where(lane == 0, id1, jnp.where(lane == 1, id2, jnp.where(lane == 2, w1,
                                                                         jnp.where(lane == 3, w2, 0.0))))


def _outproj_kernel(yd_ref, ys_ref, bonus_ref, gate_ref, lg_ref, lb_ref, h_ref,
                    w1_ref, w2_ref, g_ref, b_ref, rw_hi_ref, rw_lo_ref, rb_ref, h1_ref, info_ref, *, parts):
    rows = h_ref.shape[0] // parts
    sls = [pl.ds(k * rows, rows) for k in range(parts)]
    d = lambda a, b: jnp.dot(a, b, preferred_element_type=F32)
    yr = [_rwkv_out(ys_ref[sl, :], bonus_ref[sl, :], gate_ref[sl, :], lg_ref[...], lb_ref[...]) for sl in sls]
    mix = [d(yd_ref[sl, :], w1_ref[...]) + d(yr[k].astype(BF16), w2_ref[...]) for k, sl in enumerate(sls)]
    h1 = [_layer_norm(DEEPNORM_ALPHA * h_ref[sl, :] + mix[k], g_ref[...], b_ref[...]) for k, sl in enumerate(sls)]
    for k, sl in enumerate(sls):
        h1_ref[sl, :] = h1[k]
        hi = h1[k].astype(BF16)
        lo = (h1[k] - hi.astype(F32)).astype(BF16)
        logits = d(hi, rw_hi_ref[...]) + d(lo, rw_hi_ref[...]) + d(hi, rw_lo_ref[...]) + rb_ref[...]
        info_ref[sl, :] = _route(logits)


def outproj_ln_route(yd, ys, bonus, gate, lnx_g, lnx_b, h, w_out_a, w_out_b, g, b, rw_hi, rw_lo, rb, tm):
    t, d = h.shape
    ka = yd.shape[1]
    kb = ys.shape[1]
    full = lambda shp: pl.BlockSpec(shp, lambda i: (0, 0))
    tile = lambda n: pl.BlockSpec((tm, n), lambda i: (i, 0))
    return pl.pallas_call(
        functools.partial(_outproj_kernel, parts=2),
        grid=(t // tm,),
        in_specs=[
            tile(ka), tile(kb), tile(kb), tile(kb), full((1, kb)), full((1, kb)), tile(d),
            full((ka, d)), full((kb, d)), full((1, d)), full((1, d)),
            full((d, ROUTER_LANES)), full((d, ROUTER_LANES)), full((1, ROUTER_LANES)),
        ],
        out_specs=[
            pl.BlockSpec((tm, d), lambda i: (i, 0)),
            pl.BlockSpec((tm, ROUTER_LANES), lambda i: (i, 0)),
        ],
        out_shape=[jax.ShapeDtypeStruct((t, d), F32), jax.ShapeDtypeStruct((t, ROUTER_LANES), F32)],
        compiler_params=pltpu.CompilerParams(dimension_semantics=("parallel",),
                                             vmem_limit_bytes=MOE_VMEM_LIMIT),
        name="outproj_ln_route",
    )(yd, ys, bonus, gate, lnx_g.reshape(1, kb), lnx_b.reshape(1, kb), h, w_out_a, w_out_b,
      g.reshape(1, d), b.reshape(1, d), rw_hi, rw_lo, rb)


def _moe_rank_kernel(e_ref, pos_ref, ends_ref, run_scr, start_scr, *, tm, row_tile):
    ph = pl.program_id(0)
    i = pl.program_id(1)
    lane = lax.broadcasted_iota(jnp.int32, (tm, LANES), 1)
    onehot = e_ref[...] == lane
    oh = jnp.where(onehot, 1.0, 0.0)

    @pl.when((ph == 0) & (i == 0))
    def _():
        run_scr[...] = jnp.zeros(run_scr.shape, F32)

    @pl.when(ph == 0)
    def _():
        run_scr[...] += jnp.sum(oh, axis=0, keepdims=True)

    @pl.when((ph == 1) & (i == 0))
    def _():
        padded = jnp.floor((run_scr[...] + (row_tile - 1)) * (1.0 / row_tile)) * row_tile
        ri = lax.broadcasted_iota(jnp.int32, (LANES, LANES), 0)
        ci = lax.broadcasted_iota(jnp.int32, (LANES, LANES), 1)
        upper = jnp.where(ri < ci, 1.0, 0.0).astype(BF16)
        parts = _split3(jnp.broadcast_to(padded, (8, LANES)))
        starts = sum(jnp.dot(q, upper, preferred_element_type=F32) for q in parts)[0:1]
        start_scr[...] = starts
        ends_ref[...] = starts + padded
        run_scr[...] = jnp.zeros(run_scr.shape, F32)

    @pl.when(ph == 1)
    def _():
        ri = lax.broadcasted_iota(jnp.int32, (tm, tm), 0)
        ci = lax.broadcasted_iota(jnp.int32, (tm, tm), 1)
        before = jnp.where(ri > ci, 1.0, 0.0).astype(BF16)
        row = jnp.dot(before, oh.astype(BF16), preferred_element_type=F32) + (run_scr[...] + start_scr[...])
        pos_ref[...] = jnp.sum(jnp.where(onehot, row, 0.0), axis=1, keepdims=True).astype(jnp.int32)
        run_scr[...] += jnp.sum(oh, axis=0, keepdims=True)


def moe_rank(e_col, tm, row_tile):
    n = e_col.shape[0]
    return pl.pallas_call(
        functools.partial(_moe_rank_kernel, tm=tm, row_tile=row_tile),
        grid=(2, n // tm),
        in_specs=[pl.BlockSpec((tm, 1), lambda ph, i: (i, 0))],
        out_specs=[pl.BlockSpec((tm, 1), lambda ph, i: (i * ph, 0)),
                   pl.BlockSpec((1, LANES), lambda ph, i: (0, 0))],
        out_shape=[jax.ShapeDtypeStruct((n, 1), jnp.int32), jax.ShapeDtypeStruct((1, LANES), F32)],
        scratch_shapes=[pltpu.VMEM((1, LANES), F32), pltpu.VMEM((1, LANES), F32)],
        compiler_params=_cparams(("arbitrary", "arbitrary")),
        name="moe_rank",
    )(e_col)


def _row_copy(src_hbm, row, dst, r, sem):
    return pltpu.make_async_copy(src_hbm.at[pl.ds(row, 1)], dst.at[pl.ds(r, 1)], sem)


def _moe_expert_kernel(src_ref, te_ref, first_ref, par_ref, nxt_ref, nact_ref,
                       x_hbm, wg_hbm, wu_hbm, wd_hbm, o_ref,
                       xbuf0, xbuf1, wgf, wuf, wdf, wgb, wub, wdb, sem, wsem, *, tm):
    j = pl.program_id(0)
    n_act = nact_ref[0]
    bufs = (xbuf0, xbuf1)

    def token_copy(tok, into, r):
        return pltpu.make_async_copy(x_hbm.at[tok], bufs[into].at[:, r, :], sem.at[into])

    def start_gather(tile, into):
        for r in range(tm):
            token_copy(src_ref[tile * tm + r], into, r).start(priority=0)

    def wait_gather(into):
        for r in range(tm):
            token_copy(0, into, r).wait()

    def weight_copies(e, par):
        return (pltpu.make_async_copy(wg_hbm.at[e], wgf.at[par], wsem.at[par]),
                pltpu.make_async_copy(wu_hbm.at[e], wuf.at[par], wsem.at[par]),
                pltpu.make_async_copy(wd_hbm.at[e], wdf.at[par], wsem.at[par]))

    @pl.when(j == 0)
    def _():
        start_gather(0, 0)
        for c in weight_copies(te_ref[0], 0):
            c.start(priority=1)

    @pl.when((j < n_act) & (first_ref[j] > 0))
    def _():
        par = par_ref[j]
        for c in weight_copies(te_ref[j], par):
            c.wait()

        @pl.when(nxt_ref[j] >= 0)
        def _():
            for c in weight_copies(nxt_ref[j], 1 - par):
                c.start(priority=1)

        wgb[...] = wgf[par].astype(BF16)
        wub[...] = wuf[par].astype(BF16)
        wdb[...] = wdf[par].astype(BF16)

    for slot in range(2):
        @pl.when((j < n_act) & (j % 2 == slot))
        def _():
            wait_gather(slot)
            start_gather(jnp.minimum(j + 1, n_act - 1), 1 - slot)
            x = jnp.concatenate([bufs[slot][c] for c in range(bufs[slot].shape[0])], axis=1).astype(BF16)
            hg = jnp.dot(x, wgb[...], preferred_element_type=F32)
            hu = jnp.dot(x, wub[...], preferred_element_type=F32)
            hid = hg * _sigmoid(hg) * hu
            o_ref[...] = jnp.dot(hid.astype(BF16), wdb[...], preferred_element_type=F32)

        @pl.when((j == n_act - 1) & (j % 2 == slot))
        def _():
            wait_gather(1 - slot)

    @pl.when(j >= n_act)
    def _():
        o_ref[...] = jnp.zeros(o_ref.shape, F32)


def moe_experts(src_tok, plan, x, wg, wu, wd, n_rows, tm):
    d, f = wg.shape[1:]
    hbm = pl.BlockSpec(memory_space=pl.ANY)
    return pl.pallas_call(
        functools.partial(_moe_expert_kernel, tm=tm),
        grid_spec=pltpu.PrefetchScalarGridSpec(
            num_scalar_prefetch=6,
            grid=(n_rows // tm,),
            in_specs=[hbm, hbm, hbm, hbm],
            out_specs=pl.BlockSpec((tm, d), lambda j, *_: (j, 0)),
            scratch_shapes=[pltpu.VMEM((d // LANES, tm, LANES), F32), pltpu.VMEM((d // LANES, tm, LANES), F32),
                            pltpu.VMEM((2, d, f), F32), pltpu.VMEM((2, d, f), F32), pltpu.VMEM((2, f, d), F32),
                            pltpu.VMEM((d, f), BF16), pltpu.VMEM((d, f), BF16), pltpu.VMEM((f, d), BF16),
                            pltpu.SemaphoreType.DMA((2,)), pltpu.SemaphoreType.DMA((2,))],
        ),
        out_shape=jax.ShapeDtypeStruct((n_rows, d), F32),
        compiler_params=pltpu.CompilerParams(dimension_semantics=("arbitrary",),
                                             vmem_limit_bytes=MOE_VMEM_LIMIT),
        name="moe_experts",
    )(src_tok, *plan, x, wg, wu, wd)


def _final_kernel(p1_ref, p2_ref, y_hbm, h1_ref, info_ref, p_ref, g_ref, b_ref, pw_ref, pg_ref, gw_ref,
                  o_ref, ybuf0, ybuf1, sem, *, tm, n_tiles, parts):
    i = pl.program_id(0)
    bufs = (ybuf0, ybuf1)

    def start_gather(tile, into):
        for r in range(tm):
            _row_copy(y_hbm, p1_ref[tile * tm + r], bufs[into].at[0], r, sem.at[into]).start(priority=0)
            _row_copy(y_hbm, p2_ref[tile * tm + r], bufs[into].at[1], r, sem.at[into]).start(priority=1)

    def wait_gather(into):
        for r in range(tm):
            _row_copy(y_hbm, 0, bufs[into].at[0], r, sem.at[into]).wait()
            _row_copy(y_hbm, 0, bufs[into].at[1], r, sem.at[into]).wait()

    @pl.when(i == 0)
    def _():
        start_gather(0, 0)

    for slot in range(2):
        @pl.when(i % 2 == slot)
        def _():
            wait_gather(slot)
            start_gather(jnp.minimum(i + 1, n_tiles - 1), 1 - slot)
            rows = tm // parts
            sls = [pl.ds(k * rows, rows) for k in range(parts)]
            h2, ple = [], []
            for sl in sls:
                info = info_ref[sl, :]
                ffn = info[:, 2:3] * bufs[slot][0, sl, :] + info[:, 3:4] * bufs[slot][1, sl, :]
                h2.append(_layer_norm(DEEPNORM_ALPHA * h1_ref[sl, :] + ffn, g_ref[...], b_ref[...]))
                pe = jnp.dot(p_ref[sl, :].astype(BF16), pw_ref[...], preferred_element_type=F32)
                ple.append(pe * lax.rsqrt(jnp.mean(pe * pe, axis=-1, keepdims=True) + LN_EPS) * pg_ref[...])
            gate = [jnp.dot(h.astype(BF16), gw_ref[...], preferred_element_type=F32) for h in h2]
            for k, sl in enumerate(sls):
                o_ref[sl, :] = h2[k] + _sigmoid(gate[k]) * ple[k]

        @pl.when((i == n_tiles - 1) & (i % 2 == slot))
        def _():
            wait_gather(1 - slot)


def final_block(pos1, pos2, ys, h1, info, p, g, b, ple_w, ple_g, gate_w, tm):
    t, d = h1.shape
    dp = p.shape[1]
    tile = lambda n: pl.BlockSpec((tm, n), lambda i, p1, p2: (i, 0))
    full = lambda shp: pl.BlockSpec(shp, lambda i, p1, p2: (0, 0))
    return pl.pallas_call(
        functools.partial(_final_kernel, tm=tm, n_tiles=t // tm, parts=2),
        grid_spec=pltpu.PrefetchScalarGridSpec(
            num_scalar_prefetch=2,
            grid=(t // tm,),
            in_specs=[pl.BlockSpec(memory_space=pl.ANY), tile(d), tile(ROUTER_LANES), tile(dp),
                      full((1, d)), full((1, d)), full((dp, d)), full((1, d)), full((d, d))],
            out_specs=tile(d),
            scratch_shapes=[pltpu.VMEM((2, tm, d), F32), pltpu.VMEM((2, tm, d), F32),
                            pltpu.SemaphoreType.DMA((2,))],
        ),
        out_shape=jax.ShapeDtypeStruct((t, d), F32),
        compiler_params=pltpu.CompilerParams(dimension_semantics=("arbitrary",),
                                             vmem_limit_bytes=MOE_VMEM_LIMIT),
        name="final_block",
    )(pos1, pos2, ys, h1, info, p, g.reshape(1, d), b.reshape(1, d), ple_w, ple_g.reshape(1, d), gate_w)


def _dispatch_plan(pos, ends_f, n_tok, tm, n_rows):
    n_tiles = n_rows // tm
    ends = ends_f[0, :N_EXPERTS].astype(jnp.int32)
    src_tok = jnp.zeros((n_rows,), jnp.int32).at[pos].set(
        jnp.arange(2 * n_tok, dtype=jnp.int32) % n_tok, unique_indices=True)
    n_active = ends[-1] // tm
    tile_ids = jnp.arange(n_tiles, dtype=jnp.int32)
    tile_expert = jnp.searchsorted(ends, jnp.minimum(tile_ids, n_active - 1) * tm, side="right")
    tile_expert = jnp.minimum(tile_expert, N_EXPERTS - 1).astype(jnp.int32)
    first = jnp.concatenate([jnp.ones((1,), jnp.int32),
                             (tile_expert[1:] != tile_expert[:-1]).astype(jnp.int32)])
    parity = (jnp.cumsum(first) - 1) % 2
    ids = jnp.arange(N_EXPERTS, dtype=jnp.int32)
    present = ends > jnp.concatenate([jnp.zeros((1,), jnp.int32), ends[:-1]])
    later = present[None, :] & (ids[None, :] > ids[:, None])
    next_of = jnp.min(jnp.where(later, ids[None, :], N_EXPERTS), axis=1)
    next_of = jnp.where(next_of == N_EXPERTS, -1, next_of)
    plan = (tile_expert, first, parity.astype(jnp.int32), next_of[tile_expert].astype(jnp.int32),
            n_active.astype(jnp.int32).reshape(1))
    return src_tok, plan


def _t5_bucket(rel):
    n = jnp.maximum(rel, 0)
    max_exact = REL_BUCKETS // 2
    nf = jnp.maximum(n, 1).astype(F32)
    large = max_exact + (jnp.log(nf / max_exact) / math.log(REL_MAX_DIST / max_exact)
                         * (REL_BUCKETS - max_exact)).astype(jnp.int32)
    large = jnp.minimum(large, REL_BUCKETS - 1)
    return jnp.where(n < max_exact, n, large)


def _bias_tables(rel_bias, tq):
    n = 3 * tq
    rel = 2 * tq - 1 - jnp.arange(n, dtype=jnp.int32)
    g = rel_bias[_t5_bucket(rel)].astype(F32).T
    skew = jnp.tile(g, (1, tq))[:, :tq * (n - 1)].reshape(-1, tq, n - 1)
    tiles = jnp.stack([skew[:, :, 2 * tq - 1:3 * tq - 1], skew[:, :, tq - 1:2 * tq - 1]], axis=1)
    far = rel_bias[_t5_bucket(jnp.full((), 2 * tq, jnp.int32))].astype(F32)
    return tiles * LOG2_E, far * LOG2_E


def kernel(x, p, ln_in_g, ln_in_b, rel_bias, w_in, diff_lam_q1, diff_lam_k1, diff_lam_q2, diff_lam_k2, diff_subln_g, rwkv_mu, rwkv_w0, rwkv_w2, rwkv_a0, rwkv_a2, rwkv_g2, rwkv_k_k, rwkv_k_a, rwkv_r_k, rwkv_lnx_g, rwkv_lnx_b, w_out, ln1_g, ln1_b, router_group_w, router_group_b, router_expert_w, router_expert_b, moe_w_gate, moe_w_up, moe_w_down, ln2_g, ln2_b, ple_w, ple_norm_g, ple_gate_w):
    bsz, seq, d = x.shape
    t = bsz * seq
    i = 0
    lambda_init = 0.8 - 0.6 * math.exp(-0.3 * i)
    tq = 256

    x2 = x.reshape(t, d)
    w_in_t = jnp.swapaxes(w_in[i], 0, 1)
    h, hb = layer_norm_rows(x2, ln_in_g, ln_in_b, 512)
    qkv = matmul_nt(hb, w_in_t, 0, O_RWKV, O_RWKV, BF16, 1024, 512)
    pr = matmul_nt(hb, w_in_t, O_RWKV, RWKV_PROJ, RWKV_PROJ_PAD, F32, 1024, 512)

    lam = (jnp.exp(jnp.sum(diff_lam_q1[i].astype(F32) * diff_lam_k1[i]))
           - jnp.exp(jnp.sum(diff_lam_q2[i].astype(F32) * diff_lam_k2[i])) + lambda_init)
    bias_tiles, far_bias = _bias_tables(rel_bias, tq)
    y_diff = diff_attention(qkv.reshape(bsz, seq, O_RWKV), bias_tiles, far_bias, lam.reshape(1),
                            diff_subln_g[i], lambda_init, tq, 8)

    mu_p = jnp.pad(rwkv_mu[i], (0, RWKV_PROJ_PAD - RWKV_PROJ))
    zeros64 = jnp.zeros((LORA_DECAY, RWKV_WIDTH), F32)
    w2p = jnp.concatenate([rwkv_w2[i], zeros64], axis=0).astype(BF16)
    a2p = jnp.concatenate([zeros64, rwkv_a2[i]], axis=0).astype(BF16)
    g2p = jnp.pad(rwkv_g2[i], ((0, 2 * LANES - LORA_GATE), (0, 0))).astype(BF16)
    r, k2, v, lw, kn, bb, g, bonus = rwkv_prep(
        pr, mu_p, rwkv_w0[i], w2p, rwkv_a0[i], a2p, g2p, rwkv_k_k[i], rwkv_k_a[i],
        rwkv_r_k[i].reshape(-1), seq, 256)
    sh = lambda a: a.reshape(bsz, seq, RWKV_WIDTH)
    y_scan = rwkv_scan(sh(r), sh(k2), sh(v), sh(lw), sh(kn), sh(bb), 64, 8)

    w_out_b = w_out[i].astype(BF16)
    rw = jnp.concatenate([router_group_w[i], router_expert_w[i]], axis=1)
    rw = jnp.pad(rw, ((0, 0), (0, ROUTER_LANES - rw.shape[1])))
    rw_hi = rw.astype(BF16)
    rw_lo = (rw - rw_hi.astype(F32)).astype(BF16)
    rb = jnp.pad(jnp.concatenate([router_group_b[i], router_expert_b[i]]),
                 (0, ROUTER_LANES - N_GROUPS - N_EXPERTS)).reshape(1, ROUTER_LANES)
    h1, info = outproj_ln_route(y_diff.reshape(t, DIFF_WIDTH), y_scan.reshape(t, RWKV_WIDTH), bonus, g,
                                rwkv_lnx_g[i], rwkv_lnx_b[i], h, w_out_b[:DIFF_WIDTH],
                                w_out_b[DIFF_WIDTH:], ln1_g[i], ln1_b[i], rw_hi, rw_lo, rb, 512)

    tm_e = MOE_ROW_TILE
    n_rows = 2 * t + N_EXPERTS * tm_e
    e_all = jnp.concatenate([info[:, 0], info[:, 1]]).astype(jnp.int32)
    pos, ends = moe_rank(e_all.reshape(-1, 1), 1024, tm_e)
    pos = pos[:, 0]
    src_tok, plan = _dispatch_plan(pos, ends, t, tm_e, n_rows)
    f = moe_w_gate.shape[-1]
    ys = moe_experts(src_tok, plan, h1.reshape(t, d // LANES, LANES), moe_w_gate[i].reshape(N_EXPERTS, d, f),
                     moe_w_up[i].reshape(N_EXPERTS, d, f), moe_w_down[i].reshape(N_EXPERTS, f, d),
                     n_rows, tm_e)

    out = final_block(pos[:t], pos[t:], ys, h1, info, p[i].reshape(t, -1), ln2_g[i], ln2_b[i],
                      ple_w[i].astype(BF16), ple_norm_g[i], ple_gate_w[i].astype(BF16), 256)
    return out.reshape(bsz, seq, d)
```

```python
import functools
import math

import jax
import jax.numpy as jnp
from jax import lax
from jax.experimental import pallas as pl
from jax.experimental.pallas import tpu as pltpu

F32 = jnp.float32
BF16 = jnp.bfloat16

DIFF_HEADS = 8
DIFF_HEAD_DIM = 64
DIFF_V_DIM = 128
DIFF_WIDTH = DIFF_HEADS * DIFF_V_DIM
DIFF_QK_WIDTH = DIFF_HEADS * 2 * DIFF_HEAD_DIM
RWKV_HEADS = 16
RWKV_HEAD_SIZE = 64
RWKV_WIDTH = RWKV_HEADS * RWKV_HEAD_SIZE
LORA_DECAY = 64
LORA_AAA = 64
LORA_GATE = 160
RWKV_PROJ = 3 * RWKV_WIDTH + LORA_DECAY + LORA_AAA + LORA_GATE
O_RWKV = 2 * DIFF_QK_WIDTH + DIFF_WIDTH
REL_BUCKETS = 32
REL_MAX_DIST = 128
N_GROUPS = 4
EXPERTS_PER_GROUP = 8
N_EXPERTS = N_GROUPS * EXPERTS_PER_GROUP
LN_EPS = 1e-5
RWKV_GN_EPS = 64e-5
NEG_INF = -1e30
DEPTH = 1
DEEPNORM_ALPHA = (2 * DEPTH) ** 0.25
LOG2_E = math.log2(math.e)

LANES = 128
RWKV_PROJ_PAD = 3584
ROUTER_LANES = LANES
VMEM_LIMIT = 48 * 1024 * 1024
MOE_VMEM_LIMIT = 56 * 1024 * 1024
MOE_ROW_TILE = 256


def _cparams(sem):
    return pltpu.CompilerParams(dimension_semantics=sem, vmem_limit_bytes=VMEM_LIMIT)


def _bdot(a, b):
    return jnp.dot(a.astype(BF16), b.astype(BF16), preferred_element_type=F32)


def _bdot_nt(a, b):
    return lax.dot_general(a.astype(BF16), b.astype(BF16), (((1,), (1,)), ((), ())),
                           preferred_element_type=F32)


def _bdot_tn(a, b):
    return lax.dot_general(a.astype(BF16), b.astype(BF16), (((0,), (0,)), ((), ())),
                           preferred_element_type=F32)


def _split3(x):
    hi = x.astype(BF16)
    r1 = x - hi.astype(F32)
    mid = r1.astype(BF16)
    lo = (r1 - mid.astype(F32)).astype(BF16)
    return hi, mid, lo


def _head_sum(x):
    lane = lax.broadcasted_iota(jnp.int32, (x.shape[0], LANES), 1)
    first = lane < RWKV_HEAD_SIZE
    cols = []
    for c in range(x.shape[1] // LANES):
        blk = x[:, c * LANES:(c + 1) * LANES]
        lo = jnp.sum(jnp.where(first, blk, 0.0), axis=1, keepdims=True)
        hi = jnp.sum(jnp.where(first, 0.0, blk), axis=1, keepdims=True)
        cols.append(jnp.where(first, lo, hi))
    return jnp.concatenate(cols, axis=1)


def _dot_sel_x(sel, x):
    hi, mid, lo = _split3(x)
    d = lambda p: jnp.dot(sel, p, preferred_element_type=F32)
    return d(hi) + d(mid) + d(lo)


def _layer_norm(x, g, b):
    mu = jnp.mean(x, axis=-1, keepdims=True)
    xc = x - mu
    var = jnp.mean(xc * xc, axis=-1, keepdims=True)
    return xc * lax.rsqrt(var + LN_EPS) * g + b


def _sigmoid(z):
    return 1.0 / (1.0 + jnp.exp(-z))


def _ln_kernel(x_ref, g_ref, b_ref, h_ref, hb_ref):
    xn = _layer_norm(x_ref[...], g_ref[...], b_ref[...])
    h_ref[...] = xn
    hb_ref[...] = xn.astype(BF16)


def layer_norm_rows(x, g, b, tm):
    m, d = x.shape
    tile = pl.BlockSpec((tm, d), lambda i: (i, 0))
    row = pl.BlockSpec((1, d), lambda i: (0, 0))
    return pl.pallas_call(
        _ln_kernel,
        grid=(m // tm,),
        in_specs=[tile, row, row],
        out_specs=[tile, tile],
        out_shape=[jax.ShapeDtypeStruct((m, d), F32), jax.ShapeDtypeStruct((m, d), BF16)],
        compiler_params=_cparams(("parallel",)),
        name="layer_norm_rows",
    )(x, g.reshape(1, d), b.reshape(1, d))


def _mm_nt_kernel(x_ref, wt_ref, o_ref, *, n_valid, tn):
    y = lax.dot_general(x_ref[...], wt_ref[...].astype(BF16), (((1,), (1,)), ((), ())),
                        preferred_element_type=F32)
    if n_valid % tn:
        col = pl.program_id(1) * tn + lax.broadcasted_iota(jnp.int32, y.shape, 1)
        y = jnp.where(col < n_valid, y, 0.0)
    o_ref[...] = y.astype(o_ref.dtype)


def matmul_nt(x, wt, row0, n_valid, n_out, out_dtype, tm, tn):
    m, k = x.shape
    blk0 = row0 // tn
    return pl.pallas_call(
        functools.partial(_mm_nt_kernel, n_valid=n_valid, tn=tn),
        grid=(m // tm, n_out // tn),
        in_specs=[pl.BlockSpec((tm, k), lambda i, j: (i, 0)),
                  pl.BlockSpec((tn, k), lambda i, j: (blk0 + j, 0))],
        out_specs=pl.BlockSpec((tm, tn), lambda i, j: (i, j)),
        out_shape=jax.ShapeDtypeStruct((m, n_out), out_dtype),
        compiler_params=_cparams(("parallel", "parallel")),
        name="matmul_nt",
    )(x, wt)


def _attn_kernel(lam_ref, far_ref, q_ref, k_ref, v_ref, bias_ref, g_ref, o_ref,
                 m_scr, l_scr, acc_scr, *, tq, hps, scale, post_scale):
    hp = pl.program_id(1)
    qi = pl.program_id(2)
    lane = lax.broadcasted_iota(jnp.int32, (tq, LANES), 1)
    q_maps = []
    for hh in range(hps):
        qs = (q_ref[0, :, hh * LANES:(hh + 1) * LANES].astype(F32) * scale).astype(BF16)
        zero = jnp.zeros_like(qs)
        q_maps += [jnp.where(lane < DIFF_HEAD_DIM, qs, zero), jnp.where(lane >= DIFF_HEAD_DIM, qs, zero)]

    m_scr[...] = jnp.full(m_scr.shape, NEG_INF, F32)
    l_scr[...] = jnp.zeros(l_scr.shape, F32)
    acc_scr[...] = jnp.zeros(acc_scr.shape, F32)

    def step(kstart, bias_index, causal):
        for hh in range(hps):
            kb = k_ref[0, pl.ds(kstart, tq), hh * LANES:(hh + 1) * LANES]
            vb = v_ref[0, pl.ds(kstart, tq), hh * LANES:(hh + 1) * LANES]
            for c in range(2 * hh, 2 * hh + 2):
                s = lax.dot_general(q_maps[c], kb, (((1,), (1,)), ((), ())), preferred_element_type=F32)
                m_old = m_scr[c]
                if bias_index is None:
                    far = far_ref[hp * hps + hh]
                    m_new = jnp.maximum(m_old, jnp.max(s, axis=1, keepdims=True) + far)
                    shift = m_new - far
                else:
                    s = s + bias_ref[hh, bias_index]
                    if causal:
                        ri = lax.broadcasted_iota(jnp.int32, (tq, tq), 0)
                        ci = lax.broadcasted_iota(jnp.int32, (tq, tq), 1)
                        s = jnp.where(ri >= ci, s, NEG_INF)
                    m_new = jnp.maximum(m_old, jnp.max(s, axis=1, keepdims=True))
                    shift = m_new
                p = jnp.exp2(s - jnp.concatenate([shift] * (tq // LANES), axis=1))
                corr = jnp.exp2(m_old - m_new)
                l_scr[c] = corr * l_scr[c] + jnp.sum(p, axis=1, keepdims=True)
                acc_scr[c] = corr * acc_scr[c] + jnp.dot(p.astype(BF16), vb, preferred_element_type=F32)
                m_scr[c] = m_new

    def far_body(kb, carry):
        step(pl.multiple_of(kb * tq, tq), None, False)
        return carry

    lax.fori_loop(0, jnp.maximum(qi - 1, 0), far_body, 0)

    @pl.when(qi >= 1)
    def _():
        step(pl.multiple_of((qi - 1) * tq, tq), 1, False)

    step(pl.multiple_of(qi * tq, tq), 0, True)

    for hh in range(hps):
        c = 2 * hh
        o = acc_scr[c] / l_scr[c] - lam_ref[0] * (acc_scr[c + 1] / l_scr[c + 1])
        ms = jnp.mean(o * o, axis=1, keepdims=True)
        o_ref[0, :, hh * LANES:(hh + 1) * LANES] = (
            o * lax.rsqrt(ms + LN_EPS) * g_ref[...] * post_scale).astype(o_ref.dtype)


def diff_attention(qkv, bias_tiles, far_bias, lam, subln_g, lambda_init, tq, hps):
    bsz, s, _ = qkv.shape
    kern = functools.partial(_attn_kernel, tq=tq, hps=hps, scale=DIFF_HEAD_DIM ** -0.5 * LOG2_E,
                             post_scale=1.0 - lambda_init)
    hw = hps * LANES
    nqk = DIFF_QK_WIDTH // hw
    return pl.pallas_call(
        kern,
        grid=(bsz, DIFF_HEADS // hps, s // tq),
        in_specs=[
            pl.BlockSpec(memory_space=pltpu.SMEM),
            pl.BlockSpec(memory_space=pltpu.SMEM),
            pl.BlockSpec((1, tq, hw), lambda b, h, i: (b, i, h)),
            pl.BlockSpec((1, s, hw), lambda b, h, i: (b, 0, nqk + h)),
            pl.BlockSpec((1, s, hw), lambda b, h, i: (b, 0, 2 * nqk + h)),
            pl.BlockSpec((hps, 2, tq, tq), lambda b, h, i: (h, 0, 0, 0)),
            pl.BlockSpec((1, LANES), lambda b, h, i: (0, 0)),
        ],
        out_specs=pl.BlockSpec((1, tq, hw), lambda b, h, i: (b, i, h)),
        out_shape=jax.ShapeDtypeStruct((bsz, s, DIFF_WIDTH), BF16),
        scratch_shapes=[pltpu.VMEM((2 * hps, tq, LANES), F32), pltpu.VMEM((2 * hps, tq, LANES), F32),
                        pltpu.VMEM((2 * hps, tq, LANES), F32)],
        compiler_params=_cparams(("parallel", "parallel", "arbitrary")),
        name="diff_attention",
    )(lam, far_bias, qkv, qkv, qkv, bias_tiles, subln_g.reshape(1, LANES))


def _rwkv_prep_kernel(x_ref, prev_ref, mu_ref, w0_ref, w2_ref, a0_ref, a2_ref, g2_ref,
                      kk_ref, ka_ref, rk_ref,
                      r_out, k_out, v_out, lw_out, kn_out, b_out, g_out, bonus_out, *, tm, seq):
    i = pl.program_id(0)
    x = x_ref[...]
    w = RWKV_WIDTH
    row = lax.broadcasted_iota(jnp.int32, x.shape, 0)
    is_start = (i * tm) % seq == 0
    last_prev = jnp.where(is_start, 0.0, prev_ref[7:8, :])
    prev = jnp.where(row == 0, last_prev, pltpu.roll(x, 1, 0))
    xs = x + (prev - x) * mu_ref[...]

    r = xs[:, 0:w]
    k = xs[:, w:2 * w]
    v = xs[:, 2 * w:3 * w]
    xwa = xs[:, 3 * w:3 * w + 2 * LORA_DECAY]
    xg = xs[:, 3 * w + LANES:3 * w + LANES + 2 * LANES]

    z = w0_ref[...] + _bdot(jnp.tanh(xwa), w2_ref[...])
    softplus_neg_z = jnp.maximum(-z, 0.0) + jnp.log(1.0 + jnp.exp(-jnp.abs(z)))
    lw_out[...] = -jnp.exp(-softplus_neg_z - 0.5)
    a_lr = _sigmoid(a0_ref[...] + _bdot(xwa, a2_ref[...]))
    g_out[...] = _bdot(_sigmoid(xg), g2_ref[...])

    kk = k * kk_ref[...]
    kn = kk / jnp.maximum(jnp.sqrt(_head_sum(kk * kk)), 1e-12)
    k2 = k * (1.0 + (a_lr - 1.0) * ka_ref[...])
    r_out[...] = r
    k_out[...] = k2
    v_out[...] = v
    kn_out[...] = kn
    b_out[...] = kn * a_lr
    bonus_out[...] = _head_sum(r * k2 * rk_ref[...]) * v


def rwkv_prep(pr, mu, w0, w2p, a0, a2p, g2p, k_k, k_a, r_k, seq, tm):
    t, wp = pr.shape
    w = RWKV_WIDTH
    row = lambda a: a.reshape(1, -1)
    full = lambda shp: pl.BlockSpec(shp, lambda i: (0, 0))
    outs = [jax.ShapeDtypeStruct((t, w), F32)] * 8
    kern = functools.partial(_rwkv_prep_kernel, tm=tm, seq=seq)
    return pl.pallas_call(
        kern,
        grid=(t // tm,),
        in_specs=[
            pl.BlockSpec((tm, wp), lambda i: (i, 0)),
            pl.BlockSpec((8, wp), lambda i: (jnp.maximum(i * (tm // 8) - 1, 0), 0)),
            full((1, wp)), full((1, w)), full((LANES, w)), full((1, w)), full((LANES, w)),
            full((2 * LANES, w)), full((1, w)), full((1, w)), full((1, w)),
        ],
        out_specs=[pl.BlockSpec((tm, w), lambda i: (i, 0))] * 8,
        out_shape=outs,
        compiler_params=_cparams(("parallel",)),
        name="rwkv_prep",
    )(pr, pr, row(mu), row(w0), w2p, row(a0), a2p, g2p, row(k_k), row(k_a), row(r_k))


def _rwkv_scan_kernel(r_ref, k_ref, v_ref, lw_ref, kn_ref, b_ref, y_ref, s_scr, *, chunk, pairs):
    c2 = 2 * chunk

    @pl.when(pl.program_id(2) == 0)
    def _():
        s_scr[...] = jnp.zeros(s_scr.shape, F32)

    ri = lax.broadcasted_iota(jnp.int32, (chunk, chunk), 0)
    ci = lax.broadcasted_iota(jnp.int32, (chunk, chunk), 1)
    ltri = jnp.where(ri >= ci, 1.0, 0.0).astype(BF16)

    lw = lw_ref[0]
    cl = _dot_sel_x(ltri, lw)
    mid = chunk // 2 - 1
    clm = cl[mid:mid + 1, :]
    cle = cl[chunk - 1:chunk, :]
    w_mid = jnp.exp(cl - clm)
    w_mid_prev = jnp.exp(cl - lw - clm)
    w_mid_inv = jnp.exp(clm - cl)
    w_abs = jnp.exp(cl)
    w_abs_prev = jnp.exp(cl - lw)
    w_end = jnp.exp(cle - cl)
    w_tot = jnp.exp(cle)

    r = r_ref[0]
    k = k_ref[0]
    v = v_ref[0]
    a = -kn_ref[0]
    b = b_ref[0]
    a_mid = a * w_mid_prev
    r_mid = r * w_mid
    b_mid = b * w_mid_inv
    k_mid = k * w_mid_inv
    a_abs = a * w_abs_prev
    r_abs = r * w_abs
    b_end = b * w_end
    k_end = k * w_end

    lane = lax.broadcasted_iota(jnp.int32, (chunk, LANES), 1)
    first = lane < RWKV_HEAD_SIZE

    def stack(x):
        return jnp.concatenate([jnp.where(first, x, 0.0), jnp.where(first, 0.0, x)], axis=0)

    rr = lax.broadcasted_iota(jnp.int32, (c2, c2), 0)
    cc = lax.broadcasted_iota(jnp.int32, (c2, c2), 1)
    same = (rr >= chunk) == (cc >= chunk)
    strict = same & (rr > cc)
    incl = same & (rr >= cc)
    eye = jnp.where(rr == cc, 1.0, 0.0)

    ps = range(pairs)
    sls = [slice(p * LANES, (p + 1) * LANES) for p in ps]
    bf = lambda x: x.astype(BF16)
    cat0 = lambda xs: jnp.concatenate(xs, axis=0)
    cat1 = lambda xs: jnp.concatenate(xs, axis=1)
    v_s = [bf(stack(v[:, sl])) for sl in sls]
    ar_m = [bf(cat0([stack(a_mid[:, sl]), stack(r_mid[:, sl])])) for sl in sls]
    bk_m = [bf(cat0([stack(b_mid[:, sl]), stack(k_mid[:, sl])])) for sl in sls]
    g_all = [_bdot_nt(ar_m[p], bk_m[p]) for p in ps]
    l_ab = [jnp.where(strict, g[:c2, :c2], 0.0) for g in g_all]
    l_ak = [bf(jnp.where(strict, g[:c2, c2:], 0.0)) for g in g_all]
    m_r = [bf(cat1([jnp.where(incl, g[c2:, :c2], 0.0), jnp.where(incl, g[c2:, c2:], 0.0)])) for g in g_all]

    t_inv = [eye + l_ab[p] for p in ps]
    n_fac = max(chunk.bit_length() - 2, 0)
    lp = [_bdot(l_ab[p], l_ab[p]) for p in ps] if n_fac else l_ab
    for _ in range(n_fac - 1):
        x = [_bdot(cat0([t_inv[p], lp[p]]), lp[p]) for p in ps]
        t_inv = [t_inv[p] + x[p][:c2] for p in ps]
        lp = [x[p][c2:] for p in ps]
    if n_fac:
        t_inv = [t_inv[p] + _bdot(t_inv[p], lp[p]) for p in ps]
    t_inv = [bf(t) for t in t_inv]

    lv = [_bdot(l_ak[p], v_s[p]) for p in ps]
    av = [bf(_bdot(t_inv[p], cat1([stack(a_abs[:, sls[p]]), lv[p]]))) for p in ps]
    zero = jnp.zeros((c2, LANES), BF16)
    ry = [_bdot(m_r[p], cat0([av[p], cat1([zero, v_s[p]])])) for p in ps]
    r_p = [stack(r_abs[:, sls[p]]) + ry[p][:, :LANES] for p in ps]

    s_old = [s_scr[p] for p in ps]
    y_s = [_bdot_nt(r_p[p], s_old[p]) + ry[p][:, LANES:] for p in ps]
    for p in ps:
        y_ref[0, :, sls[p]] = y_s[p][:chunk] + y_s[p][chunk:]
    b_e = [bf(stack(b_end[:, sl])) for sl in sls]
    k_e = [bf(stack(k_end[:, sl])) for sl in sls]
    ab = [_bdot_tn(av[p][:, :LANES], b_e[p]) for p in ps]
    q_t = [_bdot_tn(cat0([av[p][:, LANES:], v_s[p]]), cat0([b_e[p], k_e[p]])) for p in ps]
    for p in ps:
        s_scr[p] = s_old[p] * w_tot[:, sls[p]] + _bdot(s_old[p], ab[p]) + q_t[p]


def rwkv_scan(r, k, v, lw, kn, b, chunk, pairs):
    bsz, s, w = r.shape
    pw = pairs * LANES
    spec = pl.BlockSpec((1, chunk, pw), lambda bi, hi, ci: (bi, ci, hi))
    kern = functools.partial(_rwkv_scan_kernel, chunk=chunk, pairs=pairs)
    return pl.pallas_call(
        kern,
        grid=(bsz, w // pw, s // chunk),
        in_specs=[spec] * 6,
        out_specs=spec,
        out_shape=jax.ShapeDtypeStruct((bsz, s, w), F32),
        scratch_shapes=[pltpu.VMEM((pairs, LANES, LANES), F32)],
        compiler_params=_cparams(("parallel", "parallel", "arbitrary")),
        name="rwkv_scan",
    )(r, k, v, lw, kn, b)


def _rwkv_out(y, bonus, g, lnx_g, lnx_b):
    inv_n = 1.0 / RWKV_HEAD_SIZE
    mu = _head_sum(y) * inv_n
    yc = y - mu
    var = _head_sum(yc * yc) * inv_n
    yn = yc * lax.rsqrt(var + RWKV_GN_EPS) * lnx_g + lnx_b
    return (yn + bonus) * g


def _route(logits):
    lane = lax.broadcasted_iota(jnp.int32, logits.shape, 1)
    big = jnp.int32(4 * LANES)
    gmask = lane < N_GROUPS
    gl = jnp.where(gmask, logits, NEG_INF)
    gmax = jnp.max(gl, axis=1, keepdims=True)
    gidx = jnp.min(jnp.where(gl == gmax, lane, big), axis=1, keepdims=True)
    gtop = 1.0 / jnp.sum(jnp.where(gmask, jnp.exp(gl - gmax), 0.0), axis=1, keepdims=True)
    lo = N_GROUPS + EXPERTS_PER_GROUP * gidx
    emask = (lane >= lo) & (lane < lo + EXPERTS_PER_GROUP)
    el = jnp.where(emask, logits, NEG_INF)
    e1 = jnp.max(el, axis=1, keepdims=True)
    i1 = jnp.min(jnp.where(el == e1, lane, big), axis=1, keepdims=True)
    el2 = jnp.where(lane == i1, NEG_INF, el)
    e2 = jnp.max(el2, axis=1, keepdims=True)
    i2 = jnp.min(jnp.where(el2 == e2, lane, big), axis=1, keepdims=True)
    t = jnp.exp(e2 - e1)
    w1 = gtop / (1.0 + t)
    w2 = gtop * t / (1.0 + t)
    id1 = (i1 - N_GROUPS).astype(F32)
    id2 = (i2 - N_GROUPS).astype(F32)
    return jnp.where(lane == 0, id1, jnp.where(lane == 1, id2, jnp.where(lane == 2, w1,
                                                                         jnp.where(lane == 3, w2, 0.0))))


def _outproj_kernel(yd_ref, ys_ref, bonus_ref, gate_ref, lg_ref, lb_ref, h_ref,
                    w1_ref, w2_ref, g_ref, b_ref, rw_hi_ref, rw_lo_ref, rb_ref, h1_ref, info_ref, *, parts):
    rows = h_ref.shape[0] // parts
    sls = [pl.ds(k * rows, rows) for k in range(parts)]
    d = lambda a, b: jnp.dot(a, b, preferred_element_type=F32)
    yr = [_rwkv_out(ys_ref[sl, :], bonus_ref[sl, :], gate_ref[sl, :], lg_ref[...], lb_ref[...]) for sl in sls]
    mix = [d(yd_ref[sl, :], w1_ref[...]) + d(yr[k].astype(BF16), w2_ref[...]) for k, sl in enumerate(sls)]
    h1 = [_layer_norm(DEEPNORM_ALPHA * h_ref[sl, :] + mix[k], g_ref[...], b_ref[...]) for k, sl in enumerate(sls)]
    for k, sl in enumerate(sls):
        h1_ref[sl, :] = h1[k]
        hi = h1[k].astype(BF16)
        lo = (h1[k] - hi.astype(F32)).astype(BF16)
        logits = d(hi, rw_hi_ref[...]) + d(lo, rw_hi_ref[...]) + d(hi, rw_lo_ref[...]) + rb_ref[...]
        info_ref[sl, :] = _route(logits)


def outproj_ln_route(yd, ys, bonus, gate, lnx_g, lnx_b, h, w_out_a, w_out_b, g, b, rw_hi, rw_lo, rb, tm):
    t, d = h.shape
    ka = yd.shape[1]
    kb = ys.shape[1]
    full = lambda shp: pl.BlockSpec(shp, lambda i: (0, 0))
    tile = lambda n: pl.BlockSpec((tm, n), lambda i: (i, 0))
    return pl.pallas_call(
        functools.partial(_outproj_kernel, parts=2),
        grid=(t // tm,),
        in_specs=[
            tile(ka), tile(kb), tile(kb), tile(kb), full((1, kb)), full((1, kb)), tile(d),
            full((ka, d)), full((kb, d)), full((1, d)), full((1, d)),
            full((d, ROUTER_LANES)), full((d, ROUTER_LANES)), full((1, ROUTER_LANES)),
        ],
        out_specs=[
            pl.BlockSpec((tm, d), lambda i: (i, 0)),
            pl.BlockSpec((tm, ROUTER_LANES), lambda i: (i, 0)),
        ],
        out_shape=[jax.ShapeDtypeStruct((t, d), F32), jax.ShapeDtypeStruct((t, ROUTER_LANES), F32)],
        compiler_params=pltpu.CompilerParams(dimension_semantics=("parallel",),
                                             vmem_limit_bytes=MOE_VMEM_LIMIT),
        name="outproj_ln_route",
    )(yd, ys, bonus, gate, lnx_g.reshape(1, kb), lnx_b.reshape(1, kb), h, w_out_a, w_out_b,
      g.reshape(1, d), b.reshape(1, d), rw_hi, rw_lo, rb)


def _moe_rank_kernel(e_ref, pos_ref, ends_ref, run_scr, start_scr, *, tm, row_tile):
    ph = pl.program_id(0)
    i = pl.program_id(1)
    lane = lax.broadcasted_iota(jnp.int32, (tm, LANES), 1)
    onehot = e_ref[...] == lane
    oh = jnp.where(onehot, 1.0, 0.0)

    @pl.when((ph == 0) & (i == 0))
    def _():
        run_scr[...] = jnp.zeros(run_scr.shape, F32)

    @pl.when(ph == 0)
    def _():
        run_scr[...] += jnp.sum(oh, axis=0, keepdims=True)

    @pl.when((ph == 1) & (i == 0))
    def _():
        padded = jnp.floor((run_scr[...] + (row_tile - 1)) * (1.0 / row_tile)) * row_tile
        ri = lax.broadcasted_iota(jnp.int32, (LANES, LANES), 0)
        ci = lax.broadcasted_iota(jnp.int32, (LANES, LANES), 1)
        upper = jnp.where(ri < ci, 1.0, 0.0).astype(BF16)
        parts = _split3(jnp.broadcast_to(padded, (8, LANES)))
        starts = sum(jnp.dot(q, upper, preferred_element_type=F32) for q in parts)[0:1]
        start_scr[...] = starts
        ends_ref[...] = starts + padded
        run_scr[...] = jnp.zeros(run_scr.shape, F32)

    @pl.when(ph == 1)
    def _():
        ri = lax.broadcasted_iota(jnp.int32, (tm, tm), 0)
        ci = lax.broadcasted_iota(jnp.int32, (tm, tm), 1)
        before = jnp.where(ri > ci, 1.0, 0.0).astype(BF16)
        row = jnp.dot(before, oh.astype(BF16), preferred_element_type=F32) + (run_scr[...] + start_scr[...])
        pos_ref[...] = jnp.sum(jnp.where(onehot, row, 0.0), axis=1, keepdims=True).astype(jnp.int32)
        run_scr[...] += jnp.sum(oh, axis=0, keepdims=True)


def moe_rank(e_col, tm, row_tile):
    n = e_col.shape[0]
    return pl.pallas_call(
        functools.partial(_moe_rank_kernel, tm=tm, row_tile=row_tile),
        grid=(2, n // tm),
        in_specs=[pl.BlockSpec((tm, 1), lambda ph, i: (i, 0))],
        out_specs=[pl.BlockSpec((tm, 1), lambda ph, i: (i * ph, 0)),
                   pl.BlockSpec((1, LANES), lambda ph, i: (0, 0))],
        out_shape=[jax.ShapeDtypeStruct((n, 1), jnp.int32), jax.ShapeDtypeStruct((1, LANES), F32)],
        scratch_shapes=[pltpu.VMEM((1, LANES), F32), pltpu.VMEM((1, LANES), F32)],
        compiler_params=_cparams(("arbitrary", "arbitrary")),
        name="moe_rank",
    )(e_col)


def _row_copy(src_hbm, row, dst, r, sem):
    return pltpu.make_async_copy(src_hbm.at[pl.ds(row, 1)], dst.at[pl.ds(r, 1)], sem)


def _pack_bf16_halves(y):
    n = y.shape[1] // 2
    hi = pltpu.bitcast(y[:, :n].astype(BF16).astype(F32), jnp.uint32)
    lo = pltpu.bitcast(y[:, n:].astype(BF16).astype(F32), jnp.uint32)
    return hi | (lo >> 16)


def _unpack_bf16_halves(w):
    hi = pltpu.bitcast(w & jnp.uint32(0xFFFF0000), F32)
    lo = pltpu.bitcast(w << 16, F32)
    return jnp.concatenate([hi, lo], axis=1)


def _moe_expert_kernel(src_ref, te_ref, first_ref, par_ref, nxt_ref, nact_ref,
                       x_hbm, wg_hbm, wu_hbm, wd_hbm, o_ref,
                       xbuf0, xbuf1, wgf, wuf, wdf, wgb, wub, wdb, sem, wsem, *, tm):
    j = pl.program_id(0)
    n_act = nact_ref[0]
    bufs = (xbuf0, xbuf1)

    def token_copy(tok, into, r):
        return pltpu.make_async_copy(x_hbm.at[tok], bufs[into].at[:, r, :], sem.at[into])

    def start_gather(tile, into):
        for r in range(tm):
            token_copy(src_ref[tile * tm + r], into, r).start(priority=0)

    def wait_gather(into):
        for r in range(tm):
            token_copy(0, into, r).wait()

    def weight_copies(e, par):
        return (pltpu.make_async_copy(wg_hbm.at[e], wgf.at[par], wsem.at[par]),
                pltpu.make_async_copy(wu_hbm.at[e], wuf.at[par], wsem.at[par]),
                pltpu.make_async_copy(wd_hbm.at[e], wdf.at[par], wsem.at[par]))

    @pl.when(j == 0)
    def _():
        start_gather(0, 0)
        for c in weight_copies(te_ref[0], 0):
            c.start(priority=1)

    @pl.when((j < n_act) & (first_ref[j] > 0))
    def _():
        par = par_ref[j]
        for c in weight_copies(te_ref[j], par):
            c.wait()
        wgb[...] = wgf[par].astype(BF16)
        wub[...] = wuf[par].astype(BF16)
        wdb[...] = wdf[par].astype(BF16)

    for slot in range(2):
        @pl.when((j < n_act) & (j % 2 == slot))
        def _():
            wait_gather(slot)
            start_gather(jnp.minimum(j + 1, n_act - 1), 1 - slot)
            x = jnp.concatenate([bufs[slot][c] for c in range(bufs[slot].shape[0])], axis=1).astype(BF16)
            hg = jnp.dot(x, wgb[...], preferred_element_type=F32)
            hu = jnp.dot(x, wub[...], preferred_element_type=F32)
            hid = hg * _sigmoid(hg) * hu
            o_ref[...] = _pack_bf16_halves(jnp.dot(hid.astype(BF16), wdb[...], preferred_element_type=F32))

        @pl.when((j == n_act - 1) & (j % 2 == slot))
        def _():
            wait_gather(1 - slot)

    @pl.when((j < n_act) & (first_ref[j] > 0) & (nxt_ref[j] >= 0))
    def _():
        for c in weight_copies(nxt_ref[j], 1 - par_ref[j]):
            c.start(priority=1)

    @pl.when(j >= n_act)
    def _():
        o_ref[...] = jnp.zeros(o_ref.shape, jnp.uint32)


def moe_experts(src_tok, plan, x, wg, wu, wd, n_rows, tm):
    d, f = wg.shape[1:]
    hbm = pl.BlockSpec(memory_space=pl.ANY)
    return pl.pallas_call(
        functools.partial(_moe_expert_kernel, tm=tm),
        grid_spec=pltpu.PrefetchScalarGridSpec(
            num_scalar_prefetch=6,
            grid=(n_rows // tm,),
            in_specs=[hbm, hbm, hbm, hbm],
            out_specs=pl.BlockSpec((tm, d // 2), lambda j, *_: (j, 0)),
            scratch_shapes=[pltpu.VMEM((d // LANES, tm, LANES), F32), pltpu.VMEM((d // LANES, tm, LANES), F32),
                            pltpu.VMEM((2, d, f), F32), pltpu.VMEM((2, d, f), F32), pltpu.VMEM((2, f, d), F32),
                            pltpu.VMEM((d, f), BF16), pltpu.VMEM((d, f), BF16), pltpu.VMEM((f, d), BF16),
                            pltpu.SemaphoreType.DMA((2,)), pltpu.SemaphoreType.DMA((2,))],
        ),
        out_shape=jax.ShapeDtypeStruct((n_rows, d // 2), jnp.uint32),
        compiler_params=pltpu.CompilerParams(dimension_semantics=("arbitrary",),
                                             vmem_limit_bytes=MOE_VMEM_LIMIT),
        name="moe_experts",
    )(src_tok, *plan, x, wg, wu, wd)


def _final_kernel(p1_ref, p2_ref, y_hbm, h1_ref, info_ref, p_ref, g_ref, b_ref, pw_ref, pg_ref, gw_ref,
                  o_ref, ybuf0, ybuf1, sem, *, tm, n_tiles, parts):
    i = pl.program_id(0)
    bufs = (ybuf0, ybuf1)

    def start_gather(tile, into):
        for r in range(tm):
            _row_copy(y_hbm, p1_ref[tile * tm + r], bufs[into].at[0], r, sem.at[into]).start(priority=0)
            _row_copy(y_hbm, p2_ref[tile * tm + r], bufs[into].at[1], r, sem.at[into]).start(priority=1)

    def wait_gather(into):
        for r in range(tm):
            _row_copy(y_hbm, 0, bufs[into].at[0], r, sem.at[into]).wait()
            _row_copy(y_hbm, 0, bufs[into].at[1], r, sem.at[into]).wait()

    @pl.when(i == 0)
    def _():
        start_gather(0, 0)

    for slot in range(2):
        @pl.when(i % 2 == slot)
        def _():
            wait_gather(slot)
            start_gather(jnp.minimum(i + 1, n_tiles - 1), 1 - slot)
            rows = tm // parts
            sls = [pl.ds(k * rows, rows) for k in range(parts)]
            h2, ple = [], []
            for sl in sls:
                info = info_ref[sl, :]
                ffn = (info[:, 2:3] * _unpack_bf16_halves(bufs[slot][0, sl, :])
                       + info[:, 3:4] * _unpack_bf16_halves(bufs[slot][1, sl, :]))
                h2.append(_layer_norm(DEEPNORM_ALPHA * h1_ref[sl, :] + ffn, g_ref[...], b_ref[...]))
                pe = jnp.dot(p_ref[sl, :].astype(BF16), pw_ref[...], preferred_element_type=F32)
                ple.append(pe * lax.rsqrt(jnp.mean(pe * pe, axis=-1, keepdims=True) + LN_EPS) * pg_ref[...])
            gate = [jnp.dot(h.astype(BF16), gw_ref[...], preferred_element_type=F32) for h in h2]
            for k, sl in enumerate(sls):
                o_ref[sl, :] = h2[k] + _sigmoid(gate[k]) * ple[k]

        @pl.when((i == n_tiles - 1) & (i % 2 == slot))
        def _():
            wait_gather(1 - slot)


def final_block(pos1, pos2, ys, h1, info, p, g, b, ple_w, ple_g, gate_w, tm):
    t, d = h1.shape
    dp = p.shape[1]
    tile = lambda n: pl.BlockSpec((tm, n), lambda i, p1, p2: (i, 0))
    full = lambda shp: pl.BlockSpec(shp, lambda i, p1, p2: (0, 0))
    return pl.pallas_call(
        functools.partial(_final_kernel, tm=tm, n_tiles=t // tm, parts=2),
        grid_spec=pltpu.PrefetchScalarGridSpec(
            num_scalar_prefetch=2,
            grid=(t // tm,),
            in_specs=[pl.BlockSpec(memory_space=pl.ANY), tile(d), tile(ROUTER_LANES), tile(dp),
                      full((1, d)), full((1, d)), full((dp, d)), full((1, d)), full((d, d))],
            out_specs=tile(d),
            scratch_shapes=[pltpu.VMEM((2, tm, d // 2), jnp.uint32), pltpu.VMEM((2, tm, d // 2), jnp.uint32),
                            pltpu.SemaphoreType.DMA((2,))],
        ),
        out_shape=jax.ShapeDtypeStruct((t, d), F32),
        compiler_params=pltpu.CompilerParams(dimension_semantics=("arbitrary",),
                                             vmem_limit_bytes=MOE_VMEM_LIMIT),
        name="final_block",
    )(pos1, pos2, ys, h1, info, p, g.reshape(1, d), b.reshape(1, d), ple_w, ple_g.reshape(1, d), gate_w)


def _dispatch_plan(pos, ends_f, n_tok, tm, n_rows):
    n_tiles = n_rows // tm
    ends = ends_f[0, :N_EXPERTS].astype(jnp.int32)
    src_tok = jnp.zeros((n_rows,), jnp.int32).at[pos].set(
        jnp.arange(2 * n_tok, dtype=jnp.int32) % n_tok, unique_indices=True)
    n_active = ends[-1] // tm
    tile_ids = jnp.arange(n_tiles, dtype=jnp.int32)
    tile_expert = jnp.searchsorted(ends, jnp.minimum(tile_ids, n_active - 1) * tm, side="right")
    tile_expert = jnp.minimum(tile_expert, N_EXPERTS - 1).astype(jnp.int32)
    first = jnp.concatenate([jnp.ones((1,), jnp.int32),
                             (tile_expert[1:] != tile_expert[:-1]).astype(jnp.int32)])
    parity = (jnp.cumsum(first) - 1) % 2
    ids = jnp.arange(N_EXPERTS, dtype=jnp.int32)
    present = ends > jnp.concatenate([jnp.zeros((1,), jnp.int32), ends[:-1]])
    later = present[None, :] & (ids[None, :] > ids[:, None])
    next_of = jnp.min(jnp.where(later, ids[None, :], N_EXPERTS), axis=1)
    next_of = jnp.where(next_of == N_EXPERTS, -1, next_of)
    plan = (tile_expert, first, parity.astype(jnp.int32), next_of[tile_expert].astype(jnp.int32),
            n_active.astype(jnp.int32).reshape(1))
    return src_tok, plan


def _t5_bucket(rel):
    n = jnp.maximum(rel, 0)
    max_exact = REL_BUCKETS // 2
    nf = jnp.maximum(n, 1).astype(F32)
    large = max_exact + (jnp.log(nf / max_exact) / math.log(REL_MAX_DIST / max_exact)
                         * (REL_BUCKETS - max_exact)).astype(jnp.int32)
    large = jnp.minimum(large, REL_BUCKETS - 1)
    return jnp.where(n < max_exact, n, large)


def _bias_tables(rel_bias, tq):
    n = 3 * tq
    rel = 2 * tq - 1 - jnp.arange(n, dtype=jnp.int32)
    g = rel_bias[_t5_bucket(rel)].astype(F32).T
    skew = jnp.tile(g, (1, tq))[:, :tq * (n - 1)].reshape(-1, tq, n - 1)
    tiles = jnp.stack([skew[:, :, 2 * tq - 1:3 * tq - 1], skew[:, :, tq - 1:2 * tq - 1]], axis=1)
    far = rel_bias[_t5_bucket(jnp.full((), 2 * tq, jnp.int32))].astype(F32)
    return tiles * LOG2_E, far * LOG2_E


def kernel(x, p, ln_in_g, ln_in_b, rel_bias, w_in, diff_lam_q1, diff_lam_k1, diff_lam_q2, diff_lam_k2, diff_subln_g, rwkv_mu, rwkv_w0, rwkv_w2, rwkv_a0, rwkv_a2, rwkv_g2, rwkv_k_k, rwkv_k_a, rwkv_r_k, rwkv_lnx_g, rwkv_lnx_b, w_out, ln1_g, ln1_b, router_group_w, router_group_b, router_expert_w, router_expert_b, moe_w_gate, moe_w_up, moe_w_down, ln2_g, ln2_b, ple_w, ple_norm_g, ple_gate_w):
    bsz, seq, d = x.shape
    t = bsz * seq
    i = 0
    lambda_init = 0.8 - 0.6 * math.exp(-0.3 * i)
    tq = 256

    x2 = x.reshape(t, d)
    w_in_t = jnp.swapaxes(w_in[i], 0, 1)
    h, hb = layer_norm_rows(x2, ln_in_g, ln_in_b, 512)
    qkv = matmul_nt(hb, w_in_t, 0, O_RWKV, O_RWKV, BF16, 1024, 512)
    pr = matmul_nt(hb, w_in_t, O_RWKV, RWKV_PROJ, RWKV_PROJ_PAD, F32, 1024, 512)

    lam = (jnp.exp(jnp.sum(diff_lam_q1[i].astype(F32) * diff_lam_k1[i]))
           - jnp.exp(jnp.sum(diff_lam_q2[i].astype(F32) * diff_lam_k2[i])) + lambda_init)
    bias_tiles, far_bias = _bias_tables(rel_bias, tq)
    y_diff = diff_attention(qkv.reshape(bsz, seq, O_RWKV), bias_tiles, far_bias, lam.reshape(1),
                            diff_subln_g[i], lambda_init, tq, 8)

    mu_p = jnp.pad(rwkv_mu[i], (0, RWKV_PROJ_PAD - RWKV_PROJ))
    zeros64 = jnp.zeros((LORA_DECAY, RWKV_WIDTH), F32)
    w2p = jnp.concatenate([rwkv_w2[i], zeros64], axis=0).astype(BF16)
    a2p = jnp.concatenate([zeros64, rwkv_a2[i]], axis=0).astype(BF16)
    g2p = jnp.pad(rwkv_g2[i], ((0, 2 * LANES - LORA_GATE), (0, 0))).astype(BF16)
    r, k2, v, lw, kn, bb, g, bonus = rwkv_prep(
        pr, mu_p, rwkv_w0[i], w2p, rwkv_a0[i], a2p, g2p, rwkv_k_k[i], rwkv_k_a[i],
        rwkv_r_k[i].reshape(-1), seq, 256)
    sh = lambda a: a.reshape(bsz, seq, RWKV_WIDTH)
    y_scan = rwkv_scan(sh(r), sh(k2), sh(v), sh(lw), sh(kn), sh(bb), 64, 8)

    w_out_b = w_out[i].astype(BF16)
    rw = jnp.concatenate([router_group_w[i], router_expert_w[i]], axis=1)
    rw = jnp.pad(rw, ((0, 0), (0, ROUTER_LANES - rw.shape[1])))
    rw_hi = rw.astype(BF16)
    rw_lo = (rw - rw_hi.astype(F32)).astype(BF16)
    rb = jnp.pad(jnp.concatenate([router_group_b[i], router_expert_b[i]]),
                 (0, ROUTER_LANES - N_GROUPS - N_EXPERTS)).reshape(1, ROUTER_LANES)
    h1, info = outproj_ln_route(y_diff.reshape(t, DIFF_WIDTH), y_scan.reshape(t, RWKV_WIDTH), bonus, g,
                                rwkv_lnx_g[i], rwkv_lnx_b[i], h, w_out_b[:DIFF_WIDTH],
                                w_out_b[DIFF_WIDTH:], ln1_g[i], ln1_b[i], rw_hi, rw_lo, rb, 512)

    tm_e = MOE_ROW_TILE
    n_rows = 2 * t + N_EXPERTS * tm_e
    e_all = jnp.concatenate([info[:, 0], info[:, 1]]).astype(jnp.int32)
    pos, ends = moe_rank(e_all.reshape(-1, 1), 1024, tm_e)
    pos = pos[:, 0]
    src_tok, plan = _dispatch_plan(pos, ends, t, tm_e, n_rows)
    f = moe_w_gate.shape[-1]
    ys = moe_experts(src_tok, plan, h1.reshape(t, d // LANES, LANES), moe_w_gate[i].reshape(N_EXPERTS, d, f),
                     moe_w_up[i].reshape(N_EXPERTS, d, f), moe_w_down[i].reshape(N_EXPERTS, f, d),
                     n_rows, tm_e)

    out = final_block(pos[:t], pos[t:], ys, h1, info, p[i].reshape(t, -1), ln2_g[i], ln2_b[i],
                      ple_w[i].astype(BF16), ple_norm_g[i], ple_gate_w[i].astype(BF16), 256)
    return out.reshape(bsz, seq, d)
```

```python
import functools
import math

import jax
import jax.numpy as jnp
from jax import lax
from jax.experimental import pallas as pl
from jax.experimental.pallas import tpu as pltpu

F32 = jnp.float32
BF16 = jnp.bfloat16

DIFF_HEADS = 8
DIFF_HEAD_DIM = 64
DIFF_V_DIM = 128
DIFF_WIDTH = DIFF_HEADS * DIFF_V_DIM
DIFF_QK_WIDTH = DIFF_HEADS * 2 * DIFF_HEAD_DIM
RWKV_HEADS = 16
RWKV_HEAD_SIZE = 64
RWKV_WIDTH = RWKV_HEADS * RWKV_HEAD_SIZE
LORA_DECAY = 64
LORA_AAA = 64
LORA_GATE = 160
RWKV_PROJ = 3 * RWKV_WIDTH + LORA_DECAY + LORA_AAA + LORA_GATE
O_RWKV = 2 * DIFF_QK_WIDTH + DIFF_WIDTH
REL_BUCKETS = 32
REL_MAX_DIST = 128
N_GROUPS = 4
EXPERTS_PER_GROUP = 8
N_EXPERTS = N_GROUPS * EXPERTS_PER_GROUP
LN_EPS = 1e-5
RWKV_GN_EPS = 64e-5
NEG_INF = -1e30
DEPTH = 1
DEEPNORM_ALPHA = (2 * DEPTH) ** 0.25
LOG2_E = math.log2(math.e)

LANES = 128
RWKV_PROJ_PAD = 3584
ROUTER_LANES = LANES
VMEM_LIMIT = 48 * 1024 * 1024
MOE_VMEM_LIMIT = 56 * 1024 * 1024
MOE_ROW_TILE = 256


def _cparams(sem):
    return pltpu.CompilerParams(dimension_semantics=sem, vmem_limit_bytes=VMEM_LIMIT)


def _bdot(a, b):
    return jnp.dot(a.astype(BF16), b.astype(BF16), preferred_element_type=F32)


def _bdot_nt(a, b):
    return lax.dot_general(a.astype(BF16), b.astype(BF16), (((1,), (1,)), ((), ())),
                           preferred_element_type=F32)


def _bdot_tn(a, b):
    return lax.dot_general(a.astype(BF16), b.astype(BF16), (((0,), (0,)), ((), ())),
                           preferred_element_type=F32)


def _split3(x):
    hi = x.astype(BF16)
    r1 = x - hi.astype(F32)
    mid = r1.astype(BF16)
    lo = (r1 - mid.astype(F32)).astype(BF16)
    return hi, mid, lo


def _head_sum(x):
    lane = lax.broadcasted_iota(jnp.int32, (x.shape[0], LANES), 1)
    first = lane < RWKV_HEAD_SIZE
    cols = []
    for c in range(x.shape[1] // LANES):
        blk = x[:, c * LANES:(c + 1) * LANES]
        lo = jnp.sum(jnp.where(first, blk, 0.0), axis=1, keepdims=True)
        hi = jnp.sum(jnp.where(first, 0.0, blk), axis=1, keepdims=True)
        cols.append(jnp.where(first, lo, hi))
    return jnp.concatenate(cols, axis=1)


def _dot_sel_x(sel, x):
    hi, mid, lo = _split3(x)
    d = lambda p: jnp.dot(sel, p, preferred_element_type=F32)
    return d(hi) + d(mid) + d(lo)


def _layer_norm(x, g, b):
    mu = jnp.mean(x, axis=-1, keepdims=True)
    xc = x - mu
    var = jnp.mean(xc * xc, axis=-1, keepdims=True)
    return xc * lax.rsqrt(var + LN_EPS) * g + b


def _sigmoid(z):
    return 1.0 / (1.0 + jnp.exp(-z))


def _ln_kernel(x_ref, g_ref, b_ref, h_ref, hb_ref):
    xn = _layer_norm(x_ref[...], g_ref[...], b_ref[...])
    h_ref[...] = xn
    hb_ref[...] = xn.astype(BF16)


def layer_norm_rows(x, g, b, tm):
    m, d = x.shape
    tile = pl.BlockSpec((tm, d), lambda i: (i, 0))
    row = pl.BlockSpec((1, d), lambda i: (0, 0))
    return pl.pallas_call(
        _ln_kernel,
        grid=(m // tm,),
        in_specs=[tile, row, row],
        out_specs=[tile, tile],
        out_shape=[jax.ShapeDtypeStruct((m, d), F32), jax.ShapeDtypeStruct((m, d), BF16)],
        compiler_params=_cparams(("parallel",)),
        name="layer_norm_rows",
    )(x, g.reshape(1, d), b.reshape(1, d))


def _mm_nt_kernel(x_ref, wt_ref, o_ref, *, n_valid, tn):
    y = lax.dot_general(x_ref[...], wt_ref[...].astype(BF16), (((1,), (1,)), ((), ())),
                        preferred_element_type=F32)
    if n_valid % tn:
        col = pl.program_id(1) * tn + lax.broadcasted_iota(jnp.int32, y.shape, 1)
        y = jnp.where(col < n_valid, y, 0.0)
    o_ref[...] = y.astype(o_ref.dtype)


def matmul_nt(x, wt, row0, n_valid, n_out, out_dtype, tm, tn):
    m, k = x.shape
    blk0 = row0 // tn
    return pl.pallas_call(
        functools.partial(_mm_nt_kernel, n_valid=n_valid, tn=tn),
        grid=(m // tm, n_out // tn),
        in_specs=[pl.BlockSpec((tm, k), lambda i, j: (i, 0)),
                  pl.BlockSpec((tn, k), lambda i, j: (blk0 + j, 0))],
        out_specs=pl.BlockSpec((tm, tn), lambda i, j: (i, j)),
        out_shape=jax.ShapeDtypeStruct((m, n_out), out_dtype),
        compiler_params=_cparams(("parallel", "parallel")),
        name="matmul_nt",
    )(x, wt)


def _attn_kernel(lam_ref, far_ref, q_ref, k_ref, v_ref, bias_ref, g_ref, o_ref,
                 m_scr, l_scr, acc_scr, *, tq, hps, scale, post_scale):
    hp = pl.program_id(1)
    qi = pl.program_id(2)
    lane = lax.broadcasted_iota(jnp.int32, (tq, LANES), 1)
    q_maps = []
    for hh in range(hps):
        qs = (q_ref[0, :, hh * LANES:(hh + 1) * LANES].astype(F32) * scale).astype(BF16)
        zero = jnp.zeros_like(qs)
        q_maps += [jnp.where(lane < DIFF_HEAD_DIM, qs, zero), jnp.where(lane >= DIFF_HEAD_DIM, qs, zero)]

    m_scr[...] = jnp.full(m_scr.shape, NEG_INF, F32)
    l_scr[...] = jnp.zeros(l_scr.shape, F32)
    acc_scr[...] = jnp.zeros(acc_scr.shape, F32)

    def step(kstart, bias_index, causal):
        for hh in range(hps):
            kb = k_ref[0, pl.ds(kstart, tq), hh * LANES:(hh + 1) * LANES]
            vb = v_ref[0, pl.ds(kstart, tq), hh * LANES:(hh + 1) * LANES]
            for c in range(2 * hh, 2 * hh + 2):
                s = lax.dot_general(q_maps[c], kb, (((1,), (1,)), ((), ())), preferred_element_type=F32)
                m_old = m_scr[c]
                if bias_index is None:
                    far = far_ref[hp * hps + hh]
                    m_new = jnp.maximum(m_old, jnp.max(s, axis=1, keepdims=True) + far)
                    shift = m_new - far
                else:
                    s = s + bias_ref[hh, bias_index]
                    if causal:
                        ri = lax.broadcasted_iota(jnp.int32, (tq, tq), 0)
                        ci = lax.broadcasted_iota(jnp.int32, (tq, tq), 1)
                        s = jnp.where(ri >= ci, s, NEG_INF)
                    m_new = jnp.maximum(m_old, jnp.max(s, axis=1, keepdims=True))
                    shift = m_new
                p = jnp.exp2(s - jnp.concatenate([shift] * (tq // LANES), axis=1))
                corr = jnp.exp2(m_old - m_new)
                l_scr[c] = corr * l_scr[c] + jnp.sum(p, axis=1, keepdims=True)
                acc_scr[c] = corr * acc_scr[c] + jnp.dot(p.astype(BF16), vb, preferred_element_type=F32)
                m_scr[c] = m_new

    def far_body(kb, carry):
        step(pl.multiple_of(kb * tq, tq), None, False)
        return carry

    lax.fori_loop(0, jnp.maximum(qi - 1, 0), far_body, 0)

    @pl.when(qi >= 1)
    def _():
        step(pl.multiple_of((qi - 1) * tq, tq), 1, False)

    step(pl.multiple_of(qi * tq, tq), 0, True)

    for hh in range(hps):
        c = 2 * hh
        o = acc_scr[c] / l_scr[c] - lam_ref[0] * (acc_scr[c + 1] / l_scr[c + 1])
        ms = jnp.mean(o * o, axis=1, keepdims=True)
        o_ref[0, :, hh * LANES:(hh + 1) * LANES] = (
            o * lax.rsqrt(ms + LN_EPS) * g_ref[...] * post_scale).astype(o_ref.dtype)


def diff_attention(qkv, bias_tiles, far_bias, lam, subln_g, lambda_init, tq, hps):
    bsz, s, _ = qkv.shape
    kern = functools.partial(_attn_kernel, tq=tq, hps=hps, scale=DIFF_HEAD_DIM ** -0.5 * LOG2_E,
                             post_scale=1.0 - lambda_init)
    hw = hps * LANES
    nqk = DIFF_QK_WIDTH // hw
    return pl.pallas_call(
        kern,
        grid=(bsz, DIFF_HEADS // hps, s // tq),
        in_specs=[
            pl.BlockSpec(memory_space=pltpu.SMEM),
            pl.BlockSpec(memory_space=pltpu.SMEM),
            pl.BlockSpec((1, tq, hw), lambda b, h, i: (b, i, h)),
            pl.BlockSpec((1, s, hw), lambda b, h, i: (b, 0, nqk + h)),
            pl.BlockSpec((1, s, hw), lambda b, h, i: (b, 0, 2 * nqk + h)),
            pl.BlockSpec((hps, 2, tq, tq), lambda b, h, i: (h, 0, 0, 0)),
            pl.BlockSpec((1, LANES), lambda b, h, i: (0, 0)),
        ],
        out_specs=pl.BlockSpec((1, tq, hw), lambda b, h, i: (b, i, h)),
        out_shape=jax.ShapeDtypeStruct((bsz, s, DIFF_WIDTH), BF16),
        scratch_shapes=[pltpu.VMEM((2 * hps, tq, LANES), F32), pltpu.VMEM((2 * hps, tq, LANES), F32),
                        pltpu.VMEM((2 * hps, tq, LANES), F32)],
        compiler_params=_cparams(("parallel", "parallel", "arbitrary")),
        name="diff_attention",
    )(lam, far_bias, qkv, qkv, qkv, bias_tiles, subln_g.reshape(1, LANES))


def _rwkv_prep_kernel(x_ref, prev_ref, mu_ref, w0_ref, w2_ref, a0_ref, a2_ref, g2_ref,
                      kk_ref, ka_ref, rk_ref,
                      r_out, k_out, v_out, lw_out, kn_out, b_out, g_out, bonus_out, *, tm, seq):
    i = pl.program_id(0)
    x = x_ref[...]
    w = RWKV_WIDTH
    row = lax.broadcasted_iota(jnp.int32, x.shape, 0)
    is_start = (i * tm) % seq == 0
    last_prev = jnp.where(is_start, 0.0, prev_ref[7:8, :])
    prev = jnp.where(row == 0, last_prev, pltpu.roll(x, 1, 0))
    xs = x + (prev - x) * mu_ref[...]

    r = xs[:, 0:w]
    k = xs[:, w:2 * w]
    v = xs[:, 2 * w:3 * w]
    xwa = xs[:, 3 * w:3 * w + 2 * LORA_DECAY]
    xg = xs[:, 3 * w + LANES:3 * w + LANES + 2 * LANES]

    z = w0_ref[...] + _bdot(jnp.tanh(xwa), w2_ref[...])
    softplus_neg_z = jnp.maximum(-z, 0.0) + jnp.log(1.0 + jnp.exp(-jnp.abs(z)))
    lw_out[...] = -jnp.exp(-softplus_neg_z - 0.5)
    a_lr = _sigmoid(a0_ref[...] + _bdot(xwa, a2_ref[...]))
    g_out[...] = _bdot(_sigmoid(xg), g2_ref[...])

    kk = k * kk_ref[...]
    kn = kk / jnp.maximum(jnp.sqrt(_head_sum(kk * kk)), 1e-12)
    k2 = k * (1.0 + (a_lr - 1.0) * ka_ref[...])
    r_out[...] = r
    k_out[...] = k2
    v_out[...] = v
    kn_out[...] = kn
    b_out[...] = kn * a_lr
    bonus_out[...] = _head_sum(r * k2 * rk_ref[...]) * v


def rwkv_prep(pr, mu, w0, w2p, a0, a2p, g2p, k_k, k_a, r_k, seq, tm):
    t, wp = pr.shape
    w = RWKV_WIDTH
    row = lambda a: a.reshape(1, -1)
    full = lambda shp: pl.BlockSpec(shp, lambda i: (0, 0))
    outs = [jax.ShapeDtypeStruct((t, w), F32)] * 8
    kern = functools.partial(_rwkv_prep_kernel, tm=tm, seq=seq)
    return pl.pallas_call(
        kern,
        grid=(t // tm,),
        in_specs=[
            pl.BlockSpec((tm, wp), lambda i: (i, 0)),
            pl.BlockSpec((8, wp), lambda i: (jnp.maximum(i * (tm // 8) - 1, 0), 0)),
            full((1, wp)), full((1, w)), full((LANES, w)), full((1, w)), full((LANES, w)),
            full((2 * LANES, w)), full((1, w)), full((1, w)), full((1, w)),
        ],
        out_specs=[pl.BlockSpec((tm, w), lambda i: (i, 0))] * 8,
        out_shape=outs,
        compiler_params=_cparams(("parallel",)),
        name="rwkv_prep",
    )(pr, pr, row(mu), row(w0), w2p, row(a0), a2p, g2p, row(k_k), row(k_a), row(r_k))


def _rwkv_scan_kernel(r_ref, k_ref, v_ref, lw_ref, kn_ref, b_ref, y_ref, s_scr, *, chunk, pairs):
    c2 = 2 * chunk

    @pl.when(pl.program_id(2) == 0)
    def _():
        s_scr[...] = jnp.zeros(s_scr.shape, F32)

    ri = lax.broadcasted_iota(jnp.int32, (chunk, chunk), 0)
    ci = lax.broadcasted_iota(jnp.int32, (chunk, chunk), 1)
    ltri = jnp.where(ri >= ci, 1.0, 0.0).astype(BF16)

    lw = lw_ref[0]
    cl = _dot_sel_x(ltri, lw)
    mid = chunk // 2 - 1
    clm = cl[mid:mid + 1, :]
    cle = cl[chunk - 1:chunk, :]
    w_mid = jnp.exp(cl - clm)
    w_mid_prev = jnp.exp(cl - lw - clm)
    w_mid_inv = jnp.exp(clm - cl)
    w_abs = jnp.exp(cl)
    w_abs_prev = jnp.exp(cl - lw)
    w_end = jnp.exp(cle - cl)
    w_tot = jnp.exp(cle)

    r = r_ref[0]
    k = k_ref[0]
    v = v_ref[0]
    a = -kn_ref[0]
    b = b_ref[0]
    a_mid = a * w_mid_prev
    r_mid = r * w_mid
    b_mid = b * w_mid_inv
    k_mid = k * w_mid_inv
    a_abs = a * w_abs_prev
    r_abs = r * w_abs
    b_end = b * w_end
    k_end = k * w_end

    lane = lax.broadcasted_iota(jnp.int32, (chunk, LANES), 1)
    first = lane < RWKV_HEAD_SIZE

    def stack(x):
        return jnp.concatenate([jnp.where(first, x, 0.0), jnp.where(first, 0.0, x)], axis=0)

    rr = lax.broadcasted_iota(jnp.int32, (c2, c2), 0)
    cc = lax.broadcasted_iota(jnp.int32, (c2, c2), 1)
    same = (rr >= chunk) == (cc >= chunk)
    strict = same & (rr > cc)
    incl = same & (rr >= cc)
    eye = jnp.where(rr == cc, 1.0, 0.0)

    ps = range(pairs)
    sls = [slice(p * LANES, (p + 1) * LANES) for p in ps]
    bf = lambda x: x.astype(BF16)
    cat0 = lambda xs: jnp.concatenate(xs, axis=0)
    cat1 = lambda xs: jnp.concatenate(xs, axis=1)
    v_s = [bf(stack(v[:, sl])) for sl in sls]
    ar_m = [bf(cat0([stack(a_mid[:, sl]), stack(r_mid[:, sl])])) for sl in sls]
    bk_m = [bf(cat0([stack(b_mid[:, sl]), stack(k_mid[:, sl])])) for sl in sls]
    g_all = [_bdot_nt(ar_m[p], bk_m[p]) for p in ps]
    l_ab = [jnp.where(strict, g[:c2, :c2], 0.0) for g in g_all]
    l_ak = [bf(jnp.where(strict, g[:c2, c2:], 0.0)) for g in g_all]
    m_r = [bf(cat1([jnp.where(incl, g[c2:, :c2], 0.0), jnp.where(incl, g[c2:, c2:], 0.0)])) for g in g_all]

    t_inv = [eye + l_ab[p] for p in ps]
    n_fac = max(chunk.bit_length() - 2, 0)
    lp = [_bdot(l_ab[p], l_ab[p]) for p in ps] if n_fac else l_ab
    for _ in range(n_fac - 1):
        x = [_bdot(cat0([t_inv[p], lp[p]]), lp[p]) for p in ps]
        t_inv = [t_inv[p] + x[p][:c2] for p in ps]
        lp = [x[p][c2:] for p in ps]
    if n_fac:
        t_inv = [t_inv[p] + _bdot(t_inv[p], lp[p]) for p in ps]
    t_inv = [bf(t) for t in t_inv]

    lv = [_bdot(l_ak[p], v_s[p]) for p in ps]
    av = [bf(_bdot(t_inv[p], cat1([stack(a_abs[:, sls[p]]), lv[p]]))) for p in ps]
    zero = jnp.zeros((c2, LANES), BF16)
    ry = [_bdot(m_r[p], cat0([av[p], cat1([zero, v_s[p]])])) for p in ps]
    r_p = [stack(r_abs[:, sls[p]]) + ry[p][:, :LANES] for p in ps]

    s_old = [s_scr[p] for p in ps]
    y_s = [_bdot_nt(r_p[p], s_old[p]) + ry[p][:, LANES:] for p in ps]
    for p in ps:
        y_ref[0, :, sls[p]] = y_s[p][:chunk] + y_s[p][chunk:]
    b_e = [bf(stack(b_end[:, sl])) for sl in sls]
    k_e = [bf(stack(k_end[:, sl])) for sl in sls]
    ab = [_bdot_tn(av[p][:, :LANES], b_e[p]) for p in ps]
    q_t = [_bdot_tn(cat0([av[p][:, LANES:], v_s[p]]), cat0([b_e[p], k_e[p]])) for p in ps]
    for p in ps:
        s_scr[p] = s_old[p] * w_tot[:, sls[p]] + _bdot(s_old[p], ab[p]) + q_t[p]


def rwkv_scan(r, k, v, lw, kn, b, chunk, pairs):
    bsz, s, w = r.shape
    pw = pairs * LANES
    spec = pl.BlockSpec((1, chunk, pw), lambda bi, hi, ci: (bi, ci, hi))
    kern = functools.partial(_rwkv_scan_kernel, chunk=chunk, pairs=pairs)
    return pl.pallas_call(
        kern,
        grid=(bsz, w // pw, s // chunk),
        in_specs=[spec] * 6,
        out_specs=spec,
        out_shape=jax.ShapeDtypeStruct((bsz, s, w), F32),
        scratch_shapes=[pltpu.VMEM((pairs, LANES, LANES), F32)],
        compiler_params=_cparams(("parallel", "parallel", "arbitrary")),
        name="rwkv_scan",
    )(r, k, v, lw, kn, b)


def _rwkv_out(y, bonus, g, lnx_g, lnx_b):
    inv_n = 1.0 / RWKV_HEAD_SIZE
    mu = _head_sum(y) * inv_n
    yc = y - mu
    var = _head_sum(yc * yc) * inv_n
    yn = yc * lax.rsqrt(var + RWKV_GN_EPS) * lnx_g + lnx_b
    return (yn + bonus) * g


def _route(logits):
    lane = lax.broadcasted_iota(jnp.int32, logits.shape, 1)
    big = jnp.int32(4 * LANES)
    gmask = lane < N_GROUPS
    gl = jnp.where(gmask, logits, NEG_INF)
    gmax = jnp.max(gl, axis=1, keepdims=True)
    gidx = jnp.min(jnp.where(gl == gmax, lane, big), axis=1, keepdims=True)
    gtop = 1.0 / jnp.sum(jnp.where(gmask, jnp.exp(gl - gmax), 0.0), axis=1, keepdims=True)
    lo = N_GROUPS + EXPERTS_PER_GROUP * gidx
    emask = (lane >= lo) & (lane < lo + EXPERTS_PER_GROUP)
    el = jnp.where(emask, logits, NEG_INF)
    e1 = jnp.max(el, axis=1, keepdims=True)
    i1 = jnp.min(jnp.where(el == e1, lane, big), axis=1, keepdims=True)
    el2 = jnp.where(lane == i1, NEG_INF, el)
    e2 = jnp.max(el2, axis=1, keepdims=True)
    i2 = jnp.min(jnp.where(el2 == e2, lane, big), axis=1, keepdims=True)
    t = jnp.exp(e2 - e1)
    w1 = gtop / (1.0 + t)
    w2 = gtop * t / (1.0 + t)
    id1 = (i1 - N_GROUPS).astype(F32)
    id2 = (i2 - N_GROUPS).astype(F32)
    return jnp.where(lane == 0, id1, jnp.where(lane == 1, id2, jnp.where(lane == 2, w1,
                                                                         jnp.where(lane == 3, w2, 0.0))))


def _outproj_kernel(yd_ref, ys_ref, bonus_ref, gate_ref, lg_ref, lb_ref, h_ref,
                    w1_ref, w2_ref, g_ref, b_ref, rw_hi_ref, rw_lo_ref, rb_ref, h1_ref, info_ref, *, parts):
    rows = h_ref.shape[0] // parts
    sls = [pl.ds(k * rows, rows) for k in range(parts)]
    d = lambda a, b: jnp.dot(a, b, preferred_element_type=F32)
    yr = [_rwkv_out(ys_ref[sl, :], bonus_ref[sl, :], gate_ref[sl, :], lg_ref[...], lb_ref[...]) for sl in sls]
    mix = [d(yd_ref[sl, :], w1_ref[...]) + d(yr[k].astype(BF16), w2_ref[...]) for k, sl in enumerate(sls)]
    h1 = [_layer_norm(DEEPNORM_ALPHA * h_ref[sl, :] + mix[k], g_ref[...], b_ref[...]) for k, sl in enumerate(sls)]
    for k, sl in enumerate(sls):
        h1_ref[sl, :] = h1[k]
        hi = h1[k].astype(BF16)
        lo = (h1[k] - hi.astype(F32)).astype(BF16)
        logits = d(hi, rw_hi_ref[...]) + d(lo, rw_hi_ref[...]) + d(hi, rw_lo_ref[...]) + rb_ref[...]
        info_ref[sl, :] = _route(logits)


def outproj_ln_route(yd, ys, bonus, gate, lnx_g, lnx_b, h, w_out_a, w_out_b, g, b, rw_hi, rw_lo, rb, tm):
    t, d = h.shape
    ka = yd.shape[1]
    kb = ys.shape[1]
    full = lambda shp: pl.BlockSpec(shp, lambda i: (0, 0))
    tile = lambda n: pl.BlockSpec((tm, n), lambda i: (i, 0))
    return pl.pallas_call(
        functools.partial(_outproj_kernel, parts=2),
        grid=(t // tm,),
        in_specs=[
            tile(ka), tile(kb), tile(kb), tile(kb), full((1, kb)), full((1, kb)), tile(d),
            full((ka, d)), full((kb, d)), full((1, d)), full((1, d)),
            full((d, ROUTER_LANES)), full((d, ROUTER_LANES)), full((1, ROUTER_LANES)),
        ],
        out_specs=[
            pl.BlockSpec((tm, d), lambda i: (i, 0)),
            pl.BlockSpec((tm, ROUTER_LANES), lambda i: (i, 0)),
        ],
        out_shape=[jax.ShapeDtypeStruct((t, d), F32), jax.ShapeDtypeStruct((t, ROUTER_LANES), F32)],
        compiler_params=pltpu.CompilerParams(dimension_semantics=("parallel",),
                                             vmem_limit_bytes=MOE_VMEM_LIMIT),
        name="outproj_ln_route",
    )(yd, ys, bonus, gate, lnx_g.reshape(1, kb), lnx_b.reshape(1, kb), h, w_out_a, w_out_b,
      g.reshape(1, d), b.reshape(1, d), rw_hi, rw_lo, rb)


def _moe_rank_kernel(e_ref, pos_ref, ends_ref, run_scr, start_scr, *, tm, row_tile):
    ph = pl.program_id(0)
    i = pl.program_id(1)
    lane = lax.broadcasted_iota(jnp.int32, (tm, LANES), 1)
    onehot = e_ref[...] == lane
    oh = jnp.where(onehot, 1.0, 0.0)

    @pl.when((ph == 0) & (i == 0))
    def _():
        run_scr[...] = jnp.zeros(run_scr.shape, F32)

    @pl.when(ph == 0)
    def _():
        run_scr[...] += jnp.sum(oh, axis=0, keepdims=True)

    @pl.when((ph == 1) & (i == 0))
    def _():
        padded = jnp.floor((run_scr[...] + (row_tile - 1)) * (1.0 / row_tile)) * row_tile
        ri = lax.broadcasted_iota(jnp.int32, (LANES, LANES), 0)
        ci = lax.broadcasted_iota(jnp.int32, (LANES, LANES), 1)
        upper = jnp.where(ri < ci, 1.0, 0.0).astype(BF16)
        parts = _split3(jnp.broadcast_to(padded, (8, LANES)))
        starts = sum(jnp.dot(q, upper, preferred_element_type=F32) for q in parts)[0:1]
        start_scr[...] = starts
        ends_ref[...] = starts + padded
        run_scr[...] = jnp.zeros(run_scr.shape, F32)

    @pl.when(ph == 1)
    def _():
        ri = lax.broadcasted_iota(jnp.int32, (tm, tm), 0)
        ci = lax.broadcasted_iota(jnp.int32, (tm, tm), 1)
        before = jnp.where(ri > ci, 1.0, 0.0).astype(BF16)
        row = jnp.dot(before, oh.astype(BF16), preferred_element_type=F32) + (run_scr[...] + start_scr[...])
        pos_ref[...] = jnp.sum(jnp.where(onehot, row, 0.0), axis=1, keepdims=True).astype(jnp.int32)
        run_scr[...] += jnp.sum(oh, axis=0, keepdims=True)


def moe_rank(e_col, tm, row_tile):
    n = e_col.shape[0]
    return pl.pallas_call(
        functools.partial(_moe_rank_kernel, tm=tm, row_tile=row_tile),
        grid=(2, n // tm),
        in_specs=[pl.BlockSpec((tm, 1), lambda ph, i: (i, 0))],
        out_specs=[pl.BlockSpec((tm, 1), lambda ph, i: (i * ph, 0)),
                   pl.BlockSpec((1, LANES), lambda ph, i: (0, 0))],
        out_shape=[jax.ShapeDtypeStruct((n, 1), jnp.int32), jax.ShapeDtypeStruct((1, LANES), F32)],
        scratch_shapes=[pltpu.VMEM((1, LANES), F32), pltpu.VMEM((1, LANES), F32)],
        compiler_params=_cparams(("arbitrary", "arbitrary")),
        name="moe_rank",
    )(e_col)


def _row_copy(src_hbm, row, dst, r, sem):
    return pltpu.make_async_copy(src_hbm.at[pl.ds(row, 1)], dst.at[pl.ds(r, 1)], sem)


def _pack_bf16_halves(y):
    n = y.shape[1] // 2
    hi = pltpu.bitcast(y[:, :n].astype(BF16).astype(F32), jnp.uint32)
    lo = pltpu.bitcast(y[:, n:].astype(BF16).astype(F32), jnp.uint32)
    return hi | (lo >> 16)


def _unpack_bf16_halves(w):
    hi = pltpu.bitcast(w & jnp.uint32(0xFFFF0000), F32)
    lo = pltpu.bitcast(w << 16, F32)
    return jnp.concatenate([hi, lo], axis=1)


def _moe_expert_kernel(src_ref, te_ref, first_ref, par_ref, nxt_ref, nact_ref,
                       x_hbm, wg_hbm, wu_hbm, wd_hbm, o_ref,
                       xbuf0, xbuf1, wgf, wuf, wdf, wgb, wub, wdb, sem, wsem, *, tm):
    j = pl.program_id(0)
    n_act = nact_ref[0]
    bufs = (xbuf0, xbuf1)

    def token_copy(tok, into, r):
        return pltpu.make_async_copy(x_hbm.at[tok], bufs[into].at[:, r, :], sem.at[into])

    def start_gather(tile, into):
        for r in range(tm):
            token_copy(src_ref[tile * tm + r], into, r).start(priority=r % 2)

    def wait_gather(into):
        for r in range(tm):
            token_copy(0, into, r).wait()

    def weight_copies(e, par):
        return (pltpu.make_async_copy(wg_hbm.at[e], wgf.at[par], wsem.at[par]),
                pltpu.make_async_copy(wu_hbm.at[e], wuf.at[par], wsem.at[par]),
                pltpu.make_async_copy(wd_hbm.at[e], wdf.at[par], wsem.at[par]))

    @pl.when(j == 0)
    def _():
        start_gather(0, 0)
        for c in weight_copies(te_ref[0], 0):
            c.start(priority=1)

    @pl.when((j < n_act) & (first_ref[j] > 0))
    def _():
        par = par_ref[j]
        for c in weight_copies(te_ref[j], par):
            c.wait()
        wgb[...] = wgf[par].astype(BF16)
        wub[...] = wuf[par].astype(BF16)
        wdb[...] = wdf[par].astype(BF16)

    for slot in range(2):
        @pl.when((j < n_act) & (j % 2 == slot))
        def _():
            wait_gather(slot)
            start_gather(jnp.minimum(j + 1, n_act - 1), 1 - slot)
            x = jnp.concatenate([bufs[slot][c] for c in range(bufs[slot].shape[0])], axis=1).astype(BF16)
            hg = jnp.dot(x, wgb[...], preferred_element_type=F32)
            hu = jnp.dot(x, wub[...], preferred_element_type=F32)
            hid = hg * _sigmoid(hg) * hu
            o_ref[...] = _pack_bf16_halves(jnp.dot(hid.astype(BF16), wdb[...], preferred_element_type=F32))

        @pl.when((j == n_act - 1) & (j % 2 == slot))
        def _():
            wait_gather(1 - slot)

    @pl.when((j < n_act) & (first_ref[j] > 0) & (nxt_ref[j] >= 0))
    def _():
        for c in weight_copies(nxt_ref[j], 1 - par_ref[j]):
            c.start(priority=1)

    @pl.when(j >= n_act)
    def _():
        o_ref[...] = jnp.zeros(o_ref.shape, jnp.uint32)


def moe_experts(src_tok, plan, x, wg, wu, wd, n_rows, tm):
    d, f = wg.shape[1:]
    hbm = pl.BlockSpec(memory_space=pl.ANY)
    return pl.pallas_call(
        functools.partial(_moe_expert_kernel, tm=tm),
        grid_spec=pltpu.PrefetchScalarGridSpec(
            num_scalar_prefetch=6,
            grid=(n_rows // tm,),
            in_specs=[hbm, hbm, hbm, hbm],
            out_specs=pl.BlockSpec((tm, d // 2), lambda j, *_: (j, 0)),
            scratch_shapes=[pltpu.VMEM((d // LANES, tm, LANES), F32), pltpu.VMEM((d // LANES, tm, LANES), F32),
                            pltpu.VMEM((2, d, f), F32), pltpu.VMEM((2, d, f), F32), pltpu.VMEM((2, f, d), F32),
                            pltpu.VMEM((d, f), BF16), pltpu.VMEM((d, f), BF16), pltpu.VMEM((f, d), BF16),
                            pltpu.SemaphoreType.DMA((2,)), pltpu.SemaphoreType.DMA((2,))],
        ),
        out_shape=jax.ShapeDtypeStruct((n_rows, d // 2), jnp.uint32),
        compiler_params=pltpu.CompilerParams(dimension_semantics=("arbitrary",),
                                             vmem_limit_bytes=MOE_VMEM_LIMIT),
        name="moe_experts",
    )(src_tok, *plan, x, wg, wu, wd)


def _final_kernel(p1_ref, p2_ref, y_hbm, h1_ref, info_ref, p_ref, g_ref, b_ref, pw_ref, pg_ref, gw_ref,
                  o_ref, ybuf0, ybuf1, sem, *, tm, n_tiles, parts):
    i = pl.program_id(0)
    bufs = (ybuf0, ybuf1)

    def start_gather(tile, into):
        for r in range(tm):
            _row_copy(y_hbm, p1_ref[tile * tm + r], bufs[into].at[0], r, sem.at[into]).start(priority=0)
            _row_copy(y_hbm, p2_ref[tile * tm + r], bufs[into].at[1], r, sem.at[into]).start(priority=1)

    def wait_gather(into):
        for r in range(tm):
            _row_copy(y_hbm, 0, bufs[into].at[0], r, sem.at[into]).wait()
            _row_copy(y_hbm, 0, bufs[into].at[1], r, sem.at[into]).wait()

    @pl.when(i == 0)
    def _():
        start_gather(0, 0)

    for slot in range(2):
        @pl.when(i % 2 == slot)
        def _():
            wait_gather(slot)
            start_gather(jnp.minimum(i + 1, n_tiles - 1), 1 - slot)
            rows = tm // parts
            sls = [pl.ds(k * rows, rows) for k in range(parts)]
            h2, ple = [], []
            for sl in sls:
                info = info_ref[sl, :]
                ffn = (info[:, 2:3] * _unpack_bf16_halves(bufs[slot][0, sl, :])
                       + info[:, 3:4] * _unpack_bf16_halves(bufs[slot][1, sl, :]))
                h2.append(_layer_norm(DEEPNORM_ALPHA * h1_ref[sl, :] + ffn, g_ref[...], b_ref[...]))
                pe = jnp.dot(p_ref[sl, :].astype(BF16), pw_ref[...], preferred_element_type=F32)
                ple.append(pe * lax.rsqrt(jnp.mean(pe * pe, axis=-1, keepdims=True) + LN_EPS) * pg_ref[...])
            gate = [jnp.dot(h.astype(BF16), gw_ref[...], preferred_element_type=F32) for h in h2]
            for k, sl in enumerate(sls):
                o_ref[sl, :] = h2[k] + _sigmoid(gate[k]) * ple[k]

        @pl.when((i == n_tiles - 1) & (i % 2 == slot))
        def _():
            wait_gather(1 - slot)


def final_block(pos1, pos2, ys, h1, info, p, g, b, ple_w, ple_g, gate_w, tm):
    t, d = h1.shape
    dp = p.shape[1]
    tile = lambda n: pl.BlockSpec((tm, n), lambda i, p1, p2: (i, 0))
    full = lambda shp: pl.BlockSpec(shp, lambda i, p1, p2: (0, 0))
    return pl.pallas_call(
        functools.partial(_final_kernel, tm=tm, n_tiles=t // tm, parts=2),
        grid_spec=pltpu.PrefetchScalarGridSpec(
            num_scalar_prefetch=2,
            grid=(t // tm,),
            in_specs=[pl.BlockSpec(memory_space=pl.ANY), tile(d), tile(ROUTER_LANES), tile(dp),
                      full((1, d)), full((1, d)), full((dp, d)), full((1, d)), full((d, d))],
            out_specs=tile(d),
            scratch_shapes=[pltpu.VMEM((2, tm, d // 2), jnp.uint32), pltpu.VMEM((2, tm, d // 2), jnp.uint32),
                            pltpu.SemaphoreType.DMA((2,))],
        ),
        out_shape=jax.ShapeDtypeStruct((t, d), F32),
        compiler_params=pltpu.CompilerParams(dimension_semantics=("arbitrary",),
                                             vmem_limit_bytes=MOE_VMEM_LIMIT),
        name="final_block",
    )(pos1, pos2, ys, h1, info, p, g.reshape(1, d), b.reshape(1, d), ple_w, ple_g.reshape(1, d), gate_w)


def _dispatch_plan(pos, ends_f, n_tok, tm, n_rows):
    n_tiles = n_rows // tm
    ends = ends_f[0, :N_EXPERTS].astype(jnp.int32)
    src_tok = jnp.zeros((n_rows,), jnp.int32).at[pos].set(
        jnp.arange(2 * n_tok, dtype=jnp.int32) % n_tok, unique_indices=True)
    n_active = ends[-1] // tm
    tile_ids = jnp.arange(n_tiles, dtype=jnp.int32)
    tile_expert = jnp.searchsorted(ends, jnp.minimum(tile_ids, n_active - 1) * tm, side="right")
    tile_expert = jnp.minimum(tile_expert, N_EXPERTS - 1).astype(jnp.int32)
    first = jnp.concatenate([jnp.ones((1,), jnp.int32),
                             (tile_expert[1:] != tile_expert[:-1]).astype(jnp.int32)])
    parity = (jnp.cumsum(first) - 1) % 2
    ids = jnp.arange(N_EXPERTS, dtype=jnp.int32)
    present = ends > jnp.concatenate([jnp.zeros((1,), jnp.int32), ends[:-1]])
    later = present[None, :] & (ids[None, :] > ids[:, None])
    next_of = jnp.min(jnp.where(later, ids[None, :], N_EXPERTS), axis=1)
    next_of = jnp.where(next_of == N_EXPERTS, -1, next_of)
    plan = (tile_expert, first, parity.astype(jnp.int32), next_of[tile_expert].astype(jnp.int32),
            n_active.astype(jnp.int32).reshape(1))
    return src_tok, plan


def _t5_bucket(rel):
    n = jnp.maximum(rel, 0)
    max_exact = REL_BUCKETS // 2
    nf = jnp.maximum(n, 1).astype(F32)
    large = max_exact + (jnp.log(nf / max_exact) / math.log(REL_MAX_DIST / max_exact)
                         * (REL_BUCKETS - max_exact)).astype(jnp.int32)
    large = jnp.minimum(large, REL_BUCKETS - 1)
    return jnp.where(n < max_exact, n, large)


def _bias_tables(rel_bias, tq):
    n = 3 * tq
    rel = 2 * tq - 1 - jnp.arange(n, dtype=jnp.int32)
    g = rel_bias[_t5_bucket(rel)].astype(F32).T
    skew = jnp.tile(g, (1, tq))[:, :tq * (n - 1)].reshape(-1, tq, n - 1)
    tiles = jnp.stack([skew[:, :, 2 * tq - 1:3 * tq - 1], skew[:, :, tq - 1:2 * tq - 1]], axis=1)
    far = rel_bias[_t5_bucket(jnp.full((), 2 * tq, jnp.int32))].astype(F32)
    return tiles * LOG2_E, far * LOG2_E


def kernel(x, p, ln_in_g, ln_in_b, rel_bias, w_in, diff_lam_q1, diff_lam_k1, diff_lam_q2, diff_lam_k2, diff_subln_g, rwkv_mu, rwkv_w0, rwkv_w2, rwkv_a0, rwkv_a2, rwkv_g2, rwkv_k_k, rwkv_k_a, rwkv_r_k, rwkv_lnx_g, rwkv_lnx_b, w_out, ln1_g, ln1_b, router_group_w, router_group_b, router_expert_w, router_expert_b, moe_w_gate, moe_w_up, moe_w_down, ln2_g, ln2_b, ple_w, ple_norm_g, ple_gate_w):
    bsz, seq, d = x.shape
    t = bsz * seq
    i = 0
    lambda_init = 0.8 - 0.6 * math.exp(-0.3 * i)
    tq = 256

    x2 = x.reshape(t, d)
    w_in_t = jnp.swapaxes(w_in[i], 0, 1)
    h, hb = layer_norm_rows(x2, ln_in_g, ln_in_b, 512)
    qkv = matmul_nt(hb, w_in_t, 0, O_RWKV, O_RWKV, BF16, 1024, 512)
    pr = matmul_nt(hb, w_in_t, O_RWKV, RWKV_PROJ, RWKV_PROJ_PAD, F32, 1024, 512)

    lam = (jnp.exp(jnp.sum(diff_lam_q1[i].astype(F32) * diff_lam_k1[i]))
           - jnp.exp(jnp.sum(diff_lam_q2[i].astype(F32) * diff_lam_k2[i])) + lambda_init)
    bias_tiles, far_bias = _bias_tables(rel_bias, tq)
    y_diff = diff_attention(qkv.reshape(bsz, seq, O_RWKV), bias_tiles, far_bias, lam.reshape(1),
                            diff_subln_g[i], lambda_init, tq, 8)

    mu_p = jnp.pad(rwkv_mu[i], (0, RWKV_PROJ_PAD - RWKV_PROJ))
    zeros64 = jnp.zeros((LORA_DECAY, RWKV_WIDTH), F32)
    w2p = jnp.concatenate([rwkv_w2[i], zeros64], axis=0).astype(BF16)
    a2p = jnp.concatenate([zeros64, rwkv_a2[i]], axis=0).astype(BF16)
    g2p = jnp.pad(rwkv_g2[i], ((0, 2 * LANES - LORA_GATE), (0, 0))).astype(BF16)
    r, k2, v, lw, kn, bb, g, bonus = rwkv_prep(
        pr, mu_p, rwkv_w0[i], w2p, rwkv_a0[i], a2p, g2p, rwkv_k_k[i], rwkv_k_a[i],
        rwkv_r_k[i].reshape(-1), seq, 256)
    sh = lambda a: a.reshape(bsz, seq, RWKV_WIDTH)
    y_scan = rwkv_scan(sh(r), sh(k2), sh(v), sh(lw), sh(kn), sh(bb), 64, 8)

    w_out_b = w_out[i].astype(BF16)
    rw = jnp.concatenate([router_group_w[i], router_expert_w[i]], axis=1)
    rw = jnp.pad(rw, ((0, 0), (0, ROUTER_LANES - rw.shape[1])))
    rw_hi = rw.astype(BF16)
    rw_lo = (rw - rw_hi.astype(F32)).astype(BF16)
    rb = jnp.pad(jnp.concatenate([router_group_b[i], router_expert_b[i]]),
                 (0, ROUTER_LANES - N_GROUPS - N_EXPERTS)).reshape(1, ROUTER_LANES)
    h1, info = outproj_ln_route(y_diff.reshape(t, DIFF_WIDTH), y_scan.reshape(t, RWKV_WIDTH), bonus, g,
                                rwkv_lnx_g[i], rwkv_lnx_b[i], h, w_out_b[:DIFF_WIDTH],
                                w_out_b[DIFF_WIDTH:], ln1_g[i], ln1_b[i], rw_hi, rw_lo, rb, 512)

    tm_e = MOE_ROW_TILE
    n_rows = 2 * t + N_EXPERTS * tm_e
    e_all = jnp.concatenate([info[:, 0], info[:, 1]]).astype(jnp.int32)
    pos, ends = moe_rank(e_all.reshape(-1, 1), 1024, tm_e)
    pos = pos[:, 0]
    src_tok, plan = _dispatch_plan(pos, ends, t, tm_e, n_rows)
    f = moe_w_gate.shape[-1]
    ys = moe_experts(src_tok, plan, h1.reshape(t, d // LANES, LANES), moe_w_gate[i].reshape(N_EXPERTS, d, f),
                     moe_w_up[i].reshape(N_EXPERTS, d, f), moe_w_down[i].reshape(N_EXPERTS, f, d),
                     n_rows, tm_e)

    out = final_block(pos[:t], pos[t:], ys, h1, info, p[i].reshape(t, -1), ln2_g[i], ln2_b[i],
                      ple_w[i].astype(BF16), ple_norm_g[i], ple_gate_w[i].astype(BF16), 256)
    return out.reshape(bsz, seq, d)
```

```python
import functools
import math

import jax
import jax.numpy as jnp
from jax import lax
from jax.experimental import pallas as pl
from jax.experimental.pallas import tpu as pltpu

F32 = jnp.float32
BF16 = jnp.bfloat16

DIFF_HEADS = 8
DIFF_HEAD_DIM = 64
DIFF_V_DIM = 128
DIFF_WIDTH = DIFF_HEADS * DIFF_V_DIM
DIFF_QK_WIDTH = DIFF_HEADS * 2 * DIFF_HEAD_DIM
RWKV_HEADS = 16
RWKV_HEAD_SIZE = 64
RWKV_WIDTH = RWKV_HEADS * RWKV_HEAD_SIZE
LORA_DECAY = 64
LORA_AAA = 64
LORA_GATE = 160
RWKV_PROJ = 3 * RWKV_WIDTH + LORA_DECAY + LORA_AAA + LORA_GATE
O_RWKV = 2 * DIFF_QK_WIDTH + DIFF_WIDTH
REL_BUCKETS = 32
REL_MAX_DIST = 128
N_GROUPS = 4
EXPERTS_PER_GROUP = 8
N_EXPERTS = N_GROUPS * EXPERTS_PER_GROUP
LN_EPS = 1e-5
RWKV_GN_EPS = 64e-5
NEG_INF = -1e30
DEPTH = 1
DEEPNORM_ALPHA = (2 * DEPTH) ** 0.25
LOG2_E = math.log2(math.e)

LANES = 128
RWKV_PROJ_PAD = 3584
ROUTER_LANES = LANES
VMEM_LIMIT = 48 * 1024 * 1024
MOE_VMEM_LIMIT = 56 * 1024 * 1024
MOE_ROW_TILE = 256


def _cparams(sem):
    return pltpu.CompilerParams(dimension_semantics=sem, vmem_limit_bytes=VMEM_LIMIT)


def _bdot(a, b):
    return jnp.dot(a.astype(BF16), b.astype(BF16), preferred_element_type=F32)


def _bdot_nt(a, b):
    return lax.dot_general(a.astype(BF16), b.astype(BF16), (((1,), (1,)), ((), ())),
                           preferred_element_type=F32)


def _bdot_tn(a, b):
    return lax.dot_general(a.astype(BF16), b.astype(BF16), (((0,), (0,)), ((), ())),
                           preferred_element_type=F32)


def _split3(x):
    hi = x.astype(BF16)
    r1 = x - hi.astype(F32)
    mid = r1.astype(BF16)
    lo = (r1 - mid.astype(F32)).astype(BF16)
    return hi, mid, lo


def _head_sum(x):
    lane = lax.broadcasted_iota(jnp.int32, (x.shape[0], LANES), 1)
    first = lane < RWKV_HEAD_SIZE
    cols = []
    for c in range(x.shape[1] // LANES):
        blk = x[:, c * LANES:(c + 1) * LANES]
        lo = jnp.sum(jnp.where(first, blk, 0.0), axis=1, keepdims=True)
        hi = jnp.sum(jnp.where(first, 0.0, blk), axis=1, keepdims=True)
        cols.append(jnp.where(first, lo, hi))
    return jnp.concatenate(cols, axis=1)


def _dot_sel_x(sel, x):
    hi, mid, lo = _split3(x)
    d = lambda p: jnp.dot(sel, p, preferred_element_type=F32)
    return d(hi) + d(mid) + d(lo)


def _layer_norm(x, g, b):
    mu = jnp.mean(x, axis=-1, keepdims=True)
    xc = x - mu
    var = jnp.mean(xc * xc, axis=-1, keepdims=True)
    return xc * lax.rsqrt(var + LN_EPS) * g + b


def _sigmoid(z):
    return 1.0 / (1.0 + jnp.exp(-z))


def _ln_kernel(x_ref, g_ref, b_ref, h_ref, hb_ref):
    xn = _layer_norm(x_ref[...], g_ref[...], b_ref[...])
    h_ref[...] = xn
    hb_ref[...] = xn.astype(BF16)


def layer_norm_rows(x, g, b, tm):
    m, d = x.shape
    tile = pl.BlockSpec((tm, d), lambda i: (i, 0))
    row = pl.BlockSpec((1, d), lambda i: (0, 0))
    return pl.pallas_call(
        _ln_kernel,
        grid=(m // tm,),
        in_specs=[tile, row, row],
        out_specs=[tile, tile],
        out_shape=[jax.ShapeDtypeStruct((m, d), F32), jax.ShapeDtypeStruct((m, d), BF16)],
        compiler_params=_cparams(("parallel",)),
        name="layer_norm_rows",
    )(x, g.reshape(1, d), b.reshape(1, d))


def _mm_nt_kernel(x_ref, wt_ref, o_ref, *, n_valid, tn):
    y = lax.dot_general(x_ref[...], wt_ref[...].astype(BF16), (((1,), (1,)), ((), ())),
                        preferred_element_type=F32)
    if n_valid % tn:
        col = pl.program_id(1) * tn + lax.broadcasted_iota(jnp.int32, y.shape, 1)
        y = jnp.where(col < n_valid, y, 0.0)
    o_ref[...] = y.astype(o_ref.dtype)


def matmul_nt(x, wt, row0, n_valid, n_out, out_dtype, tm, tn):
    m, k = x.shape
    blk0 = row0 // tn
    return pl.pallas_call(
        functools.partial(_mm_nt_kernel, n_valid=n_valid, tn=tn),
        grid=(m // tm, n_out // tn),
        in_specs=[pl.BlockSpec((tm, k), lambda i, j: (i, 0)),
                  pl.BlockSpec((tn, k), lambda i, j: (blk0 + j, 0))],
        out_specs=pl.BlockSpec((tm, tn), lambda i, j: (i, j)),
        out_shape=jax.ShapeDtypeStruct((m, n_out), out_dtype),
        compiler_params=_cparams(("parallel", "parallel")),
        name="matmul_nt",
    )(x, wt)


def _attn_kernel(lam_ref, far_ref, q_ref, k_ref, v_ref, bias_ref, g_ref, o_ref,
                 m_scr, l_scr, acc_scr, *, tq, hps, scale, post_scale):
    hp = pl.program_id(1)
    qi = pl.program_id(2)
    lane = lax.broadcasted_iota(jnp.int32, (tq, LANES), 1)
    q_maps = []
    for hh in range(hps):
        qs = (q_ref[0, :, hh * LANES:(hh + 1) * LANES].astype(F32) * scale).astype(BF16)
        zero = jnp.zeros_like(qs)
        q_maps += [jnp.where(lane < DIFF_HEAD_DIM, qs, zero), jnp.where(lane >= DIFF_HEAD_DIM, qs, zero)]

    m_scr[...] = jnp.full(m_scr.shape, NEG_INF, F32)
    l_scr[...] = jnp.zeros(l_scr.shape, F32)
    acc_scr[...] = jnp.zeros(acc_scr.shape, F32)

    def step(kstart, bias_index, causal):
        for hh in range(hps):
            kb = k_ref[0, pl.ds(kstart, tq), hh * LANES:(hh + 1) * LANES]
            vb = v_ref[0, pl.ds(kstart, tq), hh * LANES:(hh + 1) * LANES]
            for c in range(2 * hh, 2 * hh + 2):
                s = lax.dot_general(q_maps[c], kb, (((1,), (1,)), ((), ())), preferred_element_type=F32)
                m_old = m_scr[c]
                if bias_index is None:
                    far = far_ref[hp * hps + hh]
                    m_new = jnp.maximum(m_old, jnp.max(s, axis=1, keepdims=True) + far)
                    shift = m_new - far
                else:
                    s = s + bias_ref[hh, bias_index]
                    if causal:
                        ri = lax.broadcasted_iota(jnp.int32, (tq, tq), 0)
                        ci = lax.broadcasted_iota(jnp.int32, (tq, tq), 1)
                        s = jnp.where(ri >= ci, s, NEG_INF)
                    m_new = jnp.maximum(m_old, jnp.max(s, axis=1, keepdims=True))
                    shift = m_new
                p = jnp.exp2(s - jnp.concatenate([shift] * (tq // LANES), axis=1))
                corr = jnp.exp2(m_old - m_new)
                l_scr[c] = corr * l_scr[c] + jnp.sum(p, axis=1, keepdims=True)
                acc_scr[c] = corr * acc_scr[c] + jnp.dot(p.astype(BF16), vb, preferred_element_type=F32)
                m_scr[c] = m_new

    def far_body(kb, carry):
        step(pl.multiple_of(kb * tq, tq), None, False)
        return carry

    lax.fori_loop(0, jnp.maximum(qi - 1, 0), far_body, 0)

    @pl.when(qi >= 1)
    def _():
        step(pl.multiple_of((qi - 1) * tq, tq), 1, False)

    step(pl.multiple_of(qi * tq, tq), 0, True)

    for hh in range(hps):
        c = 2 * hh
        o = acc_scr[c] / l_scr[c] - lam_ref[0] * (acc_scr[c + 1] / l_scr[c + 1])
        ms = jnp.mean(o * o, axis=1, keepdims=True)
        o_ref[0, :, hh * LANES:(hh + 1) * LANES] = (
            o * lax.rsqrt(ms + LN_EPS) * g_ref[...] * post_scale).astype(o_ref.dtype)


def diff_attention(qkv, bias_tiles, far_bias, lam, subln_g, lambda_init, tq, hps):
    bsz, s, _ = qkv.shape
    kern = functools.partial(_attn_kernel, tq=tq, hps=hps, scale=DIFF_HEAD_DIM ** -0.5 * LOG2_E,
                             post_scale=1.0 - lambda_init)
    hw = hps * LANES
    nqk = DIFF_QK_WIDTH // hw
    return pl.pallas_call(
        kern,
        grid=(bsz, DIFF_HEADS // hps, s // tq),
        in_specs=[
            pl.BlockSpec(memory_space=pltpu.SMEM),
            pl.BlockSpec(memory_space=pltpu.SMEM),
            pl.BlockSpec((1, tq, hw), lambda b, h, i: (b, i, h)),
            pl.BlockSpec((1, s, hw), lambda b, h, i: (b, 0, nqk + h)),
            pl.BlockSpec((1, s, hw), lambda b, h, i: (b, 0, 2 * nqk + h)),
            pl.BlockSpec((hps, 2, tq, tq), lambda b, h, i: (h, 0, 0, 0)),
            pl.BlockSpec((1, LANES), lambda b, h, i: (0, 0)),
        ],
        out_specs=pl.BlockSpec((1, tq, hw), lambda b, h, i: (b, i, h)),
        out_shape=jax.ShapeDtypeStruct((bsz, s, DIFF_WIDTH), BF16),
        scratch_shapes=[pltpu.VMEM((2 * hps, tq, LANES), F32), pltpu.VMEM((2 * hps, tq, LANES), F32),
                        pltpu.VMEM((2 * hps, tq, LANES), F32)],
        compiler_params=_cparams(("parallel", "parallel", "arbitrary")),
        name="diff_attention",
    )(lam, far_bias, qkv, qkv, qkv, bias_tiles, subln_g.reshape(1, LANES))


def _rwkv_prep_kernel(x_ref, prev_ref, mu_ref, w0_ref, w2_ref, a0_ref, a2_ref, g2_ref,
                      kk_ref, ka_ref, rk_ref,
                      r_out, k_out, v_out, lw_out, kn_out, b_out, g_out, bonus_out, *, tm, seq):
    i = pl.program_id(0)
    x = x_ref[...]
    w = RWKV_WIDTH
    row = lax.broadcasted_iota(jnp.int32, x.shape, 0)
    is_start = (i * tm) % seq == 0
    last_prev = jnp.where(is_start, 0.0, prev_ref[7:8, :])
    prev = jnp.where(row == 0, last_prev, pltpu.roll(x, 1, 0))
    xs = x + (prev - x) * mu_ref[...]

    r = xs[:, 0:w]
    k = xs[:, w:2 * w]
    v = xs[:, 2 * w:3 * w]
    xwa = xs[:, 3 * w:3 * w + 2 * LORA_DECAY]
    xg = xs[:, 3 * w + LANES:3 * w + LANES + 2 * LANES]

    z = w0_ref[...] + _bdot(jnp.tanh(xwa), w2_ref[...])
    softplus_neg_z = jnp.maximum(-z, 0.0) + jnp.log(1.0 + jnp.exp(-jnp.abs(z)))
    lw_out[...] = -jnp.exp(-softplus_neg_z - 0.5)
    a_lr = _sigmoid(a0_ref[...] + _bdot(xwa, a2_ref[...]))
    g_out[...] = _bdot(_sigmoid(xg), g2_ref[...])

    kk = k * kk_ref[...]
    kn = kk / jnp.maximum(jnp.sqrt(_head_sum(kk * kk)), 1e-12)
    k2 = k * (1.0 + (a_lr - 1.0) * ka_ref[...])
    r_out[...] = r
    k_out[...] = k2
    v_out[...] = v
    kn_out[...] = kn
    b_out[...] = kn * a_lr
    bonus_out[...] = _head_sum(r * k2 * rk_ref[...]) * v


def rwkv_prep(pr, mu, w0, w2p, a0, a2p, g2p, k_k, k_a, r_k, seq, tm):
    t, wp = pr.shape
    w = RWKV_WIDTH
    row = lambda a: a.reshape(1, -1)
    full = lambda shp: pl.BlockSpec(shp, lambda i: (0, 0))
    outs = [jax.ShapeDtypeStruct((t, w), F32)] * 8
    kern = functools.partial(_rwkv_prep_kernel, tm=tm, seq=seq)
    return pl.pallas_call(
        kern,
        grid=(t // tm,),
        in_specs=[
            pl.BlockSpec((tm, wp), lambda i: (i, 0)),
            pl.BlockSpec((8, wp), lambda i: (jnp.maximum(i * (tm // 8) - 1, 0), 0)),
            full((1, wp)), full((1, w)), full((LANES, w)), full((1, w)), full((LANES, w)),
            full((2 * LANES, w)), full((1, w)), full((1, w)), full((1, w)),
        ],
        out_specs=[pl.BlockSpec((tm, w), lambda i: (i, 0))] * 8,
        out_shape=outs,
        compiler_params=_cparams(("parallel",)),
        name="rwkv_prep",
    )(pr, pr, row(mu), row(w0), w2p, row(a0), a2p, g2p, row(k_k), row(k_a), row(r_k))


def _rwkv_scan_kernel(r_ref, k_ref, v_ref, lw_ref, kn_ref, b_ref, y_ref, s_scr, *, chunk, pairs):
    c2 = 2 * chunk

    @pl.when(pl.program_id(2) == 0)
    def _():
        s_scr[...] = jnp.zeros(s_scr.shape, F32)

    ri = lax.broadcasted_iota(jnp.int32, (chunk, chunk), 0)
    ci = lax.broadcasted_iota(jnp.int32, (chunk, chunk), 1)
    ltri = jnp.where(ri >= ci, 1.0, 0.0).astype(BF16)

    lw = lw_ref[0]
    cl = _dot_sel_x(ltri, lw)
    mid = chunk // 2 - 1
    clm = cl[mid:mid + 1, :]
    cle = cl[chunk - 1:chunk, :]
    w_mid = jnp.exp(cl - clm)
    w_mid_prev = jnp.exp(cl - lw - clm)
    w_mid_inv = jnp.exp(clm - cl)
    w_abs = jnp.exp(cl)
    w_abs_prev = jnp.exp(cl - lw)
    w_end = jnp.exp(cle - cl)
    w_tot = jnp.exp(cle)

    r = r_ref[0]
    k = k_ref[0]
    v = v_ref[0]
    a = -kn_ref[0]
    b = b_ref[0]
    a_mid = a * w_mid_prev
    r_mid = r * w_mid
    b_mid = b * w_mid_inv
    k_mid = k * w_mid_inv
    a_abs = a * w_abs_prev
    r_abs = r * w_abs
    b_end = b * w_end
    k_end = k * w_end

    lane = lax.broadcasted_iota(jnp.int32, (chunk, LANES), 1)
    first = lane < RWKV_HEAD_SIZE

    def stack(x):
        return jnp.concatenate([jnp.where(first, x, 0.0), jnp.where(first, 0.0, x)], axis=0)

    rr = lax.broadcasted_iota(jnp.int32, (c2, c2), 0)
    cc = lax.broadcasted_iota(jnp.int32, (c2, c2), 1)
    same = (rr >= chunk) == (cc >= chunk)
    strict = same & (rr > cc)
    incl = same & (rr >= cc)
    eye = jnp.where(rr == cc, 1.0, 0.0)

    ps = range(pairs)
    sls = [slice(p * LANES, (p + 1) * LANES) for p in ps]
    bf = lambda x: x.astype(BF16)
    cat0 = lambda xs: jnp.concatenate(xs, axis=0)
    cat1 = lambda xs: jnp.concatenate(xs, axis=1)
    v_s = [bf(stack(v[:, sl])) for sl in sls]
    ar_m = [bf(cat0([stack(a_mid[:, sl]), stack(r_mid[:, sl])])) for sl in sls]
    bk_m = [bf(cat0([stack(b_mid[:, sl]), stack(k_mid[:, sl])])) for sl in sls]
    g_all = [_bdot_nt(ar_m[p], bk_m[p]) for p in ps]
    l_ab = [jnp.where(strict, g[:c2, :c2], 0.0) for g in g_all]
    l_ak = [bf(jnp.where(strict, g[:c2, c2:], 0.0)) for g in g_all]
    m_r = [bf(cat1([jnp.where(incl, g[c2:, :c2], 0.0), jnp.where(incl, g[c2:, c2:], 0.0)])) for g in g_all]

    t_inv = [eye + l_ab[p] for p in ps]
    n_fac = max(chunk.bit_length() - 2, 0)
    lp = [_bdot(l_ab[p], l_ab[p]) for p in ps] if n_fac else l_ab
    for _ in range(n_fac - 1):
        x = [_bdot(cat0([t_inv[p], lp[p]]), lp[p]) for p in ps]
        t_inv = [t_inv[p] + x[p][:c2] for p in ps]
        lp = [x[p][c2:] for p in ps]
    if n_fac:
        t_inv = [t_inv[p] + _bdot(t_inv[p], lp[p]) for p in ps]
    t_inv = [bf(t) for t in t_inv]

    lv = [_bdot(l_ak[p], v_s[p]) for p in ps]
    av = [bf(_bdot(t_inv[p], cat1([stack(a_abs[:, sls[p]]), lv[p]]))) for p in ps]
    zero = jnp.zeros((c2, LANES), BF16)
    ry = [_bdot(m_r[p], cat0([av[p], cat1([zero, v_s[p]])])) for p in ps]
    r_p = [stack(r_abs[:, sls[p]]) + ry[p][:, :LANES] for p in ps]

    s_old = [s_scr[p] for p in ps]
    y_s = [_bdot_nt(r_p[p], s_old[p]) + ry[p][:, LANES:] for p in ps]
    for p in ps:
        y_ref[0, :, sls[p]] = y_s[p][:chunk] + y_s[p][chunk:]
    b_e = [bf(stack(b_end[:, sl])) for sl in sls]
    k_e = [bf(stack(k_end[:, sl])) for sl in sls]
    ab = [_bdot_tn(av[p][:, :LANES], b_e[p]) for p in ps]
    q_t = [_bdot_tn(cat0([av[p][:, LANES:], v_s[p]]), cat0([b_e[p], k_e[p]])) for p in ps]
    for p in ps:
        s_scr[p] = s_old[p] * w_tot[:, sls[p]] + _bdot(s_old[p], ab[p]) + q_t[p]


def rwkv_scan(r, k, v, lw, kn, b, chunk, pairs):
    bsz, s, w = r.shape
    pw = pairs * LANES
    spec = pl.BlockSpec((1, chunk, pw), lambda bi, hi, ci: (bi, ci, hi))
    kern = functools.partial(_rwkv_scan_kernel, chunk=chunk, pairs=pairs)
    return pl.pallas_call(
        kern,
        grid=(bsz, w // pw, s // chunk),
        in_specs=[spec] * 6,
        out_specs=spec,
        out_shape=jax.ShapeDtypeStruct((bsz, s, w), F32),
        scratch_shapes=[pltpu.VMEM((pairs, LANES, LANES), F32)],
        compiler_params=_cparams(("parallel", "parallel", "arbitrary")),
        name="rwkv_scan",
    )(r, k, v, lw, kn, b)


def _rwkv_out(y, bonus, g, lnx_g, lnx_b):
    inv_n = 1.0 / RWKV_HEAD_SIZE
    mu = _head_sum(y) * inv_n
    yc = y - mu
    var = _head_sum(yc * yc) * inv_n
    yn = yc * lax.rsqrt(var + RWKV_GN_EPS) * lnx_g + lnx_b
    return (yn + bonus) * g


def _route(logits):
    lane = lax.broadcasted_iota(jnp.int32, logits.shape, 1)
    big = jnp.int32(4 * LANES)
    gmask = lane < N_GROUPS
    gl = jnp.where(gmask, logits, NEG_INF)
    gmax = jnp.max(gl, axis=1, keepdims=True)
    gidx = jnp.min(jnp.where(gl == gmax, lane, big), axis=1, keepdims=True)
    gtop = 1.0 / jnp.sum(jnp.where(gmask, jnp.exp(gl - gmax), 0.0), axis=1, keepdims=True)
    lo = N_GROUPS + EXPERTS_PER_GROUP * gidx
    emask = (lane >= lo) & (lane < lo + EXPERTS_PER_GROUP)
    el = jnp.where(emask, logits, NEG_INF)
    e1 = jnp.max(el, axis=1, keepdims=True)
    i1 = jnp.min(jnp.where(el == e1, lane, big), axis=1, keepdims=True)
    el2 = jnp.where(lane == i1, NEG_INF, el)
    e2 = jnp.max(el2, axis=1, keepdims=True)
    i2 = jnp.min(jnp.where(el2 == e2, lane, big), axis=1, keepdims=True)
    t = jnp.exp(e2 - e1)
    w1 = gtop / (1.0 + t)
    w2 = gtop * t / (1.0 + t)
    id1 = (i1 - N_GROUPS).astype(F32)
    id2 = (i2 - N_GROUPS).astype(F32)
    return jnp.where(lane == 0, id1, jnp.where(lane == 1, id2, jnp.where(lane == 2, w1,
                                                                         jnp.where(lane == 3, w2, 0.0))))


def _outproj_kernel(yd_ref, ys_ref, bonus_ref, gate_ref, lg_ref, lb_ref, h_ref,
                    w1_ref, w2_ref, g_ref, b_ref, rw_hi_ref, rw_lo_ref, rb_ref, h1_ref, info_ref, *, parts):
    rows = h_ref.shape[0] // parts
    sls = [pl.ds(k * rows, rows) for k in range(parts)]
    d = lambda a, b: jnp.dot(a, b, preferred_element_type=F32)
    yr = [_rwkv_out(ys_ref[sl, :], bonus_ref[sl, :], gate_ref[sl, :], lg_ref[...], lb_ref[...]) for sl in sls]
    mix = [d(yd_ref[sl, :], w1_ref[...]) + d(yr[k].astype(BF16), w2_ref[...]) for k, sl in enumerate(sls)]
    h1 = [_layer_norm(DEEPNORM_ALPHA * h_ref[sl, :] + mix[k], g_ref[...], b_ref[...]) for k, sl in enumerate(sls)]
    for k, sl in enumerate(sls):
        h1_ref[sl, :] = h1[k]
        hi = h1[k].astype(BF16)
        lo = (h1[k] - hi.astype(F32)).astype(BF16)
        logits = d(hi, rw_hi_ref[...]) + d(lo, rw_hi_ref[...]) + d(hi, rw_lo_ref[...]) + rb_ref[...]
        info_ref[sl, :] = _route(logits)


def outproj_ln_route(yd, ys, bonus, gate, lnx_g, lnx_b, h, w_out_a, w_out_b, g, b, rw_hi, rw_lo, rb, tm):
    t, d = h.shape
    ka = yd.shape[1]
    kb = ys.shape[1]
    full = lambda shp: pl.BlockSpec(shp, lambda i: (0, 0))
    tile = lambda n: pl.BlockSpec((tm, n), lambda i: (i, 0))
    return pl.pallas_call(
        functools.partial(_outproj_kernel, parts=2),
        grid=(t // tm,),
        in_specs=[
            tile(ka), tile(kb), tile(kb), tile(kb), full((1, kb)), full((1, kb)), tile(d),
            full((ka, d)), full((kb, d)), full((1, d)), full((1, d)),
            full((d, ROUTER_LANES)), full((d, ROUTER_LANES)), full((1, ROUTER_LANES)),
        ],
        out_specs=[
            pl.BlockSpec((tm, d), lambda i: (i, 0)),
            pl.BlockSpec((tm, ROUTER_LANES), lambda i: (i, 0)),
        ],
        out_shape=[jax.ShapeDtypeStruct((t, d), F32), jax.ShapeDtypeStruct((t, ROUTER_LANES), F32)],
        compiler_params=pltpu.CompilerParams(dimension_semantics=("parallel",),
                                             vmem_limit_bytes=MOE_VMEM_LIMIT),
        name="outproj_ln_route",
    )(yd, ys, bonus, gate, lnx_g.reshape(1, kb), lnx_b.reshape(1, kb), h, w_out_a, w_out_b,
      g.reshape(1, d), b.reshape(1, d), rw_hi, rw_lo, rb)


def _moe_rank_kernel(e_ref, pos_ref, ends_ref, run_scr, start_scr, *, tm, row_tile):
    ph = pl.program_id(0)
    i = pl.program_id(1)
    lane = lax.broadcasted_iota(jnp.int32, (tm, LANES), 1)
    onehot = e_ref[...] == lane
    oh = jnp.where(onehot, 1.0, 0.0)

    @pl.when((ph == 0) & (i == 0))
    def _():
        run_scr[...] = jnp.zeros(run_scr.shape, F32)

    @pl.when(ph == 0)
    def _():
        run_scr[...] += jnp.sum(oh, axis=0, keepdims=True)

    @pl.when((ph == 1) & (i == 0))
    def _():
        padded = jnp.floor((run_scr[...] + (row_tile - 1)) * (1.0 / row_tile)) * row_tile
        ri = lax.broadcasted_iota(jnp.int32, (LANES, LANES), 0)
        ci = lax.broadcasted_iota(jnp.int32, (LANES, LANES), 1)
        upper = jnp.where(ri < ci, 1.0, 0.0).astype(BF16)
        parts = _split3(jnp.broadcast_to(padded, (8, LANES)))
        starts = sum(jnp.dot(q, upper, preferred_element_type=F32) for q in parts)[0:1]
        start_scr[...] = starts
        ends_ref[...] = starts + padded
        run_scr[...] = jnp.zeros(run_scr.shape, F32)

    @pl.when(ph == 1)
    def _():
        ri = lax.broadcasted_iota(jnp.int32, (tm, tm), 0)
        ci = lax.broadcasted_iota(jnp.int32, (tm, tm), 1)
        before = jnp.where(ri > ci, 1.0, 0.0).astype(BF16)
        row = jnp.dot(before, oh.astype(BF16), preferred_element_type=F32) + (run_scr[...] + start_scr[...])
        pos_ref[...] = jnp.sum(jnp.where(onehot, row, 0.0), axis=1, keepdims=True).astype(jnp.int32)
        run_scr[...] += jnp.sum(oh, axis=0, keepdims=True)


def moe_rank(e_col, tm, row_tile):
    n = e_col.shape[0]
    return pl.pallas_call(
        functools.partial(_moe_rank_kernel, tm=tm, row_tile=row_tile),
        grid=(2, n // tm),
        in_specs=[pl.BlockSpec((tm, 1), lambda ph, i: (i, 0))],
        out_specs=[pl.BlockSpec((tm, 1), lambda ph, i: (i * ph, 0)),
                   pl.BlockSpec((1, LANES), lambda ph, i: (0, 0))],
        out_shape=[jax.ShapeDtypeStruct((n, 1), jnp.int32), jax.ShapeDtypeStruct((1, LANES), F32)],
        scratch_shapes=[pltpu.VMEM((1, LANES), F32), pltpu.VMEM((1, LANES), F32)],
        compiler_params=_cparams(("arbitrary", "arbitrary")),
        name="moe_rank",
    )(e_col)


def _row_copy(src_hbm, row, dst, r, sem):
    return pltpu.make_async_copy(src_hbm.at[pl.ds(row, 1)], dst.at[pl.ds(r, 1)], sem)


def _pack_bf16_halves(y):
    n = y.shape[1] // 2
    hi = pltpu.bitcast(y[:, :n].astype(BF16).astype(F32), jnp.uint32)
    lo = pltpu.bitcast(y[:, n:].astype(BF16).astype(F32), jnp.uint32)
    return hi | (lo >> 16)


def _unpack_bf16_halves(w):
    hi = pltpu.bitcast(w & jnp.uint32(0xFFFF0000), F32)
    lo = pltpu.bitcast(w << 16, F32)
    return jnp.concatenate([hi, lo], axis=1)


def _moe_expert_kernel(te_ref, first_ref, par_ref, nxt_ref, nact_ref,
                       src0_ref, src_next_ref, x_hbm, wg_hbm, wu_hbm, wd_hbm, o_ref,
                       xbuf0, xbuf1, wgf, wuf, wdf, wgb, wub, wdb, sem, wsem, *, tm):
    j = pl.program_id(0)
    n_act = nact_ref[0]
    bufs = (xbuf0, xbuf1)

    def token_copy(tok, into, r):
        return pltpu.make_async_copy(x_hbm.at[tok], bufs[into].at[:, r, :], sem.at[into])

    def start_gather(src, into):
        for r in range(tm):
            token_copy(src[0, 0, r], into, r).start(priority=r % 2)

    def wait_gather(into):
        for r in range(tm):
            token_copy(0, into, r).wait()

    def weight_copies(e, par):
        return (pltpu.make_async_copy(wg_hbm.at[e], wgf.at[par], wsem.at[par]),
                pltpu.make_async_copy(wu_hbm.at[e], wuf.at[par], wsem.at[par]),
                pltpu.make_async_copy(wd_hbm.at[e], wdf.at[par], wsem.at[par]))

    @pl.when(j == 0)
    def _():
        start_gather(src0_ref, 0)
        for c in weight_copies(te_ref[0], 0):
            c.start(priority=1)

    @pl.when((j < n_act) & (first_ref[j] > 0))
    def _():
        par = par_ref[j]
        for c in weight_copies(te_ref[j], par):
            c.wait()
        wgb[...] = wgf[par].astype(BF16)
        wub[...] = wuf[par].astype(BF16)
        wdb[...] = wdf[par].astype(BF16)

    for slot in range(2):
        @pl.when((j < n_act) & (j % 2 == slot))
        def _():
            wait_gather(slot)
            start_gather(src_next_ref, 1 - slot)
            x = jnp.concatenate([bufs[slot][c] for c in range(bufs[slot].shape[0])], axis=1).astype(BF16)
            hg = jnp.dot(x, wgb[...], preferred_element_type=F32)
            hu = jnp.dot(x, wub[...], preferred_element_type=F32)
            hid = hg * _sigmoid(hg) * hu
            o_ref[...] = _pack_bf16_halves(jnp.dot(hid.astype(BF16), wdb[...], preferred_element_type=F32))

        @pl.when((j == n_act - 1) & (j % 2 == slot))
        def _():
            wait_gather(1 - slot)

    @pl.when((j < n_act) & (first_ref[j] > 0) & (nxt_ref[j] >= 0))
    def _():
        for c in weight_copies(nxt_ref[j], 1 - par_ref[j]):
            c.start(priority=1)

    @pl.when(j >= n_act)
    def _():
        o_ref[...] = jnp.zeros(o_ref.shape, jnp.uint32)


def moe_experts(src_tok, plan, x, wg, wu, wd, n_rows, tm):
    d, f = wg.shape[1:]
    hbm = pl.BlockSpec(memory_space=pl.ANY)
    src3 = src_tok.reshape(n_rows // tm, 1, tm)
    smem_tile = lambda index_map: pl.BlockSpec((1, 1, tm), index_map, memory_space=pltpu.SMEM)
    return pl.pallas_call(
        functools.partial(_moe_expert_kernel, tm=tm),
        grid_spec=pltpu.PrefetchScalarGridSpec(
            num_scalar_prefetch=5,
            grid=(n_rows // tm,),
            in_specs=[smem_tile(lambda j, *_: (0, 0, 0)),
                      smem_tile(lambda j, te, fi, pa, nx, na: (jnp.minimum(j + 1, na[0] - 1), 0, 0)),
                      hbm, hbm, hbm, hbm],
            out_specs=pl.BlockSpec((tm, d // 2), lambda j, *_: (j, 0)),
            scratch_shapes=[pltpu.VMEM((d // LANES, tm, LANES), F32), pltpu.VMEM((d // LANES, tm, LANES), F32),
                            pltpu.VMEM((2, d, f), F32), pltpu.VMEM((2, d, f), F32), pltpu.VMEM((2, f, d), F32),
                            pltpu.VMEM((d, f), BF16), pltpu.VMEM((d, f), BF16), pltpu.VMEM((f, d), BF16),
                            pltpu.SemaphoreType.DMA((2,)), pltpu.SemaphoreType.DMA((2,))],
        ),
        out_shape=jax.ShapeDtypeStruct((n_rows, d // 2), jnp.uint32),
        compiler_params=pltpu.CompilerParams(dimension_semantics=("arbitrary",),
                                             vmem_limit_bytes=MOE_VMEM_LIMIT),
        name="moe_experts",
    )(*plan, src3, src3, x, wg, wu, wd)


def _final_kernel(p1_ref, p2_ref, y_hbm, h1_ref, info_ref, p_ref, g_ref, b_ref, pw_ref, pg_ref, gw_ref,
                  o_ref, ybuf0, ybuf1, sem, *, tm, n_tiles, parts):
    i = pl.program_id(0)
    bufs = (ybuf0, ybuf1)

    def start_gather(tile, into):
        for r in range(tm):
            _row_copy(y_hbm, p1_ref[tile * tm + r], bufs[into].at[0], r, sem.at[into]).start(priority=0)
            _row_copy(y_hbm, p2_ref[tile * tm + r], bufs[into].at[1], r, sem.at[into]).start(priority=1)

    def wait_gather(into):
        for r in range(tm):
            _row_copy(y_hbm, 0, bufs[into].at[0], r, sem.at[into]).wait()
            _row_copy(y_hbm, 0, bufs[into].at[1], r, sem.at[into]).wait()

    @pl.when(i == 0)
    def _():
        start_gather(0, 0)

    for slot in range(2):
        @pl.when(i % 2 == slot)
        def _():
            wait_gather(slot)
            start_gather(jnp.minimum(i + 1, n_tiles - 1), 1 - slot)
            rows = tm // parts
            sls = [pl.ds(k * rows, rows) for k in range(parts)]
            h2, ple = [], []
            for sl in sls:
                info = info_ref[sl, :]
                ffn = (info[:, 2:3] * _unpack_bf16_halves(bufs[slot][0, sl, :])
                       + info[:, 3:4] * _unpack_bf16_halves(bufs[slot][1, sl, :]))
                h2.append(_layer_norm(DEEPNORM_ALPHA * h1_ref[sl, :] + ffn, g_ref[...], b_ref[...]))
                pe = jnp.dot(p_ref[sl, :].astype(BF16), pw_ref[...], preferred_element_type=F32)
                ple.append(pe * lax.rsqrt(jnp.mean(pe * pe, axis=-1, keepdims=True) + LN_EPS) * pg_ref[...])
            gate = [jnp.dot(h.astype(BF16), gw_ref[...], preferred_element_type=F32) for h in h2]
            for k, sl in enumerate(sls):
                o_ref[sl, :] = h2[k] + _sigmoid(gate[k]) * ple[k]

        @pl.when((i == n_tiles - 1) & (i % 2 == slot))
        def _():
            wait_gather(1 - slot)


def final_block(pos1, pos2, ys, h1, info, p, g, b, ple_w, ple_g, gate_w, tm):
    t, d = h1.shape
    dp = p.shape[1]
    tile = lambda n: pl.BlockSpec((tm, n), lambda i, p1, p2: (i, 0))
    full = lambda shp: pl.BlockSpec(shp, lambda i, p1, p2: (0, 0))
    return pl.pallas_call(
        functools.partial(_final_kernel, tm=tm, n_tiles=t // tm, parts=2),
        grid_spec=pltpu.PrefetchScalarGridSpec(
            num_scalar_prefetch=2,
            grid=(t // tm,),
            in_specs=[pl.BlockSpec(memory_space=pl.ANY), tile(d), tile(ROUTER_LANES), tile(dp),
                      full((1, d)), full((1, d)), full((dp, d)), full((1, d)), full((d, d))],
            out_specs=tile(d),
            scratch_shapes=[pltpu.VMEM((2, tm, d // 2), jnp.uint32), pltpu.VMEM((2, tm, d // 2), jnp.uint32),
                            pltpu.SemaphoreType.DMA((2,))],
        ),
        out_shape=jax.ShapeDtypeStruct((t, d), F32),
        compiler_params=pltpu.CompilerParams(dimension_semantics=("arbitrary",),
                                             vmem_limit_bytes=MOE_VMEM_LIMIT),
        name="final_block",
    )(pos1, pos2, ys, h1, info, p, g.reshape(1, d), b.reshape(1, d), ple_w, ple_g.reshape(1, d), gate_w)


def _dispatch_plan(pos, ends_f, n_tok, tm, n_rows):
    n_tiles = n_rows // tm
    ends = ends_f[0, :N_EXPERTS].astype(jnp.int32)
    src_tok = jnp.zeros((n_rows,), jnp.int32).at[pos].set(
        jnp.arange(2 * n_tok, dtype=jnp.int32) % n_tok, unique_indices=True)
    n_active = ends[-1] // tm
    tile_ids = jnp.arange(n_tiles, dtype=jnp.int32)
    tile_expert = jnp.searchsorted(ends, jnp.minimum(tile_ids, n_active - 1) * tm, side="right")
    tile_expert = jnp.minimum(tile_expert, N_EXPERTS - 1).astype(jnp.int32)
    first = jnp.concatenate([jnp.ones((1,), jnp.int32),
                             (tile_expert[1:] != tile_expert[:-1]).astype(jnp.int32)])
    parity = (jnp.cumsum(first) - 1) % 2
    ids = jnp.arange(N_EXPERTS, dtype=jnp.int32)
    present = ends > jnp.concatenate([jnp.zeros((1,), jnp.int32), ends[:-1]])
    later = present[None, :] & (ids[None, :] > ids[:, None])
    next_of = jnp.min(jnp.where(later, ids[None, :], N_EXPERTS), axis=1)
    next_of = jnp.where(next_of == N_EXPERTS, -1, next_of)
    plan = (tile_expert, first, parity.astype(jnp.int32), next_of[tile_expert].astype(jnp.int32),
            n_active.astype(jnp.int32).reshape(1))
    return src_tok, plan


def _t5_bucket(rel):
    n = jnp.maximum(rel, 0)
    max_exact = REL_BUCKETS // 2
    nf = jnp.maximum(n, 1).astype(F32)
    large = max_exact + (jnp.log(nf / max_exact) / math.log(REL_MAX_DIST / max_exact)
                         * (REL_BUCKETS - max_exact)).astype(jnp.int32)
    large = jnp.minimum(large, REL_BUCKETS - 1)
    return jnp.where(n < max_exact, n, large)


def _bias_tables(rel_bias, tq):
    n = 3 * tq
    rel = 2 * tq - 1 - jnp.arange(n, dtype=jnp.int32)
    g = rel_bias[_t5_bucket(rel)].astype(F32).T
    skew = jnp.tile(g, (1, tq))[:, :tq * (n - 1)].reshape(-1, tq, n - 1)
    tiles = jnp.stack([skew[:, :, 2 * tq - 1:3 * tq - 1], skew[:, :, tq - 1:2 * tq - 1]], axis=1)
    far = rel_bias[_t5_bucket(jnp.full((), 2 * tq, jnp.int32))].astype(F32)
    return tiles * LOG2_E, far * LOG2_E


def kernel(x, p, ln_in_g, ln_in_b, rel_bias, w_in, diff_lam_q1, diff_lam_k1, diff_lam_q2, diff_lam_k2, diff_subln_g, rwkv_mu, rwkv_w0, rwkv_w2, rwkv_a0, rwkv_a2, rwkv_g2, rwkv_k_k, rwkv_k_a, rwkv_r_k, rwkv_lnx_g, rwkv_lnx_b, w_out, ln1_g, ln1_b, router_group_w, router_group_b, router_expert_w, router_expert_b, moe_w_gate, moe_w_up, moe_w_down, ln2_g, ln2_b, ple_w, ple_norm_g, ple_gate_w):
    bsz, seq, d = x.shape
    t = bsz * seq
    i = 0
    lambda_init = 0.8 - 0.6 * math.exp(-0.3 * i)
    tq = 256

    x2 = x.reshape(t, d)
    w_in_t = jnp.swapaxes(w_in[i], 0, 1)
    h, hb = layer_norm_rows(x2, ln_in_g, ln_in_b, 512)
    qkv = matmul_nt(hb, w_in_t, 0, O_RWKV, O_RWKV, BF16, 1024, 512)
    pr = matmul_nt(hb, w_in_t, O_RWKV, RWKV_PROJ, RWKV_PROJ_PAD, F32, 1024, 512)

    lam = (jnp.exp(jnp.sum(diff_lam_q1[i].astype(F32) * diff_lam_k1[i]))
           - jnp.exp(jnp.sum(diff_lam_q2[i].astype(F32) * diff_lam_k2[i])) + lambda_init)
    bias_tiles, far_bias = _bias_tables(rel_bias, tq)
    y_diff = diff_attention(qkv.reshape(bsz, seq, O_RWKV), bias_tiles, far_bias, lam.reshape(1),
                            diff_subln_g[i], lambda_init, tq, 8)

    mu_p = jnp.pad(rwkv_mu[i], (0, RWKV_PROJ_PAD - RWKV_PROJ))
    zeros64 = jnp.zeros((LORA_DECAY, RWKV_WIDTH), F32)
    w2p = jnp.concatenate([rwkv_w2[i], zeros64], axis=0).astype(BF16)
    a2p = jnp.concatenate([zeros64, rwkv_a2[i]], axis=0).astype(BF16)
    g2p = jnp.pad(rwkv_g2[i], ((0, 2 * LANES - LORA_GATE), (0, 0))).astype(BF16)
    r, k2, v, lw, kn, bb, g, bonus = rwkv_prep(
        pr, mu_p, rwkv_w0[i], w2p, rwkv_a0[i], a2p, g2p, rwkv_k_k[i], rwkv_k_a[i],
        rwkv_r_k[i].reshape(-1), seq, 256)
    sh = lambda a: a.reshape(bsz, seq, RWKV_WIDTH)
    y_scan = rwkv_scan(sh(r), sh(k2), sh(v), sh(lw), sh(kn), sh(bb), 64, 8)

    w_out_b = w_out[i].astype(BF16)
    rw = jnp.concatenate([router_group_w[i], router_expert_w[i]], axis=1)
    rw = jnp.pad(rw, ((0, 0), (0, ROUTER_LANES - rw.shape[1])))
    rw_hi = rw.astype(BF16)
    rw_lo = (rw - rw_hi.astype(F32)).astype(BF16)
    rb = jnp.pad(jnp.concatenate([router_group_b[i], router_expert_b[i]]),
                 (0, ROUTER_LANES - N_GROUPS - N_EXPERTS)).reshape(1, ROUTER_LANES)
    h1, info = outproj_ln_route(y_diff.reshape(t, DIFF_WIDTH), y_scan.reshape(t, RWKV_WIDTH), bonus, g,
                                rwkv_lnx_g[i], rwkv_lnx_b[i], h, w_out_b[:DIFF_WIDTH],
                                w_out_b[DIFF_WIDTH:], ln1_g[i], ln1_b[i], rw_hi, rw_lo, rb, 512)

    tm_e = MOE_ROW_TILE
    n_rows = 2 * t + N_EXPERTS * tm_e
    e_all = jnp.concatenate([info[:, 0], info[:, 1]]).astype(jnp.int32)
    pos, ends = moe_rank(e_all.reshape(-1, 1), 1024, tm_e)
    pos = pos[:, 0]
    src_tok, plan = _dispatch_plan(pos, ends, t, tm_e, n_rows)
    f = moe_w_gate.shape[-1]
    ys = moe_experts(src_tok, plan, h1.reshape(t, d // LANES, LANES), moe_w_gate[i].reshape(N_EXPERTS, d, f),
                     moe_w_up[i].reshape(N_EXPERTS, d, f), moe_w_down[i].reshape(N_EXPERTS, f, d),
                     n_rows, tm_e)

    out = final_block(pos[:t], pos[t:], ys, h1, info, p[i].reshape(t, -1), ln2_g[i], ln2_b[i],
                      ple_w[i].astype(BF16), ple_norm_g[i], ple_gate_w[i].astype(BF16), 256)
    return out.reshape(bsz, seq, d)
```

```python
import functools
import math

import jax
import jax.numpy as jnp
from jax import lax
from jax.experimental import pallas as pl
from jax.experimental.pallas import tpu as pltpu

F32 = jnp.float32
BF16 = jnp.bfloat16

DIFF_HEADS = 8
DIFF_HEAD_DIM = 64
DIFF_V_DIM = 128
DIFF_WIDTH = DIFF_HEADS * DIFF_V_DIM
DIFF_QK_WIDTH = DIFF_HEADS * 2 * DIFF_HEAD_DIM
RWKV_HEADS = 16
RWKV_HEAD_SIZE = 64
RWKV_WIDTH = RWKV_HEADS * RWKV_HEAD_SIZE
LORA_DECAY = 64
LORA_AAA = 64
LORA_GATE = 160
RWKV_PROJ = 3 * RWKV_WIDTH + LORA_DECAY + LORA_AAA + LORA_GATE
O_RWKV = 2 * DIFF_QK_WIDTH + DIFF_WIDTH
REL_BUCKETS = 32
REL_MAX_DIST = 128
N_GROUPS = 4
EXPERTS_PER_GROUP = 8
N_EXPERTS = N_GROUPS * EXPERTS_PER_GROUP
LN_EPS = 1e-5
RWKV_GN_EPS = 64e-5
NEG_INF = -1e30
DEPTH = 1
DEEPNORM_ALPHA = (2 * DEPTH) ** 0.25
LOG2_E = math.log2(math.e)

LANES = 128
RWKV_PROJ_PAD = 3584
ROUTER_LANES = LANES
VMEM_LIMIT = 48 * 1024 * 1024
MOE_VMEM_LIMIT = 56 * 1024 * 1024
MOE_ROW_TILE = 256


def _cparams(sem):
    return pltpu.CompilerParams(dimension_semantics=sem, vmem_limit_bytes=VMEM_LIMIT)


def _bdot(a, b):
    return jnp.dot(a.astype(BF16), b.astype(BF16), preferred_element_type=F32)


def _bdot_nt(a, b):
    return lax.dot_general(a.astype(BF16), b.astype(BF16), (((1,), (1,)), ((), ())),
                           preferred_element_type=F32)


def _bdot_tn(a, b):
    return lax.dot_general(a.astype(BF16), b.astype(BF16), (((0,), (0,)), ((), ())),
                           preferred_element_type=F32)


def _split3(x):
    hi = x.astype(BF16)
    r1 = x - hi.astype(F32)
    mid = r1.astype(BF16)
    lo = (r1 - mid.astype(F32)).astype(BF16)
    return hi, mid, lo


def _head_sum(x):
    lane = lax.broadcasted_iota(jnp.int32, (x.shape[0], LANES), 1)
    first = lane < RWKV_HEAD_SIZE
    cols = []
    for c in range(x.shape[1] // LANES):
        blk = x[:, c * LANES:(c + 1) * LANES]
        lo = jnp.sum(jnp.where(first, blk, 0.0), axis=1, keepdims=True)
        hi = jnp.sum(jnp.where(first, 0.0, blk), axis=1, keepdims=True)
        cols.append(jnp.where(first, lo, hi))
    return jnp.concatenate(cols, axis=1)


def _dot_sel_x(sel, x):
    hi, mid, lo = _split3(x)
    d = lambda p: jnp.dot(sel, p, preferred_element_type=F32)
    return d(hi) + d(mid) + d(lo)


def _layer_norm(x, g, b):
    mu = jnp.mean(x, axis=-1, keepdims=True)
    xc = x - mu
    var = jnp.mean(xc * xc, axis=-1, keepdims=True)
    return xc * lax.rsqrt(var + LN_EPS) * g + b


def _sigmoid(z):
    return 1.0 / (1.0 + jnp.exp(-z))


def _ln_kernel(x_ref, g_ref, b_ref, h_ref, hb_ref):
    xn = _layer_norm(x_ref[...], g_ref[...], b_ref[...])
    h_ref[...] = xn
    hb_ref[...] = xn.astype(BF16)


def layer_norm_rows(x, g, b, tm):
    m, d = x.shape
    tile = pl.BlockSpec((tm, d), lambda i: (i, 0))
    row = pl.BlockSpec((1, d), lambda i: (0, 0))
    return pl.pallas_call(
        _ln_kernel,
        grid=(m // tm,),
        in_specs=[tile, row, row],
        out_specs=[tile, tile],
        out_shape=[jax.ShapeDtypeStruct((m, d), F32), jax.ShapeDtypeStruct((m, d), BF16)],
        compiler_params=_cparams(("parallel",)),
        name="layer_norm_rows",
    )(x, g.reshape(1, d), b.reshape(1, d))


def _mm_nt_kernel(x_ref, wt_ref, o_ref, *, n_valid, tn):
    y = lax.dot_general(x_ref[...], wt_ref[...].astype(BF16), (((1,), (1,)), ((), ())),
                        preferred_element_type=F32)
    if n_valid % tn:
        col = pl.program_id(1) * tn + lax.broadcasted_iota(jnp.int32, y.shape, 1)
        y = jnp.where(col < n_valid, y, 0.0)
    o_ref[...] = y.astype(o_ref.dtype)


def matmul_nt(x, wt, row0, n_valid, n_out, out_dtype, tm, tn):
    m, k = x.shape
    blk0 = row0 // tn
    return pl.pallas_call(
        functools.partial(_mm_nt_kernel, n_valid=n_valid, tn=tn),
        grid=(m // tm, n_out // tn),
        in_specs=[pl.BlockSpec((tm, k), lambda i, j: (i, 0)),
                  pl.BlockSpec((tn, k), lambda i, j: (blk0 + j, 0))],
        out_specs=pl.BlockSpec((tm, tn), lambda i, j: (i, j)),
        out_shape=jax.ShapeDtypeStruct((m, n_out), out_dtype),
        compiler_params=_cparams(("parallel", "parallel")),
        name="matmul_nt",
    )(x, wt)


def _attn_kernel(lam_ref, far_ref, q_ref, k_ref, v_ref, bias_ref, g_ref, o_ref,
                 m_scr, l_scr, acc_scr, *, tq, hps, scale, post_scale):
    hp = pl.program_id(1)
    qi = pl.program_id(2)
    lane = lax.broadcasted_iota(jnp.int32, (tq, LANES), 1)
    q_maps = []
    for hh in range(hps):
        qs = (q_ref[0, :, hh * LANES:(hh + 1) * LANES].astype(F32) * scale).astype(BF16)
        zero = jnp.zeros_like(qs)
        q_maps += [jnp.where(lane < DIFF_HEAD_DIM, qs, zero), jnp.where(lane >= DIFF_HEAD_DIM, qs, zero)]

    m_scr[...] = jnp.full(m_scr.shape, NEG_INF, F32)
    l_scr[...] = jnp.zeros(l_scr.shape, F32)
    acc_scr[...] = jnp.zeros(acc_scr.shape, F32)

    def step(kstart, bias_index, causal):
        for hh in range(hps):
            kb = k_ref[0, pl.ds(kstart, tq), hh * LANES:(hh + 1) * LANES]
            vb = v_ref[0, pl.ds(kstart, tq), hh * LANES:(hh + 1) * LANES]
            for c in range(2 * hh, 2 * hh + 2):
                s = lax.dot_general(q_maps[c], kb, (((1,), (1,)), ((), ())), preferred_element_type=F32)
                m_old = m_scr[c]
                if bias_index is None:
                    far = far_ref[hp * hps + hh]
                    m_new = jnp.maximum(m_old, jnp.max(s, axis=1, keepdims=True) + far)
                    shift = m_new - far
                else:
                    s = s + bias_ref[hh, bias_index]
                    if causal:
                        ri = lax.broadcasted_iota(jnp.int32, (tq, tq), 0)
                        ci = lax.broadcasted_iota(jnp.int32, (tq, tq), 1)
                        s = jnp.where(ri >= ci, s, NEG_INF)
                    m_new = jnp.maximum(m_old, jnp.max(s, axis=1, keepdims=True))
                    shift = m_new
                p = jnp.exp2(s - jnp.concatenate([shift] * (tq // LANES), axis=1))
                corr = jnp.exp2(m_old - m_new)
                l_scr[c] = corr * l_scr[c] + jnp.sum(p, axis=1, keepdims=True)
                acc_scr[c] = corr * acc_scr[c] + jnp.dot(p.astype(BF16), vb, preferred_element_type=F32)
                m_scr[c] = m_new

    def far_body(kb, carry):
        step(pl.multiple_of(kb * tq, tq), None, False)
        return carry

    lax.fori_loop(0, jnp.maximum(qi - 1, 0), far_body, 0)

    @pl.when(qi >= 1)
    def _():
        step(pl.multiple_of((qi - 1) * tq, tq), 1, False)

    step(pl.multiple_of(qi * tq, tq), 0, True)

    for hh in range(hps):
        c = 2 * hh
        o = acc_scr[c] / l_scr[c] - lam_ref[0] * (acc_scr[c + 1] / l_scr[c + 1])
        ms = jnp.mean(o * o, axis=1, keepdims=True)
        o_ref[0, :, hh * LANES:(hh + 1) * LANES] = (
            o * lax.rsqrt(ms + LN_EPS) * g_ref[...] * post_scale).astype(o_ref.dtype)


def diff_attention(qkv, bias_tiles, far_bias, lam, subln_g, lambda_init, tq, hps):
    bsz, s, _ = qkv.shape
    kern = functools.partial(_attn_kernel, tq=tq, hps=hps, scale=DIFF_HEAD_DIM ** -0.5 * LOG2_E,
                             post_scale=1.0 - lambda_init)
    hw = hps * LANES
    nqk = DIFF_QK_WIDTH // hw
    return pl.pallas_call(
        kern,
        grid=(bsz, DIFF_HEADS // hps, s // tq),
        in_specs=[
            pl.BlockSpec(memory_space=pltpu.SMEM),
            pl.BlockSpec(memory_space=pltpu.SMEM),
            pl.BlockSpec((1, tq, hw), lambda b, h, i: (b, i, h)),
            pl.BlockSpec((1, s, hw), lambda b, h, i: (b, 0, nqk + h)),
            pl.BlockSpec((1, s, hw), lambda b, h, i: (b, 0, 2 * nqk + h)),
            pl.BlockSpec((hps, 2, tq, tq), lambda b, h, i: (h, 0, 0, 0)),
            pl.BlockSpec((1, LANES), lambda b, h, i: (0, 0)),
        ],
        out_specs=pl.BlockSpec((1, tq, hw), lambda b, h, i: (b, i, h)),
        out_shape=jax.ShapeDtypeStruct((bsz, s, DIFF_WIDTH), BF16),
        scratch_shapes=[pltpu.VMEM((2 * hps, tq, LANES), F32), pltpu.VMEM((2 * hps, tq, LANES), F32),
                        pltpu.VMEM((2 * hps, tq, LANES), F32)],
        compiler_params=_cparams(("parallel", "parallel", "arbitrary")),
        name="diff_attention",
    )(lam, far_bias, qkv, qkv, qkv, bias_tiles, subln_g.reshape(1, LANES))


def _rwkv_prep_kernel(x_ref, prev_ref, mu_ref, w0_ref, w2_ref, a0_ref, a2_ref, g2_ref,
                      kk_ref, ka_ref, rk_ref,
                      r_out, k_out, v_out, lw_out, kn_out, b_out, g_out, bonus_out, *, tm, seq):
    i = pl.program_id(0)
    x = x_ref[...]
    w = RWKV_WIDTH
    row = lax.broadcasted_iota(jnp.int32, x.shape, 0)
    is_start = (i * tm) % seq == 0
    last_prev = jnp.where(is_start, 0.0, prev_ref[7:8, :])
    prev = jnp.where(row == 0, last_prev, pltpu.roll(x, 1, 0))
    xs = x + (prev - x) * mu_ref[...]

    r = xs[:, 0:w]
    k = xs[:, w:2 * w]
    v = xs[:, 2 * w:3 * w]
    xwa = xs[:, 3 * w:3 * w + 2 * LORA_DECAY]
    xg = xs[:, 3 * w + LANES:3 * w + LANES + 2 * LANES]

    z = w0_ref[...] + _bdot(jnp.tanh(xwa), w2_ref[...])
    softplus_neg_z = jnp.maximum(-z, 0.0) + jnp.log(1.0 + jnp.exp(-jnp.abs(z)))
    lw_out[...] = -jnp.exp(-softplus_neg_z - 0.5)
    a_lr = _sigmoid(a0_ref[...] + _bdot(xwa, a2_ref[...]))
    g_out[...] = _bdot(_sigmoid(xg), g2_ref[...])

    kk = k * kk_ref[...]
    kn = kk / jnp.maximum(jnp.sqrt(_head_sum(kk * kk)), 1e-12)
    k2 = k * (1.0 + (a_lr - 1.0) * ka_ref[...])
    r_out[...] = r
    k_out[...] = k2
    v_out[...] = v
    kn_out[...] = kn
    b_out[...] = kn * a_lr
    bonus_out[...] = _head_sum(r * k2 * rk_ref[...]) * v


def rwkv_prep(pr, mu, w0, w2p, a0, a2p, g2p, k_k, k_a, r_k, seq, tm):
    t, wp = pr.shape
    w = RWKV_WIDTH
    row = lambda a: a.reshape(1, -1)
    full = lambda shp: pl.BlockSpec(shp, lambda i: (0, 0))
    outs = [jax.ShapeDtypeStruct((t, w), F32)] * 8
    kern = functools.partial(_rwkv_prep_kernel, tm=tm, seq=seq)
    return pl.pallas_call(
        kern,
        grid=(t // tm,),
        in_specs=[
            pl.BlockSpec((tm, wp), lambda i: (i, 0)),
            pl.BlockSpec((8, wp), lambda i: (jnp.maximum(i * (tm // 8) - 1, 0), 0)),
            full((1, wp)), full((1, w)), full((LANES, w)), full((1, w)), full((LANES, w)),
            full((2 * LANES, w)), full((1, w)), full((1, w)), full((1, w)),
        ],
        out_specs=[pl.BlockSpec((tm, w), lambda i: (i, 0))] * 8,
        out_shape=outs,
        compiler_params=_cparams(("parallel",)),
        name="rwkv_prep",
    )(pr, pr, row(mu), row(w0), w2p, row(a0), a2p, g2p, row(k_k), row(k_a), row(r_k))


def _rwkv_scan_kernel(r_ref, k_ref, v_ref, lw_ref, kn_ref, b_ref, y_ref, s_scr, *, chunk, pairs):
    c2 = 2 * chunk

    @pl.when(pl.program_id(2) == 0)
    def _():
        s_scr[...] = jnp.zeros(s_scr.shape, F32)

    ri = lax.broadcasted_iota(jnp.int32, (chunk, chunk), 0)
    ci = lax.broadcasted_iota(jnp.int32, (chunk, chunk), 1)
    ltri = jnp.where(ri >= ci, 1.0, 0.0).astype(BF16)

    lw = lw_ref[0]
    cl = _dot_sel_x(ltri, lw)
    mid = chunk // 2 - 1
    clm = cl[mid:mid + 1, :]
    cle = cl[chunk - 1:chunk, :]
    w_mid = jnp.exp(cl - clm)
    w_mid_prev = jnp.exp(cl - lw - clm)
    w_mid_inv = jnp.exp(clm - cl)
    w_abs = jnp.exp(cl)
    w_abs_prev = jnp.exp(cl - lw)
    w_end = jnp.exp(cle - cl)
    w_tot = jnp.exp(cle)

    r = r_ref[0]
    k = k_ref[0]
    v = v_ref[0]
    a = -kn_ref[0]
    b = b_ref[0]
    a_mid = a * w_mid_prev
    r_mid = r * w_mid
    b_mid = b * w_mid_inv
    k_mid = k * w_mid_inv
    a_abs = a * w_abs_prev
    r_abs = r * w_abs
    b_end = b * w_end
    k_end = k * w_end

    lane = lax.broadcasted_iota(jnp.int32, (chunk, LANES), 1)
    first = lane < RWKV_HEAD_SIZE

    def stack(x):
        return jnp.concatenate([jnp.where(first, x, 0.0), jnp.where(first, 0.0, x)], axis=0)

    rr = lax.broadcasted_iota(jnp.int32, (c2, c2), 0)
    cc = lax.broadcasted_iota(jnp.int32, (c2, c2), 1)
    same = (rr >= chunk) == (cc >= chunk)
    strict = same & (rr > cc)
    incl = same & (rr >= cc)
    eye = jnp.where(rr == cc, 1.0, 0.0)

    ps = range(pairs)
    sls = [slice(p * LANES, (p + 1) * LANES) for p in ps]
    bf = lambda x: x.astype(BF16)
    cat0 = lambda xs: jnp.concatenate(xs, axis=0)
    cat1 = lambda xs: jnp.concatenate(xs, axis=1)
    v_s = [bf(stack(v[:, sl])) for sl in sls]
    ar_m = [bf(cat0([stack(a_mid[:, sl]), stack(r_mid[:, sl])])) for sl in sls]
    bk_m = [bf(cat0([stack(b_mid[:, sl]), stack(k_mid[:, sl])])) for sl in sls]
    g_all = [_bdot_nt(ar_m[p], bk_m[p]) for p in ps]
    l_ab = [jnp.where(strict, g[:c2, :c2], 0.0) for g in g_all]
    l_ak = [bf(jnp.where(strict, g[:c2, c2:], 0.0)) for g in g_all]
    m_r = [bf(cat1([jnp.where(incl, g[c2:, :c2], 0.0), jnp.where(incl, g[c2:, c2:], 0.0)])) for g in g_all]

    t_inv = [eye + l_ab[p] for p in ps]
    n_fac = max(chunk.bit_length() - 2, 0)
    lp = [_bdot(l_ab[p], l_ab[p]) for p in ps] if n_fac else l_ab
    for _ in range(n_fac - 1):
        x = [_bdot(cat0([t_inv[p], lp[p]]), lp[p]) for p in ps]
        t_inv = [t_inv[p] + x[p][:c2] for p in ps]
        lp = [x[p][c2:] for p in ps]
    if n_fac:
        t_inv = [t_inv[p] + _bdot(t_inv[p], lp[p]) for p in ps]
    t_inv = [bf(t) for t in t_inv]

    lv = [_bdot(l_ak[p], v_s[p]) for p in ps]
    av = [bf(_bdot(t_inv[p], cat1([stack(a_abs[:, sls[p]]), lv[p]]))) for p in ps]
    zero = jnp.zeros((c2, LANES), BF16)
    ry = [_bdot(m_r[p], cat0([av[p], cat1([zero, v_s[p]])])) for p in ps]
    r_p = [stack(r_abs[:, sls[p]]) + ry[p][:, :LANES] for p in ps]

    s_old = [s_scr[p] for p in ps]
    y_s = [_bdot_nt(r_p[p], s_old[p]) + ry[p][:, LANES:] for p in ps]
    for p in ps:
        y_ref[0, :, sls[p]] = y_s[p][:chunk] + y_s[p][chunk:]
    b_e = [bf(stack(b_end[:, sl])) for sl in sls]
    k_e = [bf(stack(k_end[:, sl])) for sl in sls]
    ab = [_bdot_tn(av[p][:, :LANES], b_e[p]) for p in ps]
    q_t = [_bdot_tn(cat0([av[p][:, LANES:], v_s[p]]), cat0([b_e[p], k_e[p]])) for p in ps]
    for p in ps:
        s_scr[p] = s_old[p] * w_tot[:, sls[p]] + _bdot(s_old[p], ab[p]) + q_t[p]


def rwkv_scan(r, k, v, lw, kn, b, chunk, pairs):
    bsz, s, w = r.shape
    pw = pairs * LANES
    spec = pl.BlockSpec((1, chunk, pw), lambda bi, hi, ci: (bi, ci, hi))
    kern = functools.partial(_rwkv_scan_kernel, chunk=chunk, pairs=pairs)
    return pl.pallas_call(
        kern,
        grid=(bsz, w // pw, s // chunk),
        in_specs=[spec] * 6,
        out_specs=spec,
        out_shape=jax.ShapeDtypeStruct((bsz, s, w), F32),
        scratch_shapes=[pltpu.VMEM((pairs, LANES, LANES), F32)],
        compiler_params=_cparams(("parallel", "parallel", "arbitrary")),
        name="rwkv_scan",
    )(r, k, v, lw, kn, b)


def _rwkv_out(y, bonus, g, lnx_g, lnx_b):
    inv_n = 1.0 / RWKV_HEAD_SIZE
    mu = _head_sum(y) * inv_n
    yc = y - mu
    var = _head_sum(yc * yc) * inv_n
    yn = yc * lax.rsqrt(var + RWKV_GN_EPS) * lnx_g + lnx_b
    return (yn + bonus) * g


def _route(logits):
    lane = lax.broadcasted_iota(jnp.int32, logits.shape, 1)
    big = jnp.int32(4 * LANES)
    gmask = lane < N_GROUPS
    gl = jnp.where(gmask, logits, NEG_INF)
    gmax = jnp.max(gl, axis=1, keepdims=True)
    gidx = jnp.min(jnp.where(gl == gmax, lane, big), axis=1, keepdims=True)
    gtop = 1.0 / jnp.sum(jnp.where(gmask, jnp.exp(gl - gmax), 0.0), axis=1, keepdims=True)
    lo = N_GROUPS + EXPERTS_PER_GROUP * gidx
    emask = (lane >= lo) & (lane < lo + EXPERTS_PER_GROUP)
    el = jnp.where(emask, logits, NEG_INF)
    e1 = jnp.max(el, axis=1, keepdims=True)
    i1 = jnp.min(jnp.where(el == e1, lane, big), axis=1, keepdims=True)
    el2 = jnp.where(lane == i1, NEG_INF, el)
    e2 = jnp.max(el2, axis=1, keepdims=True)
    i2 = jnp.min(jnp.where(el2 == e2, lane, big), axis=1, keepdims=True)
    t = jnp.exp(e2 - e1)
    w1 = gtop / (1.0 + t)
    w2 = gtop * t / (1.0 + t)
    id1 = (i1 - N_GROUPS).astype(F32)
    id2 = (i2 - N_GROUPS).astype(F32)
    return jnp.where(lane == 0, id1, jnp.where(lane == 1, id2, jnp.where(lane == 2, w1,
                                                                         jnp.where(lane == 3, w2, 0.0))))


def _outproj_kernel(yd_ref, ys_ref, bonus_ref, gate_ref, lg_ref, lb_ref, h_ref,
                    w1_ref, w2_ref, g_ref, b_ref, rw_hi_ref, rw_lo_ref, rb_ref, h1_ref, info_ref, *, parts):
    rows = h_ref.shape[0] // parts
    sls = [pl.ds(k * rows, rows) for k in range(parts)]
    d = lambda a, b: jnp.dot(a, b, preferred_element_type=F32)
    yr = [_rwkv_out(ys_ref[sl, :], bonus_ref[sl, :], gate_ref[sl, :], lg_ref[...], lb_ref[...]) for sl in sls]
    mix = [d(yd_ref[sl, :], w1_ref[...]) + d(yr[k].astype(BF16), w2_ref[...]) for k, sl in enumerate(sls)]
    h1 = [_layer_norm(DEEPNORM_ALPHA * h_ref[sl, :] + mix[k], g_ref[...], b_ref[...]) for k, sl in enumerate(sls)]
    for k, sl in enumerate(sls):
        h1_ref[sl, :] = h1[k]
        hi = h1[k].astype(BF16)
        lo = (h1[k] - hi.astype(F32)).astype(BF16)
        logits = d(hi, rw_hi_ref[...]) + d(lo, rw_hi_ref[...]) + d(hi, rw_lo_ref[...]) + rb_ref[...]
        info_ref[sl, :] = _route(logits)


def outproj_ln_route(yd, ys, bonus, gate, lnx_g, lnx_b, h, w_out_a, w_out_b, g, b, rw_hi, rw_lo, rb, tm):
    t, d = h.shape
    ka = yd.shape[1]
    kb = ys.shape[1]
    full = lambda shp: pl.BlockSpec(shp, lambda i: (0, 0))
    tile = lambda n: pl.BlockSpec((tm, n), lambda i: (i, 0))
    return pl.pallas_call(
        functools.partial(_outproj_kernel, parts=2),
        grid=(t // tm,),
        in_specs=[
            tile(ka), tile(kb), tile(kb), tile(kb), full((1, kb)), full((1, kb)), tile(d),
            full((ka, d)), full((kb, d)), full((1, d)), full((1, d)),
            full((d, ROUTER_LANES)), full((d, ROUTER_LANES)), full((1, ROUTER_LANES)),
        ],
        out_specs=[
            pl.BlockSpec((tm, d), lambda i: (i, 0)),
            pl.BlockSpec((tm, ROUTER_LANES), lambda i: (i, 0)),
        ],
        out_shape=[jax.ShapeDtypeStruct((t, d), F32), jax.ShapeDtypeStruct((t, ROUTER_LANES), F32)],
        compiler_params=pltpu.CompilerParams(dimension_semantics=("parallel",),
                                             vmem_limit_bytes=MOE_VMEM_LIMIT),
        name="outproj_ln_route",
    )(yd, ys, bonus, gate, lnx_g.reshape(1, kb), lnx_b.reshape(1, kb), h, w_out_a, w_out_b,
      g.reshape(1, d), b.reshape(1, d), rw_hi, rw_lo, rb)


def _moe_rank_kernel(e_ref, pos_ref, ends_ref, run_scr, start_scr, *, tm, row_tile):
    ph = pl.program_id(0)
    i = pl.program_id(1)
    lane = lax.broadcasted_iota(jnp.int32, (tm, LANES), 1)
    onehot = e_ref[...] == lane
    oh = jnp.where(onehot, 1.0, 0.0)

    @pl.when((ph == 0) & (i == 0))
    def _():
        run_scr[...] = jnp.zeros(run_scr.shape, F32)

    @pl.when(ph == 0)
    def _():
        run_scr[...] += jnp.sum(oh, axis=0, keepdims=True)

    @pl.when((ph == 1) & (i == 0))
    def _():
        padded = jnp.floor((run_scr[...] + (row_tile - 1)) * (1.0 / row_tile)) * row_tile
        ri = lax.broadcasted_iota(jnp.int32, (LANES, LANES), 0)
        ci = lax.broadcasted_iota(jnp.int32, (LANES, LANES), 1)
        upper = jnp.where(ri < ci, 1.0, 0.0).astype(BF16)
        parts = _split3(jnp.broadcast_to(padded, (8, LANES)))
        starts = sum(jnp.dot(q, upper, preferred_element_type=F32) for q in parts)[0:1]
        start_scr[...] = starts
        ends_ref[...] = starts + padded
        run_scr[...] = jnp.zeros(run_scr.shape, F32)

    @pl.when(ph == 1)
    def _():
        ri = lax.broadcasted_iota(jnp.int32, (tm, tm), 0)
        ci = lax.broadcasted_iota(jnp.int32, (tm, tm), 1)
        before = jnp.where(ri > ci, 1.0, 0.0).astype(BF16)
        row = jnp.dot(before, oh.astype(BF16), preferred_element_type=F32) + (run_scr[...] + start_scr[...])
        pos_ref[...] = jnp.sum(jnp.where(onehot, row, 0.0), axis=1, keepdims=True).astype(jnp.int32)
        run_scr[...] += jnp.sum(oh, axis=0, keepdims=True)


def moe_rank(e_col, tm, row_tile):
    n = e_col.shape[0]
    return pl.pallas_call(
        functools.partial(_moe_rank_kernel, tm=tm, row_tile=row_tile),
        grid=(2, n // tm),
        in_specs=[pl.BlockSpec((tm, 1), lambda ph, i: (i, 0))],
        out_specs=[pl.BlockSpec((tm, 1), lambda ph, i: (i * ph, 0)),
                   pl.BlockSpec((1, LANES), lambda ph, i: (0, 0))],
        out_shape=[jax.ShapeDtypeStruct((n, 1), jnp.int32), jax.ShapeDtypeStruct((1, LANES), F32)],
        scratch_shapes=[pltpu.VMEM((1, LANES), F32), pltpu.VMEM((1, LANES), F32)],
        compiler_params=_cparams(("arbitrary", "arbitrary")),
        name="moe_rank",
    )(e_col)


def _row_copy(src_hbm, row, dst, r, sem):
    return pltpu.make_async_copy(src_hbm.at[pl.ds(row, 1)], dst.at[pl.ds(r, 1)], sem)


def _pack_bf16_halves(y):
    n = y.shape[1] // 2
    hi = pltpu.bitcast(y[:, :n].astype(BF16).astype(F32), jnp.uint32)
    lo = pltpu.bitcast(y[:, n:].astype(BF16).astype(F32), jnp.uint32)
    return hi | (lo >> 16)


def _unpack_bf16_halves(w):
    hi = pltpu.bitcast(w & jnp.uint32(0xFFFF0000), F32)
    lo = pltpu.bitcast(w << 16, F32)
    return jnp.concatenate([hi, lo], axis=1)


def _moe_expert_kernel(te_ref, first_ref, par_ref, nxt_ref, nact_ref,
                       src0_ref, src1_ref, src_ahead_ref, x_hbm, wg_hbm, wu_hbm, wd_hbm, o_ref,
                       xbuf0, xbuf1, xbuf2, wgf, wuf, wdf, wgb, wub, wdb, sem, wsem, *, tm):
    j = pl.program_id(0)
    n_act = nact_ref[0]
    bufs = (xbuf0, xbuf1, xbuf2)
    nb = len(bufs)

    def token_copy(tok, into, r):
        return pltpu.make_async_copy(x_hbm.at[tok], bufs[into].at[:, r, :], sem.at[into])

    def start_gather(src, into):
        for r in range(tm):
            token_copy(src[0, 0, r], into, r).start(priority=r % 2)

    def wait_gather(into):
        for r in range(tm):
            token_copy(0, into, r).wait()

    def weight_copies(e, par):
        return (pltpu.make_async_copy(wg_hbm.at[e], wgf.at[par], wsem.at[par]),
                pltpu.make_async_copy(wu_hbm.at[e], wuf.at[par], wsem.at[par]),
                pltpu.make_async_copy(wd_hbm.at[e], wdf.at[par], wsem.at[par]))

    @pl.when(j == 0)
    def _():
        start_gather(src0_ref, 0)
        start_gather(src1_ref, 1)
        for c in weight_copies(te_ref[0], 0):
            c.start(priority=1)

    @pl.when((j < n_act) & (first_ref[j] > 0))
    def _():
        par = par_ref[j]
        for c in weight_copies(te_ref[j], par):
            c.wait()
        wgb[...] = wgf[par].astype(BF16)
        wub[...] = wuf[par].astype(BF16)
        wdb[...] = wdf[par].astype(BF16)

    for slot in range(nb):
        @pl.when((j < n_act) & (j % nb == slot))
        def _():
            wait_gather(slot)
            start_gather(src_ahead_ref, (slot + 2) % nb)
            x = jnp.concatenate([bufs[slot][c] for c in range(bufs[slot].shape[0])], axis=1).astype(BF16)
            hg = jnp.dot(x, wgb[...], preferred_element_type=F32)
            hu = jnp.dot(x, wub[...], preferred_element_type=F32)
            hid = hg * _sigmoid(hg) * hu
            o_ref[...] = _pack_bf16_halves(jnp.dot(hid.astype(BF16), wdb[...], preferred_element_type=F32))

        @pl.when((j == n_act - 1) & (j % nb == slot))
        def _():
            wait_gather((slot + 1) % nb)
            wait_gather((slot + 2) % nb)

    @pl.when((j < n_act) & (first_ref[j] > 0) & (nxt_ref[j] >= 0))
    def _():
        for c in weight_copies(nxt_ref[j], 1 - par_ref[j]):
            c.start(priority=1)

    @pl.when(j >= n_act)
    def _():
        o_ref[...] = jnp.zeros(o_ref.shape, jnp.uint32)


def moe_experts(src_tok, plan, x, wg, wu, wd, n_rows, tm):
    d, f = wg.shape[1:]
    hbm = pl.BlockSpec(memory_space=pl.ANY)
    src3 = src_tok.reshape(n_rows // tm, 1, tm)
    smem_tile = lambda index_map: pl.BlockSpec((1, 1, tm), index_map, memory_space=pltpu.SMEM)
    return pl.pallas_call(
        functools.partial(_moe_expert_kernel, tm=tm),
        grid_spec=pltpu.PrefetchScalarGridSpec(
            num_scalar_prefetch=5,
            grid=(n_rows // tm,),
            in_specs=[smem_tile(lambda j, *_: (0, 0, 0)),
                      smem_tile(lambda j, te, fi, pa, nx, na: (jnp.minimum(1, na[0] - 1), 0, 0)),
                      smem_tile(lambda j, te, fi, pa, nx, na: (jnp.minimum(j + 2, na[0] - 1), 0, 0)),
                      hbm, hbm, hbm, hbm],
            out_specs=pl.BlockSpec((tm, d // 2), lambda j, *_: (j, 0)),
            scratch_shapes=[pltpu.VMEM((d // LANES, tm, LANES), F32)] * 3 + [
                            pltpu.VMEM((2, d, f), F32), pltpu.VMEM((2, d, f), F32), pltpu.VMEM((2, f, d), F32),
                            pltpu.VMEM((d, f), BF16), pltpu.VMEM((d, f), BF16), pltpu.VMEM((f, d), BF16),
                            pltpu.SemaphoreType.DMA((3,)), pltpu.SemaphoreType.DMA((2,))],
        ),
        out_shape=jax.ShapeDtypeStruct((n_rows, d // 2), jnp.uint32),
        compiler_params=pltpu.CompilerParams(dimension_semantics=("arbitrary",),
                                             vmem_limit_bytes=MOE_VMEM_LIMIT),
        name="moe_experts",
    )(*plan, src3, src3, src3, x, wg, wu, wd)


def _final_kernel(p1_ref, p2_ref, y_hbm, h1_ref, info_ref, p_ref, g_ref, b_ref, pw_ref, pg_ref, gw_ref,
                  o_ref, ybuf0, ybuf1, sem, *, tm, n_tiles, parts):
    i = pl.program_id(0)
    bufs = (ybuf0, ybuf1)

    def start_gather(tile, into):
        for r in range(tm):
            _row_copy(y_hbm, p1_ref[tile * tm + r], bufs[into].at[0], r, sem.at[into]).start(priority=0)
            _row_copy(y_hbm, p2_ref[tile * tm + r], bufs[into].at[1], r, sem.at[into]).start(priority=1)

    def wait_gather(into):
        for r in range(tm):
            _row_copy(y_hbm, 0, bufs[into].at[0], r, sem.at[into]).wait()
            _row_copy(y_hbm, 0, bufs[into].at[1], r, sem.at[into]).wait()

    @pl.when(i == 0)
    def _():
        start_gather(0, 0)

    for slot in range(2):
        @pl.when(i % 2 == slot)
        def _():
            wait_gather(slot)
            start_gather(jnp.minimum(i + 1, n_tiles - 1), 1 - slot)
            rows = tm // parts
            sls = [pl.ds(k * rows, rows) for k in range(parts)]
            h2, ple = [], []
            for sl in sls:
                info = info_ref[sl, :]
                ffn = (info[:, 2:3] * _unpack_bf16_halves(bufs[slot][0, sl, :])
                       + info[:, 3:4] * _unpack_bf16_halves(bufs[slot][1, sl, :]))
                h2.append(_layer_norm(DEEPNORM_ALPHA * h1_ref[sl, :] + ffn, g_ref[...], b_ref[...]))
                pe = jnp.dot(p_ref[sl, :].astype(BF16), pw_ref[...], preferred_element_type=F32)
                ple.append(pe * lax.rsqrt(jnp.mean(pe * pe, axis=-1, keepdims=True) + LN_EPS) * pg_ref[...])
            gate = [jnp.dot(h.astype(BF16), gw_ref[...], preferred_element_type=F32) for h in h2]
            for k, sl in enumerate(sls):
                o_ref[sl, :] = h2[k] + _sigmoid(gate[k]) * ple[k]

        @pl.when((i == n_tiles - 1) & (i % 2 == slot))
        def _():
            wait_gather(1 - slot)


def final_block(pos1, pos2, ys, h1, info, p, g, b, ple_w, ple_g, gate_w, tm):
    t, d = h1.shape
    dp = p.shape[1]
    tile = lambda n: pl.BlockSpec((tm, n), lambda i, p1, p2: (i, 0))
    full = lambda shp: pl.BlockSpec(shp, lambda i, p1, p2: (0, 0))
    return pl.pallas_call(
        functools.partial(_final_kernel, tm=tm, n_tiles=t // tm, parts=2),
        grid_spec=pltpu.PrefetchScalarGridSpec(
            num_scalar_prefetch=2,
            grid=(t // tm,),
            in_specs=[pl.BlockSpec(memory_space=pl.ANY), tile(d), tile(ROUTER_LANES), tile(dp),
                      full((1, d)), full((1, d)), full((dp, d)), full((1, d)), full((d, d))],
            out_specs=tile(d),
            scratch_shapes=[pltpu.VMEM((2, tm, d // 2), jnp.uint32), pltpu.VMEM((2, tm, d // 2), jnp.uint32),
                            pltpu.SemaphoreType.DMA((2,))],
        ),
        out_shape=jax.ShapeDtypeStruct((t, d), F32),
        compiler_params=pltpu.CompilerParams(dimension_semantics=("arbitrary",),
                                             vmem_limit_bytes=MOE_VMEM_LIMIT),
        name="final_block",
    )(pos1, pos2, ys, h1, info, p, g.reshape(1, d), b.reshape(1, d), ple_w, ple_g.reshape(1, d), gate_w)


def _dispatch_plan(pos, ends_f, n_tok, tm, n_rows):
    n_tiles = n_rows // tm
    ends = ends_f[0, :N_EXPERTS].astype(jnp.int32)
    src_tok = jnp.zeros((n_rows,), jnp.int32).at[pos].set(
        jnp.arange(2 * n_tok, dtype=jnp.int32) % n_tok, unique_indices=True)
    n_active = ends[-1] // tm
    tile_ids = jnp.arange(n_tiles, dtype=jnp.int32)
    tile_expert = jnp.searchsorted(ends, jnp.minimum(tile_ids, n_active - 1) * tm, side="right")
    tile_expert = jnp.minimum(tile_expert, N_EXPERTS - 1).astype(jnp.int32)
    first = jnp.concatenate([jnp.ones((1,), jnp.int32),
                             (tile_expert[1:] != tile_expert[:-1]).astype(jnp.int32)])
    parity = (jnp.cumsum(first) - 1) % 2
    ids = jnp.arange(N_EXPERTS, dtype=jnp.int32)
    present = ends > jnp.concatenate([jnp.zeros((1,), jnp.int32), ends[:-1]])
    later = present[None, :] & (ids[None, :] > ids[:, None])
    next_of = jnp.min(jnp.where(later, ids[None, :], N_EXPERTS), axis=1)
    next_of = jnp.where(next_of == N_EXPERTS, -1, next_of)
    plan = (tile_expert, first, parity.astype(jnp.int32), next_of[tile_expert].astype(jnp.int32),
            n_active.astype(jnp.int32).reshape(1))
    return src_tok, plan


def _t5_bucket(rel):
    n = jnp.maximum(rel, 0)
    max_exact = REL_BUCKETS // 2
    nf = jnp.maximum(n, 1).astype(F32)
    large = max_exact + (jnp.log(nf / max_exact) / math.log(REL_MAX_DIST / max_exact)
                         * (REL_BUCKETS - max_exact)).astype(jnp.int32)
    large = jnp.minimum(large, REL_BUCKETS - 1)
    return jnp.where(n < max_exact, n, large)


def _bias_tables(rel_bias, tq):
    n = 3 * tq
    rel = 2 * tq - 1 - jnp.arange(n, dtype=jnp.int32)
    g = rel_bias[_t5_bucket(rel)].astype(F32).T
    skew = jnp.tile(g, (1, tq))[:, :tq * (n - 1)].reshape(-1, tq, n - 1)
    tiles = jnp.stack([skew[:, :, 2 * tq - 1:3 * tq - 1], skew[:, :, tq - 1:2 * tq - 1]], axis=1)
    far = rel_bias[_t5_bucket(jnp.full((), 2 * tq, jnp.int32))].astype(F32)
    return tiles * LOG2_E, far * LOG2_E


def kernel(x, p, ln_in_g, ln_in_b, rel_bias, w_in, diff_lam_q1, diff_lam_k1, diff_lam_q2, diff_lam_k2, diff_subln_g, rwkv_mu, rwkv_w0, rwkv_w2, rwkv_a0, rwkv_a2, rwkv_g2, rwkv_k_k, rwkv_k_a, rwkv_r_k, rwkv_lnx_g, rwkv_lnx_b, w_out, ln1_g, ln1_b, router_group_w, router_group_b, router_expert_w, router_expert_b, moe_w_gate, moe_w_up, moe_w_down, ln2_g, ln2_b, ple_w, ple_norm_g, ple_gate_w):
    bsz, seq, d = x.shape
    t = bsz * seq
    i = 0
    lambda_init = 0.8 - 0.6 * math.exp(-0.3 * i)
    tq = 256

    x2 = x.reshape(t, d)
    w_in_t = jnp.swapaxes(w_in[i], 0, 1)
    h, hb = layer_norm_rows(x2, ln_in_g, ln_in_b, 512)
    qkv = matmul_nt(hb, w_in_t, 0, O_RWKV, O_RWKV, BF16, 1024, 512)
    pr = matmul_nt(hb, w_in_t, O_RWKV, RWKV_PROJ, RWKV_PROJ_PAD, F32, 1024, 512)

    lam = (jnp.exp(jnp.sum(diff_lam_q1[i].astype(F32) * diff_lam_k1[i]))
           - jnp.exp(jnp.sum(diff_lam_q2[i].astype(F32) * diff_lam_k2[i])) + lambda_init)
    bias_tiles, far_bias = _bias_tables(rel_bias, tq)
    y_diff = diff_attention(qkv.reshape(bsz, seq, O_RWKV), bias_tiles, far_bias, lam.reshape(1),
                            diff_subln_g[i], lambda_init, tq, 8)

    mu_p = jnp.pad(rwkv_mu[i], (0, RWKV_PROJ_PAD - RWKV_PROJ))
    zeros64 = jnp.zeros((LORA_DECAY, RWKV_WIDTH), F32)
    w2p = jnp.concatenate([rwkv_w2[i], zeros64], axis=0).astype(BF16)
    a2p = jnp.concatenate([zeros64, rwkv_a2[i]], axis=0).astype(BF16)
    g2p = jnp.pad(rwkv_g2[i], ((0, 2 * LANES - LORA_GATE), (0, 0))).astype(BF16)
    r, k2, v, lw, kn, bb, g, bonus = rwkv_prep(
        pr, mu_p, rwkv_w0[i], w2p, rwkv_a0[i], a2p, g2p, rwkv_k_k[i], rwkv_k_a[i],
        rwkv_r_k[i].reshape(-1), seq, 256)
    sh = lambda a: a.reshape(bsz, seq, RWKV_WIDTH)
    y_scan = rwkv_scan(sh(r), sh(k2), sh(v), sh(lw), sh(kn), sh(bb), 64, 8)

    w_out_b = w_out[i].astype(BF16)
    rw = jnp.concatenate([router_group_w[i], router_expert_w[i]], axis=1)
    rw = jnp.pad(rw, ((0, 0), (0, ROUTER_LANES - rw.shape[1])))
    rw_hi = rw.astype(BF16)
    rw_lo = (rw - rw_hi.astype(F32)).astype(BF16)
    rb = jnp.pad(jnp.concatenate([router_group_b[i], router_expert_b[i]]),
                 (0, ROUTER_LANES - N_GROUPS - N_EXPERTS)).reshape(1, ROUTER_LANES)
    h1, info = outproj_ln_route(y_diff.reshape(t, DIFF_WIDTH), y_scan.reshape(t, RWKV_WIDTH), bonus, g,
                                rwkv_lnx_g[i], rwkv_lnx_b[i], h, w_out_b[:DIFF_WIDTH],
                                w_out_b[DIFF_WIDTH:], ln1_g[i], ln1_b[i], rw_hi, rw_lo, rb, 512)

    tm_e = MOE_ROW_TILE
    n_rows = 2 * t + N_EXPERTS * tm_e
    e_all = jnp.concatenate([info[:, 0], info[:, 1]]).astype(jnp.int32)
    pos, ends = moe_rank(e_all.reshape(-1, 1), 1024, tm_e)
    pos = pos[:, 0]
    src_tok, plan = _dispatch_plan(pos, ends, t, tm_e, n_rows)
    f = moe_w_gate.shape[-1]
    ys = moe_experts(src_tok, plan, h1.reshape(t, d // LANES, LANES), moe_w_gate[i].reshape(N_EXPERTS, d, f),
                     moe_w_up[i].reshape(N_EXPERTS, d, f), moe_w_down[i].reshape(N_EXPERTS, f, d),
                     n_rows, tm_e)

    out = final_block(pos[:t], pos[t:], ys, h1, info, p[i].reshape(t, -1), ln2_g[i], ln2_b[i],
                      ple_w[i].astype(BF16), ple_norm_g[i], ple_gate_w[i].astype(BF16), 256)
    return out.reshape(bsz, seq, d)
```

```python
import functools
import math

import jax
import jax.numpy as jnp
from jax import lax
from jax.experimental import pallas as pl
from jax.experimental.pallas import tpu as pltpu

F32 = jnp.float32
BF16 = jnp.bfloat16

DIFF_HEADS = 8
DIFF_HEAD_DIM = 64
DIFF_V_DIM = 128
DIFF_WIDTH = DIFF_HEADS * DIFF_V_DIM
DIFF_QK_WIDTH = DIFF_HEADS * 2 * DIFF_HEAD_DIM
RWKV_HEADS = 16
RWKV_HEAD_SIZE = 64
RWKV_WIDTH = RWKV_HEADS * RWKV_HEAD_SIZE
LORA_DECAY = 64
LORA_AAA = 64
LORA_GATE = 160
RWKV_PROJ = 3 * RWKV_WIDTH + LORA_DECAY + LORA_AAA + LORA_GATE
O_RWKV = 2 * DIFF_QK_WIDTH + DIFF_WIDTH
REL_BUCKETS = 32
REL_MAX_DIST = 128
N_GROUPS = 4
EXPERTS_PER_GROUP = 8
N_EXPERTS = N_GROUPS * EXPERTS_PER_GROUP
LN_EPS = 1e-5
RWKV_GN_EPS = 64e-5
NEG_INF = -1e30
DEPTH = 1
DEEPNORM_ALPHA = (2 * DEPTH) ** 0.25
LOG2_E = math.log2(math.e)

LANES = 128
RWKV_PROJ_PAD = 3584
ROUTER_LANES = LANES
VMEM_LIMIT = 48 * 1024 * 1024
MOE_VMEM_LIMIT = 56 * 1024 * 1024
MOE_ROW_TILE = 256


def _cparams(sem):
    return pltpu.CompilerParams(dimension_semantics=sem, vmem_limit_bytes=VMEM_LIMIT)


def _bdot(a, b):
    return jnp.dot(a.astype(BF16), b.astype(BF16), preferred_element_type=F32)


def _bdot_nt(a, b):
    return lax.dot_general(a.astype(BF16), b.astype(BF16), (((1,), (1,)), ((), ())),
                           preferred_element_type=F32)


def _bdot_tn(a, b):
    return lax.dot_general(a.astype(BF16), b.astype(BF16), (((0,), (0,)), ((), ())),
                           preferred_element_type=F32)


def _split3(x):
    hi = x.astype(BF16)
    r1 = x - hi.astype(F32)
    mid = r1.astype(BF16)
    lo = (r1 - mid.astype(F32)).astype(BF16)
    return hi, mid, lo


def _head_sum(x):
    lane = lax.broadcasted_iota(jnp.int32, (x.shape[0], LANES), 1)
    first = lane < RWKV_HEAD_SIZE
    cols = []
    for c in range(x.shape[1] // LANES):
        blk = x[:, c * LANES:(c + 1) * LANES]
        lo = jnp.sum(jnp.where(first, blk, 0.0), axis=1, keepdims=True)
        hi = jnp.sum(jnp.where(first, 0.0, blk), axis=1, keepdims=True)
        cols.append(jnp.where(first, lo, hi))
    return jnp.concatenate(cols, axis=1)


def _dot_sel_x(sel, x):
    hi, mid, lo = _split3(x)
    d = lambda p: jnp.dot(sel, p, preferred_element_type=F32)
    return d(hi) + d(mid) + d(lo)


def _layer_norm(x, g, b):
    mu = jnp.mean(x, axis=-1, keepdims=True)
    xc = x - mu
    var = jnp.mean(xc * xc, axis=-1, keepdims=True)
    return xc * lax.rsqrt(var + LN_EPS) * g + b


def _sigmoid(z):
    return 1.0 / (1.0 + jnp.exp(-z))


def _ln_kernel(x_ref, g_ref, b_ref, h_ref, hb_ref):
    xn = _layer_norm(x_ref[...], g_ref[...], b_ref[...])
    h_ref[...] = xn
    hb_ref[...] = xn.astype(BF16)


def layer_norm_rows(x, g, b, tm):
    m, d = x.shape
    tile = pl.BlockSpec((tm, d), lambda i: (i, 0))
    row = pl.BlockSpec((1, d), lambda i: (0, 0))
    return pl.pallas_call(
        _ln_kernel,
        grid=(m // tm,),
        in_specs=[tile, row, row],
        out_specs=[tile, tile],
        out_shape=[jax.ShapeDtypeStruct((m, d), F32), jax.ShapeDtypeStruct((m, d), BF16)],
        compiler_params=_cparams(("parallel",)),
        name="layer_norm_rows",
    )(x, g.reshape(1, d), b.reshape(1, d))


def _mm_nt_kernel(x_ref, wt_ref, o_ref, *, n_valid, tn):
    y = lax.dot_general(x_ref[...], wt_ref[...].astype(BF16), (((1,), (1,)), ((), ())),
                        preferred_element_type=F32)
    if n_valid % tn:
        col = pl.program_id(1) * tn + lax.broadcasted_iota(jnp.int32, y.shape, 1)
        y = jnp.where(col < n_valid, y, 0.0)
    o_ref[...] = y.astype(o_ref.dtype)


def matmul_nt(x, wt, row0, n_valid, n_out, out_dtype, tm, tn):
    m, k = x.shape
    blk0 = row0 // tn
    return pl.pallas_call(
        functools.partial(_mm_nt_kernel, n_valid=n_valid, tn=tn),
        grid=(m // tm, n_out // tn),
        in_specs=[pl.BlockSpec((tm, k), lambda i, j: (i, 0)),
                  pl.BlockSpec((tn, k), lambda i, j: (blk0 + j, 0))],
        out_specs=pl.BlockSpec((tm, tn), lambda i, j: (i, j)),
        out_shape=jax.ShapeDtypeStruct((m, n_out), out_dtype),
        compiler_params=_cparams(("parallel", "parallel")),
        name="matmul_nt",
    )(x, wt)


def _attn_kernel(lam_ref, far_ref, q_ref, k_ref, v_ref, bias_ref, g_ref, o_ref,
                 m_scr, l_scr, acc_scr, *, tq, hps, scale, post_scale):
    hp = pl.program_id(1)
    qi = pl.program_id(2)
    lane = lax.broadcasted_iota(jnp.int32, (tq, LANES), 1)
    q_maps = []
    for hh in range(hps):
        qs = (q_ref[0, :, hh * LANES:(hh + 1) * LANES].astype(F32) * scale).astype(BF16)
        zero = jnp.zeros_like(qs)
        q_maps += [jnp.where(lane < DIFF_HEAD_DIM, qs, zero), jnp.where(lane >= DIFF_HEAD_DIM, qs, zero)]

    m_scr[...] = jnp.full(m_scr.shape, NEG_INF, F32)
    l_scr[...] = jnp.zeros(l_scr.shape, F32)
    acc_scr[...] = jnp.zeros(acc_scr.shape, F32)

    def step(kstart, bias_index, causal):
        for hh in range(hps):
            kb = k_ref[0, pl.ds(kstart, tq), hh * LANES:(hh + 1) * LANES]
            vb = v_ref[0, pl.ds(kstart, tq), hh * LANES:(hh + 1) * LANES]
            for c in range(2 * hh, 2 * hh + 2):
                s = lax.dot_general(q_maps[c], kb, (((1,), (1,)), ((), ())), preferred_element_type=F32)
                m_old = m_scr[c]
                if bias_index is None:
                    far = far_ref[hp * hps + hh]
                    m_new = jnp.maximum(m_old, jnp.max(s, axis=1, keepdims=True) + far)
                    shift = m_new - far
                else:
                    s = s + bias_ref[hh, bias_index]
                    if causal:
                        ri = lax.broadcasted_iota(jnp.int32, (tq, tq), 0)
                        ci = lax.broadcasted_iota(jnp.int32, (tq, tq), 1)
                        s = jnp.where(ri >= ci, s, NEG_INF)
                    m_new = jnp.maximum(m_old, jnp.max(s, axis=1, keepdims=True))
                    shift = m_new
                p = jnp.exp2(s - jnp.concatenate([shift] * (tq // LANES), axis=1))
                corr = jnp.exp2(m_old - m_new)
                l_scr[c] = corr * l_scr[c] + jnp.sum(p, axis=1, keepdims=True)
                acc_scr[c] = corr * acc_scr[c] + jnp.dot(p.astype(BF16), vb, preferred_element_type=F32)
                m_scr[c] = m_new

    def far_body(kb, carry):
        step(pl.multiple_of(kb * tq, tq), None, False)
        return carry

    lax.fori_loop(0, jnp.maximum(qi - 1, 0), far_body, 0)

    @pl.when(qi >= 1)
    def _():
        step(pl.multiple_of((qi - 1) * tq, tq), 1, False)

    step(pl.multiple_of(qi * tq, tq), 0, True)

    for hh in range(hps):
        c = 2 * hh
        o = acc_scr[c] / l_scr[c] - lam_ref[0] * (acc_scr[c + 1] / l_scr[c + 1])
        ms = jnp.mean(o * o, axis=1, keepdims=True)
        o_ref[0, :, hh * LANES:(hh + 1) * LANES] = (
            o * lax.rsqrt(ms + LN_EPS) * g_ref[...] * post_scale).astype(o_ref.dtype)


def diff_attention(qkv, bias_tiles, far_bias, lam, subln_g, lambda_init, tq, hps):
    bsz, s, _ = qkv.shape
    kern = functools.partial(_attn_kernel, tq=tq, hps=hps, scale=DIFF_HEAD_DIM ** -0.5 * LOG2_E,
                             post_scale=1.0 - lambda_init)
    hw = hps * LANES
    nqk = DIFF_QK_WIDTH // hw
    return pl.pallas_call(
        kern,
        grid=(bsz, DIFF_HEADS // hps, s // tq),
        in_specs=[
            pl.BlockSpec(memory_space=pltpu.SMEM),
            pl.BlockSpec(memory_space=pltpu.SMEM),
            pl.BlockSpec((1, tq, hw), lambda b, h, i: (b, i, h)),
            pl.BlockSpec((1, s, hw), lambda b, h, i: (b, 0, nqk + h)),
            pl.BlockSpec((1, s, hw), lambda b, h, i: (b, 0, 2 * nqk + h)),
            pl.BlockSpec((hps, 2, tq, tq), lambda b, h, i: (h, 0, 0, 0)),
            pl.BlockSpec((1, LANES), lambda b, h, i: (0, 0)),
        ],
        out_specs=pl.BlockSpec((1, tq, hw), lambda b, h, i: (b, i, h)),
        out_shape=jax.ShapeDtypeStruct((bsz, s, DIFF_WIDTH), BF16),
        scratch_shapes=[pltpu.VMEM((2 * hps, tq, LANES), F32), pltpu.VMEM((2 * hps, tq, LANES), F32),
                        pltpu.VMEM((2 * hps, tq, LANES), F32)],
        compiler_params=_cparams(("parallel", "parallel", "arbitrary")),
        name="diff_attention",
    )(lam, far_bias, qkv, qkv, qkv, bias_tiles, subln_g.reshape(1, LANES))


def _rwkv_prep_kernel(x_ref, prev_ref, mu_ref, w0_ref, w2_ref, a0_ref, a2_ref, g2_ref,
                      kk_ref, ka_ref, rk_ref,
                      r_out, k_out, v_out, lw_out, kn_out, b_out, g_out, bonus_out, *, tm, seq):
    i = pl.program_id(0)
    x = x_ref[...]
    w = RWKV_WIDTH
    row = lax.broadcasted_iota(jnp.int32, x.shape, 0)
    is_start = (i * tm) % seq == 0
    last_prev = jnp.where(is_start, 0.0, prev_ref[7:8, :])
    prev = jnp.where(row == 0, last_prev, pltpu.roll(x, 1, 0))
    xs = x + (prev - x) * mu_ref[...]

    r = xs[:, 0:w]
    k = xs[:, w:2 * w]
    v = xs[:, 2 * w:3 * w]
    xwa = xs[:, 3 * w:3 * w + 2 * LORA_DECAY]
    xg = xs[:, 3 * w + LANES:3 * w + LANES + 2 * LANES]

    z = w0_ref[...] + _bdot(jnp.tanh(xwa), w2_ref[...])
    softplus_neg_z = jnp.maximum(-z, 0.0) + jnp.log(1.0 + jnp.exp(-jnp.abs(z)))
    lw_out[...] = -jnp.exp(-softplus_neg_z - 0.5)
    a_lr = _sigmoid(a0_ref[...] + _bdot(xwa, a2_ref[...]))
    g_out[...] = _bdot(_sigmoid(xg), g2_ref[...])

    kk = k * kk_ref[...]
    kn = kk / jnp.maximum(jnp.sqrt(_head_sum(kk * kk)), 1e-12)
    k2 = k * (1.0 + (a_lr - 1.0) * ka_ref[...])
    r_out[...] = r
    k_out[...] = k2
    v_out[...] = v
    kn_out[...] = kn
    b_out[...] = kn * a_lr
    bonus_out[...] = _head_sum(r * k2 * rk_ref[...]) * v


def rwkv_prep(pr, mu, w0, w2p, a0, a2p, g2p, k_k, k_a, r_k, seq, tm):
    t, wp = pr.shape
    w = RWKV_WIDTH
    row = lambda a: a.reshape(1, -1)
    full = lambda shp: pl.BlockSpec(shp, lambda i: (0, 0))
    outs = [jax.ShapeDtypeStruct((t, w), F32)] * 8
    kern = functools.partial(_rwkv_prep_kernel, tm=tm, seq=seq)
    return pl.pallas_call(
        kern,
        grid=(t // tm,),
        in_specs=[
            pl.BlockSpec((tm, wp), lambda i: (i, 0)),
            pl.BlockSpec((8, wp), lambda i: (jnp.maximum(i * (tm // 8) - 1, 0), 0)),
            full((1, wp)), full((1, w)), full((LANES, w)), full((1, w)), full((LANES, w)),
            full((2 * LANES, w)), full((1, w)), full((1, w)), full((1, w)),
        ],
        out_specs=[pl.BlockSpec((tm, w), lambda i: (i, 0))] * 8,
        out_shape=outs,
        compiler_params=_cparams(("parallel",)),
        name="rwkv_prep",
    )(pr, pr, row(mu), row(w0), w2p, row(a0), a2p, g2p, row(k_k), row(k_a), row(r_k))


def _rwkv_scan_kernel(r_ref, k_ref, v_ref, lw_ref, kn_ref, b_ref, y_ref, s_scr, *, chunk, pairs):
    c2 = 2 * chunk

    @pl.when(pl.program_id(2) == 0)
    def _():
        s_scr[...] = jnp.zeros(s_scr.shape, F32)

    ri = lax.broadcasted_iota(jnp.int32, (chunk, chunk), 0)
    ci = lax.broadcasted_iota(jnp.int32, (chunk, chunk), 1)
    ltri = jnp.where(ri >= ci, 1.0, 0.0).astype(BF16)

    lw = lw_ref[0]
    cl = _dot_sel_x(ltri, lw)
    mid = chunk // 2 - 1
    clm = cl[mid:mid + 1, :]
    cle = cl[chunk - 1:chunk, :]
    w_mid = jnp.exp(cl - clm)
    w_mid_prev = jnp.exp(cl - lw - clm)
    w_mid_inv = jnp.exp(clm - cl)
    w_abs = jnp.exp(cl)
    w_abs_prev = jnp.exp(cl - lw)
    w_end = jnp.exp(cle - cl)
    w_tot = jnp.exp(cle)

    r = r_ref[0]
    k = k_ref[0]
    v = v_ref[0]
    a = -kn_ref[0]
    b = b_ref[0]
    a_mid = a * w_mid_prev
    r_mid = r * w_mid
    b_mid = b * w_mid_inv
    k_mid = k * w_mid_inv
    a_abs = a * w_abs_prev
    r_abs = r * w_abs
    b_end = b * w_end
    k_end = k * w_end

    lane = lax.broadcasted_iota(jnp.int32, (chunk, LANES), 1)
    first = lane < RWKV_HEAD_SIZE

    def stack(x):
        return jnp.concatenate([jnp.where(first, x, 0.0), jnp.where(first, 0.0, x)], axis=0)

    rr = lax.broadcasted_iota(jnp.int32, (c2, c2), 0)
    cc = lax.broadcasted_iota(jnp.int32, (c2, c2), 1)
    same = (rr >= chunk) == (cc >= chunk)
    strict = same & (rr > cc)
    incl = same & (rr >= cc)
    eye = jnp.where(rr == cc, 1.0, 0.0)

    ps = range(pairs)
    sls = [slice(p * LANES, (p + 1) * LANES) for p in ps]
    bf = lambda x: x.astype(BF16)
    cat0 = lambda xs: jnp.concatenate(xs, axis=0)
    cat1 = lambda xs: jnp.concatenate(xs, axis=1)
    v_s = [bf(stack(v[:, sl])) for sl in sls]
    ar_m = [bf(cat0([stack(a_mid[:, sl]), stack(r_mid[:, sl])])) for sl in sls]
    bk_m = [bf(cat0([stack(b_mid[:, sl]), stack(k_mid[:, sl])])) for sl in sls]
    g_all = [_bdot_nt(ar_m[p], bk_m[p]) for p in ps]
    l_ab = [jnp.where(strict, g[:c2, :c2], 0.0) for g in g_all]
    l_ak = [bf(jnp.where(strict, g[:c2, c2:], 0.0)) for g in g_all]
    m_r = [bf(cat1([jnp.where(incl, g[c2:, :c2], 0.0), jnp.where(incl, g[c2:, c2:], 0.0)])) for g in g_all]

    t_inv = [eye + l_ab[p] for p in ps]
    n_fac = max(chunk.bit_length() - 2, 0)
    lp = [_bdot(l_ab[p], l_ab[p]) for p in ps] if n_fac else l_ab
    for _ in range(n_fac - 1):
        x = [_bdot(cat0([t_inv[p], lp[p]]), lp[p]) for p in ps]
        t_inv = [t_inv[p] + x[p][:c2] for p in ps]
        lp = [x[p][c2:] for p in ps]
    if n_fac:
        t_inv = [t_inv[p] + _bdot(t_inv[p], lp[p]) for p in ps]
    t_inv = [bf(t) for t in t_inv]

    lv = [_bdot(l_ak[p], v_s[p]) for p in ps]
    av = [bf(_bdot(t_inv[p], cat1([stack(a_abs[:, sls[p]]), lv[p]]))) for p in ps]
    zero = jnp.zeros((c2, LANES), BF16)
    ry = [_bdot(m_r[p], cat0([av[p], cat1([zero, v_s[p]])])) for p in ps]
    r_p = [stack(r_abs[:, sls[p]]) + ry[p][:, :LANES] for p in ps]

    s_old = [s_scr[p] for p in ps]
    y_s = [_bdot_nt(r_p[p], s_old[p]) + ry[p][:, LANES:] for p in ps]
    for p in ps:
        y_ref[0, :, sls[p]] = y_s[p][:chunk] + y_s[p][chunk:]
    b_e = [bf(stack(b_end[:, sl])) for sl in sls]
    k_e = [bf(stack(k_end[:, sl])) for sl in sls]
    ab = [_bdot_tn(av[p][:, :LANES], b_e[p]) for p in ps]
    q_t = [_bdot_tn(cat0([av[p][:, LANES:], v_s[p]]), cat0([b_e[p], k_e[p]])) for p in ps]
    for p in ps:
        s_scr[p] = s_old[p] * w_tot[:, sls[p]] + _bdot(s_old[p], ab[p]) + q_t[p]


def rwkv_scan(r, k, v, lw, kn, b, chunk, pairs):
    bsz, s, w = r.shape
    pw = pairs * LANES
    spec = pl.BlockSpec((1, chunk, pw), lambda bi, hi, ci: (bi, ci, hi))
    kern = functools.partial(_rwkv_scan_kernel, chunk=chunk, pairs=pairs)
    return pl.pallas_call(
        kern,
        grid=(bsz, w // pw, s // chunk),
        in_specs=[spec] * 6,
        out_specs=spec,
        out_shape=jax.ShapeDtypeStruct((bsz, s, w), F32),
        scratch_shapes=[pltpu.VMEM((pairs, LANES, LANES), F32)],
        compiler_params=_cparams(("parallel", "parallel", "arbitrary")),
        name="rwkv_scan",
    )(r, k, v, lw, kn, b)


def _rwkv_out(y, bonus, g, lnx_g, lnx_b):
    inv_n = 1.0 / RWKV_HEAD_SIZE
    mu = _head_sum(y) * inv_n
    yc = y - mu
    var = _head_sum(yc * yc) * inv_n
    yn = yc * lax.rsqrt(var + RWKV_GN_EPS) * lnx_g + lnx_b
    return (yn + bonus) * g


def _route(logits):
    lane = lax.broadcasted_iota(jnp.int32, logits.shape, 1)
    big = jnp.int32(4 * LANES)
    gmask = lane < N_GROUPS
    gl = jnp.where(gmask, logits, NEG_INF)
    gmax = jnp.max(gl, axis=1, keepdims=True)
    gidx = jnp.min(jnp.where(gl == gmax, lane, big), axis=1, keepdims=True)
    gtop = 1.0 / jnp.sum(jnp.where(gmask, jnp.exp(gl - gmax), 0.0), axis=1, keepdims=True)
    lo = N_GROUPS + EXPERTS_PER_GROUP * gidx
    emask = (lane >= lo) & (lane < lo + EXPERTS_PER_GROUP)
    el = jnp.where(emask, logits, NEG_INF)
    e1 = jnp.max(el, axis=1, keepdims=True)
    i1 = jnp.min(jnp.where(el == e1, lane, big), axis=1, keepdims=True)
    el2 = jnp.where(lane == i1, NEG_INF, el)
    e2 = jnp.max(el2, axis=1, keepdims=True)
    i2 = jnp.min(jnp.where(el2 == e2, lane, big), axis=1, keepdims=True)
    t = jnp.exp(e2 - e1)
    w1 = gtop / (1.0 + t)
    w2 = gtop * t / (1.0 + t)
    id1 = (i1 - N_GROUPS).astype(F32)
    id2 = (i2 - N_GROUPS).astype(F32)
    return jnp.where(lane == 0, id1, jnp.where(lane == 1, id2, jnp.where(lane == 2, w1,
                                                                         jnp.where(lane == 3, w2, 0.0))))


def _outproj_kernel(yd_ref, ys_ref, bonus_ref, gate_ref, lg_ref, lb_ref, h_ref,
                    w1_ref, w2_ref, g_ref, b_ref, rw_hi_ref, rw_lo_ref, rb_ref, h1_ref, info_ref, *, parts):
    rows = h_ref.shape[0] // parts
    sls = [pl.ds(k * rows, rows) for k in range(parts)]
    d = lambda a, b: jnp.dot(a, b, preferred_element_type=F32)
    yr = [_rwkv_out(ys_ref[sl, :], bonus_ref[sl, :], gate_ref[sl, :], lg_ref[...], lb_ref[...]) for sl in sls]
    mix = [d(yd_ref[sl, :], w1_ref[...]) + d(yr[k].astype(BF16), w2_ref[...]) for k, sl in enumerate(sls)]
    h1 = [_layer_norm(DEEPNORM_ALPHA * h_ref[sl, :] + mix[k], g_ref[...], b_ref[...]) for k, sl in enumerate(sls)]
    for k, sl in enumerate(sls):
        h1_ref[sl, :] = h1[k]
        hi = h1[k].astype(BF16)
        lo = (h1[k] - hi.astype(F32)).astype(BF16)
        logits = d(hi, rw_hi_ref[...]) + d(lo, rw_hi_ref[...]) + d(hi, rw_lo_ref[...]) + rb_ref[...]
        info_ref[sl, :] = _route(logits)


def outproj_ln_route(yd, ys, bonus, gate, lnx_g, lnx_b, h, w_out_a, w_out_b, g, b, rw_hi, rw_lo, rb, tm):
    t, d = h.shape
    ka = yd.shape[1]
    kb = ys.shape[1]
    full = lambda shp: pl.BlockSpec(shp, lambda i: (0, 0))
    tile = lambda n: pl.BlockSpec((tm, n), lambda i: (i, 0))
    return pl.pallas_call(
        functools.partial(_outproj_kernel, parts=2),
        grid=(t // tm,),
        in_specs=[
            tile(ka), tile(kb), tile(kb), tile(kb), full((1, kb)), full((1, kb)), tile(d),
            full((ka, d)), full((kb, d)), full((1, d)), full((1, d)),
            full((d, ROUTER_LANES)), full((d, ROUTER_LANES)), full((1, ROUTER_LANES)),
        ],
        out_specs=[
            pl.BlockSpec((tm, d), lambda i: (i, 0)),
            pl.BlockSpec((tm, ROUTER_LANES), lambda i: (i, 0)),
        ],
        out_shape=[jax.ShapeDtypeStruct((t, d), F32), jax.ShapeDtypeStruct((t, ROUTER_LANES), F32)],
        compiler_params=pltpu.CompilerParams(dimension_semantics=("parallel",),
                                             vmem_limit_bytes=MOE_VMEM_LIMIT),
        name="outproj_ln_route",
    )(yd, ys, bonus, gate, lnx_g.reshape(1, kb), lnx_b.reshape(1, kb), h, w_out_a, w_out_b,
      g.reshape(1, d), b.reshape(1, d), rw_hi, rw_lo, rb)


def _moe_rank_kernel(e_ref, pos_ref, ends_ref, run_scr, start_scr, *, tm, row_tile):
    ph = pl.program_id(0)
    i = pl.program_id(1)
    lane = lax.broadcasted_iota(jnp.int32, (tm, LANES), 1)
    onehot = e_ref[...] == lane
    oh = jnp.where(onehot, 1.0, 0.0)

    @pl.when((ph == 0) & (i == 0))
    def _():
        run_scr[...] = jnp.zeros(run_scr.shape, F32)

    @pl.when(ph == 0)
    def _():
        run_scr[...] += jnp.sum(oh, axis=0, keepdims=True)

    @pl.when((ph == 1) & (i == 0))
    def _():
        padded = jnp.floor((run_scr[...] + (row_tile - 1)) * (1.0 / row_tile)) * row_tile
        ri = lax.broadcasted_iota(jnp.int32, (LANES, LANES), 0)
        ci = lax.broadcasted_iota(jnp.int32, (LANES, LANES), 1)
        upper = jnp.where(ri < ci, 1.0, 0.0).astype(BF16)
        parts = _split3(jnp.broadcast_to(padded, (8, LANES)))
        starts = sum(jnp.dot(q, upper, preferred_element_type=F32) for q in parts)[0:1]
        start_scr[...] = starts
        ends_ref[...] = starts + padded
        run_scr[...] = jnp.zeros(run_scr.shape, F32)

    @pl.when(ph == 1)
    def _():
        ri = lax.broadcasted_iota(jnp.int32, (tm, tm), 0)
        ci = lax.broadcasted_iota(jnp.int32, (tm, tm), 1)
        before = jnp.where(ri > ci, 1.0, 0.0).astype(BF16)
        row = jnp.dot(before, oh.astype(BF16), preferred_element_type=F32) + (run_scr[...] + start_scr[...])
        pos_ref[...] = jnp.sum(jnp.where(onehot, row, 0.0), axis=1, keepdims=True).astype(jnp.int32)
        run_scr[...] += jnp.sum(oh, axis=0, keepdims=True)


def moe_rank(e_col, tm, row_tile):
    n = e_col.shape[0]
    return pl.pallas_call(
        functools.partial(_moe_rank_kernel, tm=tm, row_tile=row_tile),
        grid=(2, n // tm),
        in_specs=[pl.BlockSpec((tm, 1), lambda ph, i: (i, 0))],
        out_specs=[pl.BlockSpec((tm, 1), lambda ph, i: (i * ph, 0)),
                   pl.BlockSpec((1, LANES), lambda ph, i: (0, 0))],
        out_shape=[jax.ShapeDtypeStruct((n, 1), jnp.int32), jax.ShapeDtypeStruct((1, LANES), F32)],
        scratch_shapes=[pltpu.VMEM((1, LANES), F32), pltpu.VMEM((1, LANES), F32)],
        compiler_params=_cparams(("arbitrary", "arbitrary")),
        name="moe_rank",
    )(e_col)


def _row_copy(src_hbm, row, dst, r, sem):
    return pltpu.make_async_copy(src_hbm.at[pl.ds(row, 1)], dst.at[pl.ds(r, 1)], sem)


def _pack_bf16_halves(y):
    n = y.shape[1] // 2
    hi = pltpu.bitcast(y[:, :n].astype(BF16).astype(F32), jnp.uint32)
    lo = pltpu.bitcast(y[:, n:].astype(BF16).astype(F32), jnp.uint32)
    return hi | (lo >> 16)


def _unpack_bf16_halves(w):
    hi = pltpu.bitcast(w & jnp.uint32(0xFFFF0000), F32)
    lo = pltpu.bitcast(w << 16, F32)
    return jnp.concatenate([hi, lo], axis=1)


def _moe_expert_kernel(te_ref, first_ref, par_ref, nxt_ref, nact_ref,
                       src0_ref, src1_ref, src2_ref, src_ahead_ref, x_hbm, wg_hbm, wu_hbm, wd_hbm, o_ref,
                       xbuf0, xbuf1, xbuf2, xbuf3, wgf, wuf, wdf, wgb, wub, wdb, sem, wsem, *, tm):
    j = pl.program_id(0)
    n_act = nact_ref[0]
    bufs = (xbuf0, xbuf1, xbuf2, xbuf3)
    nb = len(bufs)

    def token_copy(tok, into, r):
        return pltpu.make_async_copy(x_hbm.at[tok], bufs[into].at[:, r, :], sem.at[into])

    def start_gather(src, into):
        for r in range(tm):
            token_copy(src[0, 0, r], into, r).start(priority=r % 2)

    def wait_gather(into):
        for r in range(tm):
            token_copy(0, into, r).wait()

    def weight_copies(e, par):
        return (pltpu.make_async_copy(wg_hbm.at[e], wgf.at[par], wsem.at[par]),
                pltpu.make_async_copy(wu_hbm.at[e], wuf.at[par], wsem.at[par]),
                pltpu.make_async_copy(wd_hbm.at[e], wdf.at[par], wsem.at[par]))

    @pl.when(j == 0)
    def _():
        start_gather(src0_ref, 0)
        start_gather(src1_ref, 1)
        start_gather(src2_ref, 2)
        for c in weight_copies(te_ref[0], 0):
            c.start(priority=1)

    @pl.when((j < n_act) & (first_ref[j] > 0))
    def _():
        par = par_ref[j]
        for c in weight_copies(te_ref[j], par):
            c.wait()
        wgb[...] = wgf[par].astype(BF16)
        wub[...] = wuf[par].astype(BF16)
        wdb[...] = wdf[par].astype(BF16)

    for slot in range(nb):
        @pl.when((j < n_act) & (j % nb == slot))
        def _():
            wait_gather(slot)
            start_gather(src_ahead_ref, (slot + nb - 1) % nb)
            x = jnp.concatenate([bufs[slot][c] for c in range(bufs[slot].shape[0])], axis=1).astype(BF16)
            hg = jnp.dot(x, wgb[...], preferred_element_type=F32)
            hu = jnp.dot(x, wub[...], preferred_element_type=F32)
            hid = hg * _sigmoid(hg) * hu
            o_ref[...] = _pack_bf16_halves(jnp.dot(hid.astype(BF16), wdb[...], preferred_element_type=F32))

        @pl.when((j == n_act - 1) & (j % nb == slot))
        def _():
            for ahead in range(1, nb):
                wait_gather((slot + ahead) % nb)

    @pl.when((j < n_act) & (first_ref[j] > 0) & (nxt_ref[j] >= 0))
    def _():
        for c in weight_copies(nxt_ref[j], 1 - par_ref[j]):
            c.start(priority=1)

    @pl.when(j >= n_act)
    def _():
        o_ref[...] = jnp.zeros(o_ref.shape, jnp.uint32)


def moe_experts(src_tok, plan, x, wg, wu, wd, n_rows, tm):
    d, f = wg.shape[1:]
    hbm = pl.BlockSpec(memory_space=pl.ANY)
    src3 = src_tok.reshape(n_rows // tm, 1, tm)
    smem_tile = lambda index_map: pl.BlockSpec((1, 1, tm), index_map, memory_space=pltpu.SMEM)
    return pl.pallas_call(
        functools.partial(_moe_expert_kernel, tm=tm),
        grid_spec=pltpu.PrefetchScalarGridSpec(
            num_scalar_prefetch=5,
            grid=(n_rows // tm,),
            in_specs=[smem_tile(lambda j, *_: (0, 0, 0)),
                      smem_tile(lambda j, te, fi, pa, nx, na: (jnp.minimum(1, na[0] - 1), 0, 0)),
                      smem_tile(lambda j, te, fi, pa, nx, na: (jnp.minimum(2, na[0] - 1), 0, 0)),
                      smem_tile(lambda j, te, fi, pa, nx, na: (jnp.minimum(j + 3, na[0] - 1), 0, 0)),
                      hbm, hbm, hbm, hbm],
            out_specs=pl.BlockSpec((tm, d // 2), lambda j, *_: (j, 0)),
            scratch_shapes=[pltpu.VMEM((d // LANES, tm, LANES), F32)] * 4 + [
                            pltpu.VMEM((2, d, f), F32), pltpu.VMEM((2, d, f), F32), pltpu.VMEM((2, f, d), F32),
                            pltpu.VMEM((d, f), BF16), pltpu.VMEM((d, f), BF16), pltpu.VMEM((f, d), BF16),
                            pltpu.SemaphoreType.DMA((4,)), pltpu.SemaphoreType.DMA((2,))],
        ),
        out_shape=jax.ShapeDtypeStruct((n_rows, d // 2), jnp.uint32),
        compiler_params=pltpu.CompilerParams(dimension_semantics=("arbitrary",),
                                             vmem_limit_bytes=MOE_VMEM_LIMIT),
        name="moe_experts",
    )(*plan, src3, src3, src3, src3, x, wg, wu, wd)


def _final_kernel(p1_ref, p2_ref, y_hbm, h1_ref, info_ref, p_ref, g_ref, b_ref, pw_ref, pg_ref, gw_ref,
                  o_ref, ybuf0, ybuf1, sem, *, tm, n_tiles, parts):
    i = pl.program_id(0)
    bufs = (ybuf0, ybuf1)

    def start_gather(tile, into):
        for r in range(tm):
            _row_copy(y_hbm, p1_ref[tile * tm + r], bufs[into].at[0], r, sem.at[into]).start(priority=0)
            _row_copy(y_hbm, p2_ref[tile * tm + r], bufs[into].at[1], r, sem.at[into]).start(priority=1)

    def wait_gather(into):
        for r in range(tm):
            _row_copy(y_hbm, 0, bufs[into].at[0], r, sem.at[into]).wait()
            _row_copy(y_hbm, 0, bufs[into].at[1], r, sem.at[into]).wait()

    @pl.when(i == 0)
    def _():
        start_gather(0, 0)

    for slot in range(2):
        @pl.when(i % 2 == slot)
        def _():
            wait_gather(slot)
            start_gather(jnp.minimum(i + 1, n_tiles - 1), 1 - slot)
            rows = tm // parts
            sls = [pl.ds(k * rows, rows) for k in range(parts)]
            h2, ple = [], []
            for sl in sls:
                info = info_ref[sl, :]
                ffn = (info[:, 2:3] * _unpack_bf16_halves(bufs[slot][0, sl, :])
                       + info[:, 3:4] * _unpack_bf16_halves(bufs[slot][1, sl, :]))
                h2.append(_layer_norm(DEEPNORM_ALPHA * h1_ref[sl, :] + ffn, g_ref[...], b_ref[...]))
                pe = jnp.dot(p_ref[sl, :].astype(BF16), pw_ref[...], preferred_element_type=F32)
                ple.append(pe * lax.rsqrt(jnp.mean(pe * pe, axis=-1, keepdims=True) + LN_EPS) * pg_ref[...])
            gate = [jnp.dot(h.astype(BF16), gw_ref[...], preferred_element_type=F32) for h in h2]
            for k, sl in enumerate(sls):
                o_ref[sl, :] = h2[k] + _sigmoid(gate[k]) * ple[k]

        @pl.when((i == n_tiles - 1) & (i % 2 == slot))
        def _():
            wait_gather(1 - slot)


def final_block(pos1, pos2, ys, h1, info, p, g, b, ple_w, ple_g, gate_w, tm):
    t, d = h1.shape
    dp = p.shape[1]
    tile = lambda n: pl.BlockSpec((tm, n), lambda i, p1, p2: (i, 0))
    full = lambda shp: pl.BlockSpec(shp, lambda i, p1, p2: (0, 0))
    return pl.pallas_call(
        functools.partial(_final_kernel, tm=tm, n_tiles=t // tm, parts=2),
        grid_spec=pltpu.PrefetchScalarGridSpec(
            num_scalar_prefetch=2,
            grid=(t // tm,),
            in_specs=[pl.BlockSpec(memory_space=pl.ANY), tile(d), tile(ROUTER_LANES), tile(dp),
                      full((1, d)), full((1, d)), full((dp, d)), full((1, d)), full((d, d))],
            out_specs=tile(d),
            scratch_shapes=[pltpu.VMEM((2, tm, d // 2), jnp.uint32), pltpu.VMEM((2, tm, d // 2), jnp.uint32),
                            pltpu.SemaphoreType.DMA((2,))],
        ),
        out_shape=jax.ShapeDtypeStruct((t, d), F32),
        compiler_params=pltpu.CompilerParams(dimension_semantics=("arbitrary",),
                                             vmem_limit_bytes=MOE_VMEM_LIMIT),
        name="final_block",
    )(pos1, pos2, ys, h1, info, p, g.reshape(1, d), b.reshape(1, d), ple_w, ple_g.reshape(1, d), gate_w)


def _dispatch_plan(pos, ends_f, n_tok, tm, n_rows):
    n_tiles = n_rows // tm
    ends = ends_f[0, :N_EXPERTS].astype(jnp.int32)
    src_tok = jnp.zeros((n_rows,), jnp.int32).at[pos].set(
        jnp.arange(2 * n_tok, dtype=jnp.int32) % n_tok, unique_indices=True)
    n_active = ends[-1] // tm
    tile_ids = jnp.arange(n_tiles, dtype=jnp.int32)
    tile_expert = jnp.searchsorted(ends, jnp.minimum(tile_ids, n_active - 1) * tm, side="right")
    tile_expert = jnp.minimum(tile_expert, N_EXPERTS - 1).astype(jnp.int32)
    first = jnp.concatenate([jnp.ones((1,), jnp.int32),
                             (tile_expert[1:] != tile_expert[:-1]).astype(jnp.int32)])
    parity = (jnp.cumsum(first) - 1) % 2
    ids = jnp.arange(N_EXPERTS, dtype=jnp.int32)
    present = ends > jnp.concatenate([jnp.zeros((1,), jnp.int32), ends[:-1]])
    later = present[None, :] & (ids[None, :] > ids[:, None])
    next_of = jnp.min(jnp.where(later, ids[None, :], N_EXPERTS), axis=1)
    next_of = jnp.where(next_of == N_EXPERTS, -1, next_of)
    plan = (tile_expert, first, parity.astype(jnp.int32), next_of[tile_expert].astype(jnp.int32),
            n_active.astype(jnp.int32).reshape(1))
    return src_tok, plan


def _t5_bucket(rel):
    n = jnp.maximum(rel, 0)
    max_exact = REL_BUCKETS // 2
    nf = jnp.maximum(n, 1).astype(F32)
    large = max_exact + (jnp.log(nf / max_exact) / math.log(REL_MAX_DIST / max_exact)
                         * (REL_BUCKETS - max_exact)).astype(jnp.int32)
    large = jnp.minimum(large, REL_BUCKETS - 1)
    return jnp.where(n < max_exact, n, large)


def _bias_tables(rel_bias, tq):
    n = 3 * tq
    rel = 2 * tq - 1 - jnp.arange(n, dtype=jnp.int32)
    g = rel_bias[_t5_bucket(rel)].astype(F32).T
    skew = jnp.tile(g, (1, tq))[:, :tq * (n - 1)].reshape(-1, tq, n - 1)
    tiles = jnp.stack([skew[:, :, 2 * tq - 1:3 * tq - 1], skew[:, :, tq - 1:2 * tq - 1]], axis=1)
    far = rel_bias[_t5_bucket(jnp.full((), 2 * tq, jnp.int32))].astype(F32)
    return tiles * LOG2_E, far * LOG2_E


def kernel(x, p, ln_in_g, ln_in_b, rel_bias, w_in, diff_lam_q1, diff_lam_k1, diff_lam_q2, diff_lam_k2, diff_subln_g, rwkv_mu, rwkv_w0, rwkv_w2, rwkv_a0, rwkv_a2, rwkv_g2, rwkv_k_k, rwkv_k_a, rwkv_r_k, rwkv_lnx_g, rwkv_lnx_b, w_out, ln1_g, ln1_b, router_group_w, router_group_b, router_expert_w, router_expert_b, moe_w_gate, moe_w_up, moe_w_down, ln2_g, ln2_b, ple_w, ple_norm_g, ple_gate_w):
    bsz, seq, d = x.shape
    t = bsz * seq
    i = 0
    lambda_init = 0.8 - 0.6 * math.exp(-0.3 * i)
    tq = 256

    x2 = x.reshape(t, d)
    w_in_t = jnp.swapaxes(w_in[i], 0, 1)
    h, hb = layer_norm_rows(x2, ln_in_g, ln_in_b, 512)
    qkv = matmul_nt(hb, w_in_t, 0, O_RWKV, O_RWKV, BF16, 1024, 512)
    pr = matmul_nt(hb, w_in_t, O_RWKV, RWKV_PROJ, RWKV_PROJ_PAD, F32, 1024, 512)

    lam = (jnp.exp(jnp.sum(diff_lam_q1[i].astype(F32) * diff_lam_k1[i]))
           - jnp.exp(jnp.sum(diff_lam_q2[i].astype(F32) * diff_lam_k2[i])) + lambda_init)
    bias_tiles, far_bias = _bias_tables(rel_bias, tq)
    y_diff = diff_attention(qkv.reshape(bsz, seq, O_RWKV), bias_tiles, far_bias, lam.reshape(1),
                            diff_subln_g[i], lambda_init, tq, 8)

    mu_p = jnp.pad(rwkv_mu[i], (0, RWKV_PROJ_PAD - RWKV_PROJ))
    zeros64 = jnp.zeros((LORA_DECAY, RWKV_WIDTH), F32)
    w2p = jnp.concatenate([rwkv_w2[i], zeros64], axis=0).astype(BF16)
    a2p = jnp.concatenate([zeros64, rwkv_a2[i]], axis=0).astype(BF16)
    g2p = jnp.pad(rwkv_g2[i], ((0, 2 * LANES - LORA_GATE), (0, 0))).astype(BF16)
    r, k2, v, lw, kn, bb, g, bonus = rwkv_prep(
        pr, mu_p, rwkv_w0[i], w2p, rwkv_a0[i], a2p, g2p, rwkv_k_k[i], rwkv_k_a[i],
        rwkv_r_k[i].reshape(-1), seq, 256)
    sh = lambda a: a.reshape(bsz, seq, RWKV_WIDTH)
    y_scan = rwkv_scan(sh(r), sh(k2), sh(v), sh(lw), sh(kn), sh(bb), 64, 8)

    w_out_b = w_out[i].astype(BF16)
    rw = jnp.concatenate([router_group_w[i], router_expert_w[i]], axis=1)
    rw = jnp.pad(rw, ((0, 0), (0, ROUTER_LANES - rw.shape[1])))
    rw_hi = rw.astype(BF16)
    rw_lo = (rw - rw_hi.astype(F32)).astype(BF16)
    rb = jnp.pad(jnp.concatenate([router_group_b[i], router_expert_b[i]]),
                 (0, ROUTER_LANES - N_GROUPS - N_EXPERTS)).reshape(1, ROUTER_LANES)
    h1, info = outproj_ln_route(y_diff.reshape(t, DIFF_WIDTH), y_scan.reshape(t, RWKV_WIDTH), bonus, g,
                                rwkv_lnx_g[i], rwkv_lnx_b[i], h, w_out_b[:DIFF_WIDTH],
                                w_out_b[DIFF_WIDTH:], ln1_g[i], ln1_b[i], rw_hi, rw_lo, rb, 512)

    tm_e = MOE_ROW_TILE
    n_rows = 2 * t + N_EXPERTS * tm_e
    e_all = jnp.concatenate([info[:, 0], info[:, 1]]).astype(jnp.int32)
    pos, ends = moe_rank(e_all.reshape(-1, 1), 1024, tm_e)
    pos = pos[:, 0]
    src_tok, plan = _dispatch_plan(pos, ends, t, tm_e, n_rows)
    f = moe_w_gate.shape[-1]
    ys = moe_experts(src_tok, plan, h1.reshape(t, d // LANES, LANES), moe_w_gate[i].reshape(N_EXPERTS, d, f),
                     moe_w_up[i].reshape(N_EXPERTS, d, f), moe_w_down[i].reshape(N_EXPERTS, f, d),
                     n_rows, tm_e)

    out = final_block(pos[:t], pos[t:], ys, h1, info, p[i].reshape(t, -1), ln2_g[i], ln2_b[i],
                      ple_w[i].astype(BF16), ple_norm_g[i], ple_gate_w[i].astype(BF16), 256)
    return out.reshape(bsz, seq, d)
```

```python
import functools
import math

import jax
import jax.numpy as jnp
from jax import lax
from jax.experimental import pallas as pl
from jax.experimental.pallas import tpu as pltpu

F32 = jnp.float32
BF16 = jnp.bfloat16

DIFF_HEADS = 8
DIFF_HEAD_DIM = 64
DIFF_V_DIM = 128
DIFF_WIDTH = DIFF_HEADS * DIFF_V_DIM
DIFF_QK_WIDTH = DIFF_HEADS * 2 * DIFF_HEAD_DIM
RWKV_HEADS = 16
RWKV_HEAD_SIZE = 64
RWKV_WIDTH = RWKV_HEADS * RWKV_HEAD_SIZE
LORA_DECAY = 64
LORA_AAA = 64
LORA_GATE = 160
RWKV_PROJ = 3 * RWKV_WIDTH + LORA_DECAY + LORA_AAA + LORA_GATE
O_RWKV = 2 * DIFF_QK_WIDTH + DIFF_WIDTH
REL_BUCKETS = 32
REL_MAX_DIST = 128
N_GROUPS = 4
EXPERTS_PER_GROUP = 8
N_EXPERTS = N_GROUPS * EXPERTS_PER_GROUP
LN_EPS = 1e-5
RWKV_GN_EPS = 64e-5
NEG_INF = -1e30
DEPTH = 1
DEEPNORM_ALPHA = (2 * DEPTH) ** 0.25
LOG2_E = math.log2(math.e)

LANES = 128
RWKV_PROJ_PAD = 3584
ROUTER_LANES = LANES
VMEM_LIMIT = 48 * 1024 * 1024
MOE_VMEM_LIMIT = 56 * 1024 * 1024
MOE_ROW_TILE = 256


def _cparams(sem):
    return pltpu.CompilerParams(dimension_semantics=sem, vmem_limit_bytes=VMEM_LIMIT)


def _bdot(a, b):
    return jnp.dot(a.astype(BF16), b.astype(BF16), preferred_element_type=F32)


def _bdot_nt(a, b):
    return lax.dot_general(a.astype(BF16), b.astype(BF16), (((1,), (1,)), ((), ())),
                           preferred_element_type=F32)


def _bdot_tn(a, b):
    return lax.dot_general(a.astype(BF16), b.astype(BF16), (((0,), (0,)), ((), ())),
                           preferred_element_type=F32)


def _split3(x):
    hi = x.astype(BF16)
    r1 = x - hi.astype(F32)
    mid = r1.astype(BF16)
    lo = (r1 - mid.astype(F32)).astype(BF16)
    return hi, mid, lo


def _head_sum(x):
    lane = lax.broadcasted_iota(jnp.int32, (x.shape[0], LANES), 1)
    first = lane < RWKV_HEAD_SIZE
    cols = []
    for c in range(x.shape[1] // LANES):
        blk = x[:, c * LANES:(c + 1) * LANES]
        lo = jnp.sum(jnp.where(first, blk, 0.0), axis=1, keepdims=True)
        hi = jnp.sum(jnp.where(first, 0.0, blk), axis=1, keepdims=True)
        cols.append(jnp.where(first, lo, hi))
    return jnp.concatenate(cols, axis=1)


def _dot_sel_x(sel, x):
    hi, mid, lo = _split3(x)
    d = lambda p: jnp.dot(sel, p, preferred_element_type=F32)
    return d(hi) + d(mid) + d(lo)


def _layer_norm(x, g, b):
    mu = jnp.mean(x, axis=-1, keepdims=True)
    xc = x - mu
    var = jnp.mean(xc * xc, axis=-1, keepdims=True)
    return xc * lax.rsqrt(var + LN_EPS) * g + b


def _sigmoid(z):
    return 1.0 / (1.0 + jnp.exp(-z))


def _ln_kernel(x_ref, g_ref, b_ref, h_ref, hb_ref):
    xn = _layer_norm(x_ref[...], g_ref[...], b_ref[...])
    h_ref[...] = xn
    hb_ref[...] = xn.astype(BF16)


def layer_norm_rows(x, g, b, tm):
    m, d = x.shape
    tile = pl.BlockSpec((tm, d), lambda i: (i, 0))
    row = pl.BlockSpec((1, d), lambda i: (0, 0))
    return pl.pallas_call(
        _ln_kernel,
        grid=(m // tm,),
        in_specs=[tile, row, row],
        out_specs=[tile, tile],
        out_shape=[jax.ShapeDtypeStruct((m, d), F32), jax.ShapeDtypeStruct((m, d), BF16)],
        compiler_params=_cparams(("parallel",)),
        name="layer_norm_rows",
    )(x, g.reshape(1, d), b.reshape(1, d))


def _mm_nt_kernel(x_ref, wt_ref, o_ref, *, n_valid, tn):
    y = lax.dot_general(x_ref[...], wt_ref[...].astype(BF16), (((1,), (1,)), ((), ())),
                        preferred_element_type=F32)
    if n_valid % tn:
        col = pl.program_id(1) * tn + lax.broadcasted_iota(jnp.int32, y.shape, 1)
        y = jnp.where(col < n_valid, y, 0.0)
    o_ref[...] = y.astype(o_ref.dtype)


def matmul_nt(x, wt, row0, n_valid, n_out, out_dtype, tm, tn):
    m, k = x.shape
    blk0 = row0 // tn
    return pl.pallas_call(
        functools.partial(_mm_nt_kernel, n_valid=n_valid, tn=tn),
        grid=(m // tm, n_out // tn),
        in_specs=[pl.BlockSpec((tm, k), lambda i, j: (i, 0)),
                  pl.BlockSpec((tn, k), lambda i, j: (blk0 + j, 0))],
        out_specs=pl.BlockSpec((tm, tn), lambda i, j: (i, j)),
        out_shape=jax.ShapeDtypeStruct((m, n_out), out_dtype),
        compiler_params=_cparams(("parallel", "parallel")),
        name="matmul_nt",
    )(x, wt)


def _attn_kernel(lam_ref, far_ref, q_ref, k_ref, v_ref, bias_ref, g_ref, o_ref,
                 m_scr, l_scr, acc_scr, *, tq, hps, scale, post_scale):
    hp = pl.program_id(1)
    qi = pl.program_id(2)
    lane = lax.broadcasted_iota(jnp.int32, (tq, LANES), 1)
    q_maps = []
    for hh in range(hps):
        qs = (q_ref[0, :, hh * LANES:(hh + 1) * LANES].astype(F32) * scale).astype(BF16)
        zero = jnp.zeros_like(qs)
        q_maps += [jnp.where(lane < DIFF_HEAD_DIM, qs, zero), jnp.where(lane >= DIFF_HEAD_DIM, qs, zero)]

    m_scr[...] = jnp.full(m_scr.shape, NEG_INF, F32)
    l_scr[...] = jnp.zeros(l_scr.shape, F32)
    acc_scr[...] = jnp.zeros(acc_scr.shape, F32)

    def step(kstart, bias_index, causal):
        for hh in range(hps):
            kb = k_ref[0, pl.ds(kstart, tq), hh * LANES:(hh + 1) * LANES]
            vb = v_ref[0, pl.ds(kstart, tq), hh * LANES:(hh + 1) * LANES]
            for c in range(2 * hh, 2 * hh + 2):
                s = lax.dot_general(q_maps[c], kb, (((1,), (1,)), ((), ())), preferred_element_type=F32)
                m_old = m_scr[c]
                if bias_index is None:
                    far = far_ref[hp * hps + hh]
                    m_new = jnp.maximum(m_old, jnp.max(s, axis=1, keepdims=True) + far)
                    shift = m_new - far
                else:
                    s = s + bias_ref[hh, bias_index]
                    if causal:
                        ri = lax.broadcasted_iota(jnp.int32, (tq, tq), 0)
                        ci = lax.broadcasted_iota(jnp.int32, (tq, tq), 1)
                        s = jnp.where(ri >= ci, s, NEG_INF)
                    m_new = jnp.maximum(m_old, jnp.max(s, axis=1, keepdims=True))
                    shift = m_new
                p = jnp.exp2(s - jnp.concatenate([shift] * (tq // LANES), axis=1))
                corr = jnp.exp2(m_old - m_new)
                l_scr[c] = corr * l_scr[c] + jnp.sum(p, axis=1, keepdims=True)
                acc_scr[c] = corr * acc_scr[c] + jnp.dot(p.astype(BF16), vb, preferred_element_type=F32)
                m_scr[c] = m_new

    def far_body(kb, carry):
        step(pl.multiple_of(kb * tq, tq), None, False)
        return carry

    lax.fori_loop(0, jnp.maximum(qi - 1, 0), far_body, 0)

    @pl.when(qi >= 1)
    def _():
        step(pl.multiple_of((qi - 1) * tq, tq), 1, False)

    step(pl.multiple_of(qi * tq, tq), 0, True)

    for hh in range(hps):
        c = 2 * hh
        o = acc_scr[c] / l_scr[c] - lam_ref[0] * (acc_scr[c + 1] / l_scr[c + 1])
        ms = jnp.mean(o * o, axis=1, keepdims=True)
        o_ref[0, :, hh * LANES:(hh + 1) * LANES] = (
            o * lax.rsqrt(ms + LN_EPS) * g_ref[...] * post_scale).astype(o_ref.dtype)


def diff_attention(qkv, bias_tiles, far_bias, lam, subln_g, lambda_init, tq, hps):
    bsz, s, _ = qkv.shape
    kern = functools.partial(_attn_kernel, tq=tq, hps=hps, scale=DIFF_HEAD_DIM ** -0.5 * LOG2_E,
                             post_scale=1.0 - lambda_init)
    hw = hps * LANES
    nqk = DIFF_QK_WIDTH // hw
    return pl.pallas_call(
        kern,
        grid=(bsz, DIFF_HEADS // hps, s // tq),
        in_specs=[
            pl.BlockSpec(memory_space=pltpu.SMEM),
            pl.BlockSpec(memory_space=pltpu.SMEM),
            pl.BlockSpec((1, tq, hw), lambda b, h, i: (b, i, h)),
            pl.BlockSpec((1, s, hw), lambda b, h, i: (b, 0, nqk + h)),
            pl.BlockSpec((1, s, hw), lambda b, h, i: (b, 0, 2 * nqk + h)),
            pl.BlockSpec((hps, 2, tq, tq), lambda b, h, i: (h, 0, 0, 0)),
            pl.BlockSpec((1, LANES), lambda b, h, i: (0, 0)),
        ],
        out_specs=pl.BlockSpec((1, tq, hw), lambda b, h, i: (b, i, h)),
        out_shape=jax.ShapeDtypeStruct((bsz, s, DIFF_WIDTH), BF16),
        scratch_shapes=[pltpu.VMEM((2 * hps, tq, LANES), F32), pltpu.VMEM((2 * hps, tq, LANES), F32),
                        pltpu.VMEM((2 * hps, tq, LANES), F32)],
        compiler_params=_cparams(("parallel", "parallel", "arbitrary")),
        name="diff_attention",
    )(lam, far_bias, qkv, qkv, qkv, bias_tiles, subln_g.reshape(1, LANES))


def _rwkv_prep_kernel(x_ref, prev_ref, mu_ref, w0_ref, w2_ref, a0_ref, a2_ref, g2_ref,
                      kk_ref, ka_ref, rk_ref,
                      r_out, k_out, v_out, lw_out, kn_out, b_out, g_out, bonus_out, *, tm, seq):
    i = pl.program_id(0)
    x = x_ref[...]
    w = RWKV_WIDTH
    row = lax.broadcasted_iota(jnp.int32, x.shape, 0)
    is_start = (i * tm) % seq == 0
    last_prev = jnp.where(is_start, 0.0, prev_ref[7:8, :])
    prev = jnp.where(row == 0, last_prev, pltpu.roll(x, 1, 0))
    xs = x + (prev - x) * mu_ref[...]

    r = xs[:, 0:w]
    k = xs[:, w:2 * w]
    v = xs[:, 2 * w:3 * w]
    xwa = xs[:, 3 * w:3 * w + 2 * LORA_DECAY]
    xg = xs[:, 3 * w + LANES:3 * w + LANES + 2 * LANES]

    z = w0_ref[...] + _bdot(jnp.tanh(xwa), w2_ref[...])
    softplus_neg_z = jnp.maximum(-z, 0.0) + jnp.log(1.0 + jnp.exp(-jnp.abs(z)))
    lw_out[...] = -jnp.exp(-softplus_neg_z - 0.5)
    a_lr = _sigmoid(a0_ref[...] + _bdot(xwa, a2_ref[...]))
    g_out[...] = _bdot(_sigmoid(xg), g2_ref[...])

    kk = k * kk_ref[...]
    kn = kk / jnp.maximum(jnp.sqrt(_head_sum(kk * kk)), 1e-12)
    k2 = k * (1.0 + (a_lr - 1.0) * ka_ref[...])
    r_out[...] = r
    k_out[...] = k2
    v_out[...] = v
    kn_out[...] = kn
    b_out[...] = kn * a_lr
    bonus_out[...] = _head_sum(r * k2 * rk_ref[...]) * v


def rwkv_prep(pr, mu, w0, w2p, a0, a2p, g2p, k_k, k_a, r_k, seq, tm):
    t, wp = pr.shape
    w = RWKV_WIDTH
    row = lambda a: a.reshape(1, -1)
    full = lambda shp: pl.BlockSpec(shp, lambda i: (0, 0))
    outs = [jax.ShapeDtypeStruct((t, w), F32)] * 8
    kern = functools.partial(_rwkv_prep_kernel, tm=tm, seq=seq)
    return pl.pallas_call(
        kern,
        grid=(t // tm,),
        in_specs=[
            pl.BlockSpec((tm, wp), lambda i: (i, 0)),
            pl.BlockSpec((8, wp), lambda i: (jnp.maximum(i * (tm // 8) - 1, 0), 0)),
            full((1, wp)), full((1, w)), full((LANES, w)), full((1, w)), full((LANES, w)),
            full((2 * LANES, w)), full((1, w)), full((1, w)), full((1, w)),
        ],
        out_specs=[pl.BlockSpec((tm, w), lambda i: (i, 0))] * 8,
        out_shape=outs,
        compiler_params=_cparams(("parallel",)),
        name="rwkv_prep",
    )(pr, pr, row(mu), row(w0), w2p, row(a0), a2p, g2p, row(k_k), row(k_a), row(r_k))


def _rwkv_scan_kernel(r_ref, k_ref, v_ref, lw_ref, kn_ref, b_ref, y_ref, s_scr, *, chunk, pairs):
    c2 = 2 * chunk

    @pl.when(pl.program_id(2) == 0)
    def _():
        s_scr[...] = jnp.zeros(s_scr.shape, F32)

    ri = lax.broadcasted_iota(jnp.int32, (chunk, chunk), 0)
    ci = lax.broadcasted_iota(jnp.int32, (chunk, chunk), 1)
    ltri = jnp.where(ri >= ci, 1.0, 0.0).astype(BF16)

    lw = lw_ref[0]
    cl = _dot_sel_x(ltri, lw)
    mid = chunk // 2 - 1
    clm = cl[mid:mid + 1, :]
    cle = cl[chunk - 1:chunk, :]
    w_mid = jnp.exp(cl - clm)
    w_mid_prev = jnp.exp(cl - lw - clm)
    w_mid_inv = jnp.exp(clm - cl)
    w_abs = jnp.exp(cl)
    w_abs_prev = jnp.exp(cl - lw)
    w_end = jnp.exp(cle - cl)
    w_tot = jnp.exp(cle)

    r = r_ref[0]
    k = k_ref[0]
    v = v_ref[0]
    a = -kn_ref[0]
    b = b_ref[0]
    a_mid = a * w_mid_prev
    r_mid = r * w_mid
    b_mid = b * w_mid_inv
    k_mid = k * w_mid_inv
    a_abs = a * w_abs_prev
    r_abs = r * w_abs
    b_end = b * w_end
    k_end = k * w_end

    lane = lax.broadcasted_iota(jnp.int32, (chunk, LANES), 1)
    first = lane < RWKV_HEAD_SIZE

    def stack(x):
        return jnp.concatenate([jnp.where(first, x, 0.0), jnp.where(first, 0.0, x)], axis=0)

    rr = lax.broadcasted_iota(jnp.int32, (c2, c2), 0)
    cc = lax.broadcasted_iota(jnp.int32, (c2, c2), 1)
    same = (rr >= chunk) == (cc >= chunk)
    strict = same & (rr > cc)
    incl = same & (rr >= cc)
    eye = jnp.where(rr == cc, 1.0, 0.0)

    ps = range(pairs)
    sls = [slice(p * LANES, (p + 1) * LANES) for p in ps]
    bf = lambda x: x.astype(BF16)
    cat0 = lambda xs: jnp.concatenate(xs, axis=0)
    cat1 = lambda xs: jnp.concatenate(xs, axis=1)
    v_s = [bf(stack(v[:, sl])) for sl in sls]
    ar_m = [bf(cat0([stack(a_mid[:, sl]), stack(r_mid[:, sl])])) for sl in sls]
    bk_m = [bf(cat0([stack(b_mid[:, sl]), stack(k_mid[:, sl])])) for sl in sls]
    g_all = [_bdot_nt(ar_m[p], bk_m[p]) for p in ps]
    l_ab = [jnp.where(strict, g[:c2, :c2], 0.0) for g in g_all]
    l_ak = [bf(jnp.where(strict, g[:c2, c2:], 0.0)) for g in g_all]
    m_r = [bf(cat1([jnp.where(incl, g[c2:, :c2], 0.0), jnp.where(incl, g[c2:, c2:], 0.0)])) for g in g_all]

    t_inv = [eye + l_ab[p] for p in ps]
    n_fac = max(chunk.bit_length() - 2, 0)
    lp = [_bdot(l_ab[p], l_ab[p]) for p in ps] if n_fac else l_ab
    for _ in range(n_fac - 1):
        x = [_bdot(cat0([t_inv[p], lp[p]]), lp[p]) for p in ps]
        t_inv = [t_inv[p] + x[p][:c2] for p in ps]
        lp = [x[p][c2:] for p in ps]
    if n_fac:
        t_inv = [t_inv[p] + _bdot(t_inv[p], lp[p]) for p in ps]
    t_inv = [bf(t) for t in t_inv]

    lv = [_bdot(l_ak[p], v_s[p]) for p in ps]
    av = [bf(_bdot(t_inv[p], cat1([stack(a_abs[:, sls[p]]), lv[p]]))) for p in ps]
    zero = jnp.zeros((c2, LANES), BF16)
    ry = [_bdot(m_r[p], cat0([av[p], cat1([zero, v_s[p]])])) for p in ps]
    r_p = [stack(r_abs[:, sls[p]]) + ry[p][:, :LANES] for p in ps]

    s_old = [s_scr[p] for p in ps]
    y_s = [_bdot_nt(r_p[p], s_old[p]) + ry[p][:, LANES:] for p in ps]
    for p in ps:
        y_ref[0, :, sls[p]] = y_s[p][:chunk] + y_s[p][chunk:]
    b_e = [bf(stack(b_end[:, sl])) for sl in sls]
    k_e = [bf(stack(k_end[:, sl])) for sl in sls]
    ab = [_bdot_tn(av[p][:, :LANES], b_e[p]) for p in ps]
    q_t = [_bdot_tn(cat0([av[p][:, LANES:], v_s[p]]), cat0([b_e[p], k_e[p]])) for p in ps]
    for p in ps:
        s_scr[p] = s_old[p] * w_tot[:, sls[p]] + _bdot(s_old[p], ab[p]) + q_t[p]


def rwkv_scan(r, k, v, lw, kn, b, chunk, pairs):
    bsz, s, w = r.shape
    pw = pairs * LANES
    spec = pl.BlockSpec((1, chunk, pw), lambda bi, hi, ci: (bi, ci, hi))
    kern = functools.partial(_rwkv_scan_kernel, chunk=chunk, pairs=pairs)
    return pl.pallas_call(
        kern,
        grid=(bsz, w // pw, s // chunk),
        in_specs=[spec] * 6,
        out_specs=spec,
        out_shape=jax.ShapeDtypeStruct((bsz, s, w), F32),
        scratch_shapes=[pltpu.VMEM((pairs, LANES, LANES), F32)],
        compiler_params=_cparams(("parallel", "parallel", "arbitrary")),
        name="rwkv_scan",
    )(r, k, v, lw, kn, b)


def _rwkv_out(y, bonus, g, lnx_g, lnx_b):
    inv_n = 1.0 / RWKV_HEAD_SIZE
    mu = _head_sum(y) * inv_n
    yc = y - mu
    var = _head_sum(yc * yc) * inv_n
    yn = yc * lax.rsqrt(var + RWKV_GN_EPS) * lnx_g + lnx_b
    return (yn + bonus) * g


def _route(logits):
    lane = lax.broadcasted_iota(jnp.int32, logits.shape, 1)
    big = jnp.int32(4 * LANES)
    gmask = lane < N_GROUPS
    gl = jnp.where(gmask, logits, NEG_INF)
    gmax = jnp.max(gl, axis=1, keepdims=True)
    gidx = jnp.min(jnp.where(gl == gmax, lane, big), axis=1, keepdims=True)
    gtop = 1.0 / jnp.sum(jnp.where(gmask, jnp.exp(gl - gmax), 0.0), axis=1, keepdims=True)
    lo = N_GROUPS + EXPERTS_PER_GROUP * gidx
    emask = (lane >= lo) & (lane < lo + EXPERTS_PER_GROUP)
    el = jnp.where(emask, logits, NEG_INF)
    e1 = jnp.max(el, axis=1, keepdims=True)
    i1 = jnp.min(jnp.where(el == e1, lane, big), axis=1, keepdims=True)
    el2 = jnp.where(lane == i1, NEG_INF, el)
    e2 = jnp.max(el2, axis=1, keepdims=True)
    i2 = jnp.min(jnp.where(el2 == e2, lane, big), axis=1, keepdims=True)
    t = jnp.exp(e2 - e1)
    w1 = gtop / (1.0 + t)
    w2 = gtop * t / (1.0 + t)
    id1 = (i1 - N_GROUPS).astype(F32)
    id2 = (i2 - N_GROUPS).astype(F32)
    return jnp.where(lane == 0, id1, jnp.where(lane == 1, id2, jnp.where(lane == 2, w1,
                                                                         jnp.where(lane == 3, w2, 0.0))))


def _outproj_kernel(yd_ref, ys_ref, bonus_ref, gate_ref, lg_ref, lb_ref, h_ref,
                    w1_ref, w2_ref, g_ref, b_ref, rw_hi_ref, rw_lo_ref, rb_ref, h1_ref, info_ref, *, parts):
    rows = h_ref.shape[0] // parts
    sls = [pl.ds(k * rows, rows) for k in range(parts)]
    d = lambda a, b: jnp.dot(a, b, preferred_element_type=F32)
    yr = [_rwkv_out(ys_ref[sl, :], bonus_ref[sl, :], gate_ref[sl, :], lg_ref[...], lb_ref[...]) for sl in sls]
    mix = [d(yd_ref[sl, :], w1_ref[...]) + d(yr[k].astype(BF16), w2_ref[...]) for k, sl in enumerate(sls)]
    h1 = [_layer_norm(DEEPNORM_ALPHA * h_ref[sl, :] + mix[k], g_ref[...], b_ref[...]) for k, sl in enumerate(sls)]
    for k, sl in enumerate(sls):
        h1_ref[sl, :] = h1[k]
        hi = h1[k].astype(BF16)
        lo = (h1[k] - hi.astype(F32)).astype(BF16)
        logits = d(hi, rw_hi_ref[...]) + d(lo, rw_hi_ref[...]) + d(hi, rw_lo_ref[...]) + rb_ref[...]
        info_ref[sl, :] = _route(logits)


def outproj_ln_route(yd, ys, bonus, gate, lnx_g, lnx_b, h, w_out_a, w_out_b, g, b, rw_hi, rw_lo, rb, tm):
    t, d = h.shape
    ka = yd.shape[1]
    kb = ys.shape[1]
    full = lambda shp: pl.BlockSpec(shp, lambda i: (0, 0))
    tile = lambda n: pl.BlockSpec((tm, n), lambda i: (i, 0))
    return pl.pallas_call(
        functools.partial(_outproj_kernel, parts=2),
        grid=(t // tm,),
        in_specs=[
            tile(ka), tile(kb), tile(kb), tile(kb), full((1, kb)), full((1, kb)), tile(d),
            full((ka, d)), full((kb, d)), full((1, d)), full((1, d)),
            full((d, ROUTER_LANES)), full((d, ROUTER_LANES)), full((1, ROUTER_LANES)),
        ],
        out_specs=[
            pl.BlockSpec((tm, d), lambda i: (i, 0)),
            pl.BlockSpec((tm, ROUTER_LANES), lambda i: (i, 0)),
        ],
        out_shape=[jax.ShapeDtypeStruct((t, d), F32), jax.ShapeDtypeStruct((t, ROUTER_LANES), F32)],
        compiler_params=pltpu.CompilerParams(dimension_semantics=("parallel",),
                                             vmem_limit_bytes=MOE_VMEM_LIMIT),
        name="outproj_ln_route",
    )(yd, ys, bonus, gate, lnx_g.reshape(1, kb), lnx_b.reshape(1, kb), h, w_out_a, w_out_b,
      g.reshape(1, d), b.reshape(1, d), rw_hi, rw_lo, rb)


def _moe_rank_kernel(e_ref, pos_ref, ends_ref, run_scr, start_scr, *, tm, row_tile):
    ph = pl.program_id(0)
    i = pl.program_id(1)
    lane = lax.broadcasted_iota(jnp.int32, (tm, LANES), 1)
    onehot = e_ref[...] == lane
    oh = jnp.where(onehot, 1.0, 0.0)

    @pl.when((ph == 0) & (i == 0))
    def _():
        run_scr[...] = jnp.zeros(run_scr.shape, F32)

    @pl.when(ph == 0)
    def _():
        run_scr[...] += jnp.sum(oh, axis=0, keepdims=True)

    @pl.when((ph == 1) & (i == 0))
    def _():
        padded = jnp.floor((run_scr[...] + (row_tile - 1)) * (1.0 / row_tile)) * row_tile
        ri = lax.broadcasted_iota(jnp.int32, (LANES, LANES), 0)
        ci = lax.broadcasted_iota(jnp.int32, (LANES, LANES), 1)
        upper = jnp.where(ri < ci, 1.0, 0.0).astype(BF16)
        parts = _split3(jnp.broadcast_to(padded, (8, LANES)))
        starts = sum(jnp.dot(q, upper, preferred_element_type=F32) for q in parts)[0:1]
        start_scr[...] = starts
        ends_ref[...] = starts + padded
        run_scr[...] = jnp.zeros(run_scr.shape, F32)

    @pl.when(ph == 1)
    def _():
        ri = lax.broadcasted_iota(jnp.int32, (tm, tm), 0)
        ci = lax.broadcasted_iota(jnp.int32, (tm, tm), 1)
        before = jnp.where(ri > ci, 1.0, 0.0).astype(BF16)
        row = jnp.dot(before, oh.astype(BF16), preferred_element_type=F32) + (run_scr[...] + start_scr[...])
        pos_ref[...] = jnp.sum(jnp.where(onehot, row, 0.0), axis=1, keepdims=True).astype(jnp.int32)
        run_scr[...] += jnp.sum(oh, axis=0, keepdims=True)


def moe_rank(e_col, tm, row_tile):
    n = e_col.shape[0]
    return pl.pallas_call(
        functools.partial(_moe_rank_kernel, tm=tm, row_tile=row_tile),
        grid=(2, n // tm),
        in_specs=[pl.BlockSpec((tm, 1), lambda ph, i: (i, 0))],
        out_specs=[pl.BlockSpec((tm, 1), lambda ph, i: (i * ph, 0)),
                   pl.BlockSpec((1, LANES), lambda ph, i: (0, 0))],
        out_shape=[jax.ShapeDtypeStruct((n, 1), jnp.int32), jax.ShapeDtypeStruct((1, LANES), F32)],
        scratch_shapes=[pltpu.VMEM((1, LANES), F32), pltpu.VMEM((1, LANES), F32)],
        compiler_params=_cparams(("arbitrary", "arbitrary")),
        name="moe_rank",
    )(e_col)


def _row_copy(src_hbm, row, dst, r, sem):
    return pltpu.make_async_copy(src_hbm.at[pl.ds(row, 1)], dst.at[pl.ds(r, 1)], sem)


def _pack_bf16_halves(y):
    n = y.shape[1] // 2
    hi = pltpu.bitcast(y[:, :n].astype(BF16).astype(F32), jnp.uint32)
    lo = pltpu.bitcast(y[:, n:].astype(BF16).astype(F32), jnp.uint32)
    return hi | (lo >> 16)


def _unpack_bf16_halves(w):
    hi = pltpu.bitcast(w & jnp.uint32(0xFFFF0000), F32)
    lo = pltpu.bitcast(w << 16, F32)
    return jnp.concatenate([hi, lo], axis=1)


def _moe_expert_kernel(te_ref, first_ref, par_ref, nxt_ref, nact_ref,
                       src0_ref, src1_ref, src2_ref, src_ahead_ref, x_hbm, wg_hbm, wu_hbm, wd_hbm, o_ref,
                       xbuf0, xbuf1, xbuf2, xbuf3, wgf, wuf, wdf, wgb, wub, wdb, sem, wsem, *, tm):
    j = pl.program_id(0)
    n_act = nact_ref[0]
    bufs = (xbuf0, xbuf1, xbuf2, xbuf3)
    nb = len(bufs)

    def token_copy(tok, into, r):
        return pltpu.make_async_copy(x_hbm.at[tok], bufs[into].at[:, r, :], sem.at[into])

    def start_gather(src, into):
        for r in range(tm):
            token_copy(src[0, 0, r], into, r).start(priority=r % 2)

    def wait_gather(into):
        for r in range(tm):
            token_copy(0, into, r).wait()

    def weight_copies(e, par):
        return (pltpu.make_async_copy(wg_hbm.at[e], wgf.at[par], wsem.at[par]),
                pltpu.make_async_copy(wu_hbm.at[e], wuf.at[par], wsem.at[par]),
                pltpu.make_async_copy(wd_hbm.at[e], wdf.at[par], wsem.at[par]))

    @pl.when(j == 0)
    def _():
        start_gather(src0_ref, 0)
        start_gather(src1_ref, 1)
        start_gather(src2_ref, 2)
        for c in weight_copies(te_ref[0], 0):
            c.start(priority=1)

    @pl.when((j < n_act) & (first_ref[j] > 0))
    def _():
        par = par_ref[j]
        for c in weight_copies(te_ref[j], par):
            c.wait()
        wgb[...] = wgf[par].astype(BF16)
        wub[...] = wuf[par].astype(BF16)
        wdb[...] = wdf[par].astype(BF16)

    for slot in range(nb):
        @pl.when((j < n_act) & (j % nb == slot))
        def _():
            wait_gather(slot)
            start_gather(src_ahead_ref, (slot + nb - 1) % nb)
            x = jnp.concatenate([bufs[slot][c] for c in range(bufs[slot].shape[0])], axis=1).astype(BF16)
            hg = jnp.dot(x, wgb[...], preferred_element_type=F32)
            hu = jnp.dot(x, wub[...], preferred_element_type=F32)
            hid = hg * _sigmoid(hg) * hu
            o_ref[...] = _pack_bf16_halves(jnp.dot(hid.astype(BF16), wdb[...], preferred_element_type=F32))

        @pl.when((j == n_act - 1) & (j % nb == slot))
        def _():
            for ahead in range(1, nb):
                wait_gather((slot + ahead) % nb)

    @pl.when((j < n_act) & (first_ref[j] > 0) & (nxt_ref[j] >= 0))
    def _():
        for c in weight_copies(nxt_ref[j], 1 - par_ref[j]):
            c.start(priority=1)

    @pl.when(j >= n_act)
    def _():
        o_ref[...] = jnp.zeros(o_ref.shape, jnp.uint32)


def moe_experts(src_tok, plan, x, wg, wu, wd, n_rows, tm):
    d, f = wg.shape[1:]
    hbm = pl.BlockSpec(memory_space=pl.ANY)
    src3 = src_tok.reshape(n_rows // tm, 1, tm)
    smem_tile = lambda index_map: pl.BlockSpec((1, 1, tm), index_map, memory_space=pltpu.SMEM)
    return pl.pallas_call(
        functools.partial(_moe_expert_kernel, tm=tm),
        grid_spec=pltpu.PrefetchScalarGridSpec(
            num_scalar_prefetch=5,
            grid=(n_rows // tm,),
            in_specs=[smem_tile(lambda j, *_: (0, 0, 0)),
                      smem_tile(lambda j, te, fi, pa, nx, na: (jnp.minimum(1, na[0] - 1), 0, 0)),
                      smem_tile(lambda j, te, fi, pa, nx, na: (jnp.minimum(2, na[0] - 1), 0, 0)),
                      smem_tile(lambda j, te, fi, pa, nx, na: (jnp.minimum(j + 3, na[0] - 1), 0, 0)),
                      hbm, hbm, hbm, hbm],
            out_specs=pl.BlockSpec((tm, d // 2), lambda j, *_: (j, 0)),
            scratch_shapes=[pltpu.VMEM((d // LANES, tm, LANES), F32)] * 4 + [
                            pltpu.VMEM((2, d, f), F32), pltpu.VMEM((2, d, f), F32), pltpu.VMEM((2, f, d), F32),
                            pltpu.VMEM((d, f), BF16), pltpu.VMEM((d, f), BF16), pltpu.VMEM((f, d), BF16),
                            pltpu.SemaphoreType.DMA((4,)), pltpu.SemaphoreType.DMA((2,))],
        ),
        out_shape=jax.ShapeDtypeStruct((n_rows, d // 2), jnp.uint32),
        compiler_params=pltpu.CompilerParams(dimension_semantics=("arbitrary",),
                                             vmem_limit_bytes=MOE_VMEM_LIMIT),
        name="moe_experts",
    )(*plan, src3, src3, src3, src3, x, wg, wu, wd)


def _final_kernel(p1_ref, p2_ref, y_hbm, h1_ref, info_ref, p_ref, g_ref, b_ref, pw_ref, pg_ref, gw_ref,
                  o_ref, ybuf0, ybuf1, sem, *, tm, n_tiles, parts):
    i = pl.program_id(0)
    bufs = (ybuf0, ybuf1)

    def start_gather(tile, into):
        for r in range(tm):
            _row_copy(y_hbm, p1_ref[tile * tm + r], bufs[into].at[0], r, sem.at[into]).start(priority=0)
            _row_copy(y_hbm, p2_ref[tile * tm + r], bufs[into].at[1], r, sem.at[into]).start(priority=1)

    def wait_gather(into):
        for r in range(tm):
            _row_copy(y_hbm, 0, bufs[into].at[0], r, sem.at[into]).wait()
            _row_copy(y_hbm, 0, bufs[into].at[1], r, sem.at[into]).wait()

    @pl.when(i == 0)
    def _():
        start_gather(0, 0)

    for slot in range(2):
        @pl.when(i % 2 == slot)
        def _():
            wait_gather(slot)
            start_gather(jnp.minimum(i + 1, n_tiles - 1), 1 - slot)
            rows = tm // parts
            sls = [pl.ds(k * rows, rows) for k in range(parts)]
            h2, ple = [], []
            for sl in sls:
                info = info_ref[sl, :]
                ffn = (info[:, 2:3] * _unpack_bf16_halves(bufs[slot][0, sl, :])
                       + info[:, 3:4] * _unpack_bf16_halves(bufs[slot][1, sl, :]))
                h2.append(_layer_norm(DEEPNORM_ALPHA * h1_ref[sl, :] + ffn, g_ref[...], b_ref[...]))
                pe = jnp.dot(p_ref[sl, :].astype(BF16), pw_ref[...], preferred_element_type=F32)
                ple.append(pe * lax.rsqrt(jnp.mean(pe * pe, axis=-1, keepdims=True) + LN_EPS) * pg_ref[...])
            gate = [jnp.dot(h.astype(BF16), gw_ref[...], preferred_element_type=F32) for h in h2]
            for k, sl in enumerate(sls):
                o_ref[sl, :] = h2[k] + _sigmoid(gate[k]) * ple[k]

        @pl.when((i == n_tiles - 1) & (i % 2 == slot))
        def _():
            wait_gather(1 - slot)


def final_block(pos1, pos2, ys, h1, info, p, g, b, ple_w, ple_g, gate_w, tm):
    t, d = h1.shape
    dp = p.shape[1]
    tile = lambda n: pl.BlockSpec((tm, n), lambda i, p1, p2: (i, 0))
    full = lambda shp: pl.BlockSpec(shp, lambda i, p1, p2: (0, 0))
    return pl.pallas_call(
        functools.partial(_final_kernel, tm=tm, n_tiles=t // tm, parts=2),
        grid_spec=pltpu.PrefetchScalarGridSpec(
            num_scalar_prefetch=2,
            grid=(t // tm,),
            in_specs=[pl.BlockSpec(memory_space=pl.ANY), tile(d), tile(ROUTER_LANES), tile(dp),
                      full((1, d)), full((1, d)), full((dp, d)), full((1, d)), full((d, d))],
            out_specs=tile(d),
            scratch_shapes=[pltpu.VMEM((2, tm, d // 2), jnp.uint32), pltpu.VMEM((2, tm, d // 2), jnp.uint32),
                            pltpu.SemaphoreType.DMA((2,))],
        ),
        out_shape=jax.ShapeDtypeStruct((t, d), F32),
        compiler_params=pltpu.CompilerParams(dimension_semantics=("arbitrary",),
                                             vmem_limit_bytes=MOE_VMEM_LIMIT),
        name="final_block",
    )(pos1, pos2, ys, h1, info, p, g.reshape(1, d), b.reshape(1, d), ple_w, ple_g.reshape(1, d), gate_w)


def _dispatch_plan(pos, ends_f, n_tok, tm, n_rows):
    n_tiles = n_rows // tm
    ends = ends_f[0, :N_EXPERTS].astype(jnp.int32)
    src_tok = jnp.zeros((n_rows,), jnp.int32).at[pos].set(
        jnp.arange(2 * n_tok, dtype=jnp.int32) % n_tok, unique_indices=True)
    n_active = ends[-1] // tm
    tile_ids = jnp.arange(n_tiles, dtype=jnp.int32)
    tile_expert = jnp.searchsorted(ends, jnp.minimum(tile_ids, n_active - 1) * tm, side="right")
    tile_expert = jnp.minimum(tile_expert, N_EXPERTS - 1).astype(jnp.int32)
    first = jnp.concatenate([jnp.ones((1,), jnp.int32),
                             (tile_expert[1:] != tile_expert[:-1]).astype(jnp.int32)])
    parity = (jnp.cumsum(first) - 1) % 2
    ids = jnp.arange(N_EXPERTS, dtype=jnp.int32)
    present = ends > jnp.concatenate([jnp.zeros((1,), jnp.int32), ends[:-1]])
    later = present[None, :] & (ids[None, :] > ids[:, None])
    next_of = jnp.min(jnp.where(later, ids[None, :], N_EXPERTS), axis=1)
    next_of = jnp.where(next_of == N_EXPERTS, -1, next_of)
    plan = (tile_expert, first, parity.astype(jnp.int32), next_of[tile_expert].astype(jnp.int32),
            n_active.astype(jnp.int32).reshape(1))
    return src_tok, plan


def _t5_bucket(rel):
    n = jnp.maximum(rel, 0)
    max_exact = REL_BUCKETS // 2
    nf = jnp.maximum(n, 1).astype(F32)
    large = max_exact + (jnp.log(nf / max_exact) / math.log(REL_MAX_DIST / max_exact)
                         * (REL_BUCKETS - max_exact)).astype(jnp.int32)
    large = jnp.minimum(large, REL_BUCKETS - 1)
    return jnp.where(n < max_exact, n, large)


def _bias_tables(rel_bias, tq):
    n = 3 * tq
    rel = 2 * tq - 1 - jnp.arange(n, dtype=jnp.int32)
    g = rel_bias[_t5_bucket(rel)].astype(F32).T
    skew = jnp.tile(g, (1, tq))[:, :tq * (n - 1)].reshape(-1, tq, n - 1)
    tiles = jnp.stack([skew[:, :, 2 * tq - 1:3 * tq - 1], skew[:, :, tq - 1:2 * tq - 1]], axis=1)
    far = rel_bias[_t5_bucket(jnp.full((), 2 * tq, jnp.int32))].astype(F32)
    return tiles * LOG2_E, far * LOG2_E


def kernel(x, p, ln_in_g, ln_in_b, rel_bias, w_in, diff_lam_q1, diff_lam_k1, diff_lam_q2, diff_lam_k2, diff_subln_g, rwkv_mu, rwkv_w0, rwkv_w2, rwkv_a0, rwkv_a2, rwkv_g2, rwkv_k_k, rwkv_k_a, rwkv_r_k, rwkv_lnx_g, rwkv_lnx_b, w_out, ln1_g, ln1_b, router_group_w, router_group_b, router_expert_w, router_expert_b, moe_w_gate, moe_w_up, moe_w_down, ln2_g, ln2_b, ple_w, ple_norm_g, ple_gate_w):
    bsz, seq, d = x.shape
    t = bsz * seq
    i = 0
    lambda_init = 0.8 - 0.6 * math.exp(-0.3 * i)
    tq = 256

    x2 = x.reshape(t, d)
    w_in_t = jnp.swapaxes(w_in[i], 0, 1)
    h, hb = layer_norm_rows(x2, ln_in_g, ln_in_b, 512)
    qkv = matmul_nt(hb, w_in_t, 0, O_RWKV, O_RWKV, BF16, 2048, 512)
    pr = matmul_nt(hb, w_in_t, O_RWKV, RWKV_PROJ, RWKV_PROJ_PAD, F32, 2048, 512)

    lam = (jnp.exp(jnp.sum(diff_lam_q1[i].astype(F32) * diff_lam_k1[i]))
           - jnp.exp(jnp.sum(diff_lam_q2[i].astype(F32) * diff_lam_k2[i])) + lambda_init)
    bias_tiles, far_bias = _bias_tables(rel_bias, tq)
    y_diff = diff_attention(qkv.reshape(bsz, seq, O_RWKV), bias_tiles, far_bias, lam.reshape(1),
                            diff_subln_g[i], lambda_init, tq, 8)

    mu_p = jnp.pad(rwkv_mu[i], (0, RWKV_PROJ_PAD - RWKV_PROJ))
    zeros64 = jnp.zeros((LORA_DECAY, RWKV_WIDTH), F32)
    w2p = jnp.concatenate([rwkv_w2[i], zeros64], axis=0).astype(BF16)
    a2p = jnp.concatenate([zeros64, rwkv_a2[i]], axis=0).astype(BF16)
    g2p = jnp.pad(rwkv_g2[i], ((0, 2 * LANES - LORA_GATE), (0, 0))).astype(BF16)
    r, k2, v, lw, kn, bb, g, bonus = rwkv_prep(
        pr, mu_p, rwkv_w0[i], w2p, rwkv_a0[i], a2p, g2p, rwkv_k_k[i], rwkv_k_a[i],
        rwkv_r_k[i].reshape(-1), seq, 256)
    sh = lambda a: a.reshape(bsz, seq, RWKV_WIDTH)
    y_scan = rwkv_scan(sh(r), sh(k2), sh(v), sh(lw), sh(kn), sh(bb), 64, 8)

    w_out_b = w_out[i].astype(BF16)
    rw = jnp.concatenate([router_group_w[i], router_expert_w[i]], axis=1)
    rw = jnp.pad(rw, ((0, 0), (0, ROUTER_LANES - rw.shape[1])))
    rw_hi = rw.astype(BF16)
    rw_lo = (rw - rw_hi.astype(F32)).astype(BF16)
    rb = jnp.pad(jnp.concatenate([router_group_b[i], router_expert_b[i]]),
                 (0, ROUTER_LANES - N_GROUPS - N_EXPERTS)).reshape(1, ROUTER_LANES)
    h1, info = outproj_ln_route(y_diff.reshape(t, DIFF_WIDTH), y_scan.reshape(t, RWKV_WIDTH), bonus, g,
                                rwkv_lnx_g[i], rwkv_lnx_b[i], h, w_out_b[:DIFF_WIDTH],
                                w_out_b[DIFF_WIDTH:], ln1_g[i], ln1_b[i], rw_hi, rw_lo, rb, 512)

    tm_e = MOE_ROW_TILE
    n_rows = 2 * t + N_EXPERTS * tm_e
    e_all = jnp.concatenate([info[:, 0], info[:, 1]]).astype(jnp.int32)
    pos, ends = moe_rank(e_all.reshape(-1, 1), 1024, tm_e)
    pos = pos[:, 0]
    src_tok, plan = _dispatch_plan(pos, ends, t, tm_e, n_rows)
    f = moe_w_gate.shape[-1]
    ys = moe_experts(src_tok, plan, h1.reshape(t, d // LANES, LANES), moe_w_gate[i].reshape(N_EXPERTS, d, f),
                     moe_w_up[i].reshape(N_EXPERTS, d, f), moe_w_down[i].reshape(N_EXPERTS, f, d),
                     n_rows, tm_e)

    out = final_block(pos[:t], pos[t:], ys, h1, info, p[i].reshape(t, -1), ln2_g[i], ln2_b[i],
                      ple_w[i].astype(BF16), ple_norm_g[i], ple_gate_w[i].astype(BF16), 256)
    return out.reshape(bsz, seq, d)
```

```python
import functools
import math

import jax
import jax.numpy as jnp
from jax import lax
from jax.experimental import pallas as pl
from jax.experimental.pallas import tpu as pltpu

F32 = jnp.float32
BF16 = jnp.bfloat16

DIFF_HEADS = 8
DIFF_HEAD_DIM = 64
DIFF_V_DIM = 128
DIFF_WIDTH = DIFF_HEADS * DIFF_V_DIM
DIFF_QK_WIDTH = DIFF_HEADS * 2 * DIFF_HEAD_DIM
RWKV_HEADS = 16
RWKV_HEAD_SIZE = 64
RWKV_WIDTH = RWKV_HEADS * RWKV_HEAD_SIZE
LORA_DECAY = 64
LORA_AAA = 64
LORA_GATE = 160
RWKV_PROJ = 3 * RWKV_WIDTH + LORA_DECAY + LORA_AAA + LORA_GATE
O_RWKV = 2 * DIFF_QK_WIDTH + DIFF_WIDTH
REL_BUCKETS = 32
REL_MAX_DIST = 128
N_GROUPS = 4
EXPERTS_PER_GROUP = 8
N_EXPERTS = N_GROUPS * EXPERTS_PER_GROUP
LN_EPS = 1e-5
RWKV_GN_EPS = 64e-5
NEG_INF = -1e30
DEPTH = 1
DEEPNORM_ALPHA = (2 * DEPTH) ** 0.25
LOG2_E = math.log2(math.e)

LANES = 128
RWKV_PROJ_PAD = 3584
ROUTER_LANES = LANES
VMEM_LIMIT = 48 * 1024 * 1024
MOE_VMEM_LIMIT = 56 * 1024 * 1024
MOE_ROW_TILE = 256

LN_ROWS = 512
PROJ_ROWS = 2048
PROJ_COLS = 512
ATTN_Q_TILE = 256
ATTN_HEADS_PER_STEP = DIFF_HEADS
PREP_ROWS = 256
SCAN_CHUNK = 64
SCAN_PAIRS = RWKV_HEADS // 2
OUTPROJ_ROWS = 512
RANK_ROWS = 1024
FINAL_ROWS = 256


def _cparams(sem):
    return pltpu.CompilerParams(dimension_semantics=sem, vmem_limit_bytes=VMEM_LIMIT)


def _bdot(a, b):
    return jnp.dot(a.astype(BF16), b.astype(BF16), preferred_element_type=F32)


def _bdot_nt(a, b):
    return lax.dot_general(a.astype(BF16), b.astype(BF16), (((1,), (1,)), ((), ())),
                           preferred_element_type=F32)


def _bdot_tn(a, b):
    return lax.dot_general(a.astype(BF16), b.astype(BF16), (((0,), (0,)), ((), ())),
                           preferred_element_type=F32)


def _split3(x):
    hi = x.astype(BF16)
    r1 = x - hi.astype(F32)
    mid = r1.astype(BF16)
    lo = (r1 - mid.astype(F32)).astype(BF16)
    return hi, mid, lo


def _head_sum(x):
    lane = lax.broadcasted_iota(jnp.int32, (x.shape[0], LANES), 1)
    first = lane < RWKV_HEAD_SIZE
    cols = []
    for c in range(x.shape[1] // LANES):
        blk = x[:, c * LANES:(c + 1) * LANES]
        lo = jnp.sum(jnp.where(first, blk, 0.0), axis=1, keepdims=True)
        hi = jnp.sum(jnp.where(first, 0.0, blk), axis=1, keepdims=True)
        cols.append(jnp.where(first, lo, hi))
    return jnp.concatenate(cols, axis=1)


def _dot_sel_x(sel, x):
    hi, mid, lo = _split3(x)
    d = lambda p: jnp.dot(sel, p, preferred_element_type=F32)
    return d(hi) + d(mid) + d(lo)


def _layer_norm(x, g, b):
    mu = jnp.mean(x, axis=-1, keepdims=True)
    xc = x - mu
    var = jnp.mean(xc * xc, axis=-1, keepdims=True)
    return xc * lax.rsqrt(var + LN_EPS) * g + b


def _sigmoid(z):
    return 1.0 / (1.0 + jnp.exp(-z))


def _ln_kernel(x_ref, g_ref, b_ref, h_ref, hb_ref):
    xn = _layer_norm(x_ref[...], g_ref[...], b_ref[...])
    h_ref[...] = xn
    hb_ref[...] = xn.astype(BF16)


def layer_norm_rows(x, g, b, tm):
    m, d = x.shape
    tile = pl.BlockSpec((tm, d), lambda i: (i, 0))
    row = pl.BlockSpec((1, d), lambda i: (0, 0))
    return pl.pallas_call(
        _ln_kernel,
        grid=(m // tm,),
        in_specs=[tile, row, row],
        out_specs=[tile, tile],
        out_shape=[jax.ShapeDtypeStruct((m, d), F32), jax.ShapeDtypeStruct((m, d), BF16)],
        compiler_params=_cparams(("parallel",)),
        name="layer_norm_rows",
    )(x, g.reshape(1, d), b.reshape(1, d))


def _mm_nt_kernel(x_ref, wt_ref, o_ref, *, n_valid, tn):
    y = lax.dot_general(x_ref[...], wt_ref[...].astype(BF16), (((1,), (1,)), ((), ())),
                        preferred_element_type=F32)
    if n_valid % tn:
        col = pl.program_id(1) * tn + lax.broadcasted_iota(jnp.int32, y.shape, 1)
        y = jnp.where(col < n_valid, y, 0.0)
    o_ref[...] = y.astype(o_ref.dtype)


def matmul_nt(x, wt, row0, n_valid, n_out, out_dtype, tm, tn):
    m, k = x.shape
    blk0 = row0 // tn
    return pl.pallas_call(
        functools.partial(_mm_nt_kernel, n_valid=n_valid, tn=tn),
        grid=(m // tm, n_out // tn),
        in_specs=[pl.BlockSpec((tm, k), lambda i, j: (i, 0)),
                  pl.BlockSpec((tn, k), lambda i, j: (blk0 + j, 0))],
        out_specs=pl.BlockSpec((tm, tn), lambda i, j: (i, j)),
        out_shape=jax.ShapeDtypeStruct((m, n_out), out_dtype),
        compiler_params=_cparams(("parallel", "parallel")),
        name="matmul_nt",
    )(x, wt)


def _attn_kernel(lam_ref, far_ref, q_ref, k_ref, v_ref, bias_ref, g_ref, o_ref,
                 m_scr, l_scr, acc_scr, *, tq, hps, scale, post_scale):
    hp = pl.program_id(1)
    qi = pl.program_id(2)
    lane = lax.broadcasted_iota(jnp.int32, (tq, LANES), 1)
    q_maps = []
    for hh in range(hps):
        qs = (q_ref[0, :, hh * LANES:(hh + 1) * LANES].astype(F32) * scale).astype(BF16)
        zero = jnp.zeros_like(qs)
        q_maps += [jnp.where(lane < DIFF_HEAD_DIM, qs, zero), jnp.where(lane >= DIFF_HEAD_DIM, qs, zero)]

    m_scr[...] = jnp.full(m_scr.shape, NEG_INF, F32)
    l_scr[...] = jnp.zeros(l_scr.shape, F32)
    acc_scr[...] = jnp.zeros(acc_scr.shape, F32)

    def step(kstart, bias_index, causal):
        for hh in range(hps):
            kb = k_ref[0, pl.ds(kstart, tq), hh * LANES:(hh + 1) * LANES]
            vb = v_ref[0, pl.ds(kstart, tq), hh * LANES:(hh + 1) * LANES]
            for c in range(2 * hh, 2 * hh + 2):
                s = lax.dot_general(q_maps[c], kb, (((1,), (1,)), ((), ())), preferred_element_type=F32)
                m_old = m_scr[c]
                if bias_index is None:
                    far = far_ref[hp * hps + hh]
                    m_new = jnp.maximum(m_old, jnp.max(s, axis=1, keepdims=True) + far)
                    shift = m_new - far
                else:
                    s = s + bias_ref[hh, bias_index]
                    if causal:
                        ri = lax.broadcasted_iota(jnp.int32, (tq, tq), 0)
                        ci = lax.broadcasted_iota(jnp.int32, (tq, tq), 1)
                        s = jnp.where(ri >= ci, s, NEG_INF)
                    m_new = jnp.maximum(m_old, jnp.max(s, axis=1, keepdims=True))
                    shift = m_new
                p = jnp.exp2(s - jnp.concatenate([shift] * (tq // LANES), axis=1))
                corr = jnp.exp2(m_old - m_new)
                l_scr[c] = corr * l_scr[c] + jnp.sum(p, axis=1, keepdims=True)
                acc_scr[c] = corr * acc_scr[c] + jnp.dot(p.astype(BF16), vb, preferred_element_type=F32)
                m_scr[c] = m_new

    def far_body(kb, carry):
        step(pl.multiple_of(kb * tq, tq), None, False)
        return carry

    lax.fori_loop(0, jnp.maximum(qi - 1, 0), far_body, 0)

    @pl.when(qi >= 1)
    def _():
        step(pl.multiple_of((qi - 1) * tq, tq), 1, False)

    step(pl.multiple_of(qi * tq, tq), 0, True)

    for hh in range(hps):
        c = 2 * hh
        o = acc_scr[c] / l_scr[c] - lam_ref[0] * (acc_scr[c + 1] / l_scr[c + 1])
        ms = jnp.mean(o * o, axis=1, keepdims=True)
        o_ref[0, :, hh * LANES:(hh + 1) * LANES] = (
            o * lax.rsqrt(ms + LN_EPS) * g_ref[...] * post_scale).astype(o_ref.dtype)


def diff_attention(qkv, bias_tiles, far_bias, lam, subln_g, lambda_init, tq, hps):
    bsz, s, _ = qkv.shape
    kern = functools.partial(_attn_kernel, tq=tq, hps=hps, scale=DIFF_HEAD_DIM ** -0.5 * LOG2_E,
                             post_scale=1.0 - lambda_init)
    hw = hps * LANES
    nqk = DIFF_QK_WIDTH // hw
    return pl.pallas_call(
        kern,
        grid=(bsz, DIFF_HEADS // hps, s // tq),
        in_specs=[
            pl.BlockSpec(memory_space=pltpu.SMEM),
            pl.BlockSpec(memory_space=pltpu.SMEM),
            pl.BlockSpec((1, tq, hw), lambda b, h, i: (b, i, h)),
            pl.BlockSpec((1, s, hw), lambda b, h, i: (b, 0, nqk + h)),
            pl.BlockSpec((1, s, hw), lambda b, h, i: (b, 0, 2 * nqk + h)),
            pl.BlockSpec((hps, 2, tq, tq), lambda b, h, i: (h, 0, 0, 0)),
            pl.BlockSpec((1, LANES), lambda b, h, i: (0, 0)),
        ],
        out_specs=pl.BlockSpec((1, tq, hw), lambda b, h, i: (b, i, h)),
        out_shape=jax.ShapeDtypeStruct((bsz, s, DIFF_WIDTH), BF16),
        scratch_shapes=[pltpu.VMEM((2 * hps, tq, LANES), F32), pltpu.VMEM((2 * hps, tq, LANES), F32),
                        pltpu.VMEM((2 * hps, tq, LANES), F32)],
        compiler_params=_cparams(("parallel", "parallel", "arbitrary")),
        name="diff_attention",
    )(lam, far_bias, qkv, qkv, qkv, bias_tiles, subln_g.reshape(1, LANES))


def _rwkv_prep_kernel(x_ref, prev_ref, mu_ref, w0_ref, w2_ref, a0_ref, a2_ref, g2_ref,
                      kk_ref, ka_ref, rk_ref,
                      r_out, k_out, v_out, lw_out, kn_out, b_out, g_out, bonus_out, *, tm, seq):
    i = pl.program_id(0)
    x = x_ref[...]
    w = RWKV_WIDTH
    row = lax.broadcasted_iota(jnp.int32, x.shape, 0)
    is_start = (i * tm) % seq == 0
    last_prev = jnp.where(is_start, 0.0, prev_ref[7:8, :])
    prev = jnp.where(row == 0, last_prev, pltpu.roll(x, 1, 0))
    xs = x + (prev - x) * mu_ref[...]

    r = xs[:, 0:w]
    k = xs[:, w:2 * w]
    v = xs[:, 2 * w:3 * w]
    xwa = xs[:, 3 * w:3 * w + 2 * LORA_DECAY]
    xg = xs[:, 3 * w + LANES:3 * w + LANES + 2 * LANES]

    z = w0_ref[...] + _bdot(jnp.tanh(xwa), w2_ref[...])
    softplus_neg_z = jnp.maximum(-z, 0.0) + jnp.log(1.0 + jnp.exp(-jnp.abs(z)))
    lw_out[...] = -jnp.exp(-softplus_neg_z - 0.5)
    a_lr = _sigmoid(a0_ref[...] + _bdot(xwa, a2_ref[...]))
    g_out[...] = _bdot(_sigmoid(xg), g2_ref[...])

    kk = k * kk_ref[...]
    kn = kk / jnp.maximum(jnp.sqrt(_head_sum(kk * kk)), 1e-12)
    k2 = k * (1.0 + (a_lr - 1.0) * ka_ref[...])
    r_out[...] = r
    k_out[...] = k2
    v_out[...] = v
    kn_out[...] = kn
    b_out[...] = kn * a_lr
    bonus_out[...] = _head_sum(r * k2 * rk_ref[...]) * v


def rwkv_prep(pr, mu, w0, w2p, a0, a2p, g2p, k_k, k_a, r_k, seq, tm):
    t, wp = pr.shape
    w = RWKV_WIDTH
    row = lambda a: a.reshape(1, -1)
    full = lambda shp: pl.BlockSpec(shp, lambda i: (0, 0))
    outs = [jax.ShapeDtypeStruct((t, w), F32)] * 8
    kern = functools.partial(_rwkv_prep_kernel, tm=tm, seq=seq)
    return pl.pallas_call(
        kern,
        grid=(t // tm,),
        in_specs=[
            pl.BlockSpec((tm, wp), lambda i: (i, 0)),
            pl.BlockSpec((8, wp), lambda i: (jnp.maximum(i * (tm // 8) - 1, 0), 0)),
            full((1, wp)), full((1, w)), full((LANES, w)), full((1, w)), full((LANES, w)),
            full((2 * LANES, w)), full((1, w)), full((1, w)), full((1, w)),
        ],
        out_specs=[pl.BlockSpec((tm, w), lambda i: (i, 0))] * 8,
        out_shape=outs,
        compiler_params=_cparams(("parallel",)),
        name="rwkv_prep",
    )(pr, pr, row(mu), row(w0), w2p, row(a0), a2p, g2p, row(k_k), row(k_a), row(r_k))


def _rwkv_scan_kernel(r_ref, k_ref, v_ref, lw_ref, kn_ref, b_ref, y_ref, s_scr, *, chunk, pairs):
    c2 = 2 * chunk

    @pl.when(pl.program_id(2) == 0)
    def _():
        s_scr[...] = jnp.zeros(s_scr.shape, F32)

    ri = lax.broadcasted_iota(jnp.int32, (chunk, chunk), 0)
    ci = lax.broadcasted_iota(jnp.int32, (chunk, chunk), 1)
    ltri = jnp.where(ri >= ci, 1.0, 0.0).astype(BF16)

    lw = lw_ref[0]
    cl = _dot_sel_x(ltri, lw)
    mid = chunk // 2 - 1
    clm = cl[mid:mid + 1, :]
    cle = cl[chunk - 1:chunk, :]
    w_mid = jnp.exp(cl - clm)
    w_mid_prev = jnp.exp(cl - lw - clm)
    w_mid_inv = jnp.exp(clm - cl)
    w_abs = jnp.exp(cl)
    w_abs_prev = jnp.exp(cl - lw)
    w_end = jnp.exp(cle - cl)
    w_tot = jnp.exp(cle)

    r = r_ref[0]
    k = k_ref[0]
    v = v_ref[0]
    a = -kn_ref[0]
    b = b_ref[0]
    a_mid = a * w_mid_prev
    r_mid = r * w_mid
    b_mid = b * w_mid_inv
    k_mid = k * w_mid_inv
    a_abs = a * w_abs_prev
    r_abs = r * w_abs
    b_end = b * w_end
    k_end = k * w_end

    lane = lax.broadcasted_iota(jnp.int32, (chunk, LANES), 1)
    first = lane < RWKV_HEAD_SIZE

    def stack(x):
        return jnp.concatenate([jnp.where(first, x, 0.0), jnp.where(first, 0.0, x)], axis=0)

    rr = lax.broadcasted_iota(jnp.int32, (c2, c2), 0)
    cc = lax.broadcasted_iota(jnp.int32, (c2, c2), 1)
    same = (rr >= chunk) == (cc >= chunk)
    strict = same & (rr > cc)
    incl = same & (rr >= cc)
    eye = jnp.where(rr == cc, 1.0, 0.0)

    ps = range(pairs)
    sls = [slice(p * LANES, (p + 1) * LANES) for p in ps]
    bf = lambda x: x.astype(BF16)
    cat0 = lambda xs: jnp.concatenate(xs, axis=0)
    cat1 = lambda xs: jnp.concatenate(xs, axis=1)
    v_s = [bf(stack(v[:, sl])) for sl in sls]
    ar_m = [bf(cat0([stack(a_mid[:, sl]), stack(r_mid[:, sl])])) for sl in sls]
    bk_m = [bf(cat0([stack(b_mid[:, sl]), stack(k_mid[:, sl])])) for sl in sls]
    g_all = [_bdot_nt(ar_m[p], bk_m[p]) for p in ps]
    l_ab = [jnp.where(strict, g[:c2, :c2], 0.0) for g in g_all]
    l_ak = [bf(jnp.where(strict, g[:c2, c2:], 0.0)) for g in g_all]
    m_r = [bf(cat1([jnp.where(incl, g[c2:, :c2], 0.0), jnp.where(incl, g[c2:, c2:], 0.0)])) for g in g_all]

    t_inv = [eye + l_ab[p] for p in ps]
    n_fac = max(chunk.bit_length() - 2, 0)
    lp = [_bdot(l_ab[p], l_ab[p]) for p in ps] if n_fac else l_ab
    for _ in range(n_fac - 1):
        x = [_bdot(cat0([t_inv[p], lp[p]]), lp[p]) for p in ps]
        t_inv = [t_inv[p] + x[p][:c2] for p in ps]
        lp = [x[p][c2:] for p in ps]
    if n_fac:
        t_inv = [t_inv[p] + _bdot(t_inv[p], lp[p]) for p in ps]
    t_inv = [bf(t) for t in t_inv]

    lv = [_bdot(l_ak[p], v_s[p]) for p in ps]
    av = [bf(_bdot(t_inv[p], cat1([stack(a_abs[:, sls[p]]), lv[p]]))) for p in ps]
    zero = jnp.zeros((c2, LANES), BF16)
    ry = [_bdot(m_r[p], cat0([av[p], cat1([zero, v_s[p]])])) for p in ps]
    r_p = [stack(r_abs[:, sls[p]]) + ry[p][:, :LANES] for p in ps]

    s_old = [s_scr[p] for p in ps]
    y_s = [_bdot_nt(r_p[p], s_old[p]) + ry[p][:, LANES:] for p in ps]
    for p in ps:
        y_ref[0, :, sls[p]] = y_s[p][:chunk] + y_s[p][chunk:]
    b_e = [bf(stack(b_end[:, sl])) for sl in sls]
    k_e = [bf(stack(k_end[:, sl])) for sl in sls]
    ab = [_bdot_tn(av[p][:, :LANES], b_e[p]) for p in ps]
    q_t = [_bdot_tn(cat0([av[p][:, LANES:], v_s[p]]), cat0([b_e[p], k_e[p]])) for p in ps]
    for p in ps:
        s_scr[p] = s_old[p] * w_tot[:, sls[p]] + _bdot(s_old[p], ab[p]) + q_t[p]


def rwkv_scan(r, k, v, lw, kn, b, chunk, pairs):
    bsz, s, w = r.shape
    pw = pairs * LANES
    spec = pl.BlockSpec((1, chunk, pw), lambda bi, hi, ci: (bi, ci, hi))
    kern = functools.partial(_rwkv_scan_kernel, chunk=chunk, pairs=pairs)
    return pl.pallas_call(
        kern,
        grid=(bsz, w // pw, s // chunk),
        in_specs=[spec] * 6,
        out_specs=spec,
        out_shape=jax.ShapeDtypeStruct((bsz, s, w), F32),
        scratch_shapes=[pltpu.VMEM((pairs, LANES, LANES), F32)],
        compiler_params=_cparams(("parallel", "parallel", "arbitrary")),
        name="rwkv_scan",
    )(r, k, v, lw, kn, b)


def _rwkv_out(y, bonus, g, lnx_g, lnx_b):
    inv_n = 1.0 / RWKV_HEAD_SIZE
    mu = _head_sum(y) * inv_n
    yc = y - mu
    var = _head_sum(yc * yc) * inv_n
    yn = yc * lax.rsqrt(var + RWKV_GN_EPS) * lnx_g + lnx_b
    return (yn + bonus) * g


def _route(logits):
    lane = lax.broadcasted_iota(jnp.int32, logits.shape, 1)
    big = jnp.int32(4 * LANES)
    gmask = lane < N_GROUPS
    gl = jnp.where(gmask, logits, NEG_INF)
    gmax = jnp.max(gl, axis=1, keepdims=True)
    gidx = jnp.min(jnp.where(gl == gmax, lane, big), axis=1, keepdims=True)
    gtop = 1.0 / jnp.sum(jnp.where(gmask, jnp.exp(gl - gmax), 0.0), axis=1, keepdims=True)
    lo = N_GROUPS + EXPERTS_PER_GROUP * gidx
    emask = (lane >= lo) & (lane < lo + EXPERTS_PER_GROUP)
    el = jnp.where(emask, logits, NEG_INF)
    e1 = jnp.max(el, axis=1, keepdims=True)
    i1 = jnp.min(jnp.where(el == e1, lane, big), axis=1, keepdims=True)
    el2 = jnp.where(lane == i1, NEG_INF, el)
    e2 = jnp.max(el2, axis=1, keepdims=True)
    i2 = jnp.min(jnp.where(el2 == e2, lane, big), axis=1, keepdims=True)
    t = jnp.exp(e2 - e1)
    w1 = gtop / (1.0 + t)
    w2 = gtop * t / (1.0 + t)
    id1 = (i1 - N_GROUPS).astype(F32)
    id2 = (i2 - N_GROUPS).astype(F32)
    return jnp.where(lane == 0, id1, jnp.where(lane == 1, id2, jnp.where(lane == 2, w1,
                                                                         jnp.where(lane == 3, w2, 0.0))))


def _outproj_kernel(yd_ref, ys_ref, bonus_ref, gate_ref, lg_ref, lb_ref, h_ref,
                    w1_ref, w2_ref, g_ref, b_ref, rw_hi_ref, rw_lo_ref, rb_ref, h1_ref, info_ref, *, parts):
    rows = h_ref.shape[0] // parts
    sls = [pl.ds(k * rows, rows) for k in range(parts)]
    d = lambda a, b: jnp.dot(a, b, preferred_element_type=F32)
    yr = [_rwkv_out(ys_ref[sl, :], bonus_ref[sl, :], gate_ref[sl, :], lg_ref[...], lb_ref[...]) for sl in sls]
    mix = [d(yd_ref[sl, :], w1_ref[...]) + d(yr[k].astype(BF16), w2_ref[...]) for k, sl in enumerate(sls)]
    h1 = [_layer_norm(DEEPNORM_ALPHA * h_ref[sl, :] + mix[k], g_ref[...], b_ref[...]) for k, sl in enumerate(sls)]
    for k, sl in enumerate(sls):
        h1_ref[sl, :] = h1[k]
        hi = h1[k].astype(BF16)
        lo = (h1[k] - hi.astype(F32)).astype(BF16)
        logits = d(hi, rw_hi_ref[...]) + d(lo, rw_hi_ref[...]) + d(hi, rw_lo_ref[...]) + rb_ref[...]
        info_ref[sl, :] = _route(logits)


def outproj_ln_route(yd, ys, bonus, gate, lnx_g, lnx_b, h, w_out_a, w_out_b, g, b, rw_hi, rw_lo, rb, tm):
    t, d = h.shape
    ka = yd.shape[1]
    kb = ys.shape[1]
    full = lambda shp: pl.BlockSpec(shp, lambda i: (0, 0))
    tile = lambda n: pl.BlockSpec((tm, n), lambda i: (i, 0))
    return pl.pallas_call(
        functools.partial(_outproj_kernel, parts=2),
        grid=(t // tm,),
        in_specs=[
            tile(ka), tile(kb), tile(kb), tile(kb), full((1, kb)), full((1, kb)), tile(d),
            full((ka, d)), full((kb, d)), full((1, d)), full((1, d)),
            full((d, ROUTER_LANES)), full((d, ROUTER_LANES)), full((1, ROUTER_LANES)),
        ],
        out_specs=[
            pl.BlockSpec((tm, d), lambda i: (i, 0)),
            pl.BlockSpec((tm, ROUTER_LANES), lambda i: (i, 0)),
        ],
        out_shape=[jax.ShapeDtypeStruct((t, d), F32), jax.ShapeDtypeStruct((t, ROUTER_LANES), F32)],
        compiler_params=pltpu.CompilerParams(dimension_semantics=("parallel",),
                                             vmem_limit_bytes=MOE_VMEM_LIMIT),
        name="outproj_ln_route",
    )(yd, ys, bonus, gate, lnx_g.reshape(1, kb), lnx_b.reshape(1, kb), h, w_out_a, w_out_b,
      g.reshape(1, d), b.reshape(1, d), rw_hi, rw_lo, rb)


def _moe_rank_kernel(e_ref, pos_ref, ends_ref, run_scr, start_scr, *, tm, row_tile):
    ph = pl.program_id(0)
    i = pl.program_id(1)
    lane = lax.broadcasted_iota(jnp.int32, (tm, LANES), 1)
    onehot = e_ref[...] == lane
    oh = jnp.where(onehot, 1.0, 0.0)

    @pl.when((ph == 0) & (i == 0))
    def _():
        run_scr[...] = jnp.zeros(run_scr.shape, F32)

    @pl.when(ph == 0)
    def _():
        run_scr[...] += jnp.sum(oh, axis=0, keepdims=True)

    @pl.when((ph == 1) & (i == 0))
    def _():
        padded = jnp.floor((run_scr[...] + (row_tile - 1)) * (1.0 / row_tile)) * row_tile
        ri = lax.broadcasted_iota(jnp.int32, (LANES, LANES), 0)
        ci = lax.broadcasted_iota(jnp.int32, (LANES, LANES), 1)
        upper = jnp.where(ri < ci, 1.0, 0.0).astype(BF16)
        parts = _split3(jnp.broadcast_to(padded, (8, LANES)))
        starts = sum(jnp.dot(q, upper, preferred_element_type=F32) for q in parts)[0:1]
        start_scr[...] = starts
        ends_ref[...] = starts + padded
        run_scr[...] = jnp.zeros(run_scr.shape, F32)

    @pl.when(ph == 1)
    def _():
        ri = lax.broadcasted_iota(jnp.int32, (tm, tm), 0)
        ci = lax.broadcasted_iota(jnp.int32, (tm, tm), 1)
        before = jnp.where(ri > ci, 1.0, 0.0).astype(BF16)
        row = jnp.dot(before, oh.astype(BF16), preferred_element_type=F32) + (run_scr[...] + start_scr[...])
        pos_ref[...] = jnp.sum(jnp.where(onehot, row, 0.0), axis=1, keepdims=True).astype(jnp.int32)
        run_scr[...] += jnp.sum(oh, axis=0, keepdims=True)


def moe_rank(e_col, tm, row_tile):
    n = e_col.shape[0]
    return pl.pallas_call(
        functools.partial(_moe_rank_kernel, tm=tm, row_tile=row_tile),
        grid=(2, n // tm),
        in_specs=[pl.BlockSpec((tm, 1), lambda ph, i: (i, 0))],
        out_specs=[pl.BlockSpec((tm, 1), lambda ph, i: (i * ph, 0)),
                   pl.BlockSpec((1, LANES), lambda ph, i: (0, 0))],
        out_shape=[jax.ShapeDtypeStruct((n, 1), jnp.int32), jax.ShapeDtypeStruct((1, LANES), F32)],
        scratch_shapes=[pltpu.VMEM((1, LANES), F32), pltpu.VMEM((1, LANES), F32)],
        compiler_params=_cparams(("arbitrary", "arbitrary")),
        name="moe_rank",
    )(e_col)


def _row_copy(src_hbm, row, dst, r, sem):
    return pltpu.make_async_copy(src_hbm.at[pl.ds(row, 1)], dst.at[pl.ds(r, 1)], sem)


def _pack_bf16_halves(y):
    n = y.shape[1] // 2
    hi = pltpu.bitcast(y[:, :n].astype(BF16).astype(F32), jnp.uint32)
    lo = pltpu.bitcast(y[:, n:].astype(BF16).astype(F32), jnp.uint32)
    return hi | (lo >> 16)


def _unpack_bf16_halves(w):
    hi = pltpu.bitcast(w & jnp.uint32(0xFFFF0000), F32)
    lo = pltpu.bitcast(w << 16, F32)
    return jnp.concatenate([hi, lo], axis=1)


def _moe_expert_kernel(te_ref, first_ref, par_ref, nxt_ref, nact_ref,
                       src0_ref, src1_ref, src2_ref, src_ahead_ref, x_hbm, wg_hbm, wu_hbm, wd_hbm, o_ref,
                       xbuf0, xbuf1, xbuf2, xbuf3, wgf, wuf, wdf, wgb, wub, wdb, sem, wsem, *, tm):
    j = pl.program_id(0)
    n_act = nact_ref[0]
    bufs = (xbuf0, xbuf1, xbuf2, xbuf3)
    nb = len(bufs)

    def token_copy(tok, into, r):
        return pltpu.make_async_copy(x_hbm.at[tok], bufs[into].at[:, r, :], sem.at[into])

    def start_gather(src, into):
        for r in range(tm):
            token_copy(src[0, 0, r], into, r).start(priority=r % 2)

    def wait_gather(into):
        for r in range(tm):
            token_copy(0, into, r).wait()

    def weight_copies(e, par):
        return (pltpu.make_async_copy(wg_hbm.at[e], wgf.at[par], wsem.at[par]),
                pltpu.make_async_copy(wu_hbm.at[e], wuf.at[par], wsem.at[par]),
                pltpu.make_async_copy(wd_hbm.at[e], wdf.at[par], wsem.at[par]))

    @pl.when(j == 0)
    def _():
        start_gather(src0_ref, 0)
        start_gather(src1_ref, 1)
        start_gather(src2_ref, 2)
        for c in weight_copies(te_ref[0], 0):
            c.start(priority=1)

    @pl.when((j < n_act) & (first_ref[j] > 0))
    def _():
        par = par_ref[j]
        for c in weight_copies(te_ref[j], par):
            c.wait()
        wgb[...] = wgf[par].astype(BF16)
        wub[...] = wuf[par].astype(BF16)
        wdb[...] = wdf[par].astype(BF16)

    for slot in range(nb):
        @pl.when((j < n_act) & (j % nb == slot))
        def _():
            wait_gather(slot)
            start_gather(src_ahead_ref, (slot + nb - 1) % nb)
            x = jnp.concatenate([bufs[slot][c] for c in range(bufs[slot].shape[0])], axis=1).astype(BF16)
            hg = jnp.dot(x, wgb[...], preferred_element_type=F32)
            hu = jnp.dot(x, wub[...], preferred_element_type=F32)
            hid = hg * _sigmoid(hg) * hu
            o_ref[...] = _pack_bf16_halves(jnp.dot(hid.astype(BF16), wdb[...], preferred_element_type=F32))

        @pl.when((j == n_act - 1) & (j % nb == slot))
        def _():
            for ahead in range(1, nb):
                wait_gather((slot + ahead) % nb)

    @pl.when((j < n_act) & (first_ref[j] > 0) & (nxt_ref[j] >= 0))
    def _():
        for c in weight_copies(nxt_ref[j], 1 - par_ref[j]):
            c.start(priority=1)

    @pl.when(j >= n_act)
    def _():
        o_ref[...] = jnp.zeros(o_ref.shape, jnp.uint32)


def moe_experts(src_tok, plan, x, wg, wu, wd, n_rows, tm):
    d, f = wg.shape[1:]
    hbm = pl.BlockSpec(memory_space=pl.ANY)
    src3 = src_tok.reshape(n_rows // tm, 1, tm)
    smem_tile = lambda index_map: pl.BlockSpec((1, 1, tm), index_map, memory_space=pltpu.SMEM)
    return pl.pallas_call(
        functools.partial(_moe_expert_kernel, tm=tm),
        grid_spec=pltpu.PrefetchScalarGridSpec(
            num_scalar_prefetch=5,
            grid=(n_rows // tm,),
            in_specs=[smem_tile(lambda j, *_: (0, 0, 0)),
                      smem_tile(lambda j, te, fi, pa, nx, na: (jnp.minimum(1, na[0] - 1), 0, 0)),
                      smem_tile(lambda j, te, fi, pa, nx, na: (jnp.minimum(2, na[0] - 1), 0, 0)),
                      smem_tile(lambda j, te, fi, pa, nx, na: (jnp.minimum(j + 3, na[0] - 1), 0, 0)),
                      hbm, hbm, hbm, hbm],
            out_specs=pl.BlockSpec((tm, d // 2), lambda j, *_: (j, 0)),
            scratch_shapes=[pltpu.VMEM((d // LANES, tm, LANES), F32)] * 4 + [
                            pltpu.VMEM((2, d, f), F32), pltpu.VMEM((2, d, f), F32), pltpu.VMEM((2, f, d), F32),
                            pltpu.VMEM((d, f), BF16), pltpu.VMEM((d, f), BF16), pltpu.VMEM((f, d), BF16),
                            pltpu.SemaphoreType.DMA((4,)), pltpu.SemaphoreType.DMA((2,))],
        ),
        out_shape=jax.ShapeDtypeStruct((n_rows, d // 2), jnp.uint32),
        compiler_params=pltpu.CompilerParams(dimension_semantics=("arbitrary",),
                                             vmem_limit_bytes=MOE_VMEM_LIMIT),
        name="moe_experts",
    )(*plan, src3, src3, src3, src3, x, wg, wu, wd)


def _final_kernel(p1_ref, p2_ref, y_hbm, h1_ref, info_ref, p_ref, g_ref, b_ref, pw_ref, pg_ref, gw_ref,
                  o_ref, ybuf0, ybuf1, sem, *, tm, n_tiles, parts):
    i = pl.program_id(0)
    bufs = (ybuf0, ybuf1)

    def start_gather(tile, into):
        for r in range(tm):
            _row_copy(y_hbm, p1_ref[tile * tm + r], bufs[into].at[0], r, sem.at[into]).start(priority=0)
            _row_copy(y_hbm, p2_ref[tile * tm + r], bufs[into].at[1], r, sem.at[into]).start(priority=1)

    def wait_gather(into):
        for r in range(tm):
            _row_copy(y_hbm, 0, bufs[into].at[0], r, sem.at[into]).wait()
            _row_copy(y_hbm, 0, bufs[into].at[1], r, sem.at[into]).wait()

    @pl.when(i == 0)
    def _():
        start_gather(0, 0)

    for slot in range(2):
        @pl.when(i % 2 == slot)
        def _():
            wait_gather(slot)
            start_gather(jnp.minimum(i + 1, n_tiles - 1), 1 - slot)
            rows = tm // parts
            sls = [pl.ds(k * rows, rows) for k in range(parts)]
            h2, ple = [], []
            for sl in sls:
                info = info_ref[sl, :]
                ffn = (info[:, 2:3] * _unpack_bf16_halves(bufs[slot][0, sl, :])
                       + info[:, 3:4] * _unpack_bf16_halves(bufs[slot][1, sl, :]))
                h2.append(_layer_norm(DEEPNORM_ALPHA * h1_ref[sl, :] + ffn, g_ref[...], b_ref[...]))
                pe = jnp.dot(p_ref[sl, :].astype(BF16), pw_ref[...], preferred_element_type=F32)
                ple.append(pe * lax.rsqrt(jnp.mean(pe * pe, axis=-1, keepdims=True) + LN_EPS) * pg_ref[...])
            gate = [jnp.dot(h.astype(BF16), gw_ref[...], preferred_element_type=F32) for h in h2]
            for k, sl in enumerate(sls):
                o_ref[sl, :] = h2[k] + _sigmoid(gate[k]) * ple[k]

        @pl.when((i == n_tiles - 1) & (i % 2 == slot))
        def _():
            wait_gather(1 - slot)


def final_block(pos1, pos2, ys, h1, info, p, g, b, ple_w, ple_g, gate_w, tm):
    t, d = h1.shape
    dp = p.shape[1]
    tile = lambda n: pl.BlockSpec((tm, n), lambda i, p1, p2: (i, 0))
    full = lambda shp: pl.BlockSpec(shp, lambda i, p1, p2: (0, 0))
    return pl.pallas_call(
        functools.partial(_final_kernel, tm=tm, n_tiles=t // tm, parts=2),
        grid_spec=pltpu.PrefetchScalarGridSpec(
            num_scalar_prefetch=2,
            grid=(t // tm,),
            in_specs=[pl.BlockSpec(memory_space=pl.ANY), tile(d), tile(ROUTER_LANES), tile(dp),
                      full((1, d)), full((1, d)), full((dp, d)), full((1, d)), full((d, d))],
            out_specs=tile(d),
            scratch_shapes=[pltpu.VMEM((2, tm, d // 2), jnp.uint32), pltpu.VMEM((2, tm, d // 2), jnp.uint32),
                            pltpu.SemaphoreType.DMA((2,))],
        ),
        out_shape=jax.ShapeDtypeStruct((t, d), F32),
        compiler_params=pltpu.CompilerParams(dimension_semantics=("arbitrary",),
                                             vmem_limit_bytes=MOE_VMEM_LIMIT),
        name="final_block",
    )(pos1, pos2, ys, h1, info, p, g.reshape(1, d), b.reshape(1, d), ple_w, ple_g.reshape(1, d), gate_w)


def _dispatch_plan(pos, ends_f, n_tok, tm, n_rows):
    n_tiles = n_rows // tm
    ends = ends_f[0, :N_EXPERTS].astype(jnp.int32)
    src_tok = jnp.zeros((n_rows,), jnp.int32).at[pos].set(
        jnp.arange(2 * n_tok, dtype=jnp.int32) % n_tok, unique_indices=True)
    n_active = ends[-1] // tm
    tile_ids = jnp.arange(n_tiles, dtype=jnp.int32)
    tile_expert = jnp.searchsorted(ends, jnp.minimum(tile_ids, n_active - 1) * tm, side="right")
    tile_expert = jnp.minimum(tile_expert, N_EXPERTS - 1).astype(jnp.int32)
    first = jnp.concatenate([jnp.ones((1,), jnp.int32),
                             (tile_expert[1:] != tile_expert[:-1]).astype(jnp.int32)])
    parity = (jnp.cumsum(first) - 1) % 2
    ids = jnp.arange(N_EXPERTS, dtype=jnp.int32)
    present = ends > jnp.concatenate([jnp.zeros((1,), jnp.int32), ends[:-1]])
    later = present[None, :] & (ids[None, :] > ids[:, None])
    next_of = jnp.min(jnp.where(later, ids[None, :], N_EXPERTS), axis=1)
    next_of = jnp.where(next_of == N_EXPERTS, -1, next_of)
    plan = (tile_expert, first, parity.astype(jnp.int32), next_of[tile_expert].astype(jnp.int32),
            n_active.astype(jnp.int32).reshape(1))
    return src_tok, plan


def _t5_bucket(rel):
    n = jnp.maximum(rel, 0)
    max_exact = REL_BUCKETS // 2
    nf = jnp.maximum(n, 1).astype(F32)
    large = max_exact + (jnp.log(nf / max_exact) / math.log(REL_MAX_DIST / max_exact)
                         * (REL_BUCKETS - max_exact)).astype(jnp.int32)
    large = jnp.minimum(large, REL_BUCKETS - 1)
    return jnp.where(n < max_exact, n, large)


def _bias_tables(rel_bias, tq):
    n = 3 * tq
    rel = 2 * tq - 1 - jnp.arange(n, dtype=jnp.int32)
    g = rel_bias[_t5_bucket(rel)].astype(F32).T
    skew = jnp.tile(g, (1, tq))[:, :tq * (n - 1)].reshape(-1, tq, n - 1)
    tiles = jnp.stack([skew[:, :, 2 * tq - 1:3 * tq - 1], skew[:, :, tq - 1:2 * tq - 1]], axis=1)
    far = rel_bias[_t5_bucket(jnp.full((), 2 * tq, jnp.int32))].astype(F32)
    return tiles * LOG2_E, far * LOG2_E


def kernel(x, p, ln_in_g, ln_in_b, rel_bias, w_in, diff_lam_q1, diff_lam_k1, diff_lam_q2, diff_lam_k2, diff_subln_g, rwkv_mu, rwkv_w0, rwkv_w2, rwkv_a0, rwkv_a2, rwkv_g2, rwkv_k_k, rwkv_k_a, rwkv_r_k, rwkv_lnx_g, rwkv_lnx_b, w_out, ln1_g, ln1_b, router_group_w, router_group_b, router_expert_w, router_expert_b, moe_w_gate, moe_w_up, moe_w_down, ln2_g, ln2_b, ple_w, ple_norm_g, ple_gate_w):
    bsz, seq, d = x.shape
    t = bsz * seq
    i = 0
    lambda_init = 0.8 - 0.6 * math.exp(-0.3 * i)
    tq = ATTN_Q_TILE

    x2 = x.reshape(t, d)
    w_in_t = jnp.swapaxes(w_in[i], 0, 1)
    h, hb = layer_norm_rows(x2, ln_in_g, ln_in_b, LN_ROWS)
    qkv = matmul_nt(hb, w_in_t, 0, O_RWKV, O_RWKV, BF16, PROJ_ROWS, PROJ_COLS)
    pr = matmul_nt(hb, w_in_t, O_RWKV, RWKV_PROJ, RWKV_PROJ_PAD, F32, PROJ_ROWS, PROJ_COLS)

    lam = (jnp.exp(jnp.sum(diff_lam_q1[i].astype(F32) * diff_lam_k1[i]))
           - jnp.exp(jnp.sum(diff_lam_q2[i].astype(F32) * diff_lam_k2[i])) + lambda_init)
    bias_tiles, far_bias = _bias_tables(rel_bias, tq)
    y_diff = diff_attention(qkv.reshape(bsz, seq, O_RWKV), bias_tiles, far_bias, lam.reshape(1),
                            diff_subln_g[i], lambda_init, tq, ATTN_HEADS_PER_STEP)

    mu_p = jnp.pad(rwkv_mu[i], (0, RWKV_PROJ_PAD - RWKV_PROJ))
    zeros64 = jnp.zeros((LORA_DECAY, RWKV_WIDTH), F32)
    w2p = jnp.concatenate([rwkv_w2[i], zeros64], axis=0).astype(BF16)
    a2p = jnp.concatenate([zeros64, rwkv_a2[i]], axis=0).astype(BF16)
    g2p = jnp.pad(rwkv_g2[i], ((0, 2 * LANES - LORA_GATE), (0, 0))).astype(BF16)
    r, k2, v, lw, kn, bb, g, bonus = rwkv_prep(
        pr, mu_p, rwkv_w0[i], w2p, rwkv_a0[i], a2p, g2p, rwkv_k_k[i], rwkv_k_a[i],
        rwkv_r_k[i].reshape(-1), seq, PREP_ROWS)
    sh = lambda a: a.reshape(bsz, seq, RWKV_WIDTH)
    y_scan = rwkv_scan(sh(r), sh(k2), sh(v), sh(lw), sh(kn), sh(bb), SCAN_CHUNK, SCAN_PAIRS)

    w_out_b = w_out[i].astype(BF16)
    rw = jnp.concatenate([router_group_w[i], router_expert_w[i]], axis=1)
    rw = jnp.pad(rw, ((0, 0), (0, ROUTER_LANES - rw.shape[1])))
    rw_hi = rw.astype(BF16)
    rw_lo = (rw - rw_hi.astype(F32)).astype(BF16)
    rb = jnp.pad(jnp.concatenate([router_group_b[i], router_expert_b[i]]),
                 (0, ROUTER_LANES - N_GROUPS - N_EXPERTS)).reshape(1, ROUTER_LANES)
    h1, info = outproj_ln_route(y_diff.reshape(t, DIFF_WIDTH), y_scan.reshape(t, RWKV_WIDTH), bonus, g,
                                rwkv_lnx_g[i], rwkv_lnx_b[i], h, w_out_b[:DIFF_WIDTH],
                                w_out_b[DIFF_WIDTH:], ln1_g[i], ln1_b[i], rw_hi, rw_lo, rb, OUTPROJ_ROWS)

    tm_e = MOE_ROW_TILE
    n_rows = 2 * t + N_EXPERTS * tm_e
    e_all = jnp.concatenate([info[:, 0], info[:, 1]]).astype(jnp.int32)
    pos, ends = moe_rank(e_all.reshape(-1, 1), RANK_ROWS, tm_e)
    pos = pos[:, 0]
    src_tok, plan = _dispatch_plan(pos, ends, t, tm_e, n_rows)
    f = moe_w_gate.shape[-1]
    ys = moe_experts(src_tok, plan, h1.reshape(t, d // LANES, LANES), moe_w_gate[i].reshape(N_EXPERTS, d, f),
                     moe_w_up[i].reshape(N_EXPERTS, d, f), moe_w_down[i].reshape(N_EXPERTS, f, d),
                     n_rows, tm_e)

    out = final_block(pos[:t], pos[t:], ys, h1, info, p[i].reshape(t, -1), ln2_g[i], ln2_b[i],
                      ple_w[i].astype(BF16), ple_norm_g[i], ple_gate_w[i].astype(BF16), FINAL_ROWS)
    return out.reshape(bsz, seq, d)
```

```python
import functools
import math

import jax
import jax.numpy as jnp
from jax import lax
from jax.experimental import pallas as pl
from jax.experimental.pallas import tpu as pltpu

F32 = jnp.float32
BF16 = jnp.bfloat16

DIFF_HEADS = 8
DIFF_HEAD_DIM = 64
DIFF_V_DIM = 128
DIFF_WIDTH = DIFF_HEADS * DIFF_V_DIM
DIFF_QK_WIDTH = DIFF_HEADS * 2 * DIFF_HEAD_DIM
RWKV_HEADS = 16
RWKV_HEAD_SIZE = 64
RWKV_WIDTH = RWKV_HEADS * RWKV_HEAD_SIZE
LORA_DECAY = 64
LORA_AAA = 64
LORA_GATE = 160
RWKV_PROJ = 3 * RWKV_WIDTH + LORA_DECAY + LORA_AAA + LORA_GATE
O_RWKV = 2 * DIFF_QK_WIDTH + DIFF_WIDTH
REL_BUCKETS = 32
REL_MAX_DIST = 128
N_GROUPS = 4
EXPERTS_PER_GROUP = 8
N_EXPERTS = N_GROUPS * EXPERTS_PER_GROUP
LN_EPS = 1e-5
RWKV_GN_EPS = 64e-5
NEG_INF = -1e30
DEPTH = 1
DEEPNORM_ALPHA = (2 * DEPTH) ** 0.25
LOG2_E = math.log2(math.e)

LANES = 128
RWKV_PROJ_PAD = 3584
ROUTER_LANES = LANES
VMEM_LIMIT = 48 * 1024 * 1024
MOE_VMEM_LIMIT = 56 * 1024 * 1024
MOE_ROW_TILE = 256

LN_ROWS = 512
PROJ_ROWS = 2048
PROJ_COLS = 512
ATTN_Q_TILE = 256
ATTN_HEADS_PER_STEP = DIFF_HEADS
PREP_ROWS = 256
SCAN_CHUNK = 64
SCAN_PAIRS = RWKV_HEADS // 2
OUTPROJ_ROWS = 512
RANK_ROWS = 1024
FINAL_ROWS = 256


def _cparams(sem):
    return pltpu.CompilerParams(dimension_semantics=sem, vmem_limit_bytes=VMEM_LIMIT)


def _bdot(a, b):
    return jnp.dot(a.astype(BF16), b.astype(BF16), preferred_element_type=F32)


def _bdot_nt(a, b):
    return lax.dot_general(a.astype(BF16), b.astype(BF16), (((1,), (1,)), ((), ())),
                           preferred_element_type=F32)


def _bdot_tn(a, b):
    return lax.dot_general(a.astype(BF16), b.astype(BF16), (((0,), (0,)), ((), ())),
                           preferred_element_type=F32)


def _split3(x):
    hi = x.astype(BF16)
    r1 = x - hi.astype(F32)
    mid = r1.astype(BF16)
    lo = (r1 - mid.astype(F32)).astype(BF16)
    return hi, mid, lo


def _head_sum(x):
    lane = lax.broadcasted_iota(jnp.int32, (x.shape[0], LANES), 1)
    first = lane < RWKV_HEAD_SIZE
    cols = []
    for c in range(x.shape[1] // LANES):
        blk = x[:, c * LANES:(c + 1) * LANES]
        lo = jnp.sum(jnp.where(first, blk, 0.0), axis=1, keepdims=True)
        hi = jnp.sum(jnp.where(first, 0.0, blk), axis=1, keepdims=True)
        cols.append(jnp.where(first, lo, hi))
    return jnp.concatenate(cols, axis=1)


def _dot_sel_x(sel, x):
    hi, mid, lo = _split3(x)
    d = lambda p: jnp.dot(sel, p, preferred_element_type=F32)
    return d(hi) + d(mid) + d(lo)


def _layer_norm(x, g, b):
    mu = jnp.mean(x, axis=-1, keepdims=True)
    xc = x - mu
    var = jnp.mean(xc * xc, axis=-1, keepdims=True)
    return xc * lax.rsqrt(var + LN_EPS) * g + b


def _sigmoid(z):
    return 1.0 / (1.0 + jnp.exp(-z))


def _ln_kernel(x_ref, g_ref, b_ref, h_ref, hb_ref):
    xn = _layer_norm(x_ref[...], g_ref[...], b_ref[...])
    h_ref[...] = xn
    hb_ref[...] = xn.astype(BF16)


def layer_norm_rows(x, g, b, tm):
    m, d = x.shape
    tile = pl.BlockSpec((tm, d), lambda i: (i, 0))
    row = pl.BlockSpec((1, d), lambda i: (0, 0))
    return pl.pallas_call(
        _ln_kernel,
        grid=(m // tm,),
        in_specs=[tile, row, row],
        out_specs=[tile, tile],
        out_shape=[jax.ShapeDtypeStruct((m, d), F32), jax.ShapeDtypeStruct((m, d), BF16)],
        compiler_params=_cparams(("parallel",)),
        name="layer_norm_rows",
    )(x, g.reshape(1, d), b.reshape(1, d))


def _mm_nt_kernel(x_ref, wt_ref, o_ref, *, n_valid, tn):
    y = lax.dot_general(x_ref[...], wt_ref[...].astype(BF16), (((1,), (1,)), ((), ())),
                        preferred_element_type=F32)
    if n_valid % tn:
        col = pl.program_id(1) * tn + lax.broadcasted_iota(jnp.int32, y.shape, 1)
        y = jnp.where(col < n_valid, y, 0.0)
    o_ref[...] = y.astype(o_ref.dtype)


def matmul_nt(x, wt, row0, n_valid, n_out, out_dtype, tm, tn):
    m, k = x.shape
    blk0 = row0 // tn
    return pl.pallas_call(
        functools.partial(_mm_nt_kernel, n_valid=n_valid, tn=tn),
        grid=(m // tm, n_out // tn),
        in_specs=[pl.BlockSpec((tm, k), lambda i, j: (i, 0)),
                  pl.BlockSpec((tn, k), lambda i, j: (blk0 + j, 0))],
        out_specs=pl.BlockSpec((tm, tn), lambda i, j: (i, j)),
        out_shape=jax.ShapeDtypeStruct((m, n_out), out_dtype),
        compiler_params=_cparams(("parallel", "parallel")),
        name="matmul_nt",
    )(x, wt)


def _attn_kernel(lam_ref, far_ref, q_ref, k_ref, v_ref, bias_ref, g_ref, o_ref,
                 m_scr, l_scr, acc_scr, *, tq, hps, scale, post_scale):
    hp = pl.program_id(1)
    qi = pl.program_id(2)
    lane = lax.broadcasted_iota(jnp.int32, (tq, LANES), 1)
    q_maps = []
    for hh in range(hps):
        qs = (q_ref[0, :, hh * LANES:(hh + 1) * LANES].astype(F32) * scale).astype(BF16)
        zero = jnp.zeros_like(qs)
        q_maps += [jnp.where(lane < DIFF_HEAD_DIM, qs, zero), jnp.where(lane >= DIFF_HEAD_DIM, qs, zero)]

    m_scr[...] = jnp.full(m_scr.shape, NEG_INF, F32)
    l_scr[...] = jnp.zeros(l_scr.shape, F32)
    acc_scr[...] = jnp.zeros(acc_scr.shape, F32)

    def step(kstart, bias_index, causal):
        for hh in range(hps):
            kb = k_ref[0, pl.ds(kstart, tq), hh * LANES:(hh + 1) * LANES]
            vb = v_ref[0, pl.ds(kstart, tq), hh * LANES:(hh + 1) * LANES]
            for c in range(2 * hh, 2 * hh + 2):
                s = lax.dot_general(q_maps[c], kb, (((1,), (1,)), ((), ())), preferred_element_type=F32)
                m_old = m_scr[c]
                if bias_index is None:
                    far = far_ref[hp * hps + hh]
                    m_new = jnp.maximum(m_old, jnp.max(s, axis=1, keepdims=True) + far)
                    shift = m_new - far
                else:
                    s = s + bias_ref[hh, bias_index]
                    if causal:
                        ri = lax.broadcasted_iota(jnp.int32, (tq, tq), 0)
                        ci = lax.broadcasted_iota(jnp.int32, (tq, tq), 1)
                        s = jnp.where(ri >= ci, s, NEG_INF)
                    m_new = jnp.maximum(m_old, jnp.max(s, axis=1, keepdims=True))
                    shift = m_new
                p = jnp.exp2(s - jnp.concatenate([shift] * (tq // LANES), axis=1))
                corr = jnp.exp2(m_old - m_new)
                l_scr[c] = corr * l_scr[c] + jnp.sum(p, axis=1, keepdims=True)
                acc_scr[c] = corr * acc_scr[c] + jnp.dot(p.astype(BF16), vb, preferred_element_type=F32)
                m_scr[c] = m_new

    def far_body(kb, carry):
        step(pl.multiple_of(kb * tq, tq), None, False)
        return carry

    lax.fori_loop(0, jnp.maximum(qi - 1, 0), far_body, 0)

    @pl.when(qi >= 1)
    def _():
        step(pl.multiple_of((qi - 1) * tq, tq), 1, False)

    step(pl.multiple_of(qi * tq, tq), 0, True)

    for hh in range(hps):
        c = 2 * hh
        o = acc_scr[c] / l_scr[c] - lam_ref[0] * (acc_scr[c + 1] / l_scr[c + 1])
        ms = jnp.mean(o * o, axis=1, keepdims=True)
        o_ref[0, :, hh * LANES:(hh + 1) * LANES] = (
            o * lax.rsqrt(ms + LN_EPS) * g_ref[...] * post_scale).astype(o_ref.dtype)


def diff_attention(qkv, bias_tiles, far_bias, lam, subln_g, lambda_init, tq, hps):
    bsz, s, _ = qkv.shape
    kern = functools.partial(_attn_kernel, tq=tq, hps=hps, scale=DIFF_HEAD_DIM ** -0.5 * LOG2_E,
                             post_scale=1.0 - lambda_init)
    hw = hps * LANES
    nqk = DIFF_QK_WIDTH // hw
    return pl.pallas_call(
        kern,
        grid=(bsz, DIFF_HEADS // hps, s // tq),
        in_specs=[
            pl.BlockSpec(memory_space=pltpu.SMEM),
            pl.BlockSpec(memory_space=pltpu.SMEM),
            pl.BlockSpec((1, tq, hw), lambda b, h, i: (b, i, h)),
            pl.BlockSpec((1, s, hw), lambda b, h, i: (b, 0, nqk + h)),
            pl.BlockSpec((1, s, hw), lambda b, h, i: (b, 0, 2 * nqk + h)),
            pl.BlockSpec((hps, 2, tq, tq), lambda b, h, i: (h, 0, 0, 0)),
            pl.BlockSpec((1, LANES), lambda b, h, i: (0, 0)),
        ],
        out_specs=pl.BlockSpec((1, tq, hw), lambda b, h, i: (b, i, h)),
        out_shape=jax.ShapeDtypeStruct((bsz, s, DIFF_WIDTH), BF16),
        scratch_shapes=[pltpu.VMEM((2 * hps, tq, LANES), F32), pltpu.VMEM((2 * hps, tq, LANES), F32),
                        pltpu.VMEM((2 * hps, tq, LANES), F32)],
        compiler_params=_cparams(("parallel", "parallel", "arbitrary")),
        name="diff_attention",
    )(lam, far_bias, qkv, qkv, qkv, bias_tiles, subln_g.reshape(1, LANES))


def _rwkv_prep_kernel(x_ref, prev_ref, mu_ref, w0_ref, w2_ref, a0_ref, a2_ref, g2_ref,
                      kk_ref, ka_ref, rk_ref,
                      r_out, k_out, v_out, lw_out, kn_out, b_out, g_out, bonus_out, *, tm, seq):
    i = pl.program_id(0)
    x = x_ref[...]
    w = RWKV_WIDTH
    row = lax.broadcasted_iota(jnp.int32, x.shape, 0)
    is_start = (i * tm) % seq == 0
    last_prev = jnp.where(is_start, 0.0, prev_ref[7:8, :])
    prev = jnp.where(row == 0, last_prev, pltpu.roll(x, 1, 0))
    xs = x + (prev - x) * mu_ref[...]

    r = xs[:, 0:w]
    k = xs[:, w:2 * w]
    v = xs[:, 2 * w:3 * w]
    xwa = xs[:, 3 * w:3 * w + 2 * LORA_DECAY]
    xg = xs[:, 3 * w + LANES:3 * w + LANES + 2 * LANES]

    z = w0_ref[...] + _bdot(jnp.tanh(xwa), w2_ref[...])
    softplus_neg_z = jnp.maximum(-z, 0.0) + jnp.log(1.0 + jnp.exp(-jnp.abs(z)))
    lw_out[...] = -jnp.exp(-softplus_neg_z - 0.5)
    a_lr = _sigmoid(a0_ref[...] + _bdot(xwa, a2_ref[...]))
    g_out[...] = _bdot(_sigmoid(xg), g2_ref[...])

    kk = k * kk_ref[...]
    kn = kk / jnp.maximum(jnp.sqrt(_head_sum(kk * kk)), 1e-12)
    k2 = k * (1.0 + (a_lr - 1.0) * ka_ref[...])
    r_out[...] = r
    k_out[...] = k2
    v_out[...] = v
    kn_out[...] = kn
    b_out[...] = kn * a_lr
    bonus_out[...] = _head_sum(r * k2 * rk_ref[...]) * v


def rwkv_prep(pr, mu, w0, w2p, a0, a2p, g2p, k_k, k_a, r_k, seq, tm):
    t, wp = pr.shape
    w = RWKV_WIDTH
    row = lambda a: a.reshape(1, -1)
    full = lambda shp: pl.BlockSpec(shp, lambda i: (0, 0))
    outs = [jax.ShapeDtypeStruct((t, w), F32)] * 8
    kern = functools.partial(_rwkv_prep_kernel, tm=tm, seq=seq)
    return pl.pallas_call(
        kern,
        grid=(t // tm,),
        in_specs=[
            pl.BlockSpec((tm, wp), lambda i: (i, 0)),
            pl.BlockSpec((8, wp), lambda i: (jnp.maximum(i * (tm // 8) - 1, 0), 0)),
            full((1, wp)), full((1, w)), full((LANES, w)), full((1, w)), full((LANES, w)),
            full((2 * LANES, w)), full((1, w)), full((1, w)), full((1, w)),
        ],
        out_specs=[pl.BlockSpec((tm, w), lambda i: (i, 0))] * 8,
        out_shape=outs,
        compiler_params=_cparams(("parallel",)),
        name="rwkv_prep",
    )(pr, pr, row(mu), row(w0), w2p, row(a0), a2p, g2p, row(k_k), row(k_a), row(r_k))


def _rwkv_scan_kernel(r_ref, k_ref, v_ref, lw_ref, kn_ref, b_ref, y_ref, s_scr, *, chunk, pairs):
    c2 = 2 * chunk

    @pl.when(pl.program_id(2) == 0)
    def _():
        s_scr[...] = jnp.zeros(s_scr.shape, F32)

    ri = lax.broadcasted_iota(jnp.int32, (chunk, chunk), 0)
    ci = lax.broadcasted_iota(jnp.int32, (chunk, chunk), 1)
    ltri = jnp.where(ri >= ci, 1.0, 0.0).astype(BF16)

    lw = lw_ref[0]
    cl = _dot_sel_x(ltri, lw)
    mid = chunk // 2 - 1
    clm = cl[mid:mid + 1, :]
    cle = cl[chunk - 1:chunk, :]
    w_mid = jnp.exp(cl - clm)
    w_mid_prev = jnp.exp(cl - lw - clm)
    w_mid_inv = jnp.exp(clm - cl)
    w_abs = jnp.exp(cl)
    w_abs_prev = jnp.exp(cl - lw)
    w_end = jnp.exp(cle - cl)
    w_tot = jnp.exp(cle)

    r = r_ref[0]
    k = k_ref[0]
    v = v_ref[0]
    a = -kn_ref[0]
    b = b_ref[0]
    a_mid = a * w_mid_prev
    r_mid = r * w_mid
    b_mid = b * w_mid_inv
    k_mid = k * w_mid_inv
    a_abs = a * w_abs_prev
    r_abs = r * w_abs
    b_end = b * w_end
    k_end = k * w_end

    lane = lax.broadcasted_iota(jnp.int32, (chunk, LANES), 1)
    first = lane < RWKV_HEAD_SIZE

    def stack(x):
        return jnp.concatenate([jnp.where(first, x, 0.0), jnp.where(first, 0.0, x)], axis=0)

    rr = lax.broadcasted_iota(jnp.int32, (c2, c2), 0)
    cc = lax.broadcasted_iota(jnp.int32, (c2, c2), 1)
    same = (rr >= chunk) == (cc >= chunk)
    strict = same & (rr > cc)
    incl = same & (rr >= cc)
    eye = jnp.where(rr == cc, 1.0, 0.0)

    ps = range(pairs)
    sls = [slice(p * LANES, (p + 1) * LANES) for p in ps]
    bf = lambda x: x.astype(BF16)
    cat0 = lambda xs: jnp.concatenate(xs, axis=0)
    cat1 = lambda xs: jnp.concatenate(xs, axis=1)
    v_s = [bf(stack(v[:, sl])) for sl in sls]
    ar_m = [bf(cat0([stack(a_mid[:, sl]), stack(r_mid[:, sl])])) for sl in sls]
    bk_m = [bf(cat0([stack(b_mid[:, sl]), stack(k_mid[:, sl])])) for sl in sls]
    g_all = [_bdot_nt(ar_m[p], bk_m[p]) for p in ps]
    l_ab = [jnp.where(strict, g[:c2, :c2], 0.0) for g in g_all]
    l_ak = [bf(jnp.where(strict, g[:c2, c2:], 0.0)) for g in g_all]
    m_r = [bf(cat1([jnp.where(incl, g[c2:, :c2], 0.0), jnp.where(incl, g[c2:, c2:], 0.0)])) for g in g_all]

    t_inv = [eye + l_ab[p] for p in ps]
    n_fac = max(chunk.bit_length() - 2, 0)
    lp = [_bdot(l_ab[p], l_ab[p]) for p in ps] if n_fac else l_ab
    for _ in range(n_fac - 1):
        x = [_bdot(cat0([t_inv[p], lp[p]]), lp[p]) for p in ps]
        t_inv = [t_inv[p] + x[p][:c2] for p in ps]
        lp = [x[p][c2:] for p in ps]
    if n_fac:
        t_inv = [t_inv[p] + _bdot(t_inv[p], lp[p]) for p in ps]
    t_inv = [bf(t) for t in t_inv]

    lv = [_bdot(l_ak[p], v_s[p]) for p in ps]
    av = [bf(_bdot(t_inv[p], cat1([stack(a_abs[:, sls[p]]), lv[p]]))) for p in ps]
    zero = jnp.zeros((c2, LANES), BF16)
    ry = [_bdot(m_r[p], cat0([av[p], cat1([zero, v_s[p]])])) for p in ps]
    r_p = [stack(r_abs[:, sls[p]]) + ry[p][:, :LANES] for p in ps]

    s_old = [s_scr[p] for p in ps]
    y_s = [_bdot_nt(r_p[p], s_old[p]) + ry[p][:, LANES:] for p in ps]
    for p in ps:
        y_ref[0, :, sls[p]] = y_s[p][:chunk] + y_s[p][chunk:]
    b_e = [bf(stack(b_end[:, sl])) for sl in sls]
    k_e = [bf(stack(k_end[:, sl])) for sl in sls]
    ab = [_bdot_tn(av[p][:, :LANES], b_e[p]) for p in ps]
    q_t = [_bdot_tn(cat0([av[p][:, LANES:], v_s[p]]), cat0([b_e[p], k_e[p]])) for p in ps]
    for p in ps:
        s_scr[p] = s_old[p] * w_tot[:, sls[p]] + _bdot(s_old[p], ab[p]) + q_t[p]


def rwkv_scan(r, k, v, lw, kn, b, chunk, pairs):
    bsz, s, w = r.shape
    pw = pairs * LANES
    spec = pl.BlockSpec((1, chunk, pw), lambda bi, hi, ci: (bi, ci, hi))
    kern = functools.partial(_rwkv_scan_kernel, chunk=chunk, pairs=pairs)
    return pl.pallas_call(
        kern,
        grid=(bsz, w // pw, s // chunk),
        in_specs=[spec] * 6,
        out_specs=spec,
        out_shape=jax.ShapeDtypeStruct((bsz, s, w), F32),
        scratch_shapes=[pltpu.VMEM((pairs, LANES, LANES), F32)],
        compiler_params=_cparams(("parallel", "parallel", "arbitrary")),
        name="rwkv_scan",
    )(r, k, v, lw, kn, b)


def _rwkv_out(y, bonus, g, lnx_g, lnx_b):
    inv_n = 1.0 / RWKV_HEAD_SIZE
    mu = _head_sum(y) * inv_n
    yc = y - mu
    var = _head_sum(yc * yc) * inv_n
    yn = yc * lax.rsqrt(var + RWKV_GN_EPS) * lnx_g + lnx_b
    return (yn + bonus) * g


def _route(logits):
    lane = lax.broadcasted_iota(jnp.int32, logits.shape, 1)
    big = jnp.int32(4 * LANES)
    gmask = lane < N_GROUPS
    gl = jnp.where(gmask, logits, NEG_INF)
    gmax = jnp.max(gl, axis=1, keepdims=True)
    gidx = jnp.min(jnp.where(gl == gmax, lane, big), axis=1, keepdims=True)
    gtop = 1.0 / jnp.sum(jnp.where(gmask, jnp.exp(gl - gmax), 0.0), axis=1, keepdims=True)
    lo = N_GROUPS + EXPERTS_PER_GROUP * gidx
    emask = (lane >= lo) & (lane < lo + EXPERTS_PER_GROUP)
    el = jnp.where(emask, logits, NEG_INF)
    e1 = jnp.max(el, axis=1, keepdims=True)
    i1 = jnp.min(jnp.where(el == e1, lane, big), axis=1, keepdims=True)
    el2 = jnp.where(lane == i1, NEG_INF, el)
    e2 = jnp.max(el2, axis=1, keepdims=True)
    i2 = jnp.min(jnp.where(el2 == e2, lane, big), axis=1, keepdims=True)
    t = jnp.exp(e2 - e1)
    w1 = gtop / (1.0 + t)
    w2 = gtop * t / (1.0 + t)
    id1 = (i1 - N_GROUPS).astype(F32)
    id2 = (i2 - N_GROUPS).astype(F32)
    return jnp.where(lane == 0, id1, jnp.where(lane == 1, id2, jnp.where(lane == 2, w1,
                                                                         jnp.where(lane == 3, w2, 0.0))))


def _outproj_kernel(yd_ref, ys_ref, bonus_ref, gate_ref, lg_ref, lb_ref, h_ref,
                    w1_ref, w2_ref, g_ref, b_ref, rw_hi_ref, rw_lo_ref, rb_ref, h1_ref, info_ref, *, parts):
    rows = h_ref.shape[0] // parts
    sls = [pl.ds(k * rows, rows) for k in range(parts)]
    d = lambda a, b: jnp.dot(a, b, preferred_element_type=F32)
    yr = [_rwkv_out(ys_ref[sl, :], bonus_ref[sl, :], gate_ref[sl, :], lg_ref[...], lb_ref[...]) for sl in sls]
    mix = [d(yd_ref[sl, :], w1_ref[...]) + d(yr[k].astype(BF16), w2_ref[...]) for k, sl in enumerate(sls)]
    h1 = [_layer_norm(DEEPNORM_ALPHA * h_ref[sl, :] + mix[k], g_ref[...], b_ref[...]) for k, sl in enumerate(sls)]
    for k, sl in enumerate(sls):
        h1_ref[sl, :] = h1[k]
        hi = h1[k].astype(BF16)
        lo = (h1[k] - hi.astype(F32)).astype(BF16)
        logits = d(hi, rw_hi_ref[...]) + d(lo, rw_hi_ref[...]) + d(hi, rw_lo_ref[...]) + rb_ref[...]
        info_ref[sl, :] = _route(logits)


def outproj_ln_route(yd, ys, bonus, gate, lnx_g, lnx_b, h, w_out_a, w_out_b, g, b, rw_hi, rw_lo, rb, tm):
    t, d = h.shape
    ka = yd.shape[1]
    kb = ys.shape[1]
    full = lambda shp: pl.BlockSpec(shp, lambda i: (0, 0))
    tile = lambda n: pl.BlockSpec((tm, n), lambda i: (i, 0))
    return pl.pallas_call(
        functools.partial(_outproj_kernel, parts=2),
        grid=(t // tm,),
        in_specs=[
            tile(ka), tile(kb), tile(kb), tile(kb), full((1, kb)), full((1, kb)), tile(d),
            full((ka, d)), full((kb, d)), full((1, d)), full((1, d)),
            full((d, ROUTER_LANES)), full((d, ROUTER_LANES)), full((1, ROUTER_LANES)),
        ],
        out_specs=[
            pl.BlockSpec((tm, d), lambda i: (i, 0)),
            pl.BlockSpec((tm, ROUTER_LANES), lambda i: (i, 0)),
        ],
        out_shape=[jax.ShapeDtypeStruct((t, d), F32), jax.ShapeDtypeStruct((t, ROUTER_LANES), F32)],
        compiler_params=pltpu.CompilerParams(dimension_semantics=("parallel",),
                                             vmem_limit_bytes=MOE_VMEM_LIMIT),
        name="outproj_ln_route",
    )(yd, ys, bonus, gate, lnx_g.reshape(1, kb), lnx_b.reshape(1, kb), h, w_out_a, w_out_b,
      g.reshape(1, d), b.reshape(1, d), rw_hi, rw_lo, rb)


def _moe_rank_kernel(e_ref, pos_ref, ends_ref, run_scr, start_scr, *, tm, row_tile):
    ph = pl.program_id(0)
    i = pl.program_id(1)
    lane = lax.broadcasted_iota(jnp.int32, (tm, LANES), 1)
    onehot = e_ref[...] == lane
    oh = jnp.where(onehot, 1.0, 0.0)

    @pl.when((ph == 0) & (i == 0))
    def _():
        run_scr[...] = jnp.zeros(run_scr.shape, F32)

    @pl.when(ph == 0)
    def _():
        run_scr[...] += jnp.sum(oh, axis=0, keepdims=True)

    @pl.when((ph == 1) & (i == 0))
    def _():
        padded = jnp.floor((run_scr[...] + (row_tile - 1)) * (1.0 / row_tile)) * row_tile
        ri = lax.broadcasted_iota(jnp.int32, (LANES, LANES), 0)
        ci = lax.broadcasted_iota(jnp.int32, (LANES, LANES), 1)
        upper = jnp.where(ri < ci, 1.0, 0.0).astype(BF16)
        parts = _split3(jnp.broadcast_to(padded, (8, LANES)))
        starts = sum(jnp.dot(q, upper, preferred_element_type=F32) for q in parts)[0:1]
        start_scr[...] = starts
        ends_ref[...] = starts + padded
        run_scr[...] = jnp.zeros(run_scr.shape, F32)

    @pl.when(ph == 1)
    def _():
        ri = lax.broadcasted_iota(jnp.int32, (tm, tm), 0)
        ci = lax.broadcasted_iota(jnp.int32, (tm, tm), 1)
        before = jnp.where(ri > ci, 1.0, 0.0).astype(BF16)
        row = jnp.dot(before, oh.astype(BF16), preferred_element_type=F32) + (run_scr[...] + start_scr[...])
        pos_ref[...] = jnp.sum(jnp.where(onehot, row, 0.0), axis=1, keepdims=True).astype(jnp.int32)
        run_scr[...] += jnp.sum(oh, axis=0, keepdims=True)


def moe_rank(e_col, tm, row_tile):
    n = e_col.shape[0]
    return pl.pallas_call(
        functools.partial(_moe_rank_kernel, tm=tm, row_tile=row_tile),
        grid=(2, n // tm),
        in_specs=[pl.BlockSpec((tm, 1), lambda ph, i: (i, 0))],
        out_specs=[pl.BlockSpec((tm, 1), lambda ph, i: (i * ph, 0)),
                   pl.BlockSpec((1, LANES), lambda ph, i: (0, 0))],
        out_shape=[jax.ShapeDtypeStruct((n, 1), jnp.int32), jax.ShapeDtypeStruct((1, LANES), F32)],
        scratch_shapes=[pltpu.VMEM((1, LANES), F32), pltpu.VMEM((1, LANES), F32)],
        compiler_params=_cparams(("arbitrary", "arbitrary")),
        name="moe_rank",
    )(e_col)


def _row_copy(src_hbm, row, dst, r, sem):
    return pltpu.make_async_copy(src_hbm.at[pl.ds(row, 1)], dst.at[pl.ds(r, 1)], sem)


def _pack_bf16_halves(y):
    n = y.shape[1] // 2
    hi = pltpu.bitcast(y[:, :n].astype(BF16).astype(F32), jnp.uint32)
    lo = pltpu.bitcast(y[:, n:].astype(BF16).astype(F32), jnp.uint32)
    return hi | (lo >> 16)


def _unpack_bf16_halves(w):
    hi = pltpu.bitcast(w & jnp.uint32(0xFFFF0000), F32)
    lo = pltpu.bitcast(w << 16, F32)
    return jnp.concatenate([hi, lo], axis=1)


def _moe_expert_kernel(te_ref, first_ref, par_ref, nxt_ref, nact_ref,
                       src0_ref, src1_ref, src2_ref, src_ahead_ref, x_hbm, wg_hbm, wu_hbm, wd_hbm, o_ref,
                       xbuf0, xbuf1, xbuf2, xbuf3, wgf, wuf, wdf, wgb, wub, wdb, sem, wsem, *, tm):
    j = pl.program_id(0)
    n_act = nact_ref[0]
    bufs = (xbuf0, xbuf1, xbuf2, xbuf3)
    nb = len(bufs)

    def token_copy(tok, into, r):
        return pltpu.make_async_copy(x_hbm.at[tok], bufs[into].at[:, r, :], sem.at[into])

    def start_gather(src, into):
        for r in range(tm):
            token_copy(src[0, 0, r], into, r).start(priority=r % 2)

    def wait_gather(into):
        for r in range(tm):
            token_copy(0, into, r).wait()

    def weight_copies(e, par):
        return (pltpu.make_async_copy(wg_hbm.at[e], wgf.at[par], wsem.at[par]),
                pltpu.make_async_copy(wu_hbm.at[e], wuf.at[par], wsem.at[par]),
                pltpu.make_async_copy(wd_hbm.at[e], wdf.at[par], wsem.at[par]))

    @pl.when(j == 0)
    def _():
        start_gather(src0_ref, 0)
        start_gather(src1_ref, 1)
        start_gather(src2_ref, 2)
        for q, c in enumerate(weight_copies(te_ref[0], 0)):
            c.start(priority=q % 2)

    @pl.when((j < n_act) & (first_ref[j] > 0))
    def _():
        par = par_ref[j]
        for c in weight_copies(te_ref[j], par):
            c.wait()
        wgb[...] = wgf[par].astype(BF16)
        wub[...] = wuf[par].astype(BF16)
        wdb[...] = wdf[par].astype(BF16)

    for slot in range(nb):
        @pl.when((j < n_act) & (j % nb == slot))
        def _():
            wait_gather(slot)
            start_gather(src_ahead_ref, (slot + nb - 1) % nb)
            x = jnp.concatenate([bufs[slot][c] for c in range(bufs[slot].shape[0])], axis=1).astype(BF16)
            hg = jnp.dot(x, wgb[...], preferred_element_type=F32)
            hu = jnp.dot(x, wub[...], preferred_element_type=F32)
            hid = hg * _sigmoid(hg) * hu
            o_ref[...] = _pack_bf16_halves(jnp.dot(hid.astype(BF16), wdb[...], preferred_element_type=F32))

        @pl.when((j == n_act - 1) & (j % nb == slot))
        def _():
            for ahead in range(1, nb):
                wait_gather((slot + ahead) % nb)

    @pl.when((j < n_act) & (first_ref[j] > 0) & (nxt_ref[j] >= 0))
    def _():
        for q, c in enumerate(weight_copies(nxt_ref[j], 1 - par_ref[j])):
            c.start(priority=q % 2)

    @pl.when(j >= n_act)
    def _():
        o_ref[...] = jnp.zeros(o_ref.shape, jnp.uint32)


def moe_experts(src_tok, plan, x, wg, wu, wd, n_rows, tm):
    d, f = wg.shape[1:]
    hbm = pl.BlockSpec(memory_space=pl.ANY)
    src3 = src_tok.reshape(n_rows // tm, 1, tm)
    smem_tile = lambda index_map: pl.BlockSpec((1, 1, tm), index_map, memory_space=pltpu.SMEM)
    return pl.pallas_call(
        functools.partial(_moe_expert_kernel, tm=tm),
        grid_spec=pltpu.PrefetchScalarGridSpec(
            num_scalar_prefetch=5,
            grid=(n_rows // tm,),
            in_specs=[smem_tile(lambda j, *_: (0, 0, 0)),
                      smem_tile(lambda j, te, fi, pa, nx, na: (jnp.minimum(1, na[0] - 1), 0, 0)),
                      smem_tile(lambda j, te, fi, pa, nx, na: (jnp.minimum(2, na[0] - 1), 0, 0)),
                      smem_tile(lambda j, te, fi, pa, nx, na: (jnp.minimum(j + 3, na[0] - 1), 0, 0)),
                      hbm, hbm, hbm, hbm],
            out_specs=pl.BlockSpec((tm, d // 2), lambda j, *_: (j, 0)),
            scratch_shapes=[pltpu.VMEM((d // LANES, tm, LANES), F32)] * 4 + [
                            pltpu.VMEM((2, d, f), F32), pltpu.VMEM((2, d, f), F32), pltpu.VMEM((2, f, d), F32),
                            pltpu.VMEM((d, f), BF16), pltpu.VMEM((d, f), BF16), pltpu.VMEM((f, d), BF16),
                            pltpu.SemaphoreType.DMA((4,)), pltpu.SemaphoreType.DMA((2,))],
        ),
        out_shape=jax.ShapeDtypeStruct((n_rows, d // 2), jnp.uint32),
        compiler_params=pltpu.CompilerParams(dimension_semantics=("arbitrary",),
                                             vmem_limit_bytes=MOE_VMEM_LIMIT),
        name="moe_experts",
    )(*plan, src3, src3, src3, src3, x, wg, wu, wd)


def _final_kernel(p1_ref, p2_ref, y_hbm, h1_ref, info_ref, p_ref, g_ref, b_ref, pw_ref, pg_ref, gw_ref,
                  o_ref, ybuf0, ybuf1, sem, *, tm, n_tiles, parts):
    i = pl.program_id(0)
    bufs = (ybuf0, ybuf1)

    def start_gather(tile, into):
        for r in range(tm):
            _row_copy(y_hbm, p1_ref[tile * tm + r], bufs[into].at[0], r, sem.at[into]).start(priority=0)
            _row_copy(y_hbm, p2_ref[tile * tm + r], bufs[into].at[1], r, sem.at[into]).start(priority=1)

    def wait_gather(into):
        for r in range(tm):
            _row_copy(y_hbm, 0, bufs[into].at[0], r, sem.at[into]).wait()
            _row_copy(y_hbm, 0, bufs[into].at[1], r, sem.at[into]).wait()

    @pl.when(i == 0)
    def _():
        start_gather(0, 0)

    for slot in range(2):
        @pl.when(i % 2 == slot)
        def _():
            wait_gather(slot)
            start_gather(jnp.minimum(i + 1, n_tiles - 1), 1 - slot)
            rows = tm // parts
            sls = [pl.ds(k * rows, rows) for k in range(parts)]
            h2, ple = [], []
            for sl in sls:
                info = info_ref[sl, :]
                ffn = (info[:, 2:3] * _unpack_bf16_halves(bufs[slot][0, sl, :])
                       + info[:, 3:4] * _unpack_bf16_halves(bufs[slot][1, sl, :]))
                h2.append(_layer_norm(DEEPNORM_ALPHA * h1_ref[sl, :] + ffn, g_ref[...], b_ref[...]))
                pe = jnp.dot(p_ref[sl, :].astype(BF16), pw_ref[...], preferred_element_type=F32)
                ple.append(pe * lax.rsqrt(jnp.mean(pe * pe, axis=-1, keepdims=True) + LN_EPS) * pg_ref[...])
            gate = [jnp.dot(h.astype(BF16), gw_ref[...], preferred_element_type=F32) for h in h2]
            for k, sl in enumerate(sls):
                o_ref[sl, :] = h2[k] + _sigmoid(gate[k]) * ple[k]

        @pl.when((i == n_tiles - 1) & (i % 2 == slot))
        def _():
            wait_gather(1 - slot)


def final_block(pos1, pos2, ys, h1, info, p, g, b, ple_w, ple_g, gate_w, tm):
    t, d = h1.shape
    dp = p.shape[1]
    tile = lambda n: pl.BlockSpec((tm, n), lambda i, p1, p2: (i, 0))
    full = lambda shp: pl.BlockSpec(shp, lambda i, p1, p2: (0, 0))
    return pl.pallas_call(
        functools.partial(_final_kernel, tm=tm, n_tiles=t // tm, parts=2),
        grid_spec=pltpu.PrefetchScalarGridSpec(
            num_scalar_prefetch=2,
            grid=(t // tm,),
            in_specs=[pl.BlockSpec(memory_space=pl.ANY), tile(d), tile(ROUTER_LANES), tile(dp),
                      full((1, d)), full((1, d)), full((dp, d)), full((1, d)), full((d, d))],
            out_specs=tile(d),
            scratch_shapes=[pltpu.VMEM((2, tm, d // 2), jnp.uint32), pltpu.VMEM((2, tm, d // 2), jnp.uint32),
                            pltpu.SemaphoreType.DMA((2,))],
        ),
        out_shape=jax.ShapeDtypeStruct((t, d), F32),
        compiler_params=pltpu.CompilerParams(dimension_semantics=("arbitrary",),
                                             vmem_limit_bytes=MOE_VMEM_LIMIT),
        name="final_block",
    )(pos1, pos2, ys, h1, info, p, g.reshape(1, d), b.reshape(1, d), ple_w, ple_g.reshape(1, d), gate_w)


def _dispatch_plan(pos, ends_f, n_tok, tm, n_rows):
    n_tiles = n_rows // tm
    ends = ends_f[0, :N_EXPERTS].astype(jnp.int32)
    src_tok = jnp.zeros((n_rows,), jnp.int32).at[pos].set(
        jnp.arange(2 * n_tok, dtype=jnp.int32) % n_tok, unique_indices=True)
    n_active = ends[-1] // tm
    tile_ids = jnp.arange(n_tiles, dtype=jnp.int32)
    tile_expert = jnp.searchsorted(ends, jnp.minimum(tile_ids, n_active - 1) * tm, side="right")
    tile_expert = jnp.minimum(tile_expert, N_EXPERTS - 1).astype(jnp.int32)
    first = jnp.concatenate([jnp.ones((1,), jnp.int32),
                             (tile_expert[1:] != tile_expert[:-1]).astype(jnp.int32)])
    parity = (jnp.cumsum(first) - 1) % 2
    ids = jnp.arange(N_EXPERTS, dtype=jnp.int32)
    present = ends > jnp.concatenate([jnp.zeros((1,), jnp.int32), ends[:-1]])
    later = present[None, :] & (ids[None, :] > ids[:, None])
    next_of = jnp.min(jnp.where(later, ids[None, :], N_EXPERTS), axis=1)
    next_of = jnp.where(next_of == N_EXPERTS, -1, next_of)
    plan = (tile_expert, first, parity.astype(jnp.int32), next_of[tile_expert].astype(jnp.int32),
            n_active.astype(jnp.int32).reshape(1))
    return src_tok, plan


def _t5_bucket(rel):
    n = jnp.maximum(rel, 0)
    max_exact = REL_BUCKETS // 2
    nf = jnp.maximum(n, 1).astype(F32)
    large = max_exact + (jnp.log(nf / max_exact) / math.log(REL_MAX_DIST / max_exact)
                         * (REL_BUCKETS - max_exact)).astype(jnp.int32)
    large = jnp.minimum(large, REL_BUCKETS - 1)
    return jnp.where(n < max_exact, n, large)


def _bias_tables(rel_bias, tq):
    n = 3 * tq
    rel = 2 * tq - 1 - jnp.arange(n, dtype=jnp.int32)
    g = rel_bias[_t5_bucket(rel)].astype(F32).T
    skew = jnp.tile(g, (1, tq))[:, :tq * (n - 1)].reshape(-1, tq, n - 1)
    tiles = jnp.stack([skew[:, :, 2 * tq - 1:3 * tq - 1], skew[:, :, tq - 1:2 * tq - 1]], axis=1)
    far = rel_bias[_t5_bucket(jnp.full((), 2 * tq, jnp.int32))].astype(F32)
    return tiles * LOG2_E, far * LOG2_E


def kernel(x, p, ln_in_g, ln_in_b, rel_bias, w_in, diff_lam_q1, diff_lam_k1, diff_lam_q2, diff_lam_k2, diff_subln_g, rwkv_mu, rwkv_w0, rwkv_w2, rwkv_a0, rwkv_a2, rwkv_g2, rwkv_k_k, rwkv_k_a, rwkv_r_k, rwkv_lnx_g, rwkv_lnx_b, w_out, ln1_g, ln1_b, router_group_w, router_group_b, router_expert_w, router_expert_b, moe_w_gate, moe_w_up, moe_w_down, ln2_g, ln2_b, ple_w, ple_norm_g, ple_gate_w):
    bsz, seq, d = x.shape
    t = bsz * seq
    i = 0
    lambda_init = 0.8 - 0.6 * math.exp(-0.3 * i)
    tq = ATTN_Q_TILE

    x2 = x.reshape(t, d)
    w_in_t = jnp.swapaxes(w_in[i], 0, 1)
    h, hb = layer_norm_rows(x2, ln_in_g, ln_in_b, LN_ROWS)
    qkv = matmul_nt(hb, w_in_t, 0, O_RWKV, O_RWKV, BF16, PROJ_ROWS, PROJ_COLS)
    pr = matmul_nt(hb, w_in_t, O_RWKV, RWKV_PROJ, RWKV_PROJ_PAD, F32, PROJ_ROWS, PROJ_COLS)

    lam = (jnp.exp(jnp.sum(diff_lam_q1[i].astype(F32) * diff_lam_k1[i]))
           - jnp.exp(jnp.sum(diff_lam_q2[i].astype(F32) * diff_lam_k2[i])) + lambda_init)
    bias_tiles, far_bias = _bias_tables(rel_bias, tq)
    y_diff = diff_attention(qkv.reshape(bsz, seq, O_RWKV), bias_tiles, far_bias, lam.reshape(1),
                            diff_subln_g[i], lambda_init, tq, ATTN_HEADS_PER_STEP)

    mu_p = jnp.pad(rwkv_mu[i], (0, RWKV_PROJ_PAD - RWKV_PROJ))
    zeros64 = jnp.zeros((LORA_DECAY, RWKV_WIDTH), F32)
    w2p = jnp.concatenate([rwkv_w2[i], zeros64], axis=0).astype(BF16)
    a2p = jnp.concatenate([zeros64, rwkv_a2[i]], axis=0).astype(BF16)
    g2p = jnp.pad(rwkv_g2[i], ((0, 2 * LANES - LORA_GATE), (0, 0))).astype(BF16)
    r, k2, v, lw, kn, bb, g, bonus = rwkv_prep(
        pr, mu_p, rwkv_w0[i], w2p, rwkv_a0[i], a2p, g2p, rwkv_k_k[i], rwkv_k_a[i],
        rwkv_r_k[i].reshape(-1), seq, PREP_ROWS)
    sh = lambda a: a.reshape(bsz, seq, RWKV_WIDTH)
    y_scan = rwkv_scan(sh(r), sh(k2), sh(v), sh(lw), sh(kn), sh(bb), SCAN_CHUNK, SCAN_PAIRS)

    w_out_b = w_out[i].astype(BF16)
    rw = jnp.concatenate([router_group_w[i], router_expert_w[i]], axis=1)
    rw = jnp.pad(rw, ((0, 0), (0, ROUTER_LANES - rw.shape[1])))
    rw_hi = rw.astype(BF16)
    rw_lo = (rw - rw_hi.astype(F32)).astype(BF16)
    rb = jnp.pad(jnp.concatenate([router_group_b[i], router_expert_b[i]]),
                 (0, ROUTER_LANES - N_GROUPS - N_EXPERTS)).reshape(1, ROUTER_LANES)
    h1, info = outproj_ln_route(y_diff.reshape(t, DIFF_WIDTH), y_scan.reshape(t, RWKV_WIDTH), bonus, g,
                                rwkv_lnx_g[i], rwkv_lnx_b[i], h, w_out_b[:DIFF_WIDTH],
                                w_out_b[DIFF_WIDTH:], ln1_g[i], ln1_b[i], rw_hi, rw_lo, rb, OUTPROJ_ROWS)

    tm_e = MOE_ROW_TILE
    n_rows = 2 * t + N_EXPERTS * tm_e
    e_all = jnp.concatenate([info[:, 0], info[:, 1]]).astype(jnp.int32)
    pos, ends = moe_rank(e_all.reshape(-1, 1), RANK_ROWS, tm_e)
    pos = pos[:, 0]
    src_tok, plan = _dispatch_plan(pos, ends, t, tm_e, n_rows)
    f = moe_w_gate.shape[-1]
    ys = moe_experts(src_tok, plan, h1.reshape(t, d // LANES, LANES), moe_w_gate[i].reshape(N_EXPERTS, d, f),
                     moe_w_up[i].reshape(N_EXPERTS, d, f), moe_w_down[i].reshape(N_EXPERTS, f, d),
                     n_rows, tm_e)

    out = final_block(pos[:t], pos[t:], ys, h1, info, p[i].reshape(t, -1), ln2_g[i], ln2_b[i],
                      ple_w[i].astype(BF16), ple_norm_g[i], ple_gate_w[i].astype(BF16), FINAL_ROWS)
    return out.reshape(bsz, seq, d)
```

```python
import functools
import math

import jax
import jax.numpy as jnp
from jax import lax
from jax.experimental import pallas as pl
from jax.experimental.pallas import tpu as pltpu

F32 = jnp.float32
BF16 = jnp.bfloat16

DIFF_HEADS = 8
DIFF_HEAD_DIM = 64
DIFF_V_DIM = 128
DIFF_WIDTH = DIFF_HEADS * DIFF_V_DIM
DIFF_QK_WIDTH = DIFF_HEADS * 2 * DIFF_HEAD_DIM
RWKV_HEADS = 16
RWKV_HEAD_SIZE = 64
RWKV_WIDTH = RWKV_HEADS * RWKV_HEAD_SIZE
LORA_DECAY = 64
LORA_AAA = 64
LORA_GATE = 160
RWKV_PROJ = 3 * RWKV_WIDTH + LORA_DECAY + LORA_AAA + LORA_GATE
O_RWKV = 2 * DIFF_QK_WIDTH + DIFF_WIDTH
REL_BUCKETS = 32
REL_MAX_DIST = 128
N_GROUPS = 4
EXPERTS_PER_GROUP = 8
N_EXPERTS = N_GROUPS * EXPERTS_PER_GROUP
LN_EPS = 1e-5
RWKV_GN_EPS = 64e-5
NEG_INF = -1e30
DEPTH = 1
DEEPNORM_ALPHA = (2 * DEPTH) ** 0.25
LOG2_E = math.log2(math.e)

LANES = 128
RWKV_PROJ_PAD = 3584
ROUTER_LANES = LANES
VMEM_LIMIT = 48 * 1024 * 1024
MOE_VMEM_LIMIT = 56 * 1024 * 1024
MOE_ROW_TILE = 256

LN_ROWS = 512
PROJ_ROWS = 2048
PROJ_COLS = 512
ATTN_Q_TILE = 256
ATTN_HEADS_PER_STEP = DIFF_HEADS
PREP_ROWS = 256
SCAN_CHUNK = 64
SCAN_PAIRS = RWKV_HEADS // 2
OUTPROJ_ROWS = 512
RANK_ROWS = 512
FINAL_ROWS = 256


def _cparams(sem):
    return pltpu.CompilerParams(dimension_semantics=sem, vmem_limit_bytes=VMEM_LIMIT)


def _bdot(a, b):
    return jnp.dot(a.astype(BF16), b.astype(BF16), preferred_element_type=F32)


def _bdot_nt(a, b):
    return lax.dot_general(a.astype(BF16), b.astype(BF16), (((1,), (1,)), ((), ())),
                           preferred_element_type=F32)


def _bdot_tn(a, b):
    return lax.dot_general(a.astype(BF16), b.astype(BF16), (((0,), (0,)), ((), ())),
                           preferred_element_type=F32)


def _split3(x):
    hi = x.astype(BF16)
    r1 = x - hi.astype(F32)
    mid = r1.astype(BF16)
    lo = (r1 - mid.astype(F32)).astype(BF16)
    return hi, mid, lo


def _head_sum(x):
    lane = lax.broadcasted_iota(jnp.int32, (x.shape[0], LANES), 1)
    first = lane < RWKV_HEAD_SIZE
    cols = []
    for c in range(x.shape[1] // LANES):
        blk = x[:, c * LANES:(c + 1) * LANES]
        lo = jnp.sum(jnp.where(first, blk, 0.0), axis=1, keepdims=True)
        hi = jnp.sum(jnp.where(first, 0.0, blk), axis=1, keepdims=True)
        cols.append(jnp.where(first, lo, hi))
    return jnp.concatenate(cols, axis=1)


def _dot_sel_x(sel, x):
    hi, mid, lo = _split3(x)
    d = lambda p: jnp.dot(sel, p, preferred_element_type=F32)
    return d(hi) + d(mid) + d(lo)


def _layer_norm(x, g, b):
    mu = jnp.mean(x, axis=-1, keepdims=True)
    xc = x - mu
    var = jnp.mean(xc * xc, axis=-1, keepdims=True)
    return xc * lax.rsqrt(var + LN_EPS) * g + b


def _sigmoid(z):
    return 1.0 / (1.0 + jnp.exp(-z))


def _ln_kernel(x_ref, g_ref, b_ref, h_ref, hb_ref):
    xn = _layer_norm(x_ref[...], g_ref[...], b_ref[...])
    h_ref[...] = xn
    hb_ref[...] = xn.astype(BF16)


def layer_norm_rows(x, g, b, tm):
    m, d = x.shape
    tile = pl.BlockSpec((tm, d), lambda i: (i, 0))
    row = pl.BlockSpec((1, d), lambda i: (0, 0))
    return pl.pallas_call(
        _ln_kernel,
        grid=(m // tm,),
        in_specs=[tile, row, row],
        out_specs=[tile, tile],
        out_shape=[jax.ShapeDtypeStruct((m, d), F32), jax.ShapeDtypeStruct((m, d), BF16)],
        compiler_params=_cparams(("parallel",)),
        name="layer_norm_rows",
    )(x, g.reshape(1, d), b.reshape(1, d))


def _mm_nt_kernel(x_ref, wt_ref, o_ref, *, n_valid, tn):
    y = lax.dot_general(x_ref[...], wt_ref[...].astype(BF16), (((1,), (1,)), ((), ())),
                        preferred_element_type=F32)
    if n_valid % tn:
        col = pl.program_id(1) * tn + lax.broadcasted_iota(jnp.int32, y.shape, 1)
        y = jnp.where(col < n_valid, y, 0.0)
    o_ref[...] = y.astype(o_ref.dtype)


def matmul_nt(x, wt, row0, n_valid, n_out, out_dtype, tm, tn):
    m, k = x.shape
    blk0 = row0 // tn
    return pl.pallas_call(
        functools.partial(_mm_nt_kernel, n_valid=n_valid, tn=tn),
        grid=(m // tm, n_out // tn),
        in_specs=[pl.BlockSpec((tm, k), lambda i, j: (i, 0)),
                  pl.BlockSpec((tn, k), lambda i, j: (blk0 + j, 0))],
        out_specs=pl.BlockSpec((tm, tn), lambda i, j: (i, j)),
        out_shape=jax.ShapeDtypeStruct((m, n_out), out_dtype),
        compiler_params=_cparams(("parallel", "parallel")),
        name="matmul_nt",
    )(x, wt)


def _attn_kernel(lam_ref, far_ref, q_ref, k_ref, v_ref, bias_ref, g_ref, o_ref,
                 m_scr, l_scr, acc_scr, *, tq, hps, scale, post_scale):
    hp = pl.program_id(1)
    qi = pl.program_id(2)
    lane = lax.broadcasted_iota(jnp.int32, (tq, LANES), 1)
    q_maps = []
    for hh in range(hps):
        qs = (q_ref[0, :, hh * LANES:(hh + 1) * LANES].astype(F32) * scale).astype(BF16)
        zero = jnp.zeros_like(qs)
        q_maps += [jnp.where(lane < DIFF_HEAD_DIM, qs, zero), jnp.where(lane >= DIFF_HEAD_DIM, qs, zero)]

    m_scr[...] = jnp.full(m_scr.shape, NEG_INF, F32)
    l_scr[...] = jnp.zeros(l_scr.shape, F32)
    acc_scr[...] = jnp.zeros(acc_scr.shape, F32)

    def step(kstart, bias_index, causal):
        for hh in range(hps):
            kb = k_ref[0, pl.ds(kstart, tq), hh * LANES:(hh + 1) * LANES]
            vb = v_ref[0, pl.ds(kstart, tq), hh * LANES:(hh + 1) * LANES]
            for c in range(2 * hh, 2 * hh + 2):
                s = lax.dot_general(q_maps[c], kb, (((1,), (1,)), ((), ())), preferred_element_type=F32)
                m_old = m_scr[c]
                if bias_index is None:
                    far = far_ref[hp * hps + hh]
                    m_new = jnp.maximum(m_old, jnp.max(s, axis=1, keepdims=True) + far)
                    shift = m_new - far
                else:
                    s = s + bias_ref[hh, bias_index]
                    if causal:
                        ri = lax.broadcasted_iota(jnp.int32, (tq, tq), 0)
                        ci = lax.broadcasted_iota(jnp.int32, (tq, tq), 1)
                        s = jnp.where(ri >= ci, s, NEG_INF)
                    m_new = jnp.maximum(m_old, jnp.max(s, axis=1, keepdims=True))
                    shift = m_new
                p = jnp.exp2(s - jnp.concatenate([shift] * (tq // LANES), axis=1))
                corr = jnp.exp2(m_old - m_new)
                l_scr[c] = corr * l_scr[c] + jnp.sum(p, axis=1, keepdims=True)
                acc_scr[c] = corr * acc_scr[c] + jnp.dot(p.astype(BF16), vb, preferred_element_type=F32)
                m_scr[c] = m_new

    def far_body(kb, carry):
        step(pl.multiple_of(kb * tq, tq), None, False)
        return carry

    lax.fori_loop(0, jnp.maximum(qi - 1, 0), far_body, 0)

    @pl.when(qi >= 1)
    def _():
        step(pl.multiple_of((qi - 1) * tq, tq), 1, False)

    step(pl.multiple_of(qi * tq, tq), 0, True)

    for hh in range(hps):
        c = 2 * hh
        o = acc_scr[c] / l_scr[c] - lam_ref[0] * (acc_scr[c + 1] / l_scr[c + 1])
        ms = jnp.mean(o * o, axis=1, keepdims=True)
        o_ref[0, :, hh * LANES:(hh + 1) * LANES] = (
            o * lax.rsqrt(ms + LN_EPS) * g_ref[...] * post_scale).astype(o_ref.dtype)


def diff_attention(qkv, bias_tiles, far_bias, lam, subln_g, lambda_init, tq, hps):
    bsz, s, _ = qkv.shape
    kern = functools.partial(_attn_kernel, tq=tq, hps=hps, scale=DIFF_HEAD_DIM ** -0.5 * LOG2_E,
                             post_scale=1.0 - lambda_init)
    hw = hps * LANES
    nqk = DIFF_QK_WIDTH // hw
    return pl.pallas_call(
        kern,
        grid=(bsz, DIFF_HEADS // hps, s // tq),
        in_specs=[
            pl.BlockSpec(memory_space=pltpu.SMEM),
            pl.BlockSpec(memory_space=pltpu.SMEM),
            pl.BlockSpec((1, tq, hw), lambda b, h, i: (b, i, h)),
            pl.BlockSpec((1, s, hw), lambda b, h, i: (b, 0, nqk + h)),
            pl.BlockSpec((1, s, hw), lambda b, h, i: (b, 0, 2 * nqk + h)),
            pl.BlockSpec((hps, 2, tq, tq), lambda b, h, i: (h, 0, 0, 0)),
            pl.BlockSpec((1, LANES), lambda b, h, i: (0, 0)),
        ],
        out_specs=pl.BlockSpec((1, tq, hw), lambda b, h, i: (b, i, h)),
        out_shape=jax.ShapeDtypeStruct((bsz, s, DIFF_WIDTH), BF16),
        scratch_shapes=[pltpu.VMEM((2 * hps, tq, LANES), F32), pltpu.VMEM((2 * hps, tq, LANES), F32),
                        pltpu.VMEM((2 * hps, tq, LANES), F32)],
        compiler_params=_cparams(("parallel", "parallel", "arbitrary")),
        name="diff_attention",
    )(lam, far_bias, qkv, qkv, qkv, bias_tiles, subln_g.reshape(1, LANES))


def _rwkv_prep_kernel(x_ref, prev_ref, mu_ref, w0_ref, w2_ref, a0_ref, a2_ref, g2_ref,
                      kk_ref, ka_ref, rk_ref,
                      r_out, k_out, v_out, lw_out, kn_out, b_out, g_out, bonus_out, *, tm, seq):
    i = pl.program_id(0)
    x = x_ref[...]
    w = RWKV_WIDTH
    row = lax.broadcasted_iota(jnp.int32, x.shape, 0)
    is_start = (i * tm) % seq == 0
    last_prev = jnp.where(is_start, 0.0, prev_ref[7:8, :])
    prev = jnp.where(row == 0, last_prev, pltpu.roll(x, 1, 0))
    xs = x + (prev - x) * mu_ref[...]

    r = xs[:, 0:w]
    k = xs[:, w:2 * w]
    v = xs[:, 2 * w:3 * w]
    xwa = xs[:, 3 * w:3 * w + 2 * LORA_DECAY]
    xg = xs[:, 3 * w + LANES:3 * w + LANES + 2 * LANES]

    z = w0_ref[...] + _bdot(jnp.tanh(xwa), w2_ref[...])
    softplus_neg_z = jnp.maximum(-z, 0.0) + jnp.log(1.0 + jnp.exp(-jnp.abs(z)))
    lw_out[...] = -jnp.exp(-softplus_neg_z - 0.5)
    a_lr = _sigmoid(a0_ref[...] + _bdot(xwa, a2_ref[...]))
    g_out[...] = _bdot(_sigmoid(xg), g2_ref[...])

    kk = k * kk_ref[...]
    kn = kk / jnp.maximum(jnp.sqrt(_head_sum(kk * kk)), 1e-12)
    k2 = k * (1.0 + (a_lr - 1.0) * ka_ref[...])
    r_out[...] = r
    k_out[...] = k2
    v_out[...] = v
    kn_out[...] = kn
    b_out[...] = kn * a_lr
    bonus_out[...] = _head_sum(r * k2 * rk_ref[...]) * v


def rwkv_prep(pr, mu, w0, w2p, a0, a2p, g2p, k_k, k_a, r_k, seq, tm):
    t, wp = pr.shape
    w = RWKV_WIDTH
    row = lambda a: a.reshape(1, -1)
    full = lambda shp: pl.BlockSpec(shp, lambda i: (0, 0))
    outs = [jax.ShapeDtypeStruct((t, w), F32)] * 8
    kern = functools.partial(_rwkv_prep_kernel, tm=tm, seq=seq)
    return pl.pallas_call(
        kern,
        grid=(t // tm,),
        in_specs=[
            pl.BlockSpec((tm, wp), lambda i: (i, 0)),
            pl.BlockSpec((8, wp), lambda i: (jnp.maximum(i * (tm // 8) - 1, 0), 0)),
            full((1, wp)), full((1, w)), full((LANES, w)), full((1, w)), full((LANES, w)),
            full((2 * LANES, w)), full((1, w)), full((1, w)), full((1, w)),
        ],
        out_specs=[pl.BlockSpec((tm, w), lambda i: (i, 0))] * 8,
        out_shape=outs,
        compiler_params=_cparams(("parallel",)),
        name="rwkv_prep",
    )(pr, pr, row(mu), row(w0), w2p, row(a0), a2p, g2p, row(k_k), row(k_a), row(r_k))


def _rwkv_scan_kernel(r_ref, k_ref, v_ref, lw_ref, kn_ref, b_ref, y_ref, s_scr, *, chunk, pairs):
    c2 = 2 * chunk

    @pl.when(pl.program_id(2) == 0)
    def _():
        s_scr[...] = jnp.zeros(s_scr.shape, F32)

    ri = lax.broadcasted_iota(jnp.int32, (chunk, chunk), 0)
    ci = lax.broadcasted_iota(jnp.int32, (chunk, chunk), 1)
    ltri = jnp.where(ri >= ci, 1.0, 0.0).astype(BF16)

    lw = lw_ref[0]
    cl = _dot_sel_x(ltri, lw)
    mid = chunk // 2 - 1
    clm = cl[mid:mid + 1, :]
    cle = cl[chunk - 1:chunk, :]
    w_mid = jnp.exp(cl - clm)
    w_mid_prev = jnp.exp(cl - lw - clm)
    w_mid_inv = jnp.exp(clm - cl)
    w_abs = jnp.exp(cl)
    w_abs_prev = jnp.exp(cl - lw)
    w_end = jnp.exp(cle - cl)
    w_tot = jnp.exp(cle)

    r = r_ref[0]
    k = k_ref[0]
    v = v_ref[0]
    a = -kn_ref[0]
    b = b_ref[0]
    a_mid = a * w_mid_prev
    r_mid = r * w_mid
    b_mid = b * w_mid_inv
    k_mid = k * w_mid_inv
    a_abs = a * w_abs_prev
    r_abs = r * w_abs
    b_end = b * w_end
    k_end = k * w_end

    lane = lax.broadcasted_iota(jnp.int32, (chunk, LANES), 1)
    first = lane < RWKV_HEAD_SIZE

    def stack(x):
        return jnp.concatenate([jnp.where(first, x, 0.0), jnp.where(first, 0.0, x)], axis=0)

    rr = lax.broadcasted_iota(jnp.int32, (c2, c2), 0)
    cc = lax.broadcasted_iota(jnp.int32, (c2, c2), 1)
    same = (rr >= chunk) == (cc >= chunk)
    strict = same & (rr > cc)
    incl = same & (rr >= cc)
    eye = jnp.where(rr == cc, 1.0, 0.0)

    ps = range(pairs)
    sls = [slice(p * LANES, (p + 1) * LANES) for p in ps]
    bf = lambda x: x.astype(BF16)
    cat0 = lambda xs: jnp.concatenate(xs, axis=0)
    cat1 = lambda xs: jnp.concatenate(xs, axis=1)
    v_s = [bf(stack(v[:, sl])) for sl in sls]
    ar_m = [bf(cat0([stack(a_mid[:, sl]), stack(r_mid[:, sl])])) for sl in sls]
    bk_m = [bf(cat0([stack(b_mid[:, sl]), stack(k_mid[:, sl])])) for sl in sls]
    g_all = [_bdot_nt(ar_m[p], bk_m[p]) for p in ps]
    l_ab = [jnp.where(strict, g[:c2, :c2], 0.0) for g in g_all]
    l_ak = [bf(jnp.where(strict, g[:c2, c2:], 0.0)) for g in g_all]
    m_r = [bf(cat1([jnp.where(incl, g[c2:, :c2], 0.0), jnp.where(incl, g[c2:, c2:], 0.0)])) for g in g_all]

    t_inv = [eye + l_ab[p] for p in ps]
    n_fac = max(chunk.bit_length() - 2, 0)
    lp = [_bdot(l_ab[p], l_ab[p]) for p in ps] if n_fac else l_ab
    for _ in range(n_fac - 1):
        x = [_bdot(cat0([t_inv[p], lp[p]]), lp[p]) for p in ps]
        t_inv = [t_inv[p] + x[p][:c2] for p in ps]
        lp = [x[p][c2:] for p in ps]
    if n_fac:
        t_inv = [t_inv[p] + _bdot(t_inv[p], lp[p]) for p in ps]
    t_inv = [bf(t) for t in t_inv]

    lv = [_bdot(l_ak[p], v_s[p]) for p in ps]
    av = [bf(_bdot(t_inv[p], cat1([stack(a_abs[:, sls[p]]), lv[p]]))) for p in ps]
    zero = jnp.zeros((c2, LANES), BF16)
    ry = [_bdot(m_r[p], cat0([av[p], cat1([zero, v_s[p]])])) for p in ps]
    r_p = [stack(r_abs[:, sls[p]]) + ry[p][:, :LANES] for p in ps]

    s_old = [s_scr[p] for p in ps]
    y_s = [_bdot_nt(r_p[p], s_old[p]) + ry[p][:, LANES:] for p in ps]
    for p in ps:
        y_ref[0, :, sls[p]] = y_s[p][:chunk] + y_s[p][chunk:]
    b_e = [bf(stack(b_end[:, sl])) for sl in sls]
    k_e = [bf(stack(k_end[:, sl])) for sl in sls]
    ab = [_bdot_tn(av[p][:, :LANES], b_e[p]) for p in ps]
    q_t = [_bdot_tn(cat0([av[p][:, LANES:], v_s[p]]), cat0([b_e[p], k_e[p]])) for p in ps]
    for p in ps:
        s_scr[p] = s_old[p] * w_tot[:, sls[p]] + _bdot(s_old[p], ab[p]) + q_t[p]


def rwkv_scan(r, k, v, lw, kn, b, chunk, pairs):
    bsz, s, w = r.shape
    pw = pairs * LANES
    spec = pl.BlockSpec((1, chunk, pw), lambda bi, hi, ci: (bi, ci, hi))
    kern = functools.partial(_rwkv_scan_kernel, chunk=chunk, pairs=pairs)
    return pl.pallas_call(
        kern,
        grid=(bsz, w // pw, s // chunk),
        in_specs=[spec] * 6,
        out_specs=spec,
        out_shape=jax.ShapeDtypeStruct((bsz, s, w), F32),
        scratch_shapes=[pltpu.VMEM((pairs, LANES, LANES), F32)],
        compiler_params=_cparams(("parallel", "parallel", "arbitrary")),
        name="rwkv_scan",
    )(r, k, v, lw, kn, b)


def _rwkv_out(y, bonus, g, lnx_g, lnx_b):
    inv_n = 1.0 / RWKV_HEAD_SIZE
    mu = _head_sum(y) * inv_n
    yc = y - mu
    var = _head_sum(yc * yc) * inv_n
    yn = yc * lax.rsqrt(var + RWKV_GN_EPS) * lnx_g + lnx_b
    return (yn + bonus) * g


def _route(logits):
    lane = lax.broadcasted_iota(jnp.int32, logits.shape, 1)
    big = jnp.int32(4 * LANES)
    gmask = lane < N_GROUPS
    gl = jnp.where(gmask, logits, NEG_INF)
    gmax = jnp.max(gl, axis=1, keepdims=True)
    gidx = jnp.min(jnp.where(gl == gmax, lane, big), axis=1, keepdims=True)
    gtop = 1.0 / jnp.sum(jnp.where(gmask, jnp.exp(gl - gmax), 0.0), axis=1, keepdims=True)
    lo = N_GROUPS + EXPERTS_PER_GROUP * gidx
    emask = (lane >= lo) & (lane < lo + EXPERTS_PER_GROUP)
    el = jnp.where(emask, logits, NEG_INF)
    e1 = jnp.max(el, axis=1, keepdims=True)
    i1 = jnp.min(jnp.where(el == e1, lane, big), axis=1, keepdims=True)
    el2 = jnp.where(lane == i1, NEG_INF, el)
    e2 = jnp.max(el2, axis=1, keepdims=True)
    i2 = jnp.min(jnp.where(el2 == e2, lane, big), axis=1, keepdims=True)
    t = jnp.exp(e2 - e1)
    w1 = gtop / (1.0 + t)
    w2 = gtop * t / (1.0 + t)
    id1 = (i1 - N_GROUPS).astype(F32)
    id2 = (i2 - N_GROUPS).astype(F32)
    return jnp.where(lane == 0, id1, jnp.where(lane == 1, id2, jnp.where(lane == 2, w1,
                                                                         jnp.where(lane == 3, w2, 0.0))))


def _outproj_kernel(yd_ref, ys_ref, bonus_ref, gate_ref, lg_ref, lb_ref, h_ref,
                    w1_ref, w2_ref, g_ref, b_ref, rw_hi_ref, rw_lo_ref, rb_ref, h1_ref, info_ref, *, parts):
    rows = h_ref.shape[0] // parts
    sls = [pl.ds(k * rows, rows) for k in range(parts)]
    d = lambda a, b: jnp.dot(a, b, preferred_element_type=F32)
    yr = [_rwkv_out(ys_ref[sl, :], bonus_ref[sl, :], gate_ref[sl, :], lg_ref[...], lb_ref[...]) for sl in sls]
    mix = [d(yd_ref[sl, :], w1_ref[...]) + d(yr[k].astype(BF16), w2_ref[...]) for k, sl in enumerate(sls)]
    h1 = [_layer_norm(DEEPNORM_ALPHA * h_ref[sl, :] + mix[k], g_ref[...], b_ref[...]) for k, sl in enumerate(sls)]
    for k, sl in enumerate(sls):
        h1_ref[sl, :] = h1[k]
        hi = h1[k].astype(BF16)
        lo = (h1[k] - hi.astype(F32)).astype(BF16)
        logits = d(hi, rw_hi_ref[...]) + d(lo, rw_hi_ref[...]) + d(hi, rw_lo_ref[...]) + rb_ref[...]
        info_ref[sl, :] = _route(logits)


def outproj_ln_route(yd, ys, bonus, gate, lnx_g, lnx_b, h, w_out_a, w_out_b, g, b, rw_hi, rw_lo, rb, tm):
    t, d = h.shape
    ka = yd.shape[1]
    kb = ys.shape[1]
    full = lambda shp: pl.BlockSpec(shp, lambda i: (0, 0))
    tile = lambda n: pl.BlockSpec((tm, n), lambda i: (i, 0))
    return pl.pallas_call(
        functools.partial(_outproj_kernel, parts=2),
        grid=(t // tm,),
        in_specs=[
            tile(ka), tile(kb), tile(kb), tile(kb), full((1, kb)), full((1, kb)), tile(d),
            full((ka, d)), full((kb, d)), full((1, d)), full((1, d)),
            full((d, ROUTER_LANES)), full((d, ROUTER_LANES)), full((1, ROUTER_LANES)),
        ],
        out_specs=[
            pl.BlockSpec((tm, d), lambda i: (i, 0)),
            pl.BlockSpec((tm, ROUTER_LANES), lambda i: (i, 0)),
        ],
        out_shape=[jax.ShapeDtypeStruct((t, d), F32), jax.ShapeDtypeStruct((t, ROUTER_LANES), F32)],
        compiler_params=pltpu.CompilerParams(dimension_semantics=("parallel",),
                                             vmem_limit_bytes=MOE_VMEM_LIMIT),
        name="outproj_ln_route",
    )(yd, ys, bonus, gate, lnx_g.reshape(1, kb), lnx_b.reshape(1, kb), h, w_out_a, w_out_b,
      g.reshape(1, d), b.reshape(1, d), rw_hi, rw_lo, rb)


def _moe_rank_kernel(e_ref, pos_ref, ends_ref, run_scr, start_scr, *, tm, row_tile):
    ph = pl.program_id(0)
    i = pl.program_id(1)
    lane = lax.broadcasted_iota(jnp.int32, (tm, LANES), 1)
    onehot = e_ref[...] == lane
    oh = jnp.where(onehot, 1.0, 0.0)

    @pl.when((ph == 0) & (i == 0))
    def _():
        run_scr[...] = jnp.zeros(run_scr.shape, F32)

    @pl.when(ph == 0)
    def _():
        run_scr[...] += jnp.sum(oh, axis=0, keepdims=True)

    @pl.when((ph == 1) & (i == 0))
    def _():
        padded = jnp.floor((run_scr[...] + (row_tile - 1)) * (1.0 / row_tile)) * row_tile
        ri = lax.broadcasted_iota(jnp.int32, (LANES, LANES), 0)
        ci = lax.broadcasted_iota(jnp.int32, (LANES, LANES), 1)
        upper = jnp.where(ri < ci, 1.0, 0.0).astype(BF16)
        parts = _split3(jnp.broadcast_to(padded, (8, LANES)))
        starts = sum(jnp.dot(q, upper, preferred_element_type=F32) for q in parts)[0:1]
        start_scr[...] = starts
        ends_ref[...] = starts + padded
        run_scr[...] = jnp.zeros(run_scr.shape, F32)

    @pl.when(ph == 1)
    def _():
        ri = lax.broadcasted_iota(jnp.int32, (tm, tm), 0)
        ci = lax.broadcasted_iota(jnp.int32, (tm, tm), 1)
        before = jnp.where(ri > ci, 1.0, 0.0).astype(BF16)
        row = jnp.dot(before, oh.astype(BF16), preferred_element_type=F32) + (run_scr[...] + start_scr[...])
        pos_ref[...] = jnp.sum(jnp.where(onehot, row, 0.0), axis=1, keepdims=True).astype(jnp.int32)
        run_scr[...] += jnp.sum(oh, axis=0, keepdims=True)


def moe_rank(e_col, tm, row_tile):
    n = e_col.shape[0]
    return pl.pallas_call(
        functools.partial(_moe_rank_kernel, tm=tm, row_tile=row_tile),
        grid=(2, n // tm),
        in_specs=[pl.BlockSpec((tm, 1), lambda ph, i: (i, 0))],
        out_specs=[pl.BlockSpec((tm, 1), lambda ph, i: (i * ph, 0)),
                   pl.BlockSpec((1, LANES), lambda ph, i: (0, 0))],
        out_shape=[jax.ShapeDtypeStruct((n, 1), jnp.int32), jax.ShapeDtypeStruct((1, LANES), F32)],
        scratch_shapes=[pltpu.VMEM((1, LANES), F32), pltpu.VMEM((1, LANES), F32)],
        compiler_params=_cparams(("arbitrary", "arbitrary")),
        name="moe_rank",
    )(e_col)


def _row_copy(src_hbm, row, dst, r, sem):
    return pltpu.make_async_copy(src_hbm.at[pl.ds(row, 1)], dst.at[pl.ds(r, 1)], sem)


def _pack_bf16_halves(y):
    n = y.shape[1] // 2
    hi = pltpu.bitcast(y[:, :n].astype(BF16).astype(F32), jnp.uint32)
    lo = pltpu.bitcast(y[:, n:].astype(BF16).astype(F32), jnp.uint32)
    return hi | (lo >> 16)


def _unpack_bf16_halves(w):
    hi = pltpu.bitcast(w & jnp.uint32(0xFFFF0000), F32)
    lo = pltpu.bitcast(w << 16, F32)
    return jnp.concatenate([hi, lo], axis=1)


def _moe_expert_kernel(te_ref, first_ref, par_ref, nxt_ref, nact_ref,
                       src0_ref, src1_ref, src2_ref, src_ahead_ref, x_hbm, wg_hbm, wu_hbm, wd_hbm, o_ref,
                       xbuf0, xbuf1, xbuf2, xbuf3, wgf, wuf, wdf, wgb, wub, wdb, sem, wsem, *, tm):
    j = pl.program_id(0)
    n_act = nact_ref[0]
    bufs = (xbuf0, xbuf1, xbuf2, xbuf3)
    nb = len(bufs)

    def token_copy(tok, into, r):
        return pltpu.make_async_copy(x_hbm.at[tok], bufs[into].at[:, r, :], sem.at[into])

    def start_gather(src, into):
        for r in range(tm):
            token_copy(src[0, 0, r], into, r).start(priority=r % 2)

    def wait_gather(into):
        for r in range(tm):
            token_copy(0, into, r).wait()

    def weight_copies(e, par):
        return (pltpu.make_async_copy(wg_hbm.at[e], wgf.at[par], wsem.at[par]),
                pltpu.make_async_copy(wu_hbm.at[e], wuf.at[par], wsem.at[par]),
                pltpu.make_async_copy(wd_hbm.at[e], wdf.at[par], wsem.at[par]))

    @pl.when(j == 0)
    def _():
        start_gather(src0_ref, 0)
        start_gather(src1_ref, 1)
        start_gather(src2_ref, 2)
        for c in weight_copies(te_ref[0], 0):
            c.start(priority=1)

    @pl.when((j < n_act) & (first_ref[j] > 0))
    def _():
        par = par_ref[j]
        for c in weight_copies(te_ref[j], par):
            c.wait()
        wgb[...] = wgf[par].astype(BF16)
        wub[...] = wuf[par].astype(BF16)
        wdb[...] = wdf[par].astype(BF16)

    for slot in range(nb):
        @pl.when((j < n_act) & (j % nb == slot))
        def _():
            wait_gather(slot)
            start_gather(src_ahead_ref, (slot + nb - 1) % nb)
            x = jnp.concatenate([bufs[slot][c] for c in range(bufs[slot].shape[0])], axis=1).astype(BF16)
            hg = jnp.dot(x, wgb[...], preferred_element_type=F32)
            hu = jnp.dot(x, wub[...], preferred_element_type=F32)
            hid = hg * _sigmoid(hg) * hu
            o_ref[...] = _pack_bf16_halves(jnp.dot(hid.astype(BF16), wdb[...], preferred_element_type=F32))

        @pl.when((j == n_act - 1) & (j % nb == slot))
        def _():
            for ahead in range(1, nb):
                wait_gather((slot + ahead) % nb)

    @pl.when((j < n_act) & (first_ref[j] > 0) & (nxt_ref[j] >= 0))
    def _():
        for c in weight_copies(nxt_ref[j], 1 - par_ref[j]):
            c.start(priority=1)

    @pl.when(j >= n_act)
    def _():
        o_ref[...] = jnp.zeros(o_ref.shape, jnp.uint32)


def moe_experts(src_tok, plan, x, wg, wu, wd, n_rows, tm):
    d, f = wg.shape[1:]
    hbm = pl.BlockSpec(memory_space=pl.ANY)
    src3 = src_tok.reshape(n_rows // tm, 1, tm)
    smem_tile = lambda index_map: pl.BlockSpec((1, 1, tm), index_map, memory_space=pltpu.SMEM)
    return pl.pallas_call(
        functools.partial(_moe_expert_kernel, tm=tm),
        grid_spec=pltpu.PrefetchScalarGridSpec(
            num_scalar_prefetch=5,
            grid=(n_rows // tm,),
            in_specs=[smem_tile(lambda j, *_: (0, 0, 0)),
                      smem_tile(lambda j, te, fi, pa, nx, na: (jnp.minimum(1, na[0] - 1), 0, 0)),
                      smem_tile(lambda j, te, fi, pa, nx, na: (jnp.minimum(2, na[0] - 1), 0, 0)),
                      smem_tile(lambda j, te, fi, pa, nx, na: (jnp.minimum(j + 3, na[0] - 1), 0, 0)),
                      hbm, hbm, hbm, hbm],
            out_specs=pl.BlockSpec((tm, d // 2), lambda j, *_: (j, 0)),
            scratch_shapes=[pltpu.VMEM((d // LANES, tm, LANES), F32)] * 4 + [
                            pltpu.VMEM((2, d, f), F32), pltpu.VMEM((2, d, f), F32), pltpu.VMEM((2, f, d), F32),
                            pltpu.VMEM((d, f), BF16), pltpu.VMEM((d, f), BF16), pltpu.VMEM((f, d), BF16),
                            pltpu.SemaphoreType.DMA((4,)), pltpu.SemaphoreType.DMA((2,))],
        ),
        out_shape=jax.ShapeDtypeStruct((n_rows, d // 2), jnp.uint32),
        compiler_params=pltpu.CompilerParams(dimension_semantics=("arbitrary",),
                                             vmem_limit_bytes=MOE_VMEM_LIMIT),
        name="moe_experts",
    )(*plan, src3, src3, src3, src3, x, wg, wu, wd)


def _final_kernel(p1_ref, p2_ref, y_hbm, h1_ref, info_ref, p_ref, g_ref, b_ref, pw_ref, pg_ref, gw_ref,
                  o_ref, ybuf0, ybuf1, sem, *, tm, n_tiles, parts):
    i = pl.program_id(0)
    bufs = (ybuf0, ybuf1)

    def start_gather(tile, into):
        for r in range(tm):
            _row_copy(y_hbm, p1_ref[tile * tm + r], bufs[into].at[0], r, sem.at[into]).start(priority=0)
            _row_copy(y_hbm, p2_ref[tile * tm + r], bufs[into].at[1], r, sem.at[into]).start(priority=1)

    def wait_gather(into):
        for r in range(tm):
            _row_copy(y_hbm, 0, bufs[into].at[0], r, sem.at[into]).wait()
            _row_copy(y_hbm, 0, bufs[into].at[1], r, sem.at[into]).wait()

    @pl.when(i == 0)
    def _():
        start_gather(0, 0)

    for slot in range(2):
        @pl.when(i % 2 == slot)
        def _():
            wait_gather(slot)
            start_gather(jnp.minimum(i + 1, n_tiles - 1), 1 - slot)
            rows = tm // parts
            sls = [pl.ds(k * rows, rows) for k in range(parts)]
            h2, ple = [], []
            for sl in sls:
                info = info_ref[sl, :]
                ffn = (info[:, 2:3] * _unpack_bf16_halves(bufs[slot][0, sl, :])
                       + info[:, 3:4] * _unpack_bf16_halves(bufs[slot][1, sl, :]))
                h2.append(_layer_norm(DEEPNORM_ALPHA * h1_ref[sl, :] + ffn, g_ref[...], b_ref[...]))
                pe = jnp.dot(p_ref[sl, :].astype(BF16), pw_ref[...], preferred_element_type=F32)
                ple.append(pe * lax.rsqrt(jnp.mean(pe * pe, axis=-1, keepdims=True) + LN_EPS) * pg_ref[...])
            gate = [jnp.dot(h.astype(BF16), gw_ref[...], preferred_element_type=F32) for h in h2]
            for k, sl in enumerate(sls):
                o_ref[sl, :] = h2[k] + _sigmoid(gate[k]) * ple[k]

        @pl.when((i == n_tiles - 1) & (i % 2 == slot))
        def _():
            wait_gather(1 - slot)


def final_block(pos1, pos2, ys, h1, info, p, g, b, ple_w, ple_g, gate_w, tm):
    t, d = h1.shape
    dp = p.shape[1]
    tile = lambda n: pl.BlockSpec((tm, n), lambda i, p1, p2: (i, 0))
    full = lambda shp: pl.BlockSpec(shp, lambda i, p1, p2: (0, 0))
    return pl.pallas_call(
        functools.partial(_final_kernel, tm=tm, n_tiles=t // tm, parts=2),
        grid_spec=pltpu.PrefetchScalarGridSpec(
            num_scalar_prefetch=2,
            grid=(t // tm,),
            in_specs=[pl.BlockSpec(memory_space=pl.ANY), tile(d), tile(ROUTER_LANES), tile(dp),
                      full((1, d)), full((1, d)), full((dp, d)), full((1, d)), full((d, d))],
            out_specs=tile(d),
            scratch_shapes=[pltpu.VMEM((2, tm, d // 2), jnp.uint32), pltpu.VMEM((2, tm, d // 2), jnp.uint32),
                            pltpu.SemaphoreType.DMA((2,))],
        ),
        out_shape=jax.ShapeDtypeStruct((t, d), F32),
        compiler_params=pltpu.CompilerParams(dimension_semantics=("arbitrary",),
                                             vmem_limit_bytes=MOE_VMEM_LIMIT),
        name="final_block",
    )(pos1, pos2, ys, h1, info, p, g.reshape(1, d), b.reshape(1, d), ple_w, ple_g.reshape(1, d), gate_w)


def _dispatch_plan(pos, ends_f, n_tok, tm, n_rows):
    n_tiles = n_rows // tm
    ends = ends_f[0, :N_EXPERTS].astype(jnp.int32)
    src_tok = jnp.zeros((n_rows,), jnp.int32).at[pos].set(
        jnp.arange(2 * n_tok, dtype=jnp.int32) % n_tok, unique_indices=True)
    n_active = ends[-1] // tm
    tile_ids = jnp.arange(n_tiles, dtype=jnp.int32)
    tile_expert = jnp.searchsorted(ends, jnp.minimum(tile_ids, n_active - 1) * tm, side="right")
    tile_expert = jnp.minimum(tile_expert, N_EXPERTS - 1).astype(jnp.int32)
    first = jnp.concatenate([jnp.ones((1,), jnp.int32),
                             (tile_expert[1:] != tile_expert[:-1]).astype(jnp.int32)])
    parity = (jnp.cumsum(first) - 1) % 2
    ids = jnp.arange(N_EXPERTS, dtype=jnp.int32)
    present = ends > jnp.concatenate([jnp.zeros((1,), jnp.int32), ends[:-1]])
    later = present[None, :] & (ids[None, :] > ids[:, None])
    next_of = jnp.min(jnp.where(later, ids[None, :], N_EXPERTS), axis=1)
    next_of = jnp.where(next_of == N_EXPERTS, -1, next_of)
    plan = (tile_expert, first, parity.astype(jnp.int32), next_of[tile_expert].astype(jnp.int32),
            n_active.astype(jnp.int32).reshape(1))
    return src_tok, plan


def _t5_bucket(rel):
    n = jnp.maximum(rel, 0)
    max_exact = REL_BUCKETS // 2
    nf = jnp.maximum(n, 1).astype(F32)
    large = max_exact + (jnp.log(nf / max_exact) / math.log(REL_MAX_DIST / max_exact)
                         * (REL_BUCKETS - max_exact)).astype(jnp.int32)
    large = jnp.minimum(large, REL_BUCKETS - 1)
    return jnp.where(n < max_exact, n, large)


def _bias_tables(rel_bias, tq):
    n = 3 * tq
    rel = 2 * tq - 1 - jnp.arange(n, dtype=jnp.int32)
    g = rel_bias[_t5_bucket(rel)].astype(F32).T
    skew = jnp.tile(g, (1, tq))[:, :tq * (n - 1)].reshape(-1, tq, n - 1)
    tiles = jnp.stack([skew[:, :, 2 * tq - 1:3 * tq - 1], skew[:, :, tq - 1:2 * tq - 1]], axis=1)
    far = rel_bias[_t5_bucket(jnp.full((), 2 * tq, jnp.int32))].astype(F32)
    return tiles * LOG2_E, far * LOG2_E


def kernel(x, p, ln_in_g, ln_in_b, rel_bias, w_in, diff_lam_q1, diff_lam_k1, diff_lam_q2, diff_lam_k2, diff_subln_g, rwkv_mu, rwkv_w0, rwkv_w2, rwkv_a0, rwkv_a2, rwkv_g2, rwkv_k_k, rwkv_k_a, rwkv_r_k, rwkv_lnx_g, rwkv_lnx_b, w_out, ln1_g, ln1_b, router_group_w, router_group_b, router_expert_w, router_expert_b, moe_w_gate, moe_w_up, moe_w_down, ln2_g, ln2_b, ple_w, ple_norm_g, ple_gate_w):
    bsz, seq, d = x.shape
    t = bsz * seq
    i = 0
    lambda_init = 0.8 - 0.6 * math.exp(-0.3 * i)
    tq = ATTN_Q_TILE

    x2 = x.reshape(t, d)
    w_in_t = jnp.swapaxes(w_in[i], 0, 1)
    h, hb = layer_norm_rows(x2, ln_in_g, ln_in_b, LN_ROWS)
    qkv = matmul_nt(hb, w_in_t, 0, O_RWKV, O_RWKV, BF16, PROJ_ROWS, PROJ_COLS)
    pr = matmul_nt(hb, w_in_t, O_RWKV, RWKV_PROJ, RWKV_PROJ_PAD, F32, PROJ_ROWS, PROJ_COLS)

    lam = (jnp.exp(jnp.sum(diff_lam_q1[i].astype(F32) * diff_lam_k1[i]))
           - jnp.exp(jnp.sum(diff_lam_q2[i].astype(F32) * diff_lam_k2[i])) + lambda_init)
    bias_tiles, far_bias = _bias_tables(rel_bias, tq)
    y_diff = diff_attention(qkv.reshape(bsz, seq, O_RWKV), bias_tiles, far_bias, lam.reshape(1),
                            diff_subln_g[i], lambda_init, tq, ATTN_HEADS_PER_STEP)

    mu_p = jnp.pad(rwkv_mu[i], (0, RWKV_PROJ_PAD - RWKV_PROJ))
    zeros64 = jnp.zeros((LORA_DECAY, RWKV_WIDTH), F32)
    w2p = jnp.concatenate([rwkv_w2[i], zeros64], axis=0).astype(BF16)
    a2p = jnp.concatenate([zeros64, rwkv_a2[i]], axis=0).astype(BF16)
    g2p = jnp.pad(rwkv_g2[i], ((0, 2 * LANES - LORA_GATE), (0, 0))).astype(BF16)
    r, k2, v, lw, kn, bb, g, bonus = rwkv_prep(
        pr, mu_p, rwkv_w0[i], w2p, rwkv_a0[i], a2p, g2p, rwkv_k_k[i], rwkv_k_a[i],
        rwkv_r_k[i].reshape(-1), seq, PREP_ROWS)
    sh = lambda a: a.reshape(bsz, seq, RWKV_WIDTH)
    y_scan = rwkv_scan(sh(r), sh(k2), sh(v), sh(lw), sh(kn), sh(bb), SCAN_CHUNK, SCAN_PAIRS)

    w_out_b = w_out[i].astype(BF16)
    rw = jnp.concatenate([router_group_w[i], router_expert_w[i]], axis=1)
    rw = jnp.pad(rw, ((0, 0), (0, ROUTER_LANES - rw.shape[1])))
    rw_hi = rw.astype(BF16)
    rw_lo = (rw - rw_hi.astype(F32)).astype(BF16)
    rb = jnp.pad(jnp.concatenate([router_group_b[i], router_expert_b[i]]),
                 (0, ROUTER_LANES - N_GROUPS - N_EXPERTS)).reshape(1, ROUTER_LANES)
    h1, info = outproj_ln_route(y_diff.reshape(t, DIFF_WIDTH), y_scan.reshape(t, RWKV_WIDTH), bonus, g,
                                rwkv_lnx_g[i], rwkv_lnx_b[i], h, w_out_b[:DIFF_WIDTH],
                                w_out_b[DIFF_WIDTH:], ln1_g[i], ln1_b[i], rw_hi, rw_lo, rb, OUTPROJ_ROWS)

    tm_e = MOE_ROW_TILE
    n_rows = 2 * t + N_EXPERTS * tm_e
    e_all = jnp.concatenate([info[:, 0], info[:, 1]]).astype(jnp.int32)
    pos, ends = moe_rank(e_all.reshape(-1, 1), RANK_ROWS, tm_e)
    pos = pos[:, 0]
    src_tok, plan = _dispatch_plan(pos, ends, t, tm_e, n_rows)
    f = moe_w_gate.shape[-1]
    ys = moe_experts(src_tok, plan, h1.reshape(t, d // LANES, LANES), moe_w_gate[i].reshape(N_EXPERTS, d, f),
                     moe_w_up[i].reshape(N_EXPERTS, d, f), moe_w_down[i].reshape(N_EXPERTS, f, d),
                     n_rows, tm_e)

    out = final_block(pos[:t], pos[t:], ys, h1, info, p[i].reshape(t, -1), ln2_g[i], ln2_b[i],
                      ple_w[i].astype(BF16), ple_norm_g[i], ple_gate_w[i].astype(BF16), FINAL_ROWS)
    return out.reshape(bsz, seq, d)
```

```python
import functools
import math

import jax
import jax.numpy as jnp
from jax import lax
from jax.experimental import pallas as pl
from jax.experimental.pallas import tpu as pltpu

F32 = jnp.float32
BF16 = jnp.bfloat16

DIFF_HEADS = 8
DIFF_HEAD_DIM = 64
DIFF_V_DIM = 128
DIFF_WIDTH = DIFF_HEADS * DIFF_V_DIM
DIFF_QK_WIDTH = DIFF_HEADS * 2 * DIFF_HEAD_DIM
RWKV_HEADS = 16
RWKV_HEAD_SIZE = 64
RWKV_WIDTH = RWKV_HEADS * RWKV_HEAD_SIZE
LORA_DECAY = 64
LORA_AAA = 64
LORA_GATE = 160
RWKV_PROJ = 3 * RWKV_WIDTH + LORA_DECAY + LORA_AAA + LORA_GATE
O_RWKV = 2 * DIFF_QK_WIDTH + DIFF_WIDTH
REL_BUCKETS = 32
REL_MAX_DIST = 128
N_GROUPS = 4
EXPERTS_PER_GROUP = 8
N_EXPERTS = N_GROUPS * EXPERTS_PER_GROUP
LN_EPS = 1e-5
RWKV_GN_EPS = 64e-5
NEG_INF = -1e30
DEPTH = 1
DEEPNORM_ALPHA = (2 * DEPTH) ** 0.25
LOG2_E = math.log2(math.e)

LANES = 128
RWKV_PROJ_PAD = 3584
ROUTER_LANES = LANES
VMEM_LIMIT = 48 * 1024 * 1024
MOE_VMEM_LIMIT = 56 * 1024 * 1024
MOE_ROW_TILE = 256

LN_ROWS = 512
PROJ_ROWS = 2048
PROJ_COLS = 512
ATTN_Q_TILE = 256
ATTN_HEADS_PER_STEP = DIFF_HEADS
PREP_ROWS = 256
SCAN_CHUNK = 64
SCAN_PAIRS = RWKV_HEADS // 2
OUTPROJ_ROWS = 512
RANK_ROWS = 2048
FINAL_ROWS = 256


def _cparams(sem):
    return pltpu.CompilerParams(dimension_semantics=sem, vmem_limit_bytes=VMEM_LIMIT)


def _bdot(a, b):
    return jnp.dot(a.astype(BF16), b.astype(BF16), preferred_element_type=F32)


def _bdot_nt(a, b):
    return lax.dot_general(a.astype(BF16), b.astype(BF16), (((1,), (1,)), ((), ())),
                           preferred_element_type=F32)


def _bdot_tn(a, b):
    return lax.dot_general(a.astype(BF16), b.astype(BF16), (((0,), (0,)), ((), ())),
                           preferred_element_type=F32)


def _split3(x):
    hi = x.astype(BF16)
    r1 = x - hi.astype(F32)
    mid = r1.astype(BF16)
    lo = (r1 - mid.astype(F32)).astype(BF16)
    return hi, mid, lo


def _head_sum(x):
    lane = lax.broadcasted_iota(jnp.int32, (x.shape[0], LANES), 1)
    first = lane < RWKV_HEAD_SIZE
    cols = []
    for c in range(x.shape[1] // LANES):
        blk = x[:, c * LANES:(c + 1) * LANES]
        lo = jnp.sum(jnp.where(first, blk, 0.0), axis=1, keepdims=True)
        hi = jnp.sum(jnp.where(first, 0.0, blk), axis=1, keepdims=True)
        cols.append(jnp.where(first, lo, hi))
    return jnp.concatenate(cols, axis=1)


def _dot_sel_x(sel, x):
    hi, mid, lo = _split3(x)
    d = lambda p: jnp.dot(sel, p, preferred_element_type=F32)
    return d(hi) + d(mid) + d(lo)


def _layer_norm(x, g, b):
    mu = jnp.mean(x, axis=-1, keepdims=True)
    xc = x - mu
    var = jnp.mean(xc * xc, axis=-1, keepdims=True)
    return xc * lax.rsqrt(var + LN_EPS) * g + b


def _sigmoid(z):
    return 1.0 / (1.0 + jnp.exp(-z))


def _ln_kernel(x_ref, g_ref, b_ref, h_ref, hb_ref):
    xn = _layer_norm(x_ref[...], g_ref[...], b_ref[...])
    h_ref[...] = xn
    hb_ref[...] = xn.astype(BF16)


def layer_norm_rows(x, g, b, tm):
    m, d = x.shape
    tile = pl.BlockSpec((tm, d), lambda i: (i, 0))
    row = pl.BlockSpec((1, d), lambda i: (0, 0))
    return pl.pallas_call(
        _ln_kernel,
        grid=(m // tm,),
        in_specs=[tile, row, row],
        out_specs=[tile, tile],
        out_shape=[jax.ShapeDtypeStruct((m, d), F32), jax.ShapeDtypeStruct((m, d), BF16)],
        compiler_params=_cparams(("parallel",)),
        name="layer_norm_rows",
    )(x, g.reshape(1, d), b.reshape(1, d))


def _mm_nt_kernel(x_ref, wt_ref, o_ref, *, n_valid, tn):
    y = lax.dot_general(x_ref[...], wt_ref[...].astype(BF16), (((1,), (1,)), ((), ())),
                        preferred_element_type=F32)
    if n_valid % tn:
        col = pl.program_id(1) * tn + lax.broadcasted_iota(jnp.int32, y.shape, 1)
        y = jnp.where(col < n_valid, y, 0.0)
    o_ref[...] = y.astype(o_ref.dtype)


def matmul_nt(x, wt, row0, n_valid, n_out, out_dtype, tm, tn):
    m, k = x.shape
    blk0 = row0 // tn
    return pl.pallas_call(
        functools.partial(_mm_nt_kernel, n_valid=n_valid, tn=tn),
        grid=(m // tm, n_out // tn),
        in_specs=[pl.BlockSpec((tm, k), lambda i, j: (i, 0)),
                  pl.BlockSpec((tn, k), lambda i, j: (blk0 + j, 0))],
        out_specs=pl.BlockSpec((tm, tn), lambda i, j: (i, j)),
        out_shape=jax.ShapeDtypeStruct((m, n_out), out_dtype),
        compiler_params=_cparams(("parallel", "parallel")),
        name="matmul_nt",
    )(x, wt)


def _attn_kernel(lam_ref, far_ref, q_ref, k_ref, v_ref, bias_ref, g_ref, o_ref,
                 m_scr, l_scr, acc_scr, *, tq, hps, scale, post_scale):
    hp = pl.program_id(1)
    qi = pl.program_id(2)
    lane = lax.broadcasted_iota(jnp.int32, (tq, LANES), 1)
    q_maps = []
    for hh in range(hps):
        qs = (q_ref[0, :, hh * LANES:(hh + 1) * LANES].astype(F32) * scale).astype(BF16)
        zero = jnp.zeros_like(qs)
        q_maps += [jnp.where(lane < DIFF_HEAD_DIM, qs, zero), jnp.where(lane >= DIFF_HEAD_DIM, qs, zero)]

    m_scr[...] = jnp.full(m_scr.shape, NEG_INF, F32)
    l_scr[...] = jnp.zeros(l_scr.shape, F32)
    acc_scr[...] = jnp.zeros(acc_scr.shape, F32)

    def step(kstart, bias_index, causal):
        for hh in range(hps):
            kb = k_ref[0, pl.ds(kstart, tq), hh * LANES:(hh + 1) * LANES]
            vb = v_ref[0, pl.ds(kstart, tq), hh * LANES:(hh + 1) * LANES]
            for c in range(2 * hh, 2 * hh + 2):
                s = lax.dot_general(q_maps[c], kb, (((1,), (1,)), ((), ())), preferred_element_type=F32)
                m_old = m_scr[c]
                if bias_index is None:
                    far = far_ref[hp * hps + hh]
                    m_new = jnp.maximum(m_old, jnp.max(s, axis=1, keepdims=True) + far)
                    shift = m_new - far
                else:
                    s = s + bias_ref[hh, bias_index]
                    if causal:
                        ri = lax.broadcasted_iota(jnp.int32, (tq, tq), 0)
                        ci = lax.broadcasted_iota(jnp.int32, (tq, tq), 1)
                        s = jnp.where(ri >= ci, s, NEG_INF)
                    m_new = jnp.maximum(m_old, jnp.max(s, axis=1, keepdims=True))
                    shift = m_new
                p = jnp.exp2(s - jnp.concatenate([shift] * (tq // LANES), axis=1))
                corr = jnp.exp2(m_old - m_new)
                l_scr[c] = corr * l_scr[c] + jnp.sum(p, axis=1, keepdims=True)
                acc_scr[c] = corr * acc_scr[c] + jnp.dot(p.astype(BF16), vb, preferred_element_type=F32)
                m_scr[c] = m_new

    def far_body(kb, carry):
        step(pl.multiple_of(kb * tq, tq), None, False)
        return carry

    lax.fori_loop(0, jnp.maximum(qi - 1, 0), far_body, 0)

    @pl.when(qi >= 1)
    def _():
        step(pl.multiple_of((qi - 1) * tq, tq), 1, False)

    step(pl.multiple_of(qi * tq, tq), 0, True)

    for hh in range(hps):
        c = 2 * hh
        o = acc_scr[c] / l_scr[c] - lam_ref[0] * (acc_scr[c + 1] / l_scr[c + 1])
        ms = jnp.mean(o * o, axis=1, keepdims=True)
        o_ref[0, :, hh * LANES:(hh + 1) * LANES] = (
            o * lax.rsqrt(ms + LN_EPS) * g_ref[...] * post_scale).astype(o_ref.dtype)


def diff_attention(qkv, bias_tiles, far_bias, lam, subln_g, lambda_init, tq, hps):
    bsz, s, _ = qkv.shape
    kern = functools.partial(_attn_kernel, tq=tq, hps=hps, scale=DIFF_HEAD_DIM ** -0.5 * LOG2_E,
                             post_scale=1.0 - lambda_init)
    hw = hps * LANES
    nqk = DIFF_QK_WIDTH // hw
    return pl.pallas_call(
        kern,
        grid=(bsz, DIFF_HEADS // hps, s // tq),
        in_specs=[
            pl.BlockSpec(memory_space=pltpu.SMEM),
            pl.BlockSpec(memory_space=pltpu.SMEM),
            pl.BlockSpec((1, tq, hw), lambda b, h, i: (b, i, h)),
            pl.BlockSpec((1, s, hw), lambda b, h, i: (b, 0, nqk + h)),
            pl.BlockSpec((1, s, hw), lambda b, h, i: (b, 0, 2 * nqk + h)),
            pl.BlockSpec((hps, 2, tq, tq), lambda b, h, i: (h, 0, 0, 0)),
            pl.BlockSpec((1, LANES), lambda b, h, i: (0, 0)),
        ],
        out_specs=pl.BlockSpec((1, tq, hw), lambda b, h, i: (b, i, h)),
        out_shape=jax.ShapeDtypeStruct((bsz, s, DIFF_WIDTH), BF16),
        scratch_shapes=[pltpu.VMEM((2 * hps, tq, LANES), F32), pltpu.VMEM((2 * hps, tq, LANES), F32),
                        pltpu.VMEM((2 * hps, tq, LANES), F32)],
        compiler_params=_cparams(("parallel", "parallel", "arbitrary")),
        name="diff_attention",
    )(lam, far_bias, qkv, qkv, qkv, bias_tiles, subln_g.reshape(1, LANES))


def _rwkv_prep_kernel(x_ref, prev_ref, mu_ref, w0_ref, w2_ref, a0_ref, a2_ref, g2_ref,
                      kk_ref, ka_ref, rk_ref,
                      r_out, k_out, v_out, lw_out, kn_out, b_out, g_out, bonus_out, *, tm, seq):
    i = pl.program_id(0)
    x = x_ref[...]
    w = RWKV_WIDTH
    row = lax.broadcasted_iota(jnp.int32, x.shape, 0)
    is_start = (i * tm) % seq == 0
    last_prev = jnp.where(is_start, 0.0, prev_ref[7:8, :])
    prev = jnp.where(row == 0, last_prev, pltpu.roll(x, 1, 0))
    xs = x + (prev - x) * mu_ref[...]

    r = xs[:, 0:w]
    k = xs[:, w:2 * w]
    v = xs[:, 2 * w:3 * w]
    xwa = xs[:, 3 * w:3 * w + 2 * LORA_DECAY]
    xg = xs[:, 3 * w + LANES:3 * w + LANES + 2 * LANES]

    z = w0_ref[...] + _bdot(jnp.tanh(xwa), w2_ref[...])
    softplus_neg_z = jnp.maximum(-z, 0.0) + jnp.log(1.0 + jnp.exp(-jnp.abs(z)))
    lw_out[...] = -jnp.exp(-softplus_neg_z - 0.5)
    a_lr = _sigmoid(a0_ref[...] + _bdot(xwa, a2_ref[...]))
    g_out[...] = _bdot(_sigmoid(xg), g2_ref[...])

    kk = k * kk_ref[...]
    kn = kk / jnp.maximum(jnp.sqrt(_head_sum(kk * kk)), 1e-12)
    k2 = k * (1.0 + (a_lr - 1.0) * ka_ref[...])
    r_out[...] = r
    k_out[...] = k2
    v_out[...] = v
    kn_out[...] = kn
    b_out[...] = kn * a_lr
    bonus_out[...] = _head_sum(r * k2 * rk_ref[...]) * v


def rwkv_prep(pr, mu, w0, w2p, a0, a2p, g2p, k_k, k_a, r_k, seq, tm):
    t, wp = pr.shape
    w = RWKV_WIDTH
    row = lambda a: a.reshape(1, -1)
    full = lambda shp: pl.BlockSpec(shp, lambda i: (0, 0))
    outs = [jax.ShapeDtypeStruct((t, w), F32)] * 8
    kern = functools.partial(_rwkv_prep_kernel, tm=tm, seq=seq)
    return pl.pallas_call(
        kern,
        grid=(t // tm,),
        in_specs=[
            pl.BlockSpec((tm, wp), lambda i: (i, 0)),
            pl.BlockSpec((8, wp), lambda i: (jnp.maximum(i * (tm // 8) - 1, 0), 0)),
            full((1, wp)), full((1, w)), full((LANES, w)), full((1, w)), full((LANES, w)),
            full((2 * LANES, w)), full((1, w)), full((1, w)), full((1, w)),
        ],
        out_specs=[pl.BlockSpec((tm, w), lambda i: (i, 0))] * 8,
        out_shape=outs,
        compiler_params=_cparams(("parallel",)),
        name="rwkv_prep",
    )(pr, pr, row(mu), row(w0), w2p, row(a0), a2p, g2p, row(k_k), row(k_a), row(r_k))


def _rwkv_scan_kernel(r_ref, k_ref, v_ref, lw_ref, kn_ref, b_ref, y_ref, s_scr, *, chunk, pairs):
    c2 = 2 * chunk

    @pl.when(pl.program_id(2) == 0)
    def _():
        s_scr[...] = jnp.zeros(s_scr.shape, F32)

    ri = lax.broadcasted_iota(jnp.int32, (chunk, chunk), 0)
    ci = lax.broadcasted_iota(jnp.int32, (chunk, chunk), 1)
    ltri = jnp.where(ri >= ci, 1.0, 0.0).astype(BF16)

    lw = lw_ref[0]
    cl = _dot_sel_x(ltri, lw)
    mid = chunk // 2 - 1
    clm = cl[mid:mid + 1, :]
    cle = cl[chunk - 1:chunk, :]
    w_mid = jnp.exp(cl - clm)
    w_mid_prev = jnp.exp(cl - lw - clm)
    w_mid_inv = jnp.exp(clm - cl)
    w_abs = jnp.exp(cl)
    w_abs_prev = jnp.exp(cl - lw)
    w_end = jnp.exp(cle - cl)
    w_tot = jnp.exp(cle)

    r = r_ref[0]
    k = k_ref[0]
    v = v_ref[0]
    a = -kn_ref[0]
    b = b_ref[0]
    a_mid = a * w_mid_prev
    r_mid = r * w_mid
    b_mid = b * w_mid_inv
    k_mid = k * w_mid_inv
    a_abs = a * w_abs_prev
    r_abs = r * w_abs
    b_end = b * w_end
    k_end = k * w_end

    lane = lax.broadcasted_iota(jnp.int32, (chunk, LANES), 1)
    first = lane < RWKV_HEAD_SIZE

    def stack(x):
        return jnp.concatenate([jnp.where(first, x, 0.0), jnp.where(first, 0.0, x)], axis=0)

    rr = lax.broadcasted_iota(jnp.int32, (c2, c2), 0)
    cc = lax.broadcasted_iota(jnp.int32, (c2, c2), 1)
    same = (rr >= chunk) == (cc >= chunk)
    strict = same & (rr > cc)
    incl = same & (rr >= cc)
    eye = jnp.where(rr == cc, 1.0, 0.0)

    ps = range(pairs)
    sls = [slice(p * LANES, (p + 1) * LANES) for p in ps]
    bf = lambda x: x.astype(BF16)
    cat0 = lambda xs: jnp.concatenate(xs, axis=0)
    cat1 = lambda xs: jnp.concatenate(xs, axis=1)
    v_s = [bf(stack(v[:, sl])) for sl in sls]
    ar_m = [bf(cat0([stack(a_mid[:, sl]), stack(r_mid[:, sl])])) for sl in sls]
    bk_m = [bf(cat0([stack(b_mid[:, sl]), stack(k_mid[:, sl])])) for sl in sls]
    g_all = [_bdot_nt(ar_m[p], bk_m[p]) for p in ps]
    l_ab = [jnp.where(strict, g[:c2, :c2], 0.0) for g in g_all]
    l_ak = [bf(jnp.where(strict, g[:c2, c2:], 0.0)) for g in g_all]
    m_r = [bf(cat1([jnp.where(incl, g[c2:, :c2], 0.0), jnp.where(incl, g[c2:, c2:], 0.0)])) for g in g_all]

    t_inv = [eye + l_ab[p] for p in ps]
    n_fac = max(chunk.bit_length() - 2, 0)
    lp = [_bdot(l_ab[p], l_ab[p]) for p in ps] if n_fac else l_ab
    for _ in range(n_fac - 1):
        x = [_bdot(cat0([t_inv[p], lp[p]]), lp[p]) for p in ps]
        t_inv = [t_inv[p] + x[p][:c2] for p in ps]
        lp = [x[p][c2:] for p in ps]
    if n_fac:
        t_inv = [t_inv[p] + _bdot(t_inv[p], lp[p]) for p in ps]
    t_inv = [bf(t) for t in t_inv]

    lv = [_bdot(l_ak[p], v_s[p]) for p in ps]
    av = [bf(_bdot(t_inv[p], cat1([stack(a_abs[:, sls[p]]), lv[p]]))) for p in ps]
    zero = jnp.zeros((c2, LANES), BF16)
    ry = [_bdot(m_r[p], cat0([av[p], cat1([zero, v_s[p]])])) for p in ps]
    r_p = [stack(r_abs[:, sls[p]]) + ry[p][:, :LANES] for p in ps]

    s_old = [s_scr[p] for p in ps]
    y_s = [_bdot_nt(r_p[p], s_old[p]) + ry[p][:, LANES:] for p in ps]
    for p in ps:
        y_ref[0, :, sls[p]] = y_s[p][:chunk] + y_s[p][chunk:]
    b_e = [bf(stack(b_end[:, sl])) for sl in sls]
    k_e = [bf(stack(k_end[:, sl])) for sl in sls]
    ab = [_bdot_tn(av[p][:, :LANES], b_e[p]) for p in ps]
    q_t = [_bdot_tn(cat0([av[p][:, LANES:], v_s[p]]), cat0([b_e[p], k_e[p]])) for p in ps]
    for p in ps:
        s_scr[p] = s_old[p] * w_tot[:, sls[p]] + _bdot(s_old[p], ab[p]) + q_t[p]


def rwkv_scan(r, k, v, lw, kn, b, chunk, pairs):
    bsz, s, w = r.shape
    pw = pairs * LANES
    spec = pl.BlockSpec((1, chunk, pw), lambda bi, hi, ci: (bi, ci, hi))
    kern = functools.partial(_rwkv_scan_kernel, chunk=chunk, pairs=pairs)
    return pl.pallas_call(
        kern,
        grid=(bsz, w // pw, s // chunk),
        in_specs=[spec] * 6,
        out_specs=spec,
        out_shape=jax.ShapeDtypeStruct((bsz, s, w), F32),
        scratch_shapes=[pltpu.VMEM((pairs, LANES, LANES), F32)],
        compiler_params=_cparams(("parallel", "parallel", "arbitrary")),
        name="rwkv_scan",
    )(r, k, v, lw, kn, b)


def _rwkv_out(y, bonus, g, lnx_g, lnx_b):
    inv_n = 1.0 / RWKV_HEAD_SIZE
    mu = _head_sum(y) * inv_n
    yc = y - mu
    var = _head_sum(yc * yc) * inv_n
    yn = yc * lax.rsqrt(var + RWKV_GN_EPS) * lnx_g + lnx_b
    return (yn + bonus) * g


def _route(logits):
    lane = lax.broadcasted_iota(jnp.int32, logits.shape, 1)
    big = jnp.int32(4 * LANES)
    gmask = lane < N_GROUPS
    gl = jnp.where(gmask, logits, NEG_INF)
    gmax = jnp.max(gl, axis=1, keepdims=True)
    gidx = jnp.min(jnp.where(gl == gmax, lane, big), axis=1, keepdims=True)
    gtop = 1.0 / jnp.sum(jnp.where(gmask, jnp.exp(gl - gmax), 0.0), axis=1, keepdims=True)
    lo = N_GROUPS + EXPERTS_PER_GROUP * gidx
    emask = (lane >= lo) & (lane < lo + EXPERTS_PER_GROUP)
    el = jnp.where(emask, logits, NEG_INF)
    e1 = jnp.max(el, axis=1, keepdims=True)
    i1 = jnp.min(jnp.where(el == e1, lane, big), axis=1, keepdims=True)
    el2 = jnp.where(lane == i1, NEG_INF, el)
    e2 = jnp.max(el2, axis=1, keepdims=True)
    i2 = jnp.min(jnp.where(el2 == e2, lane, big), axis=1, keepdims=True)
    t = jnp.exp(e2 - e1)
    w1 = gtop / (1.0 + t)
    w2 = gtop * t / (1.0 + t)
    id1 = (i1 - N_GROUPS).astype(F32)
    id2 = (i2 - N_GROUPS).astype(F32)
    return jnp.where(lane == 0, id1, jnp.where(lane == 1, id2, jnp.where(lane == 2, w1,
                                                                         jnp.where(lane == 3, w2, 0.0))))


def _outproj_kernel(yd_ref, ys_ref, bonus_ref, gate_ref, lg_ref, lb_ref, h_ref,
                    w1_ref, w2_ref, g_ref, b_ref, rw_hi_ref, rw_lo_ref, rb_ref, h1_ref, info_ref, *, parts):
    rows = h_ref.shape[0] // parts
    sls = [pl.ds(k * rows, rows) for k in range(parts)]
    d = lambda a, b: jnp.dot(a, b, preferred_element_type=F32)
    yr = [_rwkv_out(ys_ref[sl, :], bonus_ref[sl, :], gate_ref[sl, :], lg_ref[...], lb_ref[...]) for sl in sls]
    mix = [d(yd_ref[sl, :], w1_ref[...]) + d(yr[k].astype(BF16), w2_ref[...]) for k, sl in enumerate(sls)]
    h1 = [_layer_norm(DEEPNORM_ALPHA * h_ref[sl, :] + mix[k], g_ref[...], b_ref[...]) for k, sl in enumerate(sls)]
    for k, sl in enumerate(sls):
        h1_ref[sl, :] = h1[k]
        hi = h1[k].astype(BF16)
        lo = (h1[k] - hi.astype(F32)).astype(BF16)
        logits = d(hi, rw_hi_ref[...]) + d(lo, rw_hi_ref[...]) + d(hi, rw_lo_ref[...]) + rb_ref[...]
        info_ref[sl, :] = _route(logits)


def outproj_ln_route(yd, ys, bonus, gate, lnx_g, lnx_b, h, w_out_a, w_out_b, g, b, rw_hi, rw_lo, rb, tm):
    t, d = h.shape
    ka = yd.shape[1]
    kb = ys.shape[1]
    full = lambda shp: pl.BlockSpec(shp, lambda i: (0, 0))
    tile = lambda n: pl.BlockSpec((tm, n), lambda i: (i, 0))
    return pl.pallas_call(
        functools.partial(_outproj_kernel, parts=2),
        grid=(t // tm,),
        in_specs=[
            tile(ka), tile(kb), tile(kb), tile(kb), full((1, kb)), full((1, kb)), tile(d),
            full((ka, d)), full((kb, d)), full((1, d)), full((1, d)),
            full((d, ROUTER_LANES)), full((d, ROUTER_LANES)), full((1, ROUTER_LANES)),
        ],
        out_specs=[
            pl.BlockSpec((tm, d), lambda i: (i, 0)),
            pl.BlockSpec((tm, ROUTER_LANES), lambda i: (i, 0)),
        ],
        out_shape=[jax.ShapeDtypeStruct((t, d), F32), jax.ShapeDtypeStruct((t, ROUTER_LANES), F32)],
        compiler_params=pltpu.CompilerParams(dimension_semantics=("parallel",),
                                             vmem_limit_bytes=MOE_VMEM_LIMIT),
        name="outproj_ln_route",
    )(yd, ys, bonus, gate, lnx_g.reshape(1, kb), lnx_b.reshape(1, kb), h, w_out_a, w_out_b,
      g.reshape(1, d), b.reshape(1, d), rw_hi, rw_lo, rb)


def _moe_rank_kernel(e_ref, pos_ref, ends_ref, run_scr, start_scr, *, tm, row_tile):
    ph = pl.program_id(0)
    i = pl.program_id(1)
    lane = lax.broadcasted_iota(jnp.int32, (tm, LANES), 1)
    onehot = e_ref[...] == lane
    oh = jnp.where(onehot, 1.0, 0.0)

    @pl.when((ph == 0) & (i == 0))
    def _():
        run_scr[...] = jnp.zeros(run_scr.shape, F32)

    @pl.when(ph == 0)
    def _():
        run_scr[...] += jnp.sum(oh, axis=0, keepdims=True)

    @pl.when((ph == 1) & (i == 0))
    def _():
        padded = jnp.floor((run_scr[...] + (row_tile - 1)) * (1.0 / row_tile)) * row_tile
        ri = lax.broadcasted_iota(jnp.int32, (LANES, LANES), 0)
        ci = lax.broadcasted_iota(jnp.int32, (LANES, LANES), 1)
        upper = jnp.where(ri < ci, 1.0, 0.0).astype(BF16)
        parts = _split3(jnp.broadcast_to(padded, (8, LANES)))
        starts = sum(jnp.dot(q, upper, preferred_element_type=F32) for q in parts)[0:1]
        start_scr[...] = starts
        ends_ref[...] = starts + padded
        run_scr[...] = jnp.zeros(run_scr.shape, F32)

    @pl.when(ph == 1)
    def _():
        ri = lax.broadcasted_iota(jnp.int32, (tm, tm), 0)
        ci = lax.broadcasted_iota(jnp.int32, (tm, tm), 1)
        before = jnp.where(ri > ci, 1.0, 0.0).astype(BF16)
        row = jnp.dot(before, oh.astype(BF16), preferred_element_type=F32) + (run_scr[...] + start_scr[...])
        pos_ref[...] = jnp.sum(jnp.where(onehot, row, 0.0), axis=1, keepdims=True).astype(jnp.int32)
        run_scr[...] += jnp.sum(oh, axis=0, keepdims=True)


def moe_rank(e_col, tm, row_tile):
    n = e_col.shape[0]
    return pl.pallas_call(
        functools.partial(_moe_rank_kernel, tm=tm, row_tile=row_tile),
        grid=(2, n // tm),
        in_specs=[pl.BlockSpec((tm, 1), lambda ph, i: (i, 0))],
        out_specs=[pl.BlockSpec((tm, 1), lambda ph, i: (i * ph, 0)),
                   pl.BlockSpec((1, LANES), lambda ph, i: (0, 0))],
        out_shape=[jax.ShapeDtypeStruct((n, 1), jnp.int32), jax.ShapeDtypeStruct((1, LANES), F32)],
        scratch_shapes=[pltpu.VMEM((1, LANES), F32), pltpu.VMEM((1, LANES), F32)],
        compiler_params=_cparams(("arbitrary", "arbitrary")),
        name="moe_rank",
    )(e_col)


def _row_copy(src_hbm, row, dst, r, sem):
    return pltpu.make_async_copy(src_hbm.at[pl.ds(row, 1)], dst.at[pl.ds(r, 1)], sem)


def _pack_bf16_halves(y):
    n = y.shape[1] // 2
    hi = pltpu.bitcast(y[:, :n].astype(BF16).astype(F32), jnp.uint32)
    lo = pltpu.bitcast(y[:, n:].astype(BF16).astype(F32), jnp.uint32)
    return hi | (lo >> 16)


def _unpack_bf16_halves(w):
    hi = pltpu.bitcast(w & jnp.uint32(0xFFFF0000), F32)
    lo = pltpu.bitcast(w << 16, F32)
    return jnp.concatenate([hi, lo], axis=1)


def _moe_expert_kernel(te_ref, first_ref, par_ref, nxt_ref, nact_ref,
                       src0_ref, src1_ref, src2_ref, src_ahead_ref, x_hbm, wg_hbm, wu_hbm, wd_hbm, o_ref,
                       xbuf0, xbuf1, xbuf2, xbuf3, wgf, wuf, wdf, wgb, wub, wdb, sem, wsem, *, tm):
    j = pl.program_id(0)
    n_act = nact_ref[0]
    bufs = (xbuf0, xbuf1, xbuf2, xbuf3)
    nb = len(bufs)

    def token_copy(tok, into, r):
        return pltpu.make_async_copy(x_hbm.at[tok], bufs[into].at[:, r, :], sem.at[into])

    def start_gather(src, into):
        for r in range(tm):
            token_copy(src[0, 0, r], into, r).start(priority=r % 2)

    def wait_gather(into):
        for r in range(tm):
            token_copy(0, into, r).wait()

    def weight_copies(e, par):
        return (pltpu.make_async_copy(wg_hbm.at[e], wgf.at[par], wsem.at[par]),
                pltpu.make_async_copy(wu_hbm.at[e], wuf.at[par], wsem.at[par]),
                pltpu.make_async_copy(wd_hbm.at[e], wdf.at[par], wsem.at[par]))

    @pl.when(j == 0)
    def _():
        start_gather(src0_ref, 0)
        start_gather(src1_ref, 1)
        start_gather(src2_ref, 2)
        for c in weight_copies(te_ref[0], 0):
            c.start(priority=1)

    @pl.when((j < n_act) & (first_ref[j] > 0))
    def _():
        par = par_ref[j]
        for c in weight_copies(te_ref[j], par):
            c.wait()
        wgb[...] = wgf[par].astype(BF16)
        wub[...] = wuf[par].astype(BF16)
        wdb[...] = wdf[par].astype(BF16)

    for slot in range(nb):
        @pl.when((j < n_act) & (j % nb == slot))
        def _():
            wait_gather(slot)
            start_gather(src_ahead_ref, (slot + nb - 1) % nb)
            x = jnp.concatenate([bufs[slot][c] for c in range(bufs[slot].shape[0])], axis=1).astype(BF16)
            hg = jnp.dot(x, wgb[...], preferred_element_type=F32)
            hu = jnp.dot(x, wub[...], preferred_element_type=F32)
            hid = hg * _sigmoid(hg) * hu
            o_ref[...] = _pack_bf16_halves(jnp.dot(hid.astype(BF16), wdb[...], preferred_element_type=F32))

        @pl.when((j == n_act - 1) & (j % nb == slot))
        def _():
            for ahead in range(1, nb):
                wait_gather((slot + ahead) % nb)

    @pl.when((j < n_act) & (first_ref[j] > 0) & (nxt_ref[j] >= 0))
    def _():
        for c in weight_copies(nxt_ref[j], 1 - par_ref[j]):
            c.start(priority=1)

    @pl.when(j >= n_act)
    def _():
        o_ref[...] = jnp.zeros(o_ref.shape, jnp.uint32)


def moe_experts(src_tok, plan, x, wg, wu, wd, n_rows, tm):
    d, f = wg.shape[1:]
    hbm = pl.BlockSpec(memory_space=pl.ANY)
    src3 = src_tok.reshape(n_rows // tm, 1, tm)
    smem_tile = lambda index_map: pl.BlockSpec((1, 1, tm), index_map, memory_space=pltpu.SMEM)
    return pl.pallas_call(
        functools.partial(_moe_expert_kernel, tm=tm),
        grid_spec=pltpu.PrefetchScalarGridSpec(
            num_scalar_prefetch=5,
            grid=(n_rows // tm,),
            in_specs=[smem_tile(lambda j, *_: (0, 0, 0)),
                      smem_tile(lambda j, te, fi, pa, nx, na: (jnp.minimum(1, na[0] - 1), 0, 0)),
                      smem_tile(lambda j, te, fi, pa, nx, na: (jnp.minimum(2, na[0] - 1), 0, 0)),
                      smem_tile(lambda j, te, fi, pa, nx, na: (jnp.minimum(j + 3, na[0] - 1), 0, 0)),
                      hbm, hbm, hbm, hbm],
            out_specs=pl.BlockSpec((tm, d // 2), lambda j, *_: (j, 0)),
            scratch_shapes=[pltpu.VMEM((d // LANES, tm, LANES), F32)] * 4 + [
                            pltpu.VMEM((2, d, f), F32), pltpu.VMEM((2, d, f), F32), pltpu.VMEM((2, f, d), F32),
                            pltpu.VMEM((d, f), BF16), pltpu.VMEM((d, f), BF16), pltpu.VMEM((f, d), BF16),
                            pltpu.SemaphoreType.DMA((4,)), pltpu.SemaphoreType.DMA((2,))],
        ),
        out_shape=jax.ShapeDtypeStruct((n_rows, d // 2), jnp.uint32),
        compiler_params=pltpu.CompilerParams(dimension_semantics=("arbitrary",),
                                             vmem_limit_bytes=MOE_VMEM_LIMIT),
        name="moe_experts",
    )(*plan, src3, src3, src3, src3, x, wg, wu, wd)


def _final_kernel(p1_ref, p2_ref, y_hbm, h1_ref, info_ref, p_ref, g_ref, b_ref, pw_ref, pg_ref, gw_ref,
                  o_ref, ybuf0, ybuf1, sem, *, tm, n_tiles, parts):
    i = pl.program_id(0)
    bufs = (ybuf0, ybuf1)

    def start_gather(tile, into):
        for r in range(tm):
            _row_copy(y_hbm, p1_ref[tile * tm + r], bufs[into].at[0], r, sem.at[into]).start(priority=0)
            _row_copy(y_hbm, p2_ref[tile * tm + r], bufs[into].at[1], r, sem.at[into]).start(priority=1)

    def wait_gather(into):
        for r in range(tm):
            _row_copy(y_hbm, 0, bufs[into].at[0], r, sem.at[into]).wait()
            _row_copy(y_hbm, 0, bufs[into].at[1], r, sem.at[into]).wait()

    @pl.when(i == 0)
    def _():
        start_gather(0, 0)

    for slot in range(2):
        @pl.when(i % 2 == slot)
        def _():
            wait_gather(slot)
            start_gather(jnp.minimum(i + 1, n_tiles - 1), 1 - slot)
            rows = tm // parts
            sls = [pl.ds(k * rows, rows) for k in range(parts)]
            h2, ple = [], []
            for sl in sls:
                info = info_ref[sl, :]
                ffn = (info[:, 2:3] * _unpack_bf16_halves(bufs[slot][0, sl, :])
                       + info[:, 3:4] * _unpack_bf16_halves(bufs[slot][1, sl, :]))
                h2.append(_layer_norm(DEEPNORM_ALPHA * h1_ref[sl, :] + ffn, g_ref[...], b_ref[...]))
                pe = jnp.dot(p_ref[sl, :].astype(BF16), pw_ref[...], preferred_element_type=F32)
                ple.append(pe * lax.rsqrt(jnp.mean(pe * pe, axis=-1, keepdims=True) + LN_EPS) * pg_ref[...])
            gate = [jnp.dot(h.astype(BF16), gw_ref[...], preferred_element_type=F32) for h in h2]
            for k, sl in enumerate(sls):
                o_ref[sl, :] = h2[k] + _sigmoid(gate[k]) * ple[k]

        @pl.when((i == n_tiles - 1) & (i % 2 == slot))
        def _():
            wait_gather(1 - slot)


def final_block(pos1, pos2, ys, h1, info, p, g, b, ple_w, ple_g, gate_w, tm):
    t, d = h1.shape
    dp = p.shape[1]
    tile = lambda n: pl.BlockSpec((tm, n), lambda i, p1, p2: (i, 0))
    full = lambda shp: pl.BlockSpec(shp, lambda i, p1, p2: (0, 0))
    return pl.pallas_call(
        functools.partial(_final_kernel, tm=tm, n_tiles=t // tm, parts=2),
        grid_spec=pltpu.PrefetchScalarGridSpec(
            num_scalar_prefetch=2,
            grid=(t // tm,),
            in_specs=[pl.BlockSpec(memory_space=pl.ANY), tile(d), tile(ROUTER_LANES), tile(dp),
                      full((1, d)), full((1, d)), full((dp, d)), full((1, d)), full((d, d))],
            out_specs=tile(d),
            scratch_shapes=[pltpu.VMEM((2, tm, d // 2), jnp.uint32), pltpu.VMEM((2, tm, d // 2), jnp.uint32),
                            pltpu.SemaphoreType.DMA((2,))],
        ),
        out_shape=jax.ShapeDtypeStruct((t, d), F32),
        compiler_params=pltpu.CompilerParams(dimension_semantics=("arbitrary",),
                                             vmem_limit_bytes=MOE_VMEM_LIMIT),
        name="final_block",
    )(pos1, pos2, ys, h1, info, p, g.reshape(1, d), b.reshape(1, d), ple_w, ple_g.reshape(1, d), gate_w)


def _dispatch_plan(pos, ends_f, n_tok, tm, n_rows):
    n_tiles = n_rows // tm
    ends = ends_f[0, :N_EXPERTS].astype(jnp.int32)
    src_tok = jnp.zeros((n_rows,), jnp.int32).at[pos].set(
        jnp.arange(2 * n_tok, dtype=jnp.int32) % n_tok, unique_indices=True)
    n_active = ends[-1] // tm
    tile_ids = jnp.arange(n_tiles, dtype=jnp.int32)
    tile_expert = jnp.searchsorted(ends, jnp.minimum(tile_ids, n_active - 1) * tm, side="right")
    tile_expert = jnp.minimum(tile_expert, N_EXPERTS - 1).astype(jnp.int32)
    first = jnp.concatenate([jnp.ones((1,), jnp.int32),
                             (tile_expert[1:] != tile_expert[:-1]).astype(jnp.int32)])
    parity = (jnp.cumsum(first) - 1) % 2
    ids = jnp.arange(N_EXPERTS, dtype=jnp.int32)
    present = ends > jnp.concatenate([jnp.zeros((1,), jnp.int32), ends[:-1]])
    later = present[None, :] & (ids[None, :] > ids[:, None])
    next_of = jnp.min(jnp.where(later, ids[None, :], N_EXPERTS), axis=1)
    next_of = jnp.where(next_of == N_EXPERTS, -1, next_of)
    plan = (tile_expert, first, parity.astype(jnp.int32), next_of[tile_expert].astype(jnp.int32),
            n_active.astype(jnp.int32).reshape(1))
    return src_tok, plan


def _t5_bucket(rel):
    n = jnp.maximum(rel, 0)
    max_exact = REL_BUCKETS // 2
    nf = jnp.maximum(n, 1).astype(F32)
    large = max_exact + (jnp.log(nf / max_exact) / math.log(REL_MAX_DIST / max_exact)
                         * (REL_BUCKETS - max_exact)).astype(jnp.int32)
    large = jnp.minimum(large, REL_BUCKETS - 1)
    return jnp.where(n < max_exact, n, large)


def _bias_tables(rel_bias, tq):
    n = 3 * tq
    rel = 2 * tq - 1 - jnp.arange(n, dtype=jnp.int32)
    g = rel_bias[_t5_bucket(rel)].astype(F32).T
    skew = jnp.tile(g, (1, tq))[:, :tq * (n - 1)].reshape(-1, tq, n - 1)
    tiles = jnp.stack([skew[:, :, 2 * tq - 1:3 * tq - 1], skew[:, :, tq - 1:2 * tq - 1]], axis=1)
    far = rel_bias[_t5_bucket(jnp.full((), 2 * tq, jnp.int32))].astype(F32)
    return tiles * LOG2_E, far * LOG2_E


def kernel(x, p, ln_in_g, ln_in_b, rel_bias, w_in, diff_lam_q1, diff_lam_k1, diff_lam_q2, diff_lam_k2, diff_subln_g, rwkv_mu, rwkv_w0, rwkv_w2, rwkv_a0, rwkv_a2, rwkv_g2, rwkv_k_k, rwkv_k_a, rwkv_r_k, rwkv_lnx_g, rwkv_lnx_b, w_out, ln1_g, ln1_b, router_group_w, router_group_b, router_expert_w, router_expert_b, moe_w_gate, moe_w_up, moe_w_down, ln2_g, ln2_b, ple_w, ple_norm_g, ple_gate_w):
    bsz, seq, d = x.shape
    t = bsz * seq
    i = 0
    lambda_init = 0.8 - 0.6 * math.exp(-0.3 * i)
    tq = ATTN_Q_TILE

    x2 = x.reshape(t, d)
    w_in_t = jnp.swapaxes(w_in[i], 0, 1)
    h, hb = layer_norm_rows(x2, ln_in_g, ln_in_b, LN_ROWS)
    qkv = matmul_nt(hb, w_in_t, 0, O_RWKV, O_RWKV, BF16, PROJ_ROWS, PROJ_COLS)
    pr = matmul_nt(hb, w_in_t, O_RWKV, RWKV_PROJ, RWKV_PROJ_PAD, F32, PROJ_ROWS, PROJ_COLS)

    lam = (jnp.exp(jnp.sum(diff_lam_q1[i].astype(F32) * diff_lam_k1[i]))
           - jnp.exp(jnp.sum(diff_lam_q2[i].astype(F32) * diff_lam_k2[i])) + lambda_init)
    bias_tiles, far_bias = _bias_tables(rel_bias, tq)
    y_diff = diff_attention(qkv.reshape(bsz, seq, O_RWKV), bias_tiles, far_bias, lam.reshape(1),
                            diff_subln_g[i], lambda_init, tq, ATTN_HEADS_PER_STEP)

    mu_p = jnp.pad(rwkv_mu[i], (0, RWKV_PROJ_PAD - RWKV_PROJ))
    zeros64 = jnp.zeros((LORA_DECAY, RWKV_WIDTH), F32)
    w2p = jnp.concatenate([rwkv_w2[i], zeros64], axis=0).astype(BF16)
    a2p = jnp.concatenate([zeros64, rwkv_a2[i]], axis=0).astype(BF16)
    g2p = jnp.pad(rwkv_g2[i], ((0, 2 * LANES - LORA_GATE), (0, 0))).astype(BF16)
    r, k2, v, lw, kn, bb, g, bonus = rwkv_prep(
        pr, mu_p, rwkv_w0[i], w2p, rwkv_a0[i], a2p, g2p, rwkv_k_k[i], rwkv_k_a[i],
        rwkv_r_k[i].reshape(-1), seq, PREP_ROWS)
    sh = lambda a: a.reshape(bsz, seq, RWKV_WIDTH)
    y_scan = rwkv_scan(sh(r), sh(k2), sh(v), sh(lw), sh(kn), sh(bb), SCAN_CHUNK, SCAN_PAIRS)

    w_out_b = w_out[i].astype(BF16)
    rw = jnp.concatenate([router_group_w[i], router_expert_w[i]], axis=1)
    rw = jnp.pad(rw, ((0, 0), (0, ROUTER_LANES - rw.shape[1])))
    rw_hi = rw.astype(BF16)
    rw_lo = (rw - rw_hi.astype(F32)).astype(BF16)
    rb = jnp.pad(jnp.concatenate([router_group_b[i], router_expert_b[i]]),
                 (0, ROUTER_LANES - N_GROUPS - N_EXPERTS)).reshape(1, ROUTER_LANES)
    h1, info = outproj_ln_route(y_diff.reshape(t, DIFF_WIDTH), y_scan.reshape(t, RWKV_WIDTH), bonus, g,
                                rwkv_lnx_g[i], rwkv_lnx_b[i], h, w_out_b[:DIFF_WIDTH],
                                w_out_b[DIFF_WIDTH:], ln1_g[i], ln1_b[i], rw_hi, rw_lo, rb, OUTPROJ_ROWS)

    tm_e = MOE_ROW_TILE
    n_rows = 2 * t + N_EXPERTS * tm_e
    e_all = jnp.concatenate([info[:, 0], info[:, 1]]).astype(jnp.int32)
    pos, ends = moe_rank(e_all.reshape(-1, 1), RANK_ROWS, tm_e)
    pos = pos[:, 0]
    src_tok, plan = _dispatch_plan(pos, ends, t, tm_e, n_rows)
    f = moe_w_gate.shape[-1]
    ys = moe_experts(src_tok, plan, h1.reshape(t, d // LANES, LANES), moe_w_gate[i].reshape(N_EXPERTS, d, f),
                     moe_w_up[i].reshape(N_EXPERTS, d, f), moe_w_down[i].reshape(N_EXPERTS, f, d),
                     n_rows, tm_e)

    out = final_block(pos[:t], pos[t:], ys, h1, info, p[i].reshape(t, -1), ln2_g[i], ln2_b[i],
                      ple_w[i].astype(BF16), ple_norm_g[i], ple_gate_w[i].astype(BF16), FINAL_ROWS)
    return out.reshape(bsz, seq, d)
```
